```python
import math
import numpy as np
import jax
import jax.numpy as jnp
from jax import lax

D_MODEL = 4096
BATCH = 2
SEQ = 8192
DEPTH = 2

GRID_W = 64
CTX_LEN = 256
HEAD_DIM = 128
ROPE_THETA = 10000.0
NORM_EPS = 1e-6
MIX_HALF = D_MODEL // 2

GLA_DK = 128
GLA_DV = 256
GLA_HEADS = MIX_HALF // GLA_DV
GLA_QK = GLA_HEADS * GLA_DK
GLA_RANK = 16
GLA_NORMALIZER = 16.0
GLA_CHUNK = 64

SWA_HEADS = MIX_HALF // HEAD_DIM
SWA_KV_HEADS = SWA_HEADS // 4
SWA_WINDOW = 128
SWA_BLOCK = 128

NA_HEADS = MIX_HALF // HEAD_DIM
NA_ROWS = 8
NA_COLS = 16
NA_COL_BLOCK = 16
NA_KEY_COLS = NA_COL_BLOCK + NA_COLS

DIFF_HEADS = MIX_HALF // (2 * HEAD_DIM)
DIFF_DV = 2 * HEAD_DIM
DIFF_BLOCK = 128

FFN_HIDDEN = -(-8 * D_MODEL // (3 * 256)) * 256

EVEN_SPLITS = (GLA_QK, GLA_QK, GLA_HEADS * GLA_DV, GLA_HEADS * GLA_DV, GLA_RANK, GLA_RANK,
               SWA_HEADS * HEAD_DIM, SWA_KV_HEADS * HEAD_DIM, SWA_KV_HEADS * HEAD_DIM)
ODD_SPLITS = (NA_HEADS * HEAD_DIM, NA_HEADS * HEAD_DIM, NA_HEADS * HEAD_DIM,
              DIFF_HEADS * 2 * HEAD_DIM, DIFF_HEADS * 2 * HEAD_DIM, DIFF_HEADS * DIFF_DV)
EVEN_IN = sum(EVEN_SPLITS)
ODD_IN = sum(ODD_SPLITS)

kernel_name = 'hybrid_flow_backbone_gla_swa_natten_diff'


def rms_norm(t, g):
    tf = t.astype(jnp.float32)
    y = tf * lax.rsqrt(jnp.mean(tf * tf, axis=-1, keepdims=True) + NORM_EPS)
    return (y * g.astype(jnp.float32)).astype(t.dtype)


def modulate(t, shift, scale):
    return t * (1 + scale) + shift


def split_cols(t, widths):
    return jnp.split(t, np.cumsum(widths)[:-1].tolist(), axis=-1)


def heads(t, *shape):
    return t.reshape(t.shape[:2] + shape)


def swiglu(t, w_gate, w_up, w_down):
    return (jax.nn.silu(t @ w_gate) * (t @ w_up)) @ w_down


def axial_rope_tables(n_tok):
    quarter = HEAD_DIM // 4
    inv = 1.0 / (ROPE_THETA ** (jnp.arange(quarter, dtype=jnp.float32) / quarter))
    pos = jnp.arange(n_tok, dtype=jnp.int32)
    row = (pos // GRID_W).astype(jnp.float32)
    col = (pos % GRID_W).astype(jnp.float32)
    ang = jnp.concatenate([row[:, None] * inv, col[:, None] * inv], axis=-1)
    return jnp.cos(ang), jnp.sin(ang)


def apply_axial_rope(t, cos, sin):
    quarter = HEAD_DIM // 4
    bshape = (1, t.shape[1]) + (1,) * (t.ndim - 3) + (2, quarter)
    cs = cos.reshape(bshape)
    sn = sin.reshape(bshape)
    tr = t.astype(jnp.float32).reshape(t.shape[:-1] + (2, 2, quarter))
    t1 = tr[..., 0, :]
    t2 = tr[..., 1, :]
    out = jnp.stack([t1 * cs - t2 * sn, t2 * cs + t1 * sn], axis=-2)
    return out.reshape(t.shape).astype(t.dtype)


def gla_chunked(q, k, v, g, s0):
    nb, n_tok, nh, _ = q.shape
    dv = v.shape[-1]
    n_chunk = n_tok // GLA_CHUNK

    def to_chunks(t):
        t = t.astype(jnp.float32).reshape(nb, n_chunk, GLA_CHUNK, nh, t.shape[-1])
        return t.transpose(1, 0, 3, 2, 4)

    qc, kc, vc, gc = to_chunks(q), to_chunks(k), to_chunks(v), to_chunks(g)
    bc = jnp.cumsum(gc, axis=-2)
    incl = jnp.tril(jnp.ones((GLA_CHUNK, GLA_CHUNK), dtype=bool))

    def step(s, inp):
        qi, ki, vi, bi = inp
        q_dec = qi * jnp.exp(bi)
        k_inv = ki * jnp.exp(-bi)
        a = jnp.where(incl, jnp.einsum('bhik,bhjk->bhij', q_dec, k_inv), 0.0)
        o = jnp.einsum('bhij,bhjv->bhiv', a, vi) + jnp.einsum('bhik,bhkv->bhiv', q_dec, s)
        b_last = bi[:, :, -1:, :]
        k_upd = ki * jnp.exp(b_last - bi)
        s = s * jnp.exp(b_last[:, :, 0, :, None]) + jnp.einsum('bhjk,bhjv->bhkv', k_upd, vi)
        return s, o

    s_fin, o = lax.scan(step, s0, (qc, kc, vc, bc))
    return o.transpose(1, 0, 3, 2, 4).reshape(nb, n_tok, nh, dv), s_fin


def bidir_gla(q, k, v, g_f, g_b, s_f0, s_b0):
    o_f, s_f = gla_chunked(q, k, v, g_f, s_f0)
    rev = lambda t: jnp.flip(t, axis=1)
    o_b, s_b = gla_chunked(rev(q), rev(k), rev(v), rev(g_b), s_b0)
    return o_f + rev(o_b), s_f, s_b


def gla_inputs(parts, up_f, bias_f, up_b, bias_b):
    q, k, v, r, dn_f, dn_b = parts
    q = heads(q, GLA_HEADS, GLA_DK) * GLA_DK ** -0.5
    k = heads(k, GLA_HEADS, GLA_DK)
    v = heads(v, GLA_HEADS, GLA_DV)
    r = heads(r, GLA_HEADS, GLA_DV)

    def gate(dn, up, bias):
        z = (dn @ up + bias).astype(jnp.float32)
        return heads(jax.nn.log_sigmoid(z) / GLA_NORMALIZER, GLA_HEADS, GLA_DK)

    return q, k, v, r, gate(dn_f, up_f, bias_f), gate(dn_b, up_b, bias_b)


def gla_output(o, r, norm_g, dtype):
    y = rms_norm(o, norm_g) * jax.nn.silu(r.astype(jnp.float32))
    return y.reshape(o.shape[:2] + (-1,)).astype(dtype)


def swa_sink_attention(q, k, v, kc, vc, sink):
    nb, n_tok, hq, d = q.shape
    grp = hq // SWA_KV_HEADS
    lc = kc.shape[1]
    n_blk = n_tok // SWA_BLOCK
    span = SWA_BLOCK + 2 * SWA_WINDOW
    pad = ((0, 0), (SWA_WINDOW, SWA_WINDOW), (0, 0), (0, 0))
    kp = jnp.pad(k, pad)
    vp = jnp.pad(v, pad)
    qg = q.reshape(nb, n_tok, SWA_KV_HEADS, grp, d) * d ** -0.5
    qi = jnp.arange(SWA_BLOCK)[:, None]
    kj = jnp.arange(span)[None, :]
    rel_ok = jnp.abs(qi + SWA_WINDOW - kj) <= SWA_WINDOW
    sink_g = sink.astype(jnp.float32).reshape(SWA_KV_HEADS, grp, 1, 1)

    def block(bi):
        start = bi * SWA_BLOCK
        qb = lax.dynamic_slice_in_dim(qg, start, SWA_BLOCK, axis=1)
        kb = lax.dynamic_slice_in_dim(kp, start, span, axis=1)
        vb = lax.dynamic_slice_in_dim(vp, start, span, axis=1)
        key_pos = start - SWA_WINDOW + kj
        ok = rel_ok & (key_pos >= 0) & (key_pos < n_tok)
        s_win = jnp.where(ok, jnp.einsum('bqhgd,bkhd->bhgqk', qb, kb).astype(jnp.float32), -jnp.inf)
        s_ctx = jnp.einsum('bqhgd,bchd->bhgqc', qb, kc).astype(jnp.float32)
        s_sink = jnp.broadcast_to(sink_g, s_ctx.shape[:-1] + (1,))
        p = jax.nn.softmax(jnp.concatenate([s_sink, s_ctx, s_win], axis=-1), axis=-1)
        p_ctx = p[..., 1:1 + lc].astype(v.dtype)
        p_win = p[..., 1 + lc:].astype(v.dtype)
        o = jnp.einsum('bhgqc,bchd->bqhgd', p_ctx, vc) + jnp.einsum('bhgqk,bkhd->bqhgd', p_win, vb)
        return o.reshape(nb, SWA_BLOCK, hq * d)

    out = lax.map(block, jnp.arange(n_blk))
    return out.transpose(1, 0, 2, 3).reshape(nb, n_tok, hq * d)


def ctx_sink_attention(qc, kc, vc, sink):
    nb, lc, hq, d = qc.shape
    grp = hq // SWA_KV_HEADS
    qg = qc.reshape(nb, lc, SWA_KV_HEADS, grp, d) * d ** -0.5
    s = jnp.einsum('bqhgd,bchd->bhgqc', qg, kc).astype(jnp.float32)
    s_sink = jnp.broadcast_to(sink.astype(jnp.float32).reshape(SWA_KV_HEADS, grp, 1, 1), s.shape[:-1] + (1,))
    p = jax.nn.softmax(jnp.concatenate([s_sink, s], axis=-1), axis=-1)[..., 1:]
    o = jnp.einsum('bhgqc,bchd->bqhgd', p.astype(vc.dtype), vc)
    return o.reshape(nb, lc, hq * d)


def neighborhood_attention(q, k, v, kc, vc, rpb):
    nb, n_tok, nh, d = q.shape
    lc = kc.shape[1]
    rows = n_tok // GRID_W
    wr = min(NA_ROWS, rows)
    n_cb = GRID_W // NA_COL_BLOCK
    kc0 = [min(max(m * NA_COL_BLOCK - NA_COLS // 2, 0), GRID_W - NA_KEY_COLS) for m in range(n_cb)]
    n_keys = wr * NA_KEY_COLS
    qcol = np.arange(GRID_W, dtype=np.int32).reshape(n_cb, NA_COL_BLOCK)
    col_start = np.clip(qcol - NA_COLS // 2, 0, GRID_W - NA_COLS)
    kcol = np.array(kc0, dtype=np.int32)[:, None] + np.arange(NA_KEY_COLS, dtype=np.int32)[None, :]
    kcol_flat = np.tile(kcol, (1, wr))
    krow_flat = np.repeat(np.arange(wr, dtype=np.int32), NA_KEY_COLS)
    col_ok = (kcol_flat[:, None, :] >= col_start[:, :, None]) & (kcol_flat[:, None, :] < col_start[:, :, None] + NA_COLS)
    dc_idx = np.clip(kcol_flat[:, None, :] - qcol[:, :, None] + NA_COLS - 1, 0, 2 * NA_COLS - 2)
    qg = (q * d ** -0.5).reshape(nb, rows, GRID_W, nh, d)
    kg = k.reshape(nb, rows, GRID_W, nh, d)
    vg = v.reshape(nb, rows, GRID_W, nh, d)

    def row_block(r):
        rs = jnp.clip(r - wr // 2, 0, rows - wr)
        k_rows = lax.dynamic_slice_in_dim(kg, rs, wr, axis=1)
        v_rows = lax.dynamic_slice_in_dim(vg, rs, wr, axis=1)
        kb = jnp.stack([k_rows[:, :, s:s + NA_KEY_COLS] for s in kc0], axis=1).reshape(nb, n_cb, n_keys, nh, d)
        vb = jnp.stack([v_rows[:, :, s:s + NA_KEY_COLS] for s in kc0], axis=1).reshape(nb, n_cb, n_keys, nh, d)
        qb = lax.dynamic_index_in_dim(qg, r, axis=1, keepdims=False).reshape(nb, n_cb, NA_COL_BLOCK, nh, d)
        dr = rs + krow_flat - r + (NA_ROWS - 1)
        bias = rpb[:, dr[None, None, :], dc_idx].astype(jnp.float32)
        s_nb = jnp.einsum('bmqhd,bmkhd->bhmqk', qb, kb).astype(jnp.float32)
        s_nb = jnp.where(col_ok, s_nb + bias, -jnp.inf)
        s_ctx = jnp.einsum('bmqhd,bchd->bhmqc', qb, kc).astype(jnp.float32)
        p = jax.nn.softmax(jnp.concatenate([s_ctx, s_nb], axis=-1), axis=-1)
        p_ctx = p[..., :lc].astype(v.dtype)
        p_nb = p[..., lc:].astype(v.dtype)
        o = jnp.einsum('bhmqc,bchd->bmqhd', p_ctx, vc) + jnp.einsum('bhmqk,bmkhd->bmqhd', p_nb, vb)
        return o.reshape(nb, GRID_W, nh * d)

    out = lax.map(row_block, jnp.arange(rows))
    return out.transpose(1, 0, 2, 3).reshape(nb, n_tok, nh * d)


def dense_ctx_attention(qc, kc, vc):
    nb, lc, nh, d = qc.shape
    s = jnp.einsum('bqhd,bkhd->bhqk', qc * d ** -0.5, kc).astype(jnp.float32)
    p = jax.nn.softmax(s, axis=-1)
    return jnp.einsum('bhqk,bkhd->bqhd', p.astype(vc.dtype), vc).reshape(nb, lc, nh * d)


def diff_attention(q, k_all, v_all, lam):
    nb, n_q, nh, _, d = q.shape
    n_blk = n_q // DIFF_BLOCK
    qs = q * d ** -0.5

    def block(bi):
        qb = lax.dynamic_slice_in_dim(qs, bi * DIFF_BLOCK, DIFF_BLOCK, axis=1)
        s = jnp.einsum('bqhsd,bkhsd->bhsqk', qb, k_all).astype(jnp.float32)
        p = jax.nn.softmax(s, axis=-1)
        w = p[:, :, 0] - lam * p[:, :, 1]
        return jnp.einsum('bhqk,bkhv->bqhv', w.astype(v_all.dtype), v_all)

    out = lax.map(block, jnp.arange(n_blk))
    return out.transpose(1, 0, 2, 3, 4).reshape(nb, n_q, nh, v_all.shape[-1])


def even_mixer(h, hc, cos, sin, w_in, up_f, bias_f, up_b, bias_b, gla_norm_g, sink, with_ctx):
    lat = split_cols(h @ w_in, EVEN_SPLITS)
    cx = split_cols(hc @ w_in, EVEN_SPLITS)
    nb = h.shape[0]
    qa, ka, va, ra, gfa, gba = gla_inputs(lat[:6], up_f, bias_f, up_b, bias_b)
    qac, kac, vac, rac, gfac, gbac = gla_inputs(cx[:6], up_f, bias_f, up_b, bias_b)
    zeros = jnp.zeros((nb, GLA_HEADS, GLA_DK, GLA_DV), jnp.float32)
    oac, s_f, s_b = bidir_gla(qac, kac, vac, gfac, gbac, zeros, zeros)
    oa, _, _ = bidir_gla(qa, ka, va, gfa, gba, s_f, s_b)
    ya = gla_output(oa, ra, gla_norm_g, h.dtype)
    qb = apply_axial_rope(heads(lat[6], SWA_HEADS, HEAD_DIM), cos, sin)
    kb = apply_axial_rope(heads(lat[7], SWA_KV_HEADS, HEAD_DIM), cos, sin)
    vb = heads(lat[8], SWA_KV_HEADS, HEAD_DIM)
    qbc = heads(cx[6], SWA_HEADS, HEAD_DIM)
    kbc = heads(cx[7], SWA_KV_HEADS, HEAD_DIM)
    vbc = heads(cx[8], SWA_KV_HEADS, HEAD_DIM)
    yb = swa_sink_attention(qb, kb, vb, kbc, vbc, sink)
    y = jnp.concatenate([ya, yb.astype(h.dtype)], axis=-1)
    yc = None
    if with_ctx:
        yac = gla_output(oac, rac, gla_norm_g, hc.dtype)
        ybc = ctx_sink_attention(qbc, kbc, vbc, sink)
        yc = jnp.concatenate([yac, ybc.astype(hc.dtype)], axis=-1)
    return y, yc


def odd_mixer(h, hc, cos, sin, w_in, rpb, lq1, lk1, lq2, lk2, diff_norm_g, layer_idx, with_ctx):
    lat = split_cols(h @ w_in, ODD_SPLITS)
    cx = split_cols(hc @ w_in, ODD_SPLITS)
    nb, n_tok = h.shape[:2]
    lc = hc.shape[1]
    qn, kn, vn = (heads(t, NA_HEADS, HEAD_DIM) for t in lat[:3])
    qnc, knc, vnc = (heads(t, NA_HEADS, HEAD_DIM) for t in cx[:3])
    yn = neighborhood_attention(qn, kn, vn, knc, vnc, rpb)
    lam_init = 0.8 - 0.6 * math.exp(-0.3 * layer_idx)
    lam = (jnp.exp(jnp.sum(lq1 * lk1).astype(jnp.float32)) - jnp.exp(jnp.sum(lq2 * lk2).astype(jnp.float32)) + lam_init)
    qd = apply_axial_rope(heads(lat[3], DIFF_HEADS, 2, HEAD_DIM), cos, sin)
    kd = apply_axial_rope(heads(lat[4], DIFF_HEADS, 2, HEAD_DIM), cos, sin)
    vd = heads(lat[5], DIFF_HEADS, DIFF_DV)
    qdc = heads(cx[3], DIFF_HEADS, 2, HEAD_DIM)
    kdc = heads(cx[4], DIFF_HEADS, 2, HEAD_DIM)
    vdc = heads(cx[5], DIFF_HEADS, DIFF_DV)
    k_all = jnp.concatenate([kdc, kd], axis=1)
    v_all = jnp.concatenate([vdc, vd], axis=1)
    od = diff_attention(qd, k_all, v_all, lam)
    yd = (rms_norm(od, diff_norm_g) * (1 - lam_init)).reshape(nb, n_tok, -1)
    y = jnp.concatenate([yn.astype(h.dtype), yd.astype(h.dtype)], axis=-1)
    yc = None
    if with_ctx:
        ync = dense_ctx_attention(qnc, knc, vnc)
        odc = diff_attention(qdc, kdc, vdc, lam)
        ydc = (rms_norm(odc, diff_norm_g) * (1 - lam_init)).reshape(nb, lc, -1)
        yc = jnp.concatenate([ync.astype(hc.dtype), ydc.astype(hc.dtype)], axis=-1)
    return y, yc


def setup_inputs(seed: int = 0) -> dict:
    key = jax.random.key(seed)
    ks = iter(jax.random.split(key, 32))
    f32 = jnp.float32
    D = D_MODEL
    F = FFN_HIDDEN
    n_even = (DEPTH + 1) // 2
    n_odd = DEPTH // 2

    def nrm(shape, scale):
        return jax.random.normal(next(ks), shape, f32) * scale

    return {
        'x': nrm((BATCH, SEQ, D), 1.0),
        'c': nrm((BATCH, D), 1.0),
        'ctx': nrm((BATCH, CTX_LEN, D), 1.0),
        'c_ctx': nrm((D,), 1.0),
        'ada_w': nrm((DEPTH, D, 6 * D), 0.5 * D ** -0.5),
        'ada_b': nrm((DEPTH, 6 * D), 0.02),
        'norm_mix_g': 1.0 + nrm((DEPTH, D), 0.05),
        'norm_ffn_g': 1.0 + nrm((DEPTH, D), 0.05),
        'w_out': nrm((DEPTH, D, D), D ** -0.5),
        'ffn_w_gate': nrm((DEPTH, D, F), D ** -0.5),
        'ffn_w_up': nrm((DEPTH, D, F), D ** -0.5),
        'ffn_w_down': nrm((DEPTH, F, D), F ** -0.5),
        'ev_w_in': nrm((n_even, D, EVEN_IN), D ** -0.5),
        'gla_gate_up_f': nrm((n_even, GLA_RANK, GLA_QK), GLA_RANK ** -0.5),
        'gla_gate_bias_f': nrm((n_even, GLA_QK), 0.1),
        'gla_gate_up_b': nrm((n_even, GLA_RANK, GLA_QK), GLA_RANK ** -0.5),
        'gla_gate_bias_b': nrm((n_even, GLA_QK), 0.1),
        'gla_norm_g': 1.0 + nrm((n_even, GLA_DV), 0.05),
        'swa_sink': nrm((n_even, SWA_HEADS), 0.5),
        'od_w_in': nrm((n_odd, D, ODD_IN), D ** -0.5),
        'na_rpb': nrm((n_odd, NA_HEADS, 2 * NA_ROWS - 1, 2 * NA_COLS - 1), 0.1),
        'diff_lq1': nrm((n_odd, HEAD_DIM), 0.1),
        'diff_lk1': nrm((n_odd, HEAD_DIM), 0.1),
        'diff_lq2': nrm((n_odd, HEAD_DIM), 0.1),
        'diff_lk2': nrm((n_odd, HEAD_DIM), 0.1),
        'diff_norm_g': 1.0 + nrm((n_odd, DIFF_DV), 0.05),
        'final_norm_g': 1.0 + nrm((D,), 0.05),
    }


def reference(x, c, ctx, c_ctx, ada_w, ada_b, norm_mix_g, norm_ffn_g, w_out, ffn_w_gate, ffn_w_up, ffn_w_down,
              ev_w_in, gla_gate_up_f, gla_gate_bias_f, gla_gate_up_b, gla_gate_bias_b, gla_norm_g, swa_sink,
              od_w_in, na_rpb, diff_lq1, diff_lk1, diff_lq2, diff_lk2, diff_norm_g, final_norm_g):
    cos, sin = axial_rope_tables(x.shape[1])
    silu_c = jax.nn.silu(c)
    silu_cc = jax.nn.silu(c_ctx)[None, :]
    for i in range(DEPTH):
        last = i == DEPTH - 1
        mod = (silu_c @ ada_w[i] + ada_b[i])[:, None, :]
        mod_c = (silu_cc @ ada_w[i] + ada_b[i])[:, None, :]
        sh1, sc1, g1, sh2, sc2, g2 = jnp.split(mod, 6, axis=-1)
        csh1, csc1, cg1, csh2, csc2, cg2 = jnp.split(mod_c, 6, axis=-1)
        h = modulate(rms_norm(x, norm_mix_g[i]), sh1, sc1)
        hc = modulate(rms_norm(ctx, norm_mix_g[i]), csh1, csc1)
        j = i // 2
        if i % 2 == 0:
            y, yc = even_mixer(h, hc, cos, sin, ev_w_in[j], gla_gate_up_f[j], gla_gate_bias_f[j],
                               gla_gate_up_b[j], gla_gate_bias_b[j], gla_norm_g[j], swa_sink[j], not last)
        else:
            y, yc = odd_mixer(h, hc, cos, sin, od_w_in[j], na_rpb[j], diff_lq1[j], diff_lk1[j],
                              diff_lq2[j], diff_lk2[j], diff_norm_g[j], i, not last)
        x = x + g1 * (y @ w_out[i])
        x = x + g2 * swiglu(modulate(rms_norm(x, norm_ffn_g[i]), sh2, sc2), ffn_w_gate[i], ffn_w_up[i], ffn_w_down[i])
        if not last:
            ctx = ctx + cg1 * (yc @ w_out[i])
            ctx = ctx + cg2 * swiglu(modulate(rms_norm(ctx, norm_ffn_g[i]), csh2, csc2), ffn_w_gate[i], ffn_w_up[i], ffn_w_down[i])
    return rms_norm(x, final_norm_g)
```

```python
import functools
import math

import numpy as np
import jax
import jax.numpy as jnp
from jax import lax
from jax.experimental import pallas as pl
from jax.experimental.pallas import tpu as pltpu

HEAD_DIM = 128
GRID_W = 64
ROPE_THETA = 10000.0
NORM_EPS = 1e-6
GLA_DK = 128
GLA_DV = 256
GLA_RANK = 16
GLA_NORMALIZER = 16.0
GLA_CHUNK = 64
SWA_GROUP = 4
SWA_WINDOW = 128
SWA_BLOCK = 128
NA_ROWS = 8
NA_COLS = 16
DIFF_DV = 2 * HEAD_DIM
CTX_BLOCK = 256

V7X_VMEM_BUDGET = 60 * 1024 * 1024
LANE = 128
ROW_TILE = 1024
COL_TILE = 512
FFN_PAD = 1024

F32 = jnp.float32
BF16 = jnp.bfloat16


def _params(sem, vmem_bytes):
    limit = int(min(V7X_VMEM_BUDGET, vmem_bytes * 5 // 4 + (4 << 20)))
    return pltpu.CompilerParams(dimension_semantics=sem, vmem_limit_bytes=limit)


def _dot(a, b):
    return jnp.dot(a, b, preferred_element_type=F32)


def _dot_nt(a, b):
    return lax.dot_general(a, b, (((1,), (1,)), ((), ())), preferred_element_type=F32)


def _dot_tn(a, b):
    return lax.dot_general(a, b, (((0,), (0,)), ((), ())), preferred_element_type=F32)


def _silu(t):
    return t * jax.nn.sigmoid(t)


def _mod_kernel(s_ref, w_ref, b_ref, o_ref):
    a = _silu(s_ref[...]).astype(BF16)
    o_ref[...] = _dot(a, w_ref[...].astype(BF16)) + b_ref[...]


def _mod_vectors(s8, ada_w, ada_b):
    depth, d, n = ada_w.shape
    tn = COL_TILE
    return pl.pallas_call(
        _mod_kernel,
        grid=(depth, n // tn),
        in_specs=[
            pl.BlockSpec((8, d), lambda l, j: (0, 0)),
            pl.BlockSpec((None, d, tn), lambda l, j: (l, 0, j)),
            pl.BlockSpec((None, 1, tn), lambda l, j: (l, 0, j)),
        ],
        out_specs=pl.BlockSpec((None, 8, tn), lambda l, j: (l, 0, j)),
        out_shape=jax.ShapeDtypeStruct((depth, 8, n), F32),
        compiler_params=_params(("parallel", "parallel"), 2 * d * tn * 4 + d * tn * 2),
        name="adaln_mod",
    )(s8, ada_w, ada_b.reshape(depth, 1, n))


def _norm_mod_kernel(x_ref, g_ref, sh_ref, sc_ref, o_ref):
    x = x_ref[...]
    r = lax.rsqrt(jnp.mean(x * x, axis=-1, keepdims=True) + NORM_EPS)
    y = x * r * g_ref[...]
    o_ref[...] = (y * (1.0 + sc_ref[...]) + sh_ref[...]).astype(o_ref.dtype)


def _norm_mod(x2, g, mod5, layer, k_shift, k_scale, mod_row, tm=256):
    m, d = x2.shape
    return pl.pallas_call(
        _norm_mod_kernel,
        grid=(m // tm,),
        in_specs=[
            pl.BlockSpec((tm, d), lambda i: (i, 0)),
            pl.BlockSpec((1, d), lambda i: (0, 0)),
            pl.BlockSpec((None, None, None, 1, d), lambda i: (layer, mod_row(i, tm), k_shift, 0, 0)),
            pl.BlockSpec((None, None, None, 1, d), lambda i: (layer, mod_row(i, tm), k_scale, 0, 0)),
        ],
        out_specs=pl.BlockSpec((tm, d), lambda i: (i, 0)),
        out_shape=jax.ShapeDtypeStruct((m, d), BF16),
        compiler_params=_params(("parallel",), 2 * tm * d * 6),
        name="norm_mod",
    )(x2, g.reshape(1, d), mod5, mod5)


def _final_norm_kernel(x_ref, g_ref, o_ref):
    x = x_ref[...]
    r = lax.rsqrt(jnp.mean(x * x, axis=-1, keepdims=True) + NORM_EPS)
    o_ref[...] = x * r * g_ref[...]


def _final_norm(x2, g, tm=256):
    m, d = x2.shape
    return pl.pallas_call(
        _final_norm_kernel,
        grid=(m // tm,),
        in_specs=[pl.BlockSpec((tm, d), lambda i: (i, 0)), pl.BlockSpec((1, d), lambda i: (0, 0))],
        out_specs=pl.BlockSpec((tm, d), lambda i: (i, 0)),
        out_shape=jax.ShapeDtypeStruct((m, d), F32),
        compiler_params=_params(("parallel",), 2 * tm * d * 8),
        name="final_norm",
    )(x2, g.reshape(1, d))


def _col_tile(n):
    return next(t for t in range(COL_TILE, 0, -LANE) if n % t == 0)


def _mm_kernel(a_ref, w_ref, o_ref):
    o_ref[...] = _dot(a_ref[...], w_ref[...]).astype(o_ref.dtype)


def _matmul(a, w, out_dtype=F32):
    m, k = a.shape
    n = w.shape[1]
    tm = min(ROW_TILE, m)
    tn = _col_tile(n)
    ob = jnp.dtype(out_dtype).itemsize
    return pl.pallas_call(
        _mm_kernel,
        grid=(m // tm, n // tn),
        in_specs=[pl.BlockSpec((tm, k), lambda i, j: (i, 0)), pl.BlockSpec((k, tn), lambda i, j: (0, j))],
        out_specs=pl.BlockSpec((tm, tn), lambda i, j: (i, j)),
        out_shape=jax.ShapeDtypeStruct((m, n), out_dtype),
        compiler_params=_params(("parallel", "parallel"), 2 * (tm * k * 2 + k * tn * 2 + tm * tn * ob)),
        name="matmul",
    )(a, w)


def _mm_swiglu_kernel(a_ref, wg_ref, wu_ref, o_ref):
    a = a_ref[...]
    g = _dot(a, wg_ref[...])
    u = _dot(a, wu_ref[...])
    o_ref[...] = (_silu(g) * u).astype(o_ref.dtype)


def _matmul_swiglu(a, wg, wu):
    m, k = a.shape
    n = wg.shape[1]
    tm = min(ROW_TILE, m)
    tn = COL_TILE
    return pl.pallas_call(
        _mm_swiglu_kernel,
        grid=(m // tm, n // tn),
        in_specs=[
            pl.BlockSpec((tm, k), lambda i, j: (i, 0)),
            pl.BlockSpec((k, tn), lambda i, j: (0, j)),
            pl.BlockSpec((k, tn), lambda i, j: (0, j)),
        ],
        out_specs=pl.BlockSpec((tm, tn), lambda i, j: (i, j)),
        out_shape=jax.ShapeDtypeStruct((m, n), BF16),
        compiler_params=_params(("parallel", "parallel"), 2 * (tm * k * 2 + 2 * k * tn * 2 + tm * tn * 2)),
        name="matmul_swiglu",
    )(a, wg, wu)


def _mm_res_kernel(a_ref, w_ref, res_ref, gate_ref, o_ref, acc_ref, *, nk):
    p = _dot(a_ref[...], w_ref[...])
    if nk == 1:
        o_ref[...] = res_ref[...] + gate_ref[...] * p
        return
    kk = pl.program_id(2)

    @pl.when(kk == 0)
    def _():
        acc_ref[...] = p

    @pl.when(jnp.logical_and(kk > 0, kk < nk - 1))
    def _():
        acc_ref[...] += p

    @pl.when(kk == nk - 1)
    def _():
        o_ref[...] = res_ref[...] + gate_ref[...] * (acc_ref[...] + p)


def _matmul_residual(a, w, res, mod5, layer, k_gate, mod_row, nk=1):
    m, k = a.shape
    n = w.shape[1]
    tm = min(ROW_TILE, m)
    tn = COL_TILE
    tk = k // nk
    return pl.pallas_call(
        functools.partial(_mm_res_kernel, nk=nk),
        grid=(m // tm, n // tn, nk),
        in_specs=[
            pl.BlockSpec((tm, tk), lambda i, j, q: (i, q)),
            pl.BlockSpec((tk, tn), lambda i, j, q: (q, j)),
            pl.BlockSpec((tm, tn), lambda i, j, q: (i, j)),
            pl.BlockSpec((None, None, None, 1, tn), lambda i, j, q: (layer, mod_row(i, tm), k_gate, 0, j)),
        ],
        out_specs=pl.BlockSpec((tm, tn), lambda i, j, q: (i, j)),
        out_shape=jax.ShapeDtypeStruct((m, n), F32),
        scratch_shapes=[pltpu.VMEM((tm, tn), F32)],
        compiler_params=_params(
            ("parallel", "parallel", "arbitrary"), 2 * (tm * tk * 2 + tk * tn * 2 + 2 * tm * tn * 4) + tm * tn * 4
        ),
        name="matmul_residual",
    )(a, w, res, mod5)


def _rope_tables(n_tok):
    quarter = HEAD_DIM // 4
    inv = 1.0 / (ROPE_THETA ** (jnp.arange(quarter, dtype=F32) / quarter))
    pos = jnp.arange(n_tok, dtype=jnp.int32)
    row = (pos // GRID_W).astype(F32)[:, None] * inv
    col = (pos % GRID_W).astype(F32)[:, None] * inv
    cos = jnp.concatenate([jnp.cos(row), jnp.cos(row), jnp.cos(col), jnp.cos(col)], axis=-1)
    sin = jnp.concatenate([-jnp.sin(row), jnp.sin(row), -jnp.sin(col), jnp.sin(col)], axis=-1)
    return cos, sin


def _rope_kernel(x_ref, cos_ref, sin_ref, o_ref, *, scale):
    x = x_ref[...]
    lane = lax.broadcasted_iota(jnp.int32, x.shape, 1)
    first = (lane & (HEAD_DIM // 4)) == 0
    partner = jnp.where(first, pltpu.roll(x, HEAD_DIM - HEAD_DIM // 4, 1), pltpu.roll(x, HEAD_DIM // 4, 1))
    o_ref[...] = ((x * cos_ref[...] + partner * sin_ref[...]) * scale).astype(o_ref.dtype)


def _rope(p, col0, width, cos, sin, scale, tr=512):
    m = p.shape[0]
    n_tok = cos.shape[0]
    tpb = n_tok // tr
    c0 = col0 // HEAD_DIM
    return pl.pallas_call(
        functools.partial(_rope_kernel, scale=scale),
        grid=(m // tr, width // HEAD_DIM),
        in_specs=[
            pl.BlockSpec((tr, HEAD_DIM), lambda i, j: (i, c0 + j)),
            pl.BlockSpec((tr, HEAD_DIM), lambda i, j: (i % tpb, 0)),
            pl.BlockSpec((tr, HEAD_DIM), lambda i, j: (i % tpb, 0)),
        ],
        out_specs=pl.BlockSpec((tr, HEAD_DIM), lambda i, j: (i, j)),
        out_shape=jax.ShapeDtypeStruct((m, width), BF16),
        compiler_params=_params(("parallel", "parallel"), 2 * tr * HEAD_DIM * 14),
        name="rope",
    )(p, cos, sin)


def _gla_kernel(ql, kl, vl, dl, qc, kc, vc, dc, up_ref, bias_ref, ol, oc, st_ref, *, nh, scale):
    d = pl.program_id(0)
    s = pl.program_id(2)
    is_ctx = s == 0
    fwd = d == 0
    n_chunk = CTX_BLOCK // GLA_CHUNK

    @pl.when(s == 0)
    def _():
        st_ref[...] = jnp.zeros_like(st_ref)

    row = lax.broadcasted_iota(jnp.int32, (GLA_CHUNK, GLA_CHUNK), 0)
    col = lax.broadcasted_iota(jnp.int32, (GLA_CHUNK, GLA_CHUNK), 1)
    incl = jnp.where(fwd, row - col, col - row) >= 0
    tmat = incl.astype(F32)
    up = up_ref[...]
    bias = bias_ref[...]

    def chunk(t, carry):
        c = jnp.where(fwd, t, n_chunk - 1 - t)
        rows = pl.ds(pl.multiple_of(c * GLA_CHUNK, GLA_CHUNK), GLA_CHUNK)

        def ld(rc, rl, cols):
            return jnp.where(is_ctx, rc[rows, cols], rl[rows, cols])

        z = _dot(ld(dc, dl, slice(None)).astype(BF16), up) + bias
        g = (jnp.minimum(z, 0.0) - jnp.log(1.0 + jnp.exp(-jnp.abs(z)))) * (1.0 / GLA_NORMALIZER)
        cum = jnp.dot(tmat, g, precision=lax.Precision.HIGHEST, preferred_element_type=F32)
        tot = jnp.sum(g, axis=0, keepdims=True)
        for h in range(nh):
            ks = slice(h * GLA_DK, (h + 1) * GLA_DK)
            vs = slice(h * GLA_DV, (h + 1) * GLA_DV)
            bh = cum[:, ks]
            eh = tot[:, ks]
            q = ld(qc, ql, ks)
            k = ld(kc, kl, ks)
            v = ld(vc, vl, vs).astype(BF16)
            qd = (q * (scale * jnp.exp(bh))).astype(BF16)
            ki = (k * jnp.exp(-bh)).astype(BF16)
            ku = (k * jnp.exp(eh - bh)).astype(BF16)
            a = jnp.where(incl, _dot_nt(qd, ki), 0.0)
            st = st_ref[h]
            o = _dot(a.astype(BF16), v) + _dot_nt(qd, st.astype(BF16))
            st_ref[h] = st * jnp.exp(eh) + _dot_tn(v, ku)

            @pl.when(is_ctx)
            def _():
                oc[rows, vs] = o

            @pl.when(jnp.logical_not(is_ctx))
            def _():
                ol[rows, vs] = o

        return carry

    lax.fori_loop(0, n_chunk, chunk, 0)


def _gla(p_lat, p_ctx, dn_lat, dn_ctx, up2, bias2, nb, n_tok, nh):
    qk = nh * GLA_DK
    dv = nh * GLA_DV
    tb = CTX_BLOCK
    nbl = n_tok // tb

    def lat_blk(d, b, s):
        t = jnp.maximum(s - 1, 0)
        return b * nbl + jnp.where(d == 0, t, nbl - 1 - t)

    lat = lambda w, c: pl.BlockSpec((tb, w), lambda d, b, s: (lat_blk(d, b, s), c))
    ctx = lambda w, c: pl.BlockSpec((tb, w), lambda d, b, s: (b, c))
    vm = 2 * (2 * (2 * tb * qk + tb * dv + tb * LANE) * 4 + 2 * tb * dv * 4) + nh * GLA_DV * GLA_DK * 4
    return pl.pallas_call(
        functools.partial(_gla_kernel, nh=nh, scale=GLA_DK ** -0.5),
        grid=(2, nb, nbl + 1),
        in_specs=[
            lat(qk, 0), lat(qk, 1), lat(dv, 1), lat(LANE, 0),
            ctx(qk, 0), ctx(qk, 1), ctx(dv, 1), ctx(LANE, 0),
            pl.BlockSpec((None, LANE, qk), lambda d, b, s: (d, 0, 0)),
            pl.BlockSpec((None, 1, qk), lambda d, b, s: (d, 0, 0)),
        ],
        out_specs=[
            pl.BlockSpec((None, tb, dv), lambda d, b, s: (d, lat_blk(d, b, s), 0)),
            pl.BlockSpec((None, tb, dv), lambda d, b, s: (d, b, 0)),
        ],
        out_shape=[
            jax.ShapeDtypeStruct((2, nb * n_tok, dv), F32),
            jax.ShapeDtypeStruct((2, nb * tb, dv), F32),
        ],
        scratch_shapes=[pltpu.VMEM((nh, GLA_DV, GLA_DK), F32)],
        compiler_params=_params(("parallel", "parallel", "arbitrary"), vm),
        name="gla_scan",
    )(p_lat, p_lat, p_lat, dn_lat, p_ctx, p_ctx, p_ctx, dn_ctx, up2, bias2)


def _gla_out_kernel(o_ref, r_ref, g_ref, y_ref, *, nh):
    for h in range(nh):
        sl = slice(h * GLA_DV, (h + 1) * GLA_DV)
        o = o_ref[0, :, sl] + o_ref[1, :, sl]
        y = o * lax.rsqrt(jnp.mean(o * o, axis=-1, keepdims=True) + NORM_EPS) * g_ref[...]
        y_ref[:, sl] = (y * _silu(r_ref[:, sl])).astype(y_ref.dtype)


def _gla_out(o2, p, norm_g, nh, tb=256):
    _, m, dv = o2.shape
    return pl.pallas_call(
        functools.partial(_gla_out_kernel, nh=nh),
        grid=(m // tb,),
        in_specs=[
            pl.BlockSpec((2, tb, dv), lambda i: (0, i, 0)),
            pl.BlockSpec((tb, dv), lambda i: (i, 2)),
            pl.BlockSpec((1, GLA_DV), lambda i: (0, 0)),
        ],
        out_specs=pl.BlockSpec((tb, dv), lambda i: (i, 0)),
        out_shape=jax.ShapeDtypeStruct((m, dv), BF16),
        compiler_params=_params(("parallel",), 2 * tb * dv * 14),
        name="gla_out",
    )(o2, p, norm_g.reshape(1, GLA_DV))


def _swa_kernel(sink_ref, q_ref, kp, kc, kn, vp, vc, vn, kx, vx, o_ref, *, n_blk):
    hkv = pl.program_id(1)
    i = pl.program_id(2)
    kwin = jnp.concatenate([kp[...], kc[...], kn[...]], axis=0)
    vwin = jnp.concatenate([vp[...], vc[...], vn[...]], axis=0).astype(BF16)
    kctx = kx[...].astype(BF16)
    vctx = vx[...].astype(BF16)
    span = SWA_BLOCK + 2 * SWA_WINDOW
    qi = lax.broadcasted_iota(jnp.int32, (SWA_BLOCK, span), 0)
    kj = lax.broadcasted_iota(jnp.int32, (SWA_BLOCK, span), 1)
    key_pos = (i - 1) * SWA_BLOCK + kj
    ok = jnp.logical_and(jnp.abs(qi + SWA_WINDOW - kj) <= SWA_WINDOW,
                         jnp.logical_and(key_pos >= 0, key_pos < n_blk * SWA_BLOCK))
    for g in range(SWA_GROUP):
        hs = slice(g * HEAD_DIM, (g + 1) * HEAD_DIM)
        q = q_ref[:, hs]
        s_win = jnp.where(ok, _dot_nt(q, kwin), -jnp.inf)
        s_ctx = _dot_nt(q, kctx)
        sink = sink_ref[0, hkv * SWA_GROUP + g]
        m = jnp.maximum(jnp.maximum(jnp.max(s_win, axis=-1, keepdims=True),
                                    jnp.max(s_ctx, axis=-1, keepdims=True)), sink)
        p_win = jnp.exp(s_win - m)
        p_ctx = jnp.exp(s_ctx - m)
        den = (jnp.sum(p_win, axis=-1, keepdims=True) + jnp.sum(p_ctx, axis=-1, keepdims=True)
               + jnp.exp(sink - m))
        o = _dot(p_ctx.astype(BF16), vctx) + _dot(p_win.astype(BF16), vwin)
        o_ref[:, hs] = (o / den).astype(o_ref.dtype)


def _swa(sink, q_rope, k_rope, p_lat, p_ctx, kcol, vcol, nb, n_tok, n_kv):
    n_blk = n_tok // SWA_BLOCK
    tb = SWA_BLOCK
    gw = SWA_GROUP * HEAD_DIM
    kc0 = kcol // HEAD_DIM
    vc0 = vcol // HEAD_DIM
    prev = lambda b, h, i: b * n_blk + jnp.maximum(i - 1, 0)
    cur = lambda b, h, i: b * n_blk + i
    nxt = lambda b, h, i: b * n_blk + jnp.minimum(i + 1, n_blk - 1)
    kspec = lambda f: pl.BlockSpec((tb, HEAD_DIM), lambda b, h, i: (f(b, h, i), h))
    vspec = lambda f: pl.BlockSpec((tb, HEAD_DIM), lambda b, h, i: (f(b, h, i), vc0 + h))
    return pl.pallas_call(
        functools.partial(_swa_kernel, n_blk=n_blk),
        grid=(nb, n_kv, n_blk),
        in_specs=[
            pl.BlockSpec(memory_space=pltpu.SMEM),
            pl.BlockSpec((tb, gw), lambda b, h, i: (cur(b, h, i), h)),
            kspec(prev), kspec(cur), kspec(nxt),
            vspec(prev), vspec(cur), vspec(nxt),
            pl.BlockSpec((CTX_BLOCK, HEAD_DIM), lambda b, h, i: (b, kc0 + h)),
            pl.BlockSpec((CTX_BLOCK, HEAD_DIM), lambda b, h, i: (b, vc0 + h)),
        ],
        out_specs=pl.BlockSpec((tb, gw), lambda b, h, i: (cur(b, h, i), h)),
        out_shape=jax.ShapeDtypeStruct((nb * n_tok, n_kv * gw), BF16),
        compiler_params=_params(("parallel", "parallel", "parallel"), 8 << 20),
        name="swa",
    )(sink, q_rope, k_rope, k_rope, k_rope, p_lat, p_lat, p_lat, p_ctx, p_ctx)


def _ctx_sink_kernel(sink_ref, q_ref, kx, vx, o_ref, *, scale):
    hkv = pl.program_id(1)
    kctx = kx[...].astype(BF16)
    vctx = vx[...].astype(BF16)
    for g in range(SWA_GROUP):
        hs = slice(g * HEAD_DIM, (g + 1) * HEAD_DIM)
        q = (q_ref[:, hs] * scale).astype(BF16)
        s = _dot_nt(q, kctx)
        sink = sink_ref[0, hkv * SWA_GROUP + g]
        m = jnp.maximum(jnp.max(s, axis=-1, keepdims=True), sink)
        p = jnp.exp(s - m)
        den = jnp.sum(p, axis=-1, keepdims=True) + jnp.exp(sink - m)
        o_ref[:, hs] = (_dot(p.astype(BF16), vctx) / den).astype(o_ref.dtype)


def _ctx_sink(sink, p_ctx, qcol, kcol, vcol, nb, n_kv):
    gw = SWA_GROUP * HEAD_DIM
    return pl.pallas_call(
        functools.partial(_ctx_sink_kernel, scale=HEAD_DIM ** -0.5),
        grid=(nb, n_kv),
        in_specs=[
            pl.BlockSpec(memory_space=pltpu.SMEM),
            pl.BlockSpec((CTX_BLOCK, gw), lambda b, h: (b, qcol // gw + h)),
            pl.BlockSpec((CTX_BLOCK, HEAD_DIM), lambda b, h: (b, kcol // HEAD_DIM + h)),
            pl.BlockSpec((CTX_BLOCK, HEAD_DIM), lambda b, h: (b, vcol // HEAD_DIM + h)),
        ],
        out_specs=pl.BlockSpec((CTX_BLOCK, gw), lambda b, h: (b, h)),
        out_shape=jax.ShapeDtypeStruct((nb * CTX_BLOCK, n_kv * gw), BF16),
        compiler_params=_params(("parallel", "parallel"), 4 << 20),
        name="ctx_sink_attention",
    )(sink, p_ctx, p_ctx, p_ctx)


def _na_bias_table(rpb):
    qcol = np.arange(GRID_W)[:, None]
    kcol = np.arange(GRID_W)[None, :]
    col_start = np.clip(qcol - NA_COLS // 2, 0, GRID_W - NA_COLS)
    col_ok = (kcol >= col_start) & (kcol < col_start + NA_COLS)
    dc = np.clip(kcol - qcol + NA_COLS - 1, 0, 2 * NA_COLS - 2)
    delta = np.arange(NA_ROWS)[:, None]
    krow = np.arange(NA_ROWS)[None, :]
    dr = krow - delta + NA_ROWS - 1
    tab = rpb[:, dr[:, None, :, None], dc[None, :, None, :]]
    tab = jnp.where(col_ok[None, None, :, None, :], tab.astype(F32), -jnp.inf)
    return tab.reshape(rpb.shape[0], NA_ROWS, GRID_W, NA_ROWS * GRID_W)


def _na_kernel(q_ref, k_ref, v_ref, kx, vx, bias_ref, o_ref, *, nh, scale):
    for h in range(nh):
        hs = slice(h * HEAD_DIM, (h + 1) * HEAD_DIM)
        q = (q_ref[:, hs] * scale).astype(BF16)
        s_nb = _dot_nt(q, k_ref[:, hs].astype(BF16)) + bias_ref[h]
        s_ctx = _dot_nt(q, kx[:, hs].astype(BF16))
        m = jnp.maximum(jnp.max(s_nb, axis=-1, keepdims=True), jnp.max(s_ctx, axis=-1, keepdims=True))
        p_nb = jnp.exp(s_nb - m)
        p_ctx = jnp.exp(s_ctx - m)
        den = jnp.sum(p_nb, axis=-1, keepdims=True) + jnp.sum(p_ctx, axis=-1, keepdims=True)
        o = _dot(p_ctx.astype(BF16), vx[:, hs].astype(BF16)) + _dot(p_nb.astype(BF16), v_ref[:, hs].astype(BF16))
        o_ref[:, hs] = (o / den).astype(o_ref.dtype)


def _na(p_lat, p_ctx, bias_tab, nb, n_tok, nh):
    rows = n_tok // GRID_W
    w = nh * HEAD_DIM
    win = NA_ROWS * GRID_W
    rs = lambda r: jnp.clip(r - NA_ROWS // 2, 0, rows - NA_ROWS)
    kv = lambda c: pl.BlockSpec((pl.Element(win), pl.Element(w)),
                                lambda b, r: ((b * rows + rs(r)) * GRID_W, c * w))
    vm = 2 * (GRID_W * w * 4 + 2 * win * w * 4 + 2 * CTX_BLOCK * w * 4 + nh * GRID_W * win * 4 + GRID_W * w * 2)
    return pl.pallas_call(
        functools.partial(_na_kernel, nh=nh, scale=HEAD_DIM ** -0.5),
        grid=(nb, rows),
        in_specs=[
            pl.BlockSpec((GRID_W, w), lambda b, r: (b * rows + r, 0)),
            kv(1), kv(2),
            pl.BlockSpec((CTX_BLOCK, w), lambda b, r: (b, 1)),
            pl.BlockSpec((CTX_BLOCK, w), lambda b, r: (b, 2)),
            pl.BlockSpec((nh, None, GRID_W, win), lambda b, r: (0, r - rs(r), 0, 0)),
        ],
        out_specs=pl.BlockSpec((GRID_W, w), lambda b, r: (b * rows + r, 0)),
        out_shape=jax.ShapeDtypeStruct((nb * n_tok, w), BF16),
        compiler_params=_params(("parallel", "arbitrary"), vm),
        name="neighborhood_attention",
    )(p_lat, p_lat, p_lat, p_ctx, p_ctx, bias_tab)


def _diff_kernel(q_ref, k_ref, v_ref, kx, vx, lq1, lk1, lq2, lk2, g_ref, o_ref, m_ref, l_ref, acc_ref,
                 *, nk, lam_init):
    j = pl.program_id(3)

    @pl.when(j == 0)
    def _():
        m_ref[...] = jnp.full_like(m_ref, -jnp.inf)
        l_ref[...] = jnp.zeros_like(l_ref)
        acc_ref[...] = jnp.zeros_like(acc_ref)

    def step(keys, v):
        for t in range(2):
            q = q_ref[:, t * HEAD_DIM:(t + 1) * HEAD_DIM]
            s = _dot_nt(q, keys[t])
            m_old = m_ref[t]
            m_new = jnp.maximum(m_old, jnp.max(s, axis=-1, keepdims=True))
            alpha = jnp.exp(m_old - m_new)
            p = jnp.exp(s - m_new)
            l_ref[t] = alpha * l_ref[t] + jnp.sum(p, axis=-1, keepdims=True)
            acc_ref[t] = alpha * acc_ref[t] + _dot(p.astype(BF16), v)
            m_ref[t] = m_new

    @pl.when(j == 0)
    def _():
        step((kx[:, :HEAD_DIM].astype(BF16), kx[:, HEAD_DIM:].astype(BF16)), vx[...].astype(BF16))

    @pl.when(j > 0)
    def _():
        step((k_ref[:, :HEAD_DIM], k_ref[:, HEAD_DIM:]), v_ref[...].astype(BF16))

    @pl.when(j == nk - 1)
    def _():
        lam = (jnp.exp(jnp.sum(lq1[...] * lk1[...], axis=-1, keepdims=True))
               - jnp.exp(jnp.sum(lq2[...] * lk2[...], axis=-1, keepdims=True)) + lam_init)
        o = acc_ref[0] / l_ref[0] - lam * (acc_ref[1] / l_ref[1])
        y = o * lax.rsqrt(jnp.mean(o * o, axis=-1, keepdims=True) + NORM_EPS) * g_ref[...]
        o_ref[...] = (y * (1.0 - lam_init)).astype(o_ref.dtype)


def _diff(q_rope, k_rope, p_lat, p_ctx, kcol, vcol, lq1, lk1, lq2, lk2, norm_g, lam_init, nb, n_tok, nh,
          tq=512, tk=512):
    nq = n_tok // tq
    nkb = n_tok // tk
    w = 2 * HEAD_DIM
    vec = lambda: pl.BlockSpec((1, HEAD_DIM), lambda b, h, i, j: (0, 0))
    vm = 2 * (tq * w * 2 + tk * w * 2 + tk * w * 4 + 2 * CTX_BLOCK * w * 4 + tq * w * 2) + 3 * tq * w * 4 + 6 * tq * tk * 4
    return pl.pallas_call(
        functools.partial(_diff_kernel, nk=nkb + 1, lam_init=lam_init),
        grid=(nb, nh, nq, nkb + 1),
        in_specs=[
            pl.BlockSpec((tq, w), lambda b, h, i, j: (b * nq + i, h)),
            pl.BlockSpec((tk, w), lambda b, h, i, j: (b * nkb + jnp.maximum(j - 1, 0), h)),
            pl.BlockSpec((tk, w), lambda b, h, i, j: (b * nkb + jnp.maximum(j - 1, 0), vcol // w + h)),
            pl.BlockSpec((CTX_BLOCK, w), lambda b, h, i, j: (b, kcol // w + h)),
            pl.BlockSpec((CTX_BLOCK, w), lambda b, h, i, j: (b, vcol // w + h)),
            vec(), vec(), vec(), vec(),
            pl.BlockSpec((1, DIFF_DV), lambda b, h, i, j: (0, 0)),
        ],
        out_specs=pl.BlockSpec((tq, w), lambda b, h, i, j: (b * nq + i, h)),
        out_shape=jax.ShapeDtypeStruct((nb * n_tok, nh * w), BF16),
        scratch_shapes=[
            pltpu.VMEM((2, tq, 1), F32),
            pltpu.VMEM((2, tq, 1), F32),
            pltpu.VMEM((2, tq, DIFF_DV), F32),
        ],
        compiler_params=_params(("parallel", "parallel", "parallel", "arbitrary"), vm),
        name="diff_attention",
    )(q_rope, k_rope, p_lat, p_ctx, p_ctx, lq1.reshape(1, -1), lk1.reshape(1, -1), lq2.reshape(1, -1),
      lk2.reshape(1, -1), norm_g.reshape(1, -1))


def _pad_cols(w, n):
    return jnp.pad(w, ((0, 0), (0, n - w.shape[1])))


def kernel(x, c, ctx, c_ctx, ada_w, ada_b, norm_mix_g, norm_ffn_g, w_out, ffn_w_gate, ffn_w_up, ffn_w_down,
           ev_w_in, gla_gate_up_f, gla_gate_bias_f, gla_gate_up_b, gla_gate_bias_b, gla_norm_g, swa_sink,
           od_w_in, na_rpb, diff_lq1, diff_lk1, diff_lq2, diff_lk2, diff_norm_g, final_norm_g):
    nb, n_tok, d = x.shape
    lc = ctx.shape[1]
    depth = ada_w.shape[0]
    f_hidden = ffn_w_gate.shape[2]
    half = d // 2
    assert lc == CTX_BLOCK and nb < 8 and depth == 2, "layout assumes a 256-token context, batch < 8, two layers"
    gla_heads = half // GLA_DV
    gla_qk = gla_heads * GLA_DK
    n_q = half // HEAD_DIM
    n_kv = n_q // SWA_GROUP
    diff_heads = half // (2 * HEAD_DIM)

    xs = x.reshape(nb * n_tok, d)
    cs = ctx.reshape(nb * lc, d)
    cos, sin = _rope_tables(n_tok)

    s8 = jnp.zeros((8, d), F32).at[:nb].set(c).at[nb].set(c_ctx)
    mod5 = _mod_vectors(s8, ada_w, ada_b).reshape(depth, 8, 6, 1, d)
    lat_row = lambda i, tm: (i * tm) // n_tok
    ctx_row = lambda i, tm: nb

    f_pad = -(-f_hidden // FFN_PAD) * FFN_PAD

    def ffn(stream, layer, mod_row):
        wg = _pad_cols(ffn_w_gate[layer], f_pad).astype(BF16)
        wu = _pad_cols(ffn_w_up[layer], f_pad).astype(BF16)
        wd = jnp.pad(ffn_w_down[layer], ((0, f_pad - f_hidden), (0, 0))).astype(BF16)
        h2 = _norm_mod(stream, norm_ffn_g[layer], mod5, layer, 3, 4, mod_row)
        a = _matmul_swiglu(h2, wg, wu)
        return _matmul_residual(a, wd, stream, mod5, layer, 5, mod_row, nk=4)

    layer = 0
    w_in = ev_w_in[0]
    g0 = 2 * gla_qk + 2 * half
    w_main = jnp.concatenate([w_in[:, :g0], w_in[:, g0 + 2 * GLA_RANK:]], axis=1).astype(BF16)
    w_dn = _pad_cols(w_in[:, g0:g0 + 2 * GLA_RANK], LANE).astype(BF16)
    swa_q0 = g0
    swa_k0 = swa_q0 + half
    swa_v0 = swa_k0 + n_kv * HEAD_DIM
    up2 = jnp.zeros((2, LANE, gla_qk), F32)
    up2 = up2.at[0, :GLA_RANK].set(gla_gate_up_f[0]).at[1, GLA_RANK:2 * GLA_RANK].set(gla_gate_up_b[0]).astype(BF16)
    bias2 = jnp.stack([gla_gate_bias_f[0], gla_gate_bias_b[0]]).reshape(2, 1, gla_qk)

    h = _norm_mod(xs, norm_mix_g[layer], mod5, layer, 0, 1, lat_row)
    hc = _norm_mod(cs, norm_mix_g[layer], mod5, layer, 0, 1, ctx_row)
    p_lat = _matmul(h, w_main)
    p_ctx = _matmul(hc, w_main)
    dn_lat = _matmul(h, w_dn)
    dn_ctx = _matmul(hc, w_dn)

    o_lat, o_ctx = _gla(p_lat, p_ctx, dn_lat, dn_ctx, up2, bias2, nb, n_tok, gla_heads)
    ya = _gla_out(o_lat, p_lat, gla_norm_g[0], gla_heads)
    yac = _gla_out(o_ctx, p_ctx, gla_norm_g[0], gla_heads)

    sink = swa_sink[0].reshape(1, n_q)
    q_rope = _rope(p_lat, swa_q0, half, cos, sin, HEAD_DIM ** -0.5)
    k_rope = _rope(p_lat, swa_k0, n_kv * HEAD_DIM, cos, sin, 1.0)
    yb = _swa(sink, q_rope, k_rope, p_lat, p_ctx, swa_k0, swa_v0, nb, n_tok, n_kv)
    ybc = _ctx_sink(sink, p_ctx, swa_q0, swa_k0, swa_v0, nb, n_kv)

    wo = w_out[layer].astype(BF16)
    xs = _matmul_residual(jnp.concatenate([ya, yb], axis=1), wo, xs, mod5, layer, 2, lat_row)
    cs = _matmul_residual(jnp.concatenate([yac, ybc], axis=1), wo, cs, mod5, layer, 2, ctx_row)
    xs = ffn(xs, layer, lat_row)
    cs = ffn(cs, layer, ctx_row)

    layer = 1
    w_in = od_w_in[0].astype(BF16)
    h = _norm_mod(xs, norm_mix_g[layer], mod5, layer, 0, 1, lat_row)
    hc = _norm_mod(cs, norm_mix_g[layer], mod5, layer, 0, 1, ctx_row)
    p_lat = _matmul(h, w_in)
    p_ctx = _matmul(hc, w_in)

    yn = _na(p_lat, p_ctx, _na_bias_table(na_rpb[0]), nb, n_tok, n_q)

    dq0 = 3 * half
    dk0 = dq0 + half
    dv0 = dk0 + half
    lam_init = 0.8 - 0.6 * math.exp(-0.3 * layer)
    qd = _rope(p_lat, dq0, half, cos, sin, HEAD_DIM ** -0.5)
    kd = _rope(p_lat, dk0, half, cos, sin, 1.0)
    yd = _diff(qd, kd, p_lat, p_ctx, dk0, dv0, diff_lq1[0], diff_lk1[0], diff_lq2[0], diff_lk2[0],
               diff_norm_g[0], lam_init, nb, n_tok, diff_heads)

    xs = _matmul_residual(jnp.concatenate([yn, yd], axis=1), w_out[layer].astype(BF16), xs, mod5, layer, 2, lat_row)
    xs = ffn(xs, layer, lat_row)
    return _final_norm(xs, final_norm_g).reshape(nb, n_tok, d)
```

```python
import functools
import math

import numpy as np
import jax
import jax.numpy as jnp
from jax import lax
from jax.experimental import pallas as pl
from jax.experimental.pallas import tpu as pltpu

HEAD_DIM = 128
GRID_W = 64
ROPE_THETA = 10000.0
NORM_EPS = 1e-6
GLA_DK = 128
GLA_DV = 256
GLA_RANK = 16
GLA_NORMALIZER = 16.0
GLA_CHUNK = 64
SWA_GROUP = 4
SWA_WINDOW = 128
SWA_BLOCK = 128
NA_ROWS = 8
NA_COLS = 16
DIFF_DV = 2 * HEAD_DIM
DIFF_KEY_BLOCK = 512
CTX_BLOCK = 256

V7X_VMEM_BUDGET = 60 * 1024 * 1024
LANE = 128
ROW_TILE = 1024
COL_TILE = 512
FFN_PAD = 1024

F32 = jnp.float32
BF16 = jnp.bfloat16


def _params(sem, vmem_bytes):
    limit = int(min(V7X_VMEM_BUDGET, vmem_bytes * 5 // 4 + (4 << 20)))
    return pltpu.CompilerParams(dimension_semantics=sem, vmem_limit_bytes=limit)


def _dot(a, b):
    return jnp.dot(a, b, preferred_element_type=F32)


def _dot_nt(a, b):
    return lax.dot_general(a, b, (((1,), (1,)), ((), ())), preferred_element_type=F32)


def _dot_tn(a, b):
    return lax.dot_general(a, b, (((0,), (0,)), ((), ())), preferred_element_type=F32)


def _silu(t):
    return t * jax.nn.sigmoid(t)


def _mod_kernel(s_ref, w_ref, b_ref, o_ref):
    a = _silu(s_ref[...]).astype(BF16)
    o_ref[...] = _dot(a, w_ref[...].astype(BF16)) + b_ref[...]


def _mod_vectors(s8, ada_w, ada_b):
    depth, d, n = ada_w.shape
    tn = COL_TILE
    return pl.pallas_call(
        _mod_kernel,
        grid=(depth, n // tn),
        in_specs=[
            pl.BlockSpec((8, d), lambda l, j: (0, 0)),
            pl.BlockSpec((None, d, tn), lambda l, j: (l, 0, j)),
            pl.BlockSpec((None, 1, tn), lambda l, j: (l, 0, j)),
        ],
        out_specs=pl.BlockSpec((None, 8, tn), lambda l, j: (l, 0, j)),
        out_shape=jax.ShapeDtypeStruct((depth, 8, n), F32),
        compiler_params=_params(("parallel", "parallel"), 2 * d * tn * 4 + d * tn * 2),
        name="adaln_mod",
    )(s8, ada_w, ada_b.reshape(depth, 1, n))


def _norm_mod_kernel(x_ref, g_ref, sh_ref, sc_ref, o_ref):
    x = x_ref[...]
    r = lax.rsqrt(jnp.mean(x * x, axis=-1, keepdims=True) + NORM_EPS)
    y = x * r * g_ref[...]
    o_ref[...] = (y * (1.0 + sc_ref[...]) + sh_ref[...]).astype(o_ref.dtype)


def _norm_mod(x2, g, mod5, layer, k_shift, k_scale, mod_row, tm=256):
    m, d = x2.shape
    return pl.pallas_call(
        _norm_mod_kernel,
        grid=(m // tm,),
        in_specs=[
            pl.BlockSpec((tm, d), lambda i: (i, 0)),
            pl.BlockSpec((1, d), lambda i: (0, 0)),
            pl.BlockSpec((None, None, None, 1, d), lambda i: (layer, mod_row(i, tm), k_shift, 0, 0)),
            pl.BlockSpec((None, None, None, 1, d), lambda i: (layer, mod_row(i, tm), k_scale, 0, 0)),
        ],
        out_specs=pl.BlockSpec((tm, d), lambda i: (i, 0)),
        out_shape=jax.ShapeDtypeStruct((m, d), BF16),
        compiler_params=_params(("parallel",), 2 * tm * d * 6),
        name="norm_mod",
    )(x2, g.reshape(1, d), mod5, mod5)


def _final_norm_kernel(x_ref, g_ref, o_ref):
    x = x_ref[...]
    r = lax.rsqrt(jnp.mean(x * x, axis=-1, keepdims=True) + NORM_EPS)
    o_ref[...] = x * r * g_ref[...]


def _final_norm(x2, g, tm=256):
    m, d = x2.shape
    return pl.pallas_call(
        _final_norm_kernel,
        grid=(m // tm,),
        in_specs=[pl.BlockSpec((tm, d), lambda i: (i, 0)), pl.BlockSpec((1, d), lambda i: (0, 0))],
        out_specs=pl.BlockSpec((tm, d), lambda i: (i, 0)),
        out_shape=jax.ShapeDtypeStruct((m, d), F32),
        compiler_params=_params(("parallel",), 2 * tm * d * 8),
        name="final_norm",
    )(x2, g.reshape(1, d))


def _col_tile(n):
    return next(t for t in range(COL_TILE, 0, -LANE) if n % t == 0)


def _mm_kernel(a_ref, w_ref, o_ref):
    o_ref[...] = _dot(a_ref[...], w_ref[...]).astype(o_ref.dtype)


def _matmul(a, w, out_dtype=F32):
    m, k = a.shape
    n = w.shape[1]
    tm = min(ROW_TILE, m)
    tn = _col_tile(n)
    ob = jnp.dtype(out_dtype).itemsize
    return pl.pallas_call(
        _mm_kernel,
        grid=(m // tm, n // tn),
        in_specs=[pl.BlockSpec((tm, k), lambda i, j: (i, 0)), pl.BlockSpec((k, tn), lambda i, j: (0, j))],
        out_specs=pl.BlockSpec((tm, tn), lambda i, j: (i, j)),
        out_shape=jax.ShapeDtypeStruct((m, n), out_dtype),
        compiler_params=_params(("parallel", "parallel"), 2 * (tm * k * 2 + k * tn * 2 + tm * tn * ob)),
        name="matmul",
    )(a, w)


def _mm_swiglu_kernel(a_ref, wg_ref, wu_ref, o_ref):
    a = a_ref[...]
    g = _dot(a, wg_ref[...])
    u = _dot(a, wu_ref[...])
    o_ref[...] = (_silu(g) * u).astype(o_ref.dtype)


def _matmul_swiglu(a, wg, wu):
    m, k = a.shape
    n = wg.shape[1]
    tm = min(ROW_TILE, m)
    tn = COL_TILE
    return pl.pallas_call(
        _mm_swiglu_kernel,
        grid=(m // tm, n // tn),
        in_specs=[
            pl.BlockSpec((tm, k), lambda i, j: (i, 0)),
            pl.BlockSpec((k, tn), lambda i, j: (0, j)),
            pl.BlockSpec((k, tn), lambda i, j: (0, j)),
        ],
        out_specs=pl.BlockSpec((tm, tn), lambda i, j: (i, j)),
        out_shape=jax.ShapeDtypeStruct((m, n), BF16),
        compiler_params=_params(("parallel", "parallel"), 2 * (tm * k * 2 + 2 * k * tn * 2 + tm * tn * 2)),
        name="matmul_swiglu",
    )(a, wg, wu)


def _mm_res_kernel(a_ref, w_ref, res_ref, gate_ref, o_ref, *scratch, nk):
    p = _dot(a_ref[...], w_ref[...])
    if nk == 1:
        o_ref[...] = res_ref[...] + gate_ref[...] * p
        return
    acc_ref, = scratch
    kk = pl.program_id(2)

    @pl.when(kk == 0)
    def _():
        acc_ref[...] = p

    @pl.when(jnp.logical_and(kk > 0, kk < nk - 1))
    def _():
        acc_ref[...] += p

    @pl.when(kk == nk - 1)
    def _():
        o_ref[...] = res_ref[...] + gate_ref[...] * (acc_ref[...] + p)


def _matmul_residual(a, w, res, mod5, layer, k_gate, mod_row, nk=1, tn=COL_TILE):
    m, k = a.shape
    n = w.shape[1]
    tm = min(ROW_TILE, m)
    tk = k // nk
    return pl.pallas_call(
        functools.partial(_mm_res_kernel, nk=nk),
        grid=(m // tm, n // tn, nk),
        in_specs=[
            pl.BlockSpec((tm, tk), lambda i, j, q: (i, q)),
            pl.BlockSpec((tk, tn), lambda i, j, q: (q, j)),
            pl.BlockSpec((tm, tn), lambda i, j, q: (i, j)),
            pl.BlockSpec((None, None, None, 1, tn), lambda i, j, q: (layer, mod_row(i, tm), k_gate, 0, j)),
        ],
        out_specs=pl.BlockSpec((tm, tn), lambda i, j, q: (i, j)),
        out_shape=jax.ShapeDtypeStruct((m, n), F32),
        scratch_shapes=[pltpu.VMEM((tm, tn), F32)] if nk > 1 else [],
        compiler_params=_params(
            ("parallel", "parallel", "arbitrary"), 2 * (tm * tk * 2 + tk * tn * 2 + 2 * tm * tn * 4) + tm * tn * 4
        ),
        name="matmul_residual",
    )(a, w, res, mod5)


def _rope_tables(n_tok):
    quarter = HEAD_DIM // 4
    inv = 1.0 / (ROPE_THETA ** (jnp.arange(quarter, dtype=F32) / quarter))
    pos = jnp.arange(n_tok, dtype=jnp.int32)
    row = (pos // GRID_W).astype(F32)[:, None] * inv
    col = (pos % GRID_W).astype(F32)[:, None] * inv
    cos = jnp.concatenate([jnp.cos(row), jnp.cos(row), jnp.cos(col), jnp.cos(col)], axis=-1)
    sin = jnp.concatenate([-jnp.sin(row), jnp.sin(row), -jnp.sin(col), jnp.sin(col)], axis=-1)
    return cos, sin


def _rope_kernel(x_ref, cos_ref, sin_ref, o_ref, *, scale, n_heads):
    cos = cos_ref[...] * scale
    sin = sin_ref[...] * scale
    lane = lax.broadcasted_iota(jnp.int32, cos.shape, 1)
    first = (lane & (HEAD_DIM // 4)) == 0
    for h in range(n_heads):
        hs = slice(h * HEAD_DIM, (h + 1) * HEAD_DIM)
        x = x_ref[:, hs]
        partner = jnp.where(first, pltpu.roll(x, HEAD_DIM - HEAD_DIM // 4, 1), pltpu.roll(x, HEAD_DIM // 4, 1))
        o_ref[:, hs] = (x * cos + partner * sin).astype(o_ref.dtype)


def _rope(p, col0, width, cos, sin, scale, tr=256):
    m = p.shape[0]
    n_tok = cos.shape[0]
    tpb = n_tok // tr
    assert col0 % width == 0
    return pl.pallas_call(
        functools.partial(_rope_kernel, scale=scale, n_heads=width // HEAD_DIM),
        grid=(m // tr,),
        in_specs=[
            pl.BlockSpec((tr, width), lambda i: (i, col0 // width)),
            pl.BlockSpec((tr, HEAD_DIM), lambda i: (i % tpb, 0)),
            pl.BlockSpec((tr, HEAD_DIM), lambda i: (i % tpb, 0)),
        ],
        out_specs=pl.BlockSpec((tr, width), lambda i: (i, 0)),
        out_shape=jax.ShapeDtypeStruct((m, width), BF16),
        compiler_params=_params(("parallel",), 2 * tr * (width * 6 + HEAD_DIM * 8)),
        name="rope",
    )(p, cos, sin)


def _gla_kernel(ql, kl, vl, dl, qc, kc, vc, dc, up_ref, bias_ref, ol, oc, st_ref, *, nh, scale):
    d = pl.program_id(0)
    s = pl.program_id(2)
    is_ctx = s == 0
    fwd = d == 0
    n_chunk = CTX_BLOCK // GLA_CHUNK

    @pl.when(s == 0)
    def _():
        st_ref[...] = jnp.zeros_like(st_ref)

    row = lax.broadcasted_iota(jnp.int32, (GLA_CHUNK, GLA_CHUNK), 0)
    col = lax.broadcasted_iota(jnp.int32, (GLA_CHUNK, GLA_CHUNK), 1)
    incl = jnp.where(fwd, row - col, col - row) >= 0
    tmat = incl.astype(F32)
    up = up_ref[...]
    bias = bias_ref[...]

    def chunk(t, carry):
        c = jnp.where(fwd, t, n_chunk - 1 - t)
        rows = pl.ds(pl.multiple_of(c * GLA_CHUNK, GLA_CHUNK), GLA_CHUNK)

        def ld(rc, rl, cols):
            return jnp.where(is_ctx, rc[rows, cols], rl[rows, cols])

        z = _dot(ld(dc, dl, slice(None)).astype(BF16), up) + bias
        g = (jnp.minimum(z, 0.0) - jnp.log(1.0 + jnp.exp(-jnp.abs(z)))) * (1.0 / GLA_NORMALIZER)
        cum = jnp.dot(tmat, g, precision=lax.Precision.HIGHEST, preferred_element_type=F32)
        tot = jnp.sum(g, axis=0, keepdims=True)
        for h in range(nh):
            ks = slice(h * GLA_DK, (h + 1) * GLA_DK)
            vs = slice(h * GLA_DV, (h + 1) * GLA_DV)
            bh = cum[:, ks]
            eh = tot[:, ks]
            q = ld(qc, ql, ks)
            k = ld(kc, kl, ks)
            v = ld(vc, vl, vs).astype(BF16)
            qd = (q * (scale * jnp.exp(bh))).astype(BF16)
            ki = (k * jnp.exp(-bh)).astype(BF16)
            ku = (k * jnp.exp(eh - bh)).astype(BF16)
            a = jnp.where(incl, _dot_nt(qd, ki), 0.0)
            st = st_ref[h]
            o = _dot(a.astype(BF16), v) + _dot_nt(qd, st.astype(BF16))
            st_ref[h] = st * jnp.exp(eh) + _dot_tn(v, ku)

            @pl.when(is_ctx)
            def _():
                oc[rows, vs] = o

            @pl.when(jnp.logical_not(is_ctx))
            def _():
                ol[rows, vs] = o

        return carry

    lax.fori_loop(0, n_chunk, chunk, 0)


def _gla(p_lat, p_ctx, dn_lat, dn_ctx, up2, bias2, nb, n_tok, nh):
    qk = nh * GLA_DK
    dv = nh * GLA_DV
    tb = CTX_BLOCK
    nbl = n_tok // tb

    def lat_blk(d, b, s):
        t = jnp.maximum(s - 1, 0)
        return b * nbl + jnp.where(d == 0, t, nbl - 1 - t)

    lat = lambda w, c: pl.BlockSpec((tb, w), lambda d, b, s: (lat_blk(d, b, s), c))
    ctx = lambda w, c: pl.BlockSpec((tb, w), lambda d, b, s: (b, c))
    vm = 2 * (2 * (2 * tb * qk + tb * dv + tb * LANE) * 4 + 2 * tb * dv * 4) + nh * GLA_DV * GLA_DK * 4
    return pl.pallas_call(
        functools.partial(_gla_kernel, nh=nh, scale=GLA_DK ** -0.5),
        grid=(2, nb, nbl + 1),
        in_specs=[
            lat(qk, 0), lat(qk, 1), lat(dv, 1), lat(LANE, 0),
            ctx(qk, 0), ctx(qk, 1), ctx(dv, 1), ctx(LANE, 0),
            pl.BlockSpec((None, LANE, qk), lambda d, b, s: (d, 0, 0)),
            pl.BlockSpec((None, 1, qk), lambda d, b, s: (d, 0, 0)),
        ],
        out_specs=[
            pl.BlockSpec((None, tb, dv), lambda d, b, s: (d, lat_blk(d, b, s), 0)),
            pl.BlockSpec((None, tb, dv), lambda d, b, s: (d, b, 0)),
        ],
        out_shape=[
            jax.ShapeDtypeStruct((2, nb * n_tok, dv), F32),
            jax.ShapeDtypeStruct((2, nb * tb, dv), F32),
        ],
        scratch_shapes=[pltpu.VMEM((nh, GLA_DV, GLA_DK), F32)],
        compiler_params=_params(("parallel", "parallel", "arbitrary"), vm),
        name="gla_scan",
    )(p_lat, p_lat, p_lat, dn_lat, p_ctx, p_ctx, p_ctx, dn_ctx, up2, bias2)


def _gla_out_kernel(o_ref, r_ref, g_ref, y_ref, *, nh):
    for h in range(nh):
        sl = slice(h * GLA_DV, (h + 1) * GLA_DV)
        o = o_ref[0, :, sl] + o_ref[1, :, sl]
        y = o * lax.rsqrt(jnp.mean(o * o, axis=-1, keepdims=True) + NORM_EPS) * g_ref[...]
        y_ref[:, sl] = (y * _silu(r_ref[:, sl])).astype(y_ref.dtype)


def _gla_out(o2, p, norm_g, nh, tb=256):
    _, m, dv = o2.shape
    return pl.pallas_call(
        functools.partial(_gla_out_kernel, nh=nh),
        grid=(m // tb,),
        in_specs=[
            pl.BlockSpec((2, tb, dv), lambda i: (0, i, 0)),
            pl.BlockSpec((tb, dv), lambda i: (i, 2)),
            pl.BlockSpec((1, GLA_DV), lambda i: (0, 0)),
        ],
        out_specs=pl.BlockSpec((tb, dv), lambda i: (i, 0)),
        out_shape=jax.ShapeDtypeStruct((m, dv), BF16),
        compiler_params=_params(("parallel",), 2 * tb * dv * 14),
        name="gla_out",
    )(o2, p, norm_g.reshape(1, GLA_DV))


def _swa_kernel(sink_ref, q_ref, kp, kc, kn, vp, vc, vn, kx, vx, o_ref, *, n_blk):
    hkv = pl.program_id(1)
    i = pl.program_id(2)
    kwin = jnp.concatenate([kp[...], kc[...], kn[...]], axis=0)
    vwin = jnp.concatenate([vp[...], vc[...], vn[...]], axis=0).astype(BF16)
    kctx = kx[...].astype(BF16)
    vctx = vx[...].astype(BF16)
    span = SWA_BLOCK + 2 * SWA_WINDOW
    qi = lax.broadcasted_iota(jnp.int32, (SWA_BLOCK, span), 0)
    kj = lax.broadcasted_iota(jnp.int32, (SWA_BLOCK, span), 1)
    key_pos = (i - 1) * SWA_BLOCK + kj
    ok = jnp.logical_and(jnp.abs(qi + SWA_WINDOW - kj) <= SWA_WINDOW,
                         jnp.logical_and(key_pos >= 0, key_pos < n_blk * SWA_BLOCK))
    for g in range(SWA_GROUP):
        hs = slice(g * HEAD_DIM, (g + 1) * HEAD_DIM)
        q = q_ref[:, hs]
        s_win = jnp.where(ok, _dot_nt(q, kwin), -jnp.inf)
        s_ctx = _dot_nt(q, kctx)
        sink = sink_ref[0, hkv * SWA_GROUP + g]
        m = jnp.maximum(jnp.maximum(jnp.max(s_win, axis=-1, keepdims=True),
                                    jnp.max(s_ctx, axis=-1, keepdims=True)), sink)
        p_win = jnp.exp(s_win - m)
        p_ctx = jnp.exp(s_ctx - m)
        den = (jnp.sum(p_win, axis=-1, keepdims=True) + jnp.sum(p_ctx, axis=-1, keepdims=True)
               + jnp.exp(sink - m))
        o = _dot(p_ctx.astype(BF16), vctx) + _dot(p_win.astype(BF16), vwin)
        o_ref[:, hs] = (o / den).astype(o_ref.dtype)


def _swa(sink, q_rope, k_rope, p_lat, p_ctx, kcol, vcol, nb, n_tok, n_kv):
    n_blk = n_tok // SWA_BLOCK
    tb = SWA_BLOCK
    gw = SWA_GROUP * HEAD_DIM
    kc0 = kcol // HEAD_DIM
    vc0 = vcol // HEAD_DIM
    prev = lambda b, h, i: b * n_blk + jnp.maximum(i - 1, 0)
    cur = lambda b, h, i: b * n_blk + i
    nxt = lambda b, h, i: b * n_blk + jnp.minimum(i + 1, n_blk - 1)
    kspec = lambda f: pl.BlockSpec((tb, HEAD_DIM), lambda b, h, i: (f(b, h, i), h))
    vspec = lambda f: pl.BlockSpec((tb, HEAD_DIM), lambda b, h, i: (f(b, h, i), vc0 + h))
    return pl.pallas_call(
        functools.partial(_swa_kernel, n_blk=n_blk),
        grid=(nb, n_kv, n_blk),
        in_specs=[
            pl.BlockSpec(memory_space=pltpu.SMEM),
            pl.BlockSpec((tb, gw), lambda b, h, i: (cur(b, h, i), h)),
            kspec(prev), kspec(cur), kspec(nxt),
            vspec(prev), vspec(cur), vspec(nxt),
            pl.BlockSpec((CTX_BLOCK, HEAD_DIM), lambda b, h, i: (b, kc0 + h)),
            pl.BlockSpec((CTX_BLOCK, HEAD_DIM), lambda b, h, i: (b, vc0 + h)),
        ],
        out_specs=pl.BlockSpec((tb, gw), lambda b, h, i: (cur(b, h, i), h)),
        out_shape=jax.ShapeDtypeStruct((nb * n_tok, n_kv * gw), BF16),
        compiler_params=_params(("parallel", "parallel", "parallel"), 8 << 20),
        name="swa",
    )(sink, q_rope, k_rope, k_rope, k_rope, p_lat, p_lat, p_lat, p_ctx, p_ctx)


def _ctx_sink_kernel(sink_ref, q_ref, kx, vx, o_ref, *, scale):
    hkv = pl.program_id(1)
    kctx = kx[...].astype(BF16)
    vctx = vx[...].astype(BF16)
    for g in range(SWA_GROUP):
        hs = slice(g * HEAD_DIM, (g + 1) * HEAD_DIM)
        q = (q_ref[:, hs] * scale).astype(BF16)
        s = _dot_nt(q, kctx)
        sink = sink_ref[0, hkv * SWA_GROUP + g]
        m = jnp.maximum(jnp.max(s, axis=-1, keepdims=True), sink)
        p = jnp.exp(s - m)
        den = jnp.sum(p, axis=-1, keepdims=True) + jnp.exp(sink - m)
        o_ref[:, hs] = (_dot(p.astype(BF16), vctx) / den).astype(o_ref.dtype)


def _ctx_sink(sink, p_ctx, qcol, kcol, vcol, nb, n_kv):
    gw = SWA_GROUP * HEAD_DIM
    return pl.pallas_call(
        functools.partial(_ctx_sink_kernel, scale=HEAD_DIM ** -0.5),
        grid=(nb, n_kv),
        in_specs=[
            pl.BlockSpec(memory_space=pltpu.SMEM),
            pl.BlockSpec((CTX_BLOCK, gw), lambda b, h: (b, qcol // gw + h)),
            pl.BlockSpec((CTX_BLOCK, HEAD_DIM), lambda b, h: (b, kcol // HEAD_DIM + h)),
            pl.BlockSpec((CTX_BLOCK, HEAD_DIM), lambda b, h: (b, vcol // HEAD_DIM + h)),
        ],
        out_specs=pl.BlockSpec((CTX_BLOCK, gw), lambda b, h: (b, h)),
        out_shape=jax.ShapeDtypeStruct((nb * CTX_BLOCK, n_kv * gw), BF16),
        compiler_params=_params(("parallel", "parallel"), 4 << 20),
        name="ctx_sink_attention",
    )(sink, p_ctx, p_ctx, p_ctx)


def _na_bias_table(rpb):
    qcol = np.arange(GRID_W)[:, None]
    kcol = np.arange(GRID_W)[None, :]
    col_start = np.clip(qcol - NA_COLS // 2, 0, GRID_W - NA_COLS)
    col_ok = (kcol >= col_start) & (kcol < col_start + NA_COLS)
    dc = np.clip(kcol - qcol + NA_COLS - 1, 0, 2 * NA_COLS - 2)
    n_dc = 2 * NA_COLS - 1
    onehot = (dc[None] == np.arange(n_dc)[:, None, None]).astype(np.float32)
    cols = jnp.einsum('hrd,dqk->hrqk', rpb.astype(F32), onehot, precision=lax.Precision.HIGHEST)
    cols = jnp.where(col_ok[None, None], cols, -jnp.inf)
    tab = jnp.stack([cols[:, NA_ROWS - 1 - dl:2 * NA_ROWS - 1 - dl] for dl in range(NA_ROWS)], axis=1)
    tab = tab.transpose(0, 1, 3, 2, 4)
    return tab.reshape(rpb.shape[0], NA_ROWS, GRID_W, NA_ROWS * GRID_W)


def _na_kernel(q_ref, k_ref, v_ref, kx, vx, bias_ref, o_ref, *, nh, scale):
    for h in range(nh):
        hs = slice(h * HEAD_DIM, (h + 1) * HEAD_DIM)
        q = (q_ref[:, hs] * scale).astype(BF16)
        s_nb = _dot_nt(q, k_ref[:, hs].astype(BF16)) + bias_ref[h]
        s_ctx = _dot_nt(q, kx[:, hs].astype(BF16))
        m = jnp.maximum(jnp.max(s_nb, axis=-1, keepdims=True), jnp.max(s_ctx, axis=-1, keepdims=True))
        p_nb = jnp.exp(s_nb - m)
        p_ctx = jnp.exp(s_ctx - m)
        den = jnp.sum(p_nb, axis=-1, keepdims=True) + jnp.sum(p_ctx, axis=-1, keepdims=True)
        o = _dot(p_ctx.astype(BF16), vx[:, hs].astype(BF16)) + _dot(p_nb.astype(BF16), v_ref[:, hs].astype(BF16))
        o_ref[:, hs] = (o / den).astype(o_ref.dtype)


def _na(p_lat, p_ctx, bias_tab, nb, n_tok, nh):
    rows = n_tok // GRID_W
    w = nh * HEAD_DIM
    win = NA_ROWS * GRID_W
    rs = lambda r: jnp.clip(r - NA_ROWS // 2, 0, rows - NA_ROWS)
    kv = lambda c: pl.BlockSpec((pl.Element(win), pl.Element(w)),
                                lambda b, r: ((b * rows + rs(r)) * GRID_W, c * w))
    vm = 2 * (GRID_W * w * 4 + 2 * win * w * 4 + 2 * CTX_BLOCK * w * 4 + nh * GRID_W * win * 4 + GRID_W * w * 2)
    return pl.pallas_call(
        functools.partial(_na_kernel, nh=nh, scale=HEAD_DIM ** -0.5),
        grid=(nb, rows),
        in_specs=[
            pl.BlockSpec((GRID_W, w), lambda b, r: (b * rows + r, 0)),
            kv(1), kv(2),
            pl.BlockSpec((CTX_BLOCK, w), lambda b, r: (b, 1)),
            pl.BlockSpec((CTX_BLOCK, w), lambda b, r: (b, 2)),
            pl.BlockSpec((nh, None, GRID_W, win), lambda b, r: (0, r - rs(r), 0, 0)),
        ],
        out_specs=pl.BlockSpec((GRID_W, w), lambda b, r: (b * rows + r, 0)),
        out_shape=jax.ShapeDtypeStruct((nb * n_tok, w), BF16),
        compiler_params=_params(("parallel", "arbitrary"), vm),
        name="neighborhood_attention",
    )(p_lat, p_lat, p_lat, p_ctx, p_ctx, bias_tab)


def _diff_kernel(q_ref, k_ref, vt_ref, kx, vxt, lq1, lk1, lq2, lk2, g_ref, o_ref, m_ref, l_ref, acc_ref,
                 *, n_blk, tk, lam_init):
    m_ref[...] = jnp.full_like(m_ref, -jnp.inf)
    l_ref[...] = jnp.zeros_like(l_ref)
    acc_ref[...] = jnp.zeros_like(acc_ref)

    def block(keys, vt):
        for t in range(2):
            hs = slice(t * HEAD_DIM, (t + 1) * HEAD_DIM)
            s = _dot_nt(keys[:, hs], q_ref[:, hs])
            m_old = m_ref[t]
            m_new = jnp.maximum(m_old, jnp.max(s, axis=0, keepdims=True))
            alpha = jnp.exp2(m_old - m_new)
            p = jnp.exp2(s - m_new)
            l_ref[t] = alpha * l_ref[t] + jnp.sum(p, axis=0, keepdims=True)
            acc_ref[t] = alpha * acc_ref[t] + _dot(vt, p.astype(BF16))
            m_ref[t] = m_new

    block(kx[...].astype(BF16), vxt[...])

    def body(j, carry):
        block(k_ref[pl.ds(pl.multiple_of(j * tk, tk), tk), :], vt_ref[j])
        return carry

    lax.fori_loop(0, n_blk, body, 0)

    lam = (jnp.exp(jnp.sum(lq1[...] * lk1[...], axis=-1, keepdims=True))
           - jnp.exp(jnp.sum(lq2[...] * lk2[...], axis=-1, keepdims=True)) + lam_init)
    o = acc_ref[0] / l_ref[0] - lam * (acc_ref[1] / l_ref[1])
    y = o * lax.rsqrt(jnp.mean(o * o, axis=0, keepdims=True) + NORM_EPS) * (g_ref[...] * (1.0 - lam_init))
    o_ref[...] = y.T.astype(o_ref.dtype)


def _diff(q_rope, k_rope, vt_lat, p_ctx, vt_ctx, kcol, lq1, lk1, lq2, lk2, norm_g, lam_init, nb, n_tok, nh,
          tq=512):
    nq = n_tok // tq
    n_blk, tk = vt_lat.shape[2], vt_lat.shape[4]
    w = 2 * HEAD_DIM
    vec = lambda: pl.BlockSpec((1, HEAD_DIM), lambda b, h, i: (0, 0))
    vm = (2 * (tq * w * 2 + 2 * n_tok * w * 2 + CTX_BLOCK * w * 6 + tq * w * 2)
          + 2 * tq * w * 4 + 2 * 2 * tk * tq * 6 + tq * w * 8)
    return pl.pallas_call(
        functools.partial(_diff_kernel, n_blk=n_blk, tk=tk, lam_init=lam_init),
        grid=(nb, nh, nq),
        in_specs=[
            pl.BlockSpec((tq, w), lambda b, h, i: (b * nq + i, h)),
            pl.BlockSpec((n_tok, w), lambda b, h, i: (b, h)),
            pl.BlockSpec((None, None, n_blk, w, tk), lambda b, h, i: (b, h, 0, 0, 0)),
            pl.BlockSpec((CTX_BLOCK, w), lambda b, h, i: (b, kcol // w + h)),
            pl.BlockSpec((None, None, w, CTX_BLOCK), lambda b, h, i: (b, h, 0, 0)),
            vec(), vec(), vec(), vec(),
            pl.BlockSpec((DIFF_DV, 1), lambda b, h, i: (0, 0)),
        ],
        out_specs=pl.BlockSpec((tq, w), lambda b, h, i: (b * nq + i, h)),
        out_shape=jax.ShapeDtypeStruct((nb * n_tok, nh * w), BF16),
        scratch_shapes=[
            pltpu.VMEM((2, 1, tq), F32),
            pltpu.VMEM((2, 1, tq), F32),
            pltpu.VMEM((2, DIFF_DV, tq), F32),
        ],
        compiler_params=_params(("parallel", "parallel", "arbitrary"), vm),
        name="diff_attention",
    )(q_rope, k_rope, vt_lat, p_ctx, vt_ctx, lq1.reshape(1, -1), lk1.reshape(1, -1), lq2.reshape(1, -1),
      lk2.reshape(1, -1), norm_g.reshape(-1, 1))


def _pad_cols(w, n):
    return jnp.pad(w, ((0, 0), (0, n - w.shape[1])))


def kernel(x, c, ctx, c_ctx, ada_w, ada_b, norm_mix_g, norm_ffn_g, w_out, ffn_w_gate, ffn_w_up, ffn_w_down,
           ev_w_in, gla_gate_up_f, gla_gate_bias_f, gla_gate_up_b, gla_gate_bias_b, gla_norm_g, swa_sink,
           od_w_in, na_rpb, diff_lq1, diff_lk1, diff_lq2, diff_lk2, diff_norm_g, final_norm_g):
    nb, n_tok, d = x.shape
    lc = ctx.shape[1]
    depth = ada_w.shape[0]
    f_hidden = ffn_w_gate.shape[2]
    half = d // 2
    assert lc == CTX_BLOCK and nb < 8 and depth == 2, "layout assumes a 256-token context, batch < 8, two layers"
    gla_heads = half // GLA_DV
    gla_qk = gla_heads * GLA_DK
    n_q = half // HEAD_DIM
    n_kv = n_q // SWA_GROUP
    diff_heads = half // (2 * HEAD_DIM)

    xs = x.reshape(nb * n_tok, d)
    cs = ctx.reshape(nb * lc, d)
    cos, sin = _rope_tables(n_tok)

    s8 = jnp.zeros((8, d), F32).at[:nb].set(c).at[nb].set(c_ctx)
    mod5 = _mod_vectors(s8, ada_w, ada_b).reshape(depth, 8, 6, 1, d)
    lat_row = lambda i, tm: (i * tm) // n_tok
    ctx_row = lambda i, tm: nb

    f_pad = -(-f_hidden // FFN_PAD) * FFN_PAD

    def ffn(stream, layer, mod_row):
        wg = _pad_cols(ffn_w_gate[layer], f_pad).astype(BF16)
        wu = _pad_cols(ffn_w_up[layer], f_pad).astype(BF16)
        wd = jnp.pad(ffn_w_down[layer], ((0, f_pad - f_hidden), (0, 0))).astype(BF16)
        h2 = _norm_mod(stream, norm_ffn_g[layer], mod5, layer, 3, 4, mod_row)
        a = _matmul_swiglu(h2, wg, wu)
        return _matmul_residual(a, wd, stream, mod5, layer, 5, mod_row, nk=4, tn=2 * COL_TILE)

    layer = 0
    w_in = ev_w_in[0]
    g0 = 2 * gla_qk + 2 * half
    w_main = jnp.concatenate([w_in[:, :g0], w_in[:, g0 + 2 * GLA_RANK:]], axis=1).astype(BF16)
    w_dn = _pad_cols(w_in[:, g0:g0 + 2 * GLA_RANK], LANE).astype(BF16)
    swa_q0 = g0
    swa_k0 = swa_q0 + half
    swa_v0 = swa_k0 + n_kv * HEAD_DIM
    up2 = jnp.zeros((2, LANE, gla_qk), F32)
    up2 = up2.at[0, :GLA_RANK].set(gla_gate_up_f[0]).at[1, GLA_RANK:2 * GLA_RANK].set(gla_gate_up_b[0]).astype(BF16)
    bias2 = jnp.stack([gla_gate_bias_f[0], gla_gate_bias_b[0]]).reshape(2, 1, gla_qk)

    h = _norm_mod(xs, norm_mix_g[layer], mod5, layer, 0, 1, lat_row)
    hc = _norm_mod(cs, norm_mix_g[layer], mod5, layer, 0, 1, ctx_row)
    p_lat = _matmul(h, w_main)
    p_ctx = _matmul(hc, w_main)
    dn_lat = _matmul(h, w_dn)
    dn_ctx = _matmul(hc, w_dn)

    o_lat, o_ctx = _gla(p_lat, p_ctx, dn_lat, dn_ctx, up2, bias2, nb, n_tok, gla_heads)
    ya = _gla_out(o_lat, p_lat, gla_norm_g[0], gla_heads)
    yac = _gla_out(o_ctx, p_ctx, gla_norm_g[0], gla_heads)

    sink = swa_sink[0].reshape(1, n_q)
    q_rope = _rope(p_lat, swa_q0, half, cos, sin, HEAD_DIM ** -0.5)
    k_rope = _rope(p_lat, swa_k0, n_kv * HEAD_DIM, cos, sin, 1.0)
    yb = _swa(sink, q_rope, k_rope, p_lat, p_ctx, swa_k0, swa_v0, nb, n_tok, n_kv)
    ybc = _ctx_sink(sink, p_ctx, swa_q0, swa_k0, swa_v0, nb, n_kv)

    wo = w_out[layer].astype(BF16)
    xs = _matmul_residual(jnp.concatenate([ya, yb], axis=1), wo, xs, mod5, layer, 2, lat_row)
    cs = _matmul_residual(jnp.concatenate([yac, ybc], axis=1), wo, cs, mod5, layer, 2, ctx_row)
    xs = ffn(xs, layer, lat_row)
    cs = ffn(cs, layer, ctx_row)

    layer = 1
    w_in = od_w_in[0].astype(BF16)
    h = _norm_mod(xs, norm_mix_g[layer], mod5, layer, 0, 1, lat_row)
    hc = _norm_mod(cs, norm_mix_g[layer], mod5, layer, 0, 1, ctx_row)
    p_lat = _matmul(h, w_in)
    p_ctx = _matmul(hc, w_in)

    yn = _na(p_lat, p_ctx, _na_bias_table(na_rpb[0]), nb, n_tok, n_q)

    dq0 = 3 * half
    dk0 = dq0 + half
    dv0 = dk0 + half
    lam_init = 0.8 - 0.6 * math.exp(-0.3 * layer)
    qd = _rope(p_lat, dq0, half, cos, sin, HEAD_DIM ** -0.5 * math.log2(math.e))
    kd = _rope(p_lat, dk0, half, cos, sin, 1.0)
    tk = DIFF_KEY_BLOCK
    vt_lat = p_lat[:, dv0:].astype(BF16).reshape(nb, n_tok // tk, tk, diff_heads, DIFF_DV).transpose(0, 3, 1, 4, 2)
    vt_ctx = p_ctx[:, dv0:].astype(BF16).reshape(nb, lc, diff_heads, DIFF_DV).transpose(0, 2, 3, 1)
    yd = _diff(qd, kd, vt_lat, p_ctx, vt_ctx, dk0, diff_lq1[0], diff_lk1[0], diff_lq2[0], diff_lk2[0],
               diff_norm_g[0], lam_init, nb, n_tok, diff_heads)

    xs = _matmul_residual(jnp.concatenate([yn, yd], axis=1), w_out[layer].astype(BF16), xs, mod5, layer, 2, lat_row)
    xs = ffn(xs, layer, lat_row)
    return _final_norm(xs, final_norm_g).reshape(nb, n_tok, d)
```

```python
import functools
import math

import numpy as np
import jax
import jax.numpy as jnp
from jax import lax
from jax.experimental import pallas as pl
from jax.experimental.pallas import tpu as pltpu

HEAD_DIM = 128
GRID_W = 64
ROPE_THETA = 10000.0
NORM_EPS = 1e-6
GLA_DK = 128
GLA_DV = 256
GLA_RANK = 16
GLA_NORMALIZER = 16.0
GLA_CHUNK = 64
SWA_GROUP = 4
SWA_WINDOW = 128
SWA_BLOCK = 128
NA_ROWS = 8
NA_COLS = 16
DIFF_DV = 2 * HEAD_DIM
DIFF_KEY_BLOCK = 512
CTX_BLOCK = 256

V7X_VMEM_BUDGET = 60 * 1024 * 1024
LANE = 128
ROW_TILE = 1024
COL_TILE = 512
FFN_PAD = 1024

F32 = jnp.float32
BF16 = jnp.bfloat16


def _params(sem, vmem_bytes):
    limit = int(min(V7X_VMEM_BUDGET, vmem_bytes * 5 // 4 + (4 << 20)))
    return pltpu.CompilerParams(dimension_semantics=sem, vmem_limit_bytes=limit)


def _dot(a, b):
    return jnp.dot(a, b, preferred_element_type=F32)


def _dot_nt(a, b):
    return lax.dot_general(a, b, (((1,), (1,)), ((), ())), preferred_element_type=F32)


def _dot_tn(a, b):
    return lax.dot_general(a, b, (((0,), (0,)), ((), ())), preferred_element_type=F32)


def _silu(t):
    return t * jax.nn.sigmoid(t)


def _mod_kernel(s_ref, w_ref, b_ref, o_ref):
    a = _silu(s_ref[...]).astype(BF16)
    o_ref[...] = _dot(a, w_ref[...].astype(BF16)) + b_ref[...]


def _mod_vectors(s8, ada_w, ada_b):
    depth, d, n = ada_w.shape
    tn = COL_TILE
    return pl.pallas_call(
        _mod_kernel,
        grid=(depth, n // tn),
        in_specs=[
            pl.BlockSpec((8, d), lambda l, j: (0, 0)),
            pl.BlockSpec((None, d, tn), lambda l, j: (l, 0, j)),
            pl.BlockSpec((None, 1, tn), lambda l, j: (l, 0, j)),
        ],
        out_specs=pl.BlockSpec((None, 8, tn), lambda l, j: (l, 0, j)),
        out_shape=jax.ShapeDtypeStruct((depth, 8, n), F32),
        compiler_params=_params(("parallel", "parallel"), 2 * d * tn * 4 + d * tn * 2),
        name="adaln_mod",
    )(s8, ada_w, ada_b.reshape(depth, 1, n))


def _norm_mod_kernel(x_ref, g_ref, sh_ref, sc_ref, o_ref):
    x = x_ref[...]
    r = lax.rsqrt(jnp.mean(x * x, axis=-1, keepdims=True) + NORM_EPS)
    y = x * r * g_ref[...]
    o_ref[...] = (y * (1.0 + sc_ref[...]) + sh_ref[...]).astype(o_ref.dtype)


def _norm_mod(x2, g, mod5, layer, k_shift, k_scale, mod_row, tm=256):
    m, d = x2.shape
    return pl.pallas_call(
        _norm_mod_kernel,
        grid=(m // tm,),
        in_specs=[
            pl.BlockSpec((tm, d), lambda i: (i, 0)),
            pl.BlockSpec((1, d), lambda i: (0, 0)),
            pl.BlockSpec((None, None, None, 1, d), lambda i: (layer, mod_row(i, tm), k_shift, 0, 0)),
            pl.BlockSpec((None, None, None, 1, d), lambda i: (layer, mod_row(i, tm), k_scale, 0, 0)),
        ],
        out_specs=pl.BlockSpec((tm, d), lambda i: (i, 0)),
        out_shape=jax.ShapeDtypeStruct((m, d), BF16),
        compiler_params=_params(("parallel",), 2 * tm * d * 6),
        name="norm_mod",
    )(x2, g.reshape(1, d), mod5, mod5)


def _final_norm_kernel(x_ref, g_ref, o_ref):
    x = x_ref[...]
    r = lax.rsqrt(jnp.mean(x * x, axis=-1, keepdims=True) + NORM_EPS)
    o_ref[...] = x * r * g_ref[...]


def _final_norm(x2, g, tm=256):
    m, d = x2.shape
    return pl.pallas_call(
        _final_norm_kernel,
        grid=(m // tm,),
        in_specs=[pl.BlockSpec((tm, d), lambda i: (i, 0)), pl.BlockSpec((1, d), lambda i: (0, 0))],
        out_specs=pl.BlockSpec((tm, d), lambda i: (i, 0)),
        out_shape=jax.ShapeDtypeStruct((m, d), F32),
        compiler_params=_params(("parallel",), 2 * tm * d * 8),
        name="final_norm",
    )(x2, g.reshape(1, d))


def _col_tile(n):
    return next(t for t in range(COL_TILE, 0, -LANE) if n % t == 0)


def _mm_kernel(a_ref, w_ref, o_ref):
    o_ref[...] = _dot(a_ref[...], w_ref[...]).astype(o_ref.dtype)


def _matmul(a, w, out_dtype=F32):
    m, k = a.shape
    n = w.shape[1]
    tm = min(ROW_TILE, m)
    tn = _col_tile(n)
    ob = jnp.dtype(out_dtype).itemsize
    return pl.pallas_call(
        _mm_kernel,
        grid=(m // tm, n // tn),
        in_specs=[pl.BlockSpec((tm, k), lambda i, j: (i, 0)), pl.BlockSpec((k, tn), lambda i, j: (0, j))],
        out_specs=pl.BlockSpec((tm, tn), lambda i, j: (i, j)),
        out_shape=jax.ShapeDtypeStruct((m, n), out_dtype),
        compiler_params=_params(("parallel", "parallel"), 2 * (tm * k * 2 + k * tn * 2 + tm * tn * ob)),
        name="matmul",
    )(a, w)


def _mm_swiglu_kernel(a_ref, wg_ref, wu_ref, o_ref):
    a = a_ref[...]
    g = _dot(a, wg_ref[...])
    u = _dot(a, wu_ref[...])
    o_ref[...] = (_silu(g) * u).astype(o_ref.dtype)


def _matmul_swiglu(a, wg, wu, layer):
    m, k = a.shape
    n = wg.shape[2]
    tm = min(ROW_TILE, m)
    tn = COL_TILE
    return pl.pallas_call(
        _mm_swiglu_kernel,
        grid=(m // tm, n // tn),
        in_specs=[
            pl.BlockSpec((tm, k), lambda i, j: (i, 0)),
            pl.BlockSpec((None, k, tn), lambda i, j: (layer, 0, j)),
            pl.BlockSpec((None, k, tn), lambda i, j: (layer, 0, j)),
        ],
        out_specs=pl.BlockSpec((tm, tn), lambda i, j: (i, j)),
        out_shape=jax.ShapeDtypeStruct((m, n), BF16),
        compiler_params=_params(("parallel", "parallel"), 2 * (tm * k * 2 + 2 * k * tn * 2 + tm * tn * 2)),
        name="matmul_swiglu",
    )(a, wg, wu)


def _mm_res_kernel(*refs, nk, n_a):
    a_refs, (w_ref, res_ref, gate_ref, o_ref), scratch = refs[:n_a], refs[n_a:n_a + 4], refs[n_a + 4:]
    k0 = 0
    p = None
    for a_ref in a_refs:
        ka = a_ref.shape[1]
        d = _dot(a_ref[...], w_ref[k0:k0 + ka, :])
        p = d if p is None else p + d
        k0 += ka
    if nk == 1:
        o_ref[...] = res_ref[...] + gate_ref[...] * p
        return
    acc_ref, = scratch
    kk = pl.program_id(2)

    @pl.when(kk == 0)
    def _():
        acc_ref[...] = p

    @pl.when(jnp.logical_and(kk > 0, kk < nk - 1))
    def _():
        acc_ref[...] += p

    @pl.when(kk == nk - 1)
    def _():
        o_ref[...] = res_ref[...] + gate_ref[...] * (acc_ref[...] + p)


def _matmul_residual(a_slabs, w, res, mod5, layer, k_gate, mod_row, nk=1, tn=COL_TILE):
    m = a_slabs[0].shape[0]
    k, n = w.shape[1], w.shape[2]
    tm = min(ROW_TILE, m)
    tk = k // nk
    assert nk == 1 or len(a_slabs) == 1
    a_specs = [pl.BlockSpec((tm, tk if nk > 1 else a.shape[1]), lambda i, j, q: (i, q)) for a in a_slabs]
    return pl.pallas_call(
        functools.partial(_mm_res_kernel, nk=nk, n_a=len(a_slabs)),
        grid=(m // tm, n // tn, nk),
        in_specs=a_specs + [
            pl.BlockSpec((None, tk, tn), lambda i, j, q: (layer, q, j)),
            pl.BlockSpec((tm, tn), lambda i, j, q: (i, j)),
            pl.BlockSpec((None, None, None, 1, tn), lambda i, j, q: (layer, mod_row(i, tm), k_gate, 0, j)),
        ],
        out_specs=pl.BlockSpec((tm, tn), lambda i, j, q: (i, j)),
        out_shape=jax.ShapeDtypeStruct((m, n), F32),
        scratch_shapes=[pltpu.VMEM((tm, tn), F32)] if nk > 1 else [],
        compiler_params=_params(
            ("parallel", "parallel", "arbitrary"), 2 * (tm * tk * 2 + tk * tn * 2 + 2 * tm * tn * 4) + tm * tn * 4
        ),
        name="matmul_residual",
    )(*a_slabs, w, res, mod5)


def _rope_tables(n_tok):
    quarter = HEAD_DIM // 4
    inv = 1.0 / (ROPE_THETA ** (jnp.arange(quarter, dtype=F32) / quarter))
    pos = jnp.arange(n_tok, dtype=jnp.int32)
    row = (pos // GRID_W).astype(F32)[:, None] * inv
    col = (pos % GRID_W).astype(F32)[:, None] * inv
    cos = jnp.concatenate([jnp.cos(row), jnp.cos(row), jnp.cos(col), jnp.cos(col)], axis=-1)
    sin = jnp.concatenate([-jnp.sin(row), jnp.sin(row), -jnp.sin(col), jnp.sin(col)], axis=-1)
    return cos, sin


def _rope_kernel(x_ref, cos_ref, sin_ref, o_ref, *, scale, n_heads):
    cos = cos_ref[...] * scale
    sin = sin_ref[...] * scale
    lane = lax.broadcasted_iota(jnp.int32, cos.shape, 1)
    first = (lane & (HEAD_DIM // 4)) == 0
    for h in range(n_heads):
        hs = slice(h * HEAD_DIM, (h + 1) * HEAD_DIM)
        x = x_ref[:, hs]
        partner = jnp.where(first, pltpu.roll(x, HEAD_DIM - HEAD_DIM // 4, 1), pltpu.roll(x, HEAD_DIM // 4, 1))
        o_ref[:, hs] = (x * cos + partner * sin).astype(o_ref.dtype)


def _rope(p, col0, width, cos, sin, scale, tr=256):
    m = p.shape[0]
    n_tok = cos.shape[0]
    tpb = n_tok // tr
    assert col0 % width == 0
    return pl.pallas_call(
        functools.partial(_rope_kernel, scale=scale, n_heads=width // HEAD_DIM),
        grid=(m // tr,),
        in_specs=[
            pl.BlockSpec((tr, width), lambda i: (i, col0 // width)),
            pl.BlockSpec((tr, HEAD_DIM), lambda i: (i % tpb, 0)),
            pl.BlockSpec((tr, HEAD_DIM), lambda i: (i % tpb, 0)),
        ],
        out_specs=pl.BlockSpec((tr, width), lambda i: (i, 0)),
        out_shape=jax.ShapeDtypeStruct((m, width), BF16),
        compiler_params=_params(("parallel",), 2 * tr * (width * 6 + HEAD_DIM * 8)),
        name="rope",
    )(p, cos, sin)


def _gla_kernel(ql, kl, vl, dl, qc, kc, vc, dc, up_ref, bias_ref, ol, oc, st_ref, *, nh, scale):
    d = pl.program_id(0)
    s = pl.program_id(2)
    is_ctx = s == 0
    fwd = d == 0
    n_chunk = CTX_BLOCK // GLA_CHUNK

    @pl.when(s == 0)
    def _():
        st_ref[...] = jnp.zeros_like(st_ref)

    row = lax.broadcasted_iota(jnp.int32, (GLA_CHUNK, GLA_CHUNK), 0)
    col = lax.broadcasted_iota(jnp.int32, (GLA_CHUNK, GLA_CHUNK), 1)
    incl = jnp.where(fwd, row - col, col - row) >= 0
    tmat = incl.astype(F32)
    up = up_ref[...]
    bias = bias_ref[...]

    def chunk(t, carry):
        c = jnp.where(fwd, t, n_chunk - 1 - t)
        rows = pl.ds(pl.multiple_of(c * GLA_CHUNK, GLA_CHUNK), GLA_CHUNK)

        def ld(rc, rl, cols):
            return jnp.where(is_ctx, rc[rows, cols], rl[rows, cols])

        z = _dot(ld(dc, dl, slice(None)).astype(BF16), up) + bias
        g = (jnp.minimum(z, 0.0) - jnp.log(1.0 + jnp.exp(-jnp.abs(z)))) * (1.0 / GLA_NORMALIZER)
        cum = jnp.dot(tmat, g, precision=lax.Precision.HIGHEST, preferred_element_type=F32)
        tot = jnp.sum(g, axis=0, keepdims=True)
        for h in range(nh):
            ks = slice(h * GLA_DK, (h + 1) * GLA_DK)
            vs = slice(h * GLA_DV, (h + 1) * GLA_DV)
            bh = cum[:, ks]
            eh = tot[:, ks]
            q = ld(qc, ql, ks)
            k = ld(kc, kl, ks)
            v = ld(vc, vl, vs).astype(BF16)
            qd = (q * (scale * jnp.exp(bh))).astype(BF16)
            ki = (k * jnp.exp(-bh)).astype(BF16)
            ku = (k * jnp.exp(eh - bh)).astype(BF16)
            a = jnp.where(incl, _dot_nt(qd, ki), 0.0)
            st = st_ref[h]
            o = _dot(a.astype(BF16), v) + _dot_nt(qd, st.astype(BF16))
            st_ref[h] = st * jnp.exp(eh) + _dot_tn(v, ku)

            @pl.when(is_ctx)
            def _():
                oc[rows, vs] = o

            @pl.when(jnp.logical_not(is_ctx))
            def _():
                ol[rows, vs] = o

        return carry

    lax.fori_loop(0, n_chunk, chunk, 0)


def _gla(p_lat, p_ctx, dn_lat, dn_ctx, up2, bias2, nb, n_tok, nh):
    qk = nh * GLA_DK
    dv = nh * GLA_DV
    tb = CTX_BLOCK
    nbl = n_tok // tb

    def lat_blk(d, b, s):
        t = jnp.maximum(s - 1, 0)
        return b * nbl + jnp.where(d == 0, t, nbl - 1 - t)

    lat = lambda w, c: pl.BlockSpec((tb, w), lambda d, b, s: (lat_blk(d, b, s), c))
    ctx = lambda w, c: pl.BlockSpec((tb, w), lambda d, b, s: (b, c))
    vm = 2 * (2 * (2 * tb * qk + tb * dv + tb * LANE) * 4 + 2 * tb * dv * 4) + nh * GLA_DV * GLA_DK * 4
    return pl.pallas_call(
        functools.partial(_gla_kernel, nh=nh, scale=GLA_DK ** -0.5),
        grid=(2, nb, nbl + 1),
        in_specs=[
            lat(qk, 0), lat(qk, 1), lat(dv, 1), lat(LANE, 0),
            ctx(qk, 0), ctx(qk, 1), ctx(dv, 1), ctx(LANE, 0),
            pl.BlockSpec((None, LANE, qk), lambda d, b, s: (d, 0, 0)),
            pl.BlockSpec((None, 1, qk), lambda d, b, s: (d, 0, 0)),
        ],
        out_specs=[
            pl.BlockSpec((None, tb, dv), lambda d, b, s: (d, lat_blk(d, b, s), 0)),
            pl.BlockSpec((None, tb, dv), lambda d, b, s: (d, b, 0)),
        ],
        out_shape=[
            jax.ShapeDtypeStruct((2, nb * n_tok, dv), F32),
            jax.ShapeDtypeStruct((2, nb * tb, dv), F32),
        ],
        scratch_shapes=[pltpu.VMEM((nh, GLA_DV, GLA_DK), F32)],
        compiler_params=_params(("parallel", "parallel", "arbitrary"), vm),
        name="gla_scan",
    )(p_lat, p_lat, p_lat, dn_lat, p_ctx, p_ctx, p_ctx, dn_ctx, up2, bias2)


def _gla_out_kernel(o_ref, r_ref, g_ref, y_ref, *, nh):
    for h in range(nh):
        sl = slice(h * GLA_DV, (h + 1) * GLA_DV)
        o = o_ref[0, :, sl] + o_ref[1, :, sl]
        y = o * lax.rsqrt(jnp.mean(o * o, axis=-1, keepdims=True) + NORM_EPS) * g_ref[...]
        y_ref[:, sl] = (y * _silu(r_ref[:, sl])).astype(y_ref.dtype)


def _gla_out(o2, p, norm_g, nh, tb=256):
    _, m, dv = o2.shape
    return pl.pallas_call(
        functools.partial(_gla_out_kernel, nh=nh),
        grid=(m // tb,),
        in_specs=[
            pl.BlockSpec((2, tb, dv), lambda i: (0, i, 0)),
            pl.BlockSpec((tb, dv), lambda i: (i, 2)),
            pl.BlockSpec((1, GLA_DV), lambda i: (0, 0)),
        ],
        out_specs=pl.BlockSpec((tb, dv), lambda i: (i, 0)),
        out_shape=jax.ShapeDtypeStruct((m, dv), BF16),
        compiler_params=_params(("parallel",), 2 * tb * dv * 14),
        name="gla_out",
    )(o2, p, norm_g.reshape(1, GLA_DV))


def _swa_kernel(sink_ref, q_ref, kp, kc, kn, vp, vc, vn, kx, vx, o_ref, *, n_blk):
    hkv = pl.program_id(1)
    i = pl.program_id(2)
    kwin = jnp.concatenate([kp[...], kc[...], kn[...]], axis=0)
    vwin = jnp.concatenate([vp[...], vc[...], vn[...]], axis=0).astype(BF16)
    kctx = kx[...].astype(BF16)
    vctx = vx[...].astype(BF16)
    span = SWA_BLOCK + 2 * SWA_WINDOW
    qi = lax.broadcasted_iota(jnp.int32, (SWA_BLOCK, span), 0)
    kj = lax.broadcasted_iota(jnp.int32, (SWA_BLOCK, span), 1)
    key_pos = (i - 1) * SWA_BLOCK + kj
    ok = jnp.logical_and(jnp.abs(qi + SWA_WINDOW - kj) <= SWA_WINDOW,
                         jnp.logical_and(key_pos >= 0, key_pos < n_blk * SWA_BLOCK))
    for g in range(SWA_GROUP):
        hs = slice(g * HEAD_DIM, (g + 1) * HEAD_DIM)
        q = q_ref[:, hs]
        s_win = jnp.where(ok, _dot_nt(q, kwin), -jnp.inf)
        s_ctx = _dot_nt(q, kctx)
        sink = sink_ref[0, hkv * SWA_GROUP + g]
        m = jnp.maximum(jnp.maximum(jnp.max(s_win, axis=-1, keepdims=True),
                                    jnp.max(s_ctx, axis=-1, keepdims=True)), sink)
        p_win = jnp.exp(s_win - m)
        p_ctx = jnp.exp(s_ctx - m)
        den = (jnp.sum(p_win, axis=-1, keepdims=True) + jnp.sum(p_ctx, axis=-1, keepdims=True)
               + jnp.exp(sink - m))
        o = _dot(p_ctx.astype(BF16), vctx) + _dot(p_win.astype(BF16), vwin)
        o_ref[:, hs] = (o / den).astype(o_ref.dtype)


def _swa(sink, q_rope, k_rope, p_lat, p_ctx, kcol, vcol, nb, n_tok, n_kv):
    n_blk = n_tok // SWA_BLOCK
    tb = SWA_BLOCK
    gw = SWA_GROUP * HEAD_DIM
    kc0 = kcol // HEAD_DIM
    vc0 = vcol // HEAD_DIM
    prev = lambda b, h, i: b * n_blk + jnp.maximum(i - 1, 0)
    cur = lambda b, h, i: b * n_blk + i
    nxt = lambda b, h, i: b * n_blk + jnp.minimum(i + 1, n_blk - 1)
    kspec = lambda f: pl.BlockSpec((tb, HEAD_DIM), lambda b, h, i: (f(b, h, i), h))
    vspec = lambda f: pl.BlockSpec((tb, HEAD_DIM), lambda b, h, i: (f(b, h, i), vc0 + h))
    return pl.pallas_call(
        functools.partial(_swa_kernel, n_blk=n_blk),
        grid=(nb, n_kv, n_blk),
        in_specs=[
            pl.BlockSpec(memory_space=pltpu.SMEM),
            pl.BlockSpec((tb, gw), lambda b, h, i: (cur(b, h, i), h)),
            kspec(prev), kspec(cur), kspec(nxt),
            vspec(prev), vspec(cur), vspec(nxt),
            pl.BlockSpec((CTX_BLOCK, HEAD_DIM), lambda b, h, i: (b, kc0 + h)),
            pl.BlockSpec((CTX_BLOCK, HEAD_DIM), lambda b, h, i: (b, vc0 + h)),
        ],
        out_specs=pl.BlockSpec((tb, gw), lambda b, h, i: (cur(b, h, i), h)),
        out_shape=jax.ShapeDtypeStruct((nb * n_tok, n_kv * gw), BF16),
        compiler_params=_params(("parallel", "parallel", "parallel"), 8 << 20),
        name="swa",
    )(sink, q_rope, k_rope, k_rope, k_rope, p_lat, p_lat, p_lat, p_ctx, p_ctx)


def _ctx_sink_kernel(sink_ref, q_ref, kx, vx, o_ref, *, scale):
    hkv = pl.program_id(1)
    kctx = kx[...].astype(BF16)
    vctx = vx[...].astype(BF16)
    for g in range(SWA_GROUP):
        hs = slice(g * HEAD_DIM, (g + 1) * HEAD_DIM)
        q = (q_ref[:, hs] * scale).astype(BF16)
        s = _dot_nt(q, kctx)
        sink = sink_ref[0, hkv * SWA_GROUP + g]
        m = jnp.maximum(jnp.max(s, axis=-1, keepdims=True), sink)
        p = jnp.exp(s - m)
        den = jnp.sum(p, axis=-1, keepdims=True) + jnp.exp(sink - m)
        o_ref[:, hs] = (_dot(p.astype(BF16), vctx) / den).astype(o_ref.dtype)


def _ctx_sink(sink, p_ctx, qcol, kcol, vcol, nb, n_kv):
    gw = SWA_GROUP * HEAD_DIM
    return pl.pallas_call(
        functools.partial(_ctx_sink_kernel, scale=HEAD_DIM ** -0.5),
        grid=(nb, n_kv),
        in_specs=[
            pl.BlockSpec(memory_space=pltpu.SMEM),
            pl.BlockSpec((CTX_BLOCK, gw), lambda b, h: (b, qcol // gw + h)),
            pl.BlockSpec((CTX_BLOCK, HEAD_DIM), lambda b, h: (b, kcol // HEAD_DIM + h)),
            pl.BlockSpec((CTX_BLOCK, HEAD_DIM), lambda b, h: (b, vcol // HEAD_DIM + h)),
        ],
        out_specs=pl.BlockSpec((CTX_BLOCK, gw), lambda b, h: (b, h)),
        out_shape=jax.ShapeDtypeStruct((nb * CTX_BLOCK, n_kv * gw), BF16),
        compiler_params=_params(("parallel", "parallel"), 4 << 20),
        name="ctx_sink_attention",
    )(sink, p_ctx, p_ctx, p_ctx)


def _na_bias_table(rpb):
    qcol = np.arange(GRID_W)[:, None]
    kcol = np.arange(GRID_W)[None, :]
    col_start = np.clip(qcol - NA_COLS // 2, 0, GRID_W - NA_COLS)
    col_ok = (kcol >= col_start) & (kcol < col_start + NA_COLS)
    dc = np.clip(kcol - qcol + NA_COLS - 1, 0, 2 * NA_COLS - 2)
    n_dc = 2 * NA_COLS - 1
    onehot = (dc[None] == np.arange(n_dc)[:, None, None]).astype(np.float32)
    cols = jnp.einsum('hrd,dqk->hrqk', rpb.astype(F32), onehot, precision=lax.Precision.HIGHEST)
    cols = jnp.where(col_ok[None, None], cols, -jnp.inf)
    tab = jnp.stack([cols[:, NA_ROWS - 1 - dl:2 * NA_ROWS - 1 - dl] for dl in range(NA_ROWS)], axis=1)
    tab = tab.transpose(0, 1, 3, 2, 4)
    return tab.reshape(rpb.shape[0], NA_ROWS, GRID_W, NA_ROWS * GRID_W)


def _na_kernel(q_ref, k_ref, v_ref, kx, vx, bias_ref, o_ref, *, nh, scale):
    for h in range(nh):
        hs = slice(h * HEAD_DIM, (h + 1) * HEAD_DIM)
        q = (q_ref[:, hs] * scale).astype(BF16)
        s_nb = _dot_nt(q, k_ref[:, hs].astype(BF16)) + bias_ref[h]
        s_ctx = _dot_nt(q, kx[:, hs].astype(BF16))
        m = jnp.maximum(jnp.max(s_nb, axis=-1, keepdims=True), jnp.max(s_ctx, axis=-1, keepdims=True))
        p_nb = jnp.exp(s_nb - m)
        p_ctx = jnp.exp(s_ctx - m)
        den = jnp.sum(p_nb, axis=-1, keepdims=True) + jnp.sum(p_ctx, axis=-1, keepdims=True)
        o = _dot(p_ctx.astype(BF16), vx[:, hs].astype(BF16)) + _dot(p_nb.astype(BF16), v_ref[:, hs].astype(BF16))
        o_ref[:, hs] = (o / den).astype(o_ref.dtype)


def _na(p_lat, p_ctx, bias_tab, nb, n_tok, nh):
    rows = n_tok // GRID_W
    w = nh * HEAD_DIM
    win = NA_ROWS * GRID_W
    rs = lambda r: jnp.clip(r - NA_ROWS // 2, 0, rows - NA_ROWS)
    kv = lambda c: pl.BlockSpec((pl.Element(win), pl.Element(w)),
                                lambda b, r: ((b * rows + rs(r)) * GRID_W, c * w))
    vm = 2 * (GRID_W * w * 4 + 2 * win * w * 4 + 2 * CTX_BLOCK * w * 4 + nh * GRID_W * win * 4 + GRID_W * w * 2)
    return pl.pallas_call(
        functools.partial(_na_kernel, nh=nh, scale=HEAD_DIM ** -0.5),
        grid=(nb, rows),
        in_specs=[
            pl.BlockSpec((GRID_W, w), lambda b, r: (b * rows + r, 0)),
            kv(1), kv(2),
            pl.BlockSpec((CTX_BLOCK, w), lambda b, r: (b, 1)),
            pl.BlockSpec((CTX_BLOCK, w), lambda b, r: (b, 2)),
            pl.BlockSpec((nh, None, GRID_W, win), lambda b, r: (0, r - rs(r), 0, 0)),
        ],
        out_specs=pl.BlockSpec((GRID_W, w), lambda b, r: (b * rows + r, 0)),
        out_shape=jax.ShapeDtypeStruct((nb * n_tok, w), BF16),
        compiler_params=_params(("parallel", "arbitrary"), vm),
        name="neighborhood_attention",
    )(p_lat, p_lat, p_lat, p_ctx, p_ctx, bias_tab)


def _diff_kernel(q_ref, k_ref, vt_ref, kx, vxt, lq1, lk1, lq2, lk2, g_ref, o_ref, m_ref, l_ref, acc_ref,
                 sa_ref, sb_ref, *, n_blk, tk, lam_init):
    m_ref[...] = jnp.full_like(m_ref, -jnp.inf)
    l_ref[...] = jnp.zeros_like(l_ref)
    acc_ref[...] = jnp.zeros_like(acc_ref)

    def scores(keys):
        return [_dot_nt(keys[:, t * HEAD_DIM:(t + 1) * HEAD_DIM], q_ref[:, t * HEAD_DIM:(t + 1) * HEAD_DIM])
                for t in range(2)]

    def lat_scores(j, dst):
        s = scores(k_ref[pl.ds(pl.multiple_of(j * tk, tk), tk), :])
        dst[0] = s[0]
        dst[1] = s[1]

    def accumulate(s, vt):
        for t in range(2):
            m_old = m_ref[t]
            m_new = jnp.maximum(m_old, jnp.max(s[t], axis=0, keepdims=True))
            alpha = jnp.exp2(m_old - m_new)
            p = jnp.exp2(s[t] - m_new)
            l_ref[t] = alpha * l_ref[t] + jnp.sum(p, axis=0, keepdims=True)
            acc_ref[t] = alpha * acc_ref[t] + _dot(vt, p.astype(BF16))
            m_ref[t] = m_new

    s_ctx = scores(kx[...].astype(BF16))
    lat_scores(0, sa_ref)
    accumulate(s_ctx, vxt[...])

    def body(jj, carry):
        j = 2 * jj
        lat_scores(j + 1, sb_ref)
        accumulate((sa_ref[0], sa_ref[1]), vt_ref[j])
        lat_scores(jnp.minimum(j + 2, n_blk - 1), sa_ref)
        accumulate((sb_ref[0], sb_ref[1]), vt_ref[j + 1])
        return carry

    lax.fori_loop(0, n_blk // 2, body, 0)

    lam = (jnp.exp(jnp.sum(lq1[...] * lk1[...], axis=-1, keepdims=True))
           - jnp.exp(jnp.sum(lq2[...] * lk2[...], axis=-1, keepdims=True)) + lam_init)
    o = acc_ref[0] / l_ref[0] - lam * (acc_ref[1] / l_ref[1])
    y = o * lax.rsqrt(jnp.mean(o * o, axis=0, keepdims=True) + NORM_EPS) * (g_ref[...] * (1.0 - lam_init))
    o_ref[...] = y.T.astype(o_ref.dtype)


def _diff(q_rope, k_rope, vt_lat, p_ctx, vt_ctx, kcol, lq1, lk1, lq2, lk2, norm_g, lam_init, nb, n_tok, nh,
          tq=512):
    nq = n_tok // tq
    n_blk, tk = vt_lat.shape[2], vt_lat.shape[4]
    w = 2 * HEAD_DIM
    vec = lambda: pl.BlockSpec((1, HEAD_DIM), lambda b, h, i: (0, 0))
    vm = (2 * (tq * w * 2 + 2 * n_tok * w * 2 + CTX_BLOCK * w * 6 + tq * w * 2)
          + 2 * tq * w * 4 + 2 * 2 * tk * tq * 6 + tq * w * 8)
    return pl.pallas_call(
        functools.partial(_diff_kernel, n_blk=n_blk, tk=tk, lam_init=lam_init),
        grid=(nb, nh, nq),
        in_specs=[
            pl.BlockSpec((tq, w), lambda b, h, i: (b * nq + i, h)),
            pl.BlockSpec((n_tok, w), lambda b, h, i: (b, h)),
            pl.BlockSpec((None, None, n_blk, w, tk), lambda b, h, i: (b, h, 0, 0, 0)),
            pl.BlockSpec((CTX_BLOCK, w), lambda b, h, i: (b, kcol // w + h)),
            pl.BlockSpec((None, None, w, CTX_BLOCK), lambda b, h, i: (b, h, 0, 0)),
            vec(), vec(), vec(), vec(),
            pl.BlockSpec((DIFF_DV, 1), lambda b, h, i: (0, 0)),
        ],
        out_specs=pl.BlockSpec((tq, w), lambda b, h, i: (b * nq + i, h)),
        out_shape=jax.ShapeDtypeStruct((nb * n_tok, nh * w), BF16),
        scratch_shapes=[
            pltpu.VMEM((2, 1, tq), F32),
            pltpu.VMEM((2, 1, tq), F32),
            pltpu.VMEM((2, DIFF_DV, tq), F32),
            pltpu.VMEM((2, tk, tq), F32),
            pltpu.VMEM((2, tk, tq), F32),
        ],
        compiler_params=_params(("parallel", "parallel", "arbitrary"), vm),
        name="diff_attention",
    )(q_rope, k_rope, vt_lat, p_ctx, vt_ctx, lq1.reshape(1, -1), lk1.reshape(1, -1), lq2.reshape(1, -1),
      lk2.reshape(1, -1), norm_g.reshape(-1, 1))


def _pad_cols(w, n):
    return jnp.pad(w, ((0, 0), (0, n - w.shape[1])))


def kernel(x, c, ctx, c_ctx, ada_w, ada_b, norm_mix_g, norm_ffn_g, w_out, ffn_w_gate, ffn_w_up, ffn_w_down,
           ev_w_in, gla_gate_up_f, gla_gate_bias_f, gla_gate_up_b, gla_gate_bias_b, gla_norm_g, swa_sink,
           od_w_in, na_rpb, diff_lq1, diff_lk1, diff_lq2, diff_lk2, diff_norm_g, final_norm_g):
    nb, n_tok, d = x.shape
    lc = ctx.shape[1]
    depth = ada_w.shape[0]
    f_hidden = ffn_w_gate.shape[2]
    half = d // 2
    assert lc == CTX_BLOCK and nb < 8 and depth == 2, "layout assumes a 256-token context, batch < 8, two layers"
    gla_heads = half // GLA_DV
    gla_qk = gla_heads * GLA_DK
    n_q = half // HEAD_DIM
    n_kv = n_q // SWA_GROUP
    diff_heads = half // (2 * HEAD_DIM)

    xs = x.reshape(nb * n_tok, d)
    cs = ctx.reshape(nb * lc, d)
    cos, sin = _rope_tables(n_tok)

    s8 = jnp.zeros((8, d), F32).at[:nb].set(c).at[nb].set(c_ctx)
    mod5 = _mod_vectors(s8, ada_w, ada_b).reshape(depth, 8, 6, 1, d)
    lat_row = lambda i, tm: (i * tm) // n_tok
    ctx_row = lambda i, tm: nb

    f_pad = -(-f_hidden // FFN_PAD) * FFN_PAD

    wg = jnp.pad(ffn_w_gate.astype(BF16), ((0, 0), (0, 0), (0, f_pad - f_hidden)))
    wu = jnp.pad(ffn_w_up.astype(BF16), ((0, 0), (0, 0), (0, f_pad - f_hidden)))
    wd = jnp.pad(ffn_w_down.astype(BF16), ((0, 0), (0, f_pad - f_hidden), (0, 0)))
    wo = w_out.astype(BF16)

    def ffn(stream, layer, mod_row):
        h2 = _norm_mod(stream, norm_ffn_g[layer], mod5, layer, 3, 4, mod_row)
        a = _matmul_swiglu(h2, wg, wu, layer)
        return _matmul_residual([a], wd, stream, mod5, layer, 5, mod_row, nk=4, tn=2 * COL_TILE)

    layer = 0
    w_in = ev_w_in[0]
    g0 = 2 * gla_qk + 2 * half
    w_main = jnp.concatenate([w_in[:, :g0], w_in[:, g0 + 2 * GLA_RANK:]], axis=1).astype(BF16)
    w_dn = _pad_cols(w_in[:, g0:g0 + 2 * GLA_RANK], LANE).astype(BF16)
    swa_q0 = g0
    swa_k0 = swa_q0 + half
    swa_v0 = swa_k0 + n_kv * HEAD_DIM
    up2 = jnp.zeros((2, LANE, gla_qk), F32)
    up2 = up2.at[0, :GLA_RANK].set(gla_gate_up_f[0]).at[1, GLA_RANK:2 * GLA_RANK].set(gla_gate_up_b[0]).astype(BF16)
    bias2 = jnp.stack([gla_gate_bias_f[0], gla_gate_bias_b[0]]).reshape(2, 1, gla_qk)

    h = _norm_mod(xs, norm_mix_g[layer], mod5, layer, 0, 1, lat_row)
    hc = _norm_mod(cs, norm_mix_g[layer], mod5, layer, 0, 1, ctx_row)
    p_lat = _matmul(h, w_main)
    p_ctx = _matmul(hc, w_main)
    dn_lat = _matmul(h, w_dn)
    dn_ctx = _matmul(hc, w_dn)

    o_lat, o_ctx = _gla(p_lat, p_ctx, dn_lat, dn_ctx, up2, bias2, nb, n_tok, gla_heads)
    ya = _gla_out(o_lat, p_lat, gla_norm_g[0], gla_heads)
    yac = _gla_out(o_ctx, p_ctx, gla_norm_g[0], gla_heads)

    sink = swa_sink[0].reshape(1, n_q)
    q_rope = _rope(p_lat, swa_q0, half, cos, sin, HEAD_DIM ** -0.5)
    k_rope = _rope(p_lat, swa_k0, n_kv * HEAD_DIM, cos, sin, 1.0)
    yb = _swa(sink, q_rope, k_rope, p_lat, p_ctx, swa_k0, swa_v0, nb, n_tok, n_kv)
    ybc = _ctx_sink(sink, p_ctx, swa_q0, swa_k0, swa_v0, nb, n_kv)

    xs = _matmul_residual([ya, yb], wo, xs, mod5, layer, 2, lat_row)
    cs = _matmul_residual([yac, ybc], wo, cs, mod5, layer, 2, ctx_row)
    xs = ffn(xs, layer, lat_row)
    cs = ffn(cs, layer, ctx_row)

    layer = 1
    w_in = od_w_in[0].astype(BF16)
    h = _norm_mod(xs, norm_mix_g[layer], mod5, layer, 0, 1, lat_row)
    hc = _norm_mod(cs, norm_mix_g[layer], mod5, layer, 0, 1, ctx_row)
    p_lat = _matmul(h, w_in)
    p_ctx = _matmul(hc, w_in)

    yn = _na(p_lat, p_ctx, _na_bias_table(na_rpb[0]), nb, n_tok, n_q)

    dq0 = 3 * half
    dk0 = dq0 + half
    dv0 = dk0 + half
    lam_init = 0.8 - 0.6 * math.exp(-0.3 * layer)
    qd = _rope(p_lat, dq0, half, cos, sin, HEAD_DIM ** -0.5 * math.log2(math.e))
    kd = _rope(p_lat, dk0, half, cos, sin, 1.0)
    tk = DIFF_KEY_BLOCK
    vt_lat = p_lat[:, dv0:].astype(BF16).reshape(nb, n_tok // tk, tk, diff_heads, DIFF_DV).transpose(0, 3, 1, 4, 2)
    vt_ctx = p_ctx[:, dv0:].astype(BF16).reshape(nb, lc, diff_heads, DIFF_DV).transpose(0, 2, 3, 1)
    yd = _diff(qd, kd, vt_lat, p_ctx, vt_ctx, dk0, diff_lq1[0], diff_lk1[0], diff_lq2[0], diff_lk2[0],
               diff_norm_g[0], lam_init, nb, n_tok, diff_heads)

    xs = _matmul_residual([yn, yd], wo, xs, mod5, layer, 2, lat_row)
    xs = ffn(xs, layer, lat_row)
    return _final_norm(xs, final_norm_g).reshape(nb, n_tok, d)
```

```python
import functools
import math

import numpy as np
import jax
import jax.numpy as jnp
from jax import lax
from jax.experimental import pallas as pl
from jax.experimental.pallas import tpu as pltpu

HEAD_DIM = 128
GRID_W = 64
ROPE_THETA = 10000.0
NORM_EPS = 1e-6
GLA_DK = 128
GLA_DV = 256
GLA_RANK = 16
GLA_NORMALIZER = 16.0
GLA_CHUNK = 64
SWA_GROUP = 4
SWA_WINDOW = 128
SWA_BLOCK = 128
NA_ROWS = 8
NA_COLS = 16
NA_PAIR = 2
DIFF_DV = 2 * HEAD_DIM
DIFF_KEY_BLOCK = 512
CTX_BLOCK = 256

V7X_VMEM_BUDGET = 60 * 1024 * 1024
LANE = 128
ROW_TILE = 1024
COL_TILE = 512
FFN_PAD = 1024

F32 = jnp.float32
BF16 = jnp.bfloat16


def _params(sem, vmem_bytes):
    limit = int(min(V7X_VMEM_BUDGET, vmem_bytes * 5 // 4 + (4 << 20)))
    return pltpu.CompilerParams(dimension_semantics=sem, vmem_limit_bytes=limit)


def _dot(a, b):
    return jnp.dot(a, b, preferred_element_type=F32)


def _dot_nt(a, b):
    return lax.dot_general(a, b, (((1,), (1,)), ((), ())), preferred_element_type=F32)


def _dot_tn(a, b):
    return lax.dot_general(a, b, (((0,), (0,)), ((), ())), preferred_element_type=F32)


def _silu(t):
    return t * jax.nn.sigmoid(t)


def _mod_kernel(s_ref, w_ref, b_ref, o_ref):
    a = _silu(s_ref[...]).astype(BF16)
    o_ref[...] = _dot(a, w_ref[...].astype(BF16)) + b_ref[...]


def _mod_vectors(s8, ada_w, ada_b):
    depth, d, n = ada_w.shape
    tn = COL_TILE
    return pl.pallas_call(
        _mod_kernel,
        grid=(depth, n // tn),
        in_specs=[
            pl.BlockSpec((8, d), lambda l, j: (0, 0)),
            pl.BlockSpec((None, d, tn), lambda l, j: (l, 0, j)),
            pl.BlockSpec((None, 1, tn), lambda l, j: (l, 0, j)),
        ],
        out_specs=pl.BlockSpec((None, 8, tn), lambda l, j: (l, 0, j)),
        out_shape=jax.ShapeDtypeStruct((depth, 8, n), F32),
        compiler_params=_params(("parallel", "parallel"), 2 * d * tn * 4 + d * tn * 2),
        name="adaln_mod",
    )(s8, ada_w, ada_b.reshape(depth, 1, n))


def _norm_mod_kernel(x_ref, g_ref, sh_ref, sc_ref, o_ref):
    x = x_ref[...]
    r = lax.rsqrt(jnp.mean(x * x, axis=-1, keepdims=True) + NORM_EPS)
    y = x * r * g_ref[...]
    o_ref[...] = (y * (1.0 + sc_ref[...]) + sh_ref[...]).astype(o_ref.dtype)


def _norm_mod(x2, g, mod5, layer, k_shift, k_scale, mod_row, tm=256):
    m, d = x2.shape
    return pl.pallas_call(
        _norm_mod_kernel,
        grid=(m // tm,),
        in_specs=[
            pl.BlockSpec((tm, d), lambda i: (i, 0)),
            pl.BlockSpec((1, d), lambda i: (0, 0)),
            pl.BlockSpec((None, None, None, 1, d), lambda i: (layer, mod_row(i, tm), k_shift, 0, 0)),
            pl.BlockSpec((None, None, None, 1, d), lambda i: (layer, mod_row(i, tm), k_scale, 0, 0)),
        ],
        out_specs=pl.BlockSpec((tm, d), lambda i: (i, 0)),
        out_shape=jax.ShapeDtypeStruct((m, d), BF16),
        compiler_params=_params(("parallel",), 2 * tm * d * 6),
        name="norm_mod",
    )(x2, g.reshape(1, d), mod5, mod5)


def _final_norm_kernel(x_ref, g_ref, o_ref):
    x = x_ref[...]
    r = lax.rsqrt(jnp.mean(x * x, axis=-1, keepdims=True) + NORM_EPS)
    o_ref[...] = x * r * g_ref[...]


def _final_norm(x2, g, tm=256):
    m, d = x2.shape
    return pl.pallas_call(
        _final_norm_kernel,
        grid=(m // tm,),
        in_specs=[pl.BlockSpec((tm, d), lambda i: (i, 0)), pl.BlockSpec((1, d), lambda i: (0, 0))],
        out_specs=pl.BlockSpec((tm, d), lambda i: (i, 0)),
        out_shape=jax.ShapeDtypeStruct((m, d), F32),
        compiler_params=_params(("parallel",), 2 * tm * d * 8),
        name="final_norm",
    )(x2, g.reshape(1, d))


def _col_tile(n):
    return next(t for t in range(COL_TILE, 0, -LANE) if n % t == 0)


def _mm_kernel(a_ref, w_ref, o_ref):
    o_ref[...] = _dot(a_ref[...], w_ref[...]).astype(o_ref.dtype)


def _matmul(a, w, out_dtype=F32):
    m, k = a.shape
    n = w.shape[1]
    tm = min(ROW_TILE, m)
    tn = _col_tile(n)
    ob = jnp.dtype(out_dtype).itemsize
    return pl.pallas_call(
        _mm_kernel,
        grid=(m // tm, n // tn),
        in_specs=[pl.BlockSpec((tm, k), lambda i, j: (i, 0)), pl.BlockSpec((k, tn), lambda i, j: (0, j))],
        out_specs=pl.BlockSpec((tm, tn), lambda i, j: (i, j)),
        out_shape=jax.ShapeDtypeStruct((m, n), out_dtype),
        compiler_params=_params(("parallel", "parallel"), 2 * (tm * k * 2 + k * tn * 2 + tm * tn * ob)),
        name="matmul",
    )(a, w)


def _mm_swiglu_kernel(a_ref, wg_ref, wu_ref, o_ref):
    a = a_ref[...]
    g = _dot(a, wg_ref[...])
    u = _dot(a, wu_ref[...])
    o_ref[...] = (_silu(g) * u).astype(o_ref.dtype)


def _matmul_swiglu(a, wg, wu, layer):
    m, k = a.shape
    n = wg.shape[2]
    tm = min(ROW_TILE, m)
    tn = COL_TILE
    return pl.pallas_call(
        _mm_swiglu_kernel,
        grid=(m // tm, n // tn),
        in_specs=[
            pl.BlockSpec((tm, k), lambda i, j: (i, 0)),
            pl.BlockSpec((None, k, tn), lambda i, j: (layer, 0, j)),
            pl.BlockSpec((None, k, tn), lambda i, j: (layer, 0, j)),
        ],
        out_specs=pl.BlockSpec((tm, tn), lambda i, j: (i, j)),
        out_shape=jax.ShapeDtypeStruct((m, n), BF16),
        compiler_params=_params(("parallel", "parallel"), 2 * (tm * k * 2 + 2 * k * tn * 2 + tm * tn * 2)),
        name="matmul_swiglu",
    )(a, wg, wu)


def _mm_res_kernel(*refs, nk, n_a):
    a_refs, (w_ref, res_ref, gate_ref, o_ref), scratch = refs[:n_a], refs[n_a:n_a + 4], refs[n_a + 4:]
    k0 = 0
    p = None
    for a_ref in a_refs:
        ka = a_ref.shape[1]
        d = _dot(a_ref[...], w_ref[k0:k0 + ka, :])
        p = d if p is None else p + d
        k0 += ka
    if nk == 1:
        o_ref[...] = res_ref[...] + gate_ref[...] * p
        return
    acc_ref, = scratch
    kk = pl.program_id(2)

    @pl.when(kk == 0)
    def _():
        acc_ref[...] = p

    @pl.when(jnp.logical_and(kk > 0, kk < nk - 1))
    def _():
        acc_ref[...] += p

    @pl.when(kk == nk - 1)
    def _():
        o_ref[...] = res_ref[...] + gate_ref[...] * (acc_ref[...] + p)


def _matmul_residual(a_slabs, w, res, mod5, layer, k_gate, mod_row, nk=1, tn=COL_TILE):
    m = a_slabs[0].shape[0]
    k, n = w.shape[1], w.shape[2]
    tm = min(ROW_TILE, m)
    tk = k // nk
    assert nk == 1 or len(a_slabs) == 1
    a_specs = [pl.BlockSpec((tm, tk if nk > 1 else a.shape[1]), lambda i, j, q: (i, q)) for a in a_slabs]
    return pl.pallas_call(
        functools.partial(_mm_res_kernel, nk=nk, n_a=len(a_slabs)),
        grid=(m // tm, n // tn, nk),
        in_specs=a_specs + [
            pl.BlockSpec((None, tk, tn), lambda i, j, q: (layer, q, j)),
            pl.BlockSpec((tm, tn), lambda i, j, q: (i, j)),
            pl.BlockSpec((None, None, None, 1, tn), lambda i, j, q: (layer, mod_row(i, tm), k_gate, 0, j)),
        ],
        out_specs=pl.BlockSpec((tm, tn), lambda i, j, q: (i, j)),
        out_shape=jax.ShapeDtypeStruct((m, n), F32),
        scratch_shapes=[pltpu.VMEM((tm, tn), F32)] if nk > 1 else [],
        compiler_params=_params(
            ("parallel", "parallel", "arbitrary"), 2 * (tm * tk * 2 + tk * tn * 2 + 2 * tm * tn * 4) + tm * tn * 4
        ),
        name="matmul_residual",
    )(*a_slabs, w, res, mod5)


def _rope_tables(n_tok):
    quarter = HEAD_DIM // 4
    inv = 1.0 / (ROPE_THETA ** (jnp.arange(quarter, dtype=F32) / quarter))
    pos = jnp.arange(n_tok, dtype=jnp.int32)
    row = (pos // GRID_W).astype(F32)[:, None] * inv
    col = (pos % GRID_W).astype(F32)[:, None] * inv
    cos = jnp.concatenate([jnp.cos(row), jnp.cos(row), jnp.cos(col), jnp.cos(col)], axis=-1)
    sin = jnp.concatenate([-jnp.sin(row), jnp.sin(row), -jnp.sin(col), jnp.sin(col)], axis=-1)
    return cos, sin


def _rope_kernel(x_ref, cos_ref, sin_ref, o_ref, *, scale, n_heads):
    cos = cos_ref[...] * scale
    sin = sin_ref[...] * scale
    lane = lax.broadcasted_iota(jnp.int32, cos.shape, 1)
    first = (lane & (HEAD_DIM // 4)) == 0
    for h in range(n_heads):
        hs = slice(h * HEAD_DIM, (h + 1) * HEAD_DIM)
        x = x_ref[:, hs].astype(F32)
        partner = jnp.where(first, pltpu.roll(x, HEAD_DIM - HEAD_DIM // 4, 1), pltpu.roll(x, HEAD_DIM // 4, 1))
        o_ref[:, hs] = (x * cos + partner * sin).astype(o_ref.dtype)


def _rope(p, col0, width, cos, sin, scale, tr=256):
    m = p.shape[0]
    n_tok = cos.shape[0]
    tpb = n_tok // tr
    assert col0 % width == 0
    return pl.pallas_call(
        functools.partial(_rope_kernel, scale=scale, n_heads=width // HEAD_DIM),
        grid=(m // tr,),
        in_specs=[
            pl.BlockSpec((tr, width), lambda i: (i, col0 // width)),
            pl.BlockSpec((tr, HEAD_DIM), lambda i: (i % tpb, 0)),
            pl.BlockSpec((tr, HEAD_DIM), lambda i: (i % tpb, 0)),
        ],
        out_specs=pl.BlockSpec((tr, width), lambda i: (i, 0)),
        out_shape=jax.ShapeDtypeStruct((m, width), BF16),
        compiler_params=_params(("parallel",), 2 * tr * (width * 6 + HEAD_DIM * 8)),
        name="rope",
    )(p, cos, sin)


def _gla_kernel(ql, kl, vl, dl, qc, kc, vc, dc, up_ref, bias_ref, ol, oc, st_ref, *, nh, scale):
    d = pl.program_id(0)
    s = pl.program_id(2)
    is_ctx = s == 0
    fwd = d == 0
    n_chunk = CTX_BLOCK // GLA_CHUNK

    @pl.when(s == 0)
    def _():
        st_ref[...] = jnp.zeros_like(st_ref)

    row = lax.broadcasted_iota(jnp.int32, (GLA_CHUNK, GLA_CHUNK), 0)
    col = lax.broadcasted_iota(jnp.int32, (GLA_CHUNK, GLA_CHUNK), 1)
    incl = jnp.where(fwd, row - col, col - row) >= 0
    tmat = incl.astype(F32)
    up = up_ref[...]
    bias = bias_ref[...]

    def chunk(t, carry):
        c = jnp.where(fwd, t, n_chunk - 1 - t)
        rows = pl.ds(pl.multiple_of(c * GLA_CHUNK, GLA_CHUNK), GLA_CHUNK)

        def ld(rc, rl, cols):
            return jnp.where(is_ctx, rc[rows, cols], rl[rows, cols])

        z = _dot(ld(dc, dl, slice(None)).astype(BF16), up) + bias
        g = (jnp.minimum(z, 0.0) - jnp.log(1.0 + jnp.exp(-jnp.abs(z)))) * (1.0 / GLA_NORMALIZER)
        cum = jnp.dot(tmat, g, precision=lax.Precision.HIGHEST, preferred_element_type=F32)
        tot = jnp.sum(g, axis=0, keepdims=True)
        for h in range(nh):
            ks = slice(h * GLA_DK, (h + 1) * GLA_DK)
            vs = slice(h * GLA_DV, (h + 1) * GLA_DV)
            bh = cum[:, ks]
            eh = tot[:, ks]
            q = ld(qc, ql, ks)
            k = ld(kc, kl, ks)
            v = ld(vc, vl, vs).astype(BF16)
            qd = (q * (scale * jnp.exp(bh))).astype(BF16)
            ki = (k * jnp.exp(-bh)).astype(BF16)
            ku = (k * jnp.exp(eh - bh)).astype(BF16)
            a = jnp.where(incl, _dot_nt(qd, ki), 0.0)
            st = st_ref[h]
            o = _dot(a.astype(BF16), v) + _dot_nt(qd, st.astype(BF16))
            st_ref[h] = st * jnp.exp(eh) + _dot_tn(v, ku)

            @pl.when(is_ctx)
            def _():
                oc[rows, vs] = o

            @pl.when(jnp.logical_not(is_ctx))
            def _():
                ol[rows, vs] = o

        return carry

    lax.fori_loop(0, n_chunk, chunk, 0)


def _gla(p_lat, p_ctx, dn_lat, dn_ctx, up2, bias2, nb, n_tok, nh):
    qk = nh * GLA_DK
    dv = nh * GLA_DV
    tb = CTX_BLOCK
    nbl = n_tok // tb

    def lat_blk(d, b, s):
        t = jnp.maximum(s - 1, 0)
        return b * nbl + jnp.where(d == 0, t, nbl - 1 - t)

    lat = lambda w, c: pl.BlockSpec((tb, w), lambda d, b, s: (lat_blk(d, b, s), c))
    ctx = lambda w, c: pl.BlockSpec((tb, w), lambda d, b, s: (b, c))
    vm = 2 * (2 * (2 * tb * qk + tb * dv + tb * LANE) * 4 + 2 * tb * dv * 4) + nh * GLA_DV * GLA_DK * 4
    return pl.pallas_call(
        functools.partial(_gla_kernel, nh=nh, scale=GLA_DK ** -0.5),
        grid=(2, nb, nbl + 1),
        in_specs=[
            lat(qk, 0), lat(qk, 1), lat(dv, 1), lat(LANE, 0),
            ctx(qk, 0), ctx(qk, 1), ctx(dv, 1), ctx(LANE, 0),
            pl.BlockSpec((None, LANE, qk), lambda d, b, s: (d, 0, 0)),
            pl.BlockSpec((None, 1, qk), lambda d, b, s: (d, 0, 0)),
        ],
        out_specs=[
            pl.BlockSpec((None, tb, dv), lambda d, b, s: (d, lat_blk(d, b, s), 0)),
            pl.BlockSpec((None, tb, dv), lambda d, b, s: (d, b, 0)),
        ],
        out_shape=[
            jax.ShapeDtypeStruct((2, nb * n_tok, dv), F32),
            jax.ShapeDtypeStruct((2, nb * tb, dv), F32),
        ],
        scratch_shapes=[pltpu.VMEM((nh, GLA_DV, GLA_DK), F32)],
        compiler_params=_params(("parallel", "parallel", "arbitrary"), vm),
        name="gla_scan",
    )(p_lat, p_lat, p_lat, dn_lat, p_ctx, p_ctx, p_ctx, dn_ctx, up2, bias2)


def _gla_out_kernel(o_ref, r_ref, g_ref, y_ref, *, nh):
    for h in range(nh):
        sl = slice(h * GLA_DV, (h + 1) * GLA_DV)
        o = o_ref[0, :, sl] + o_ref[1, :, sl]
        y = o * lax.rsqrt(jnp.mean(o * o, axis=-1, keepdims=True) + NORM_EPS) * g_ref[...]
        y_ref[:, sl] = (y * _silu(r_ref[:, sl])).astype(y_ref.dtype)


def _gla_out(o2, p, norm_g, nh, tb=256):
    _, m, dv = o2.shape
    return pl.pallas_call(
        functools.partial(_gla_out_kernel, nh=nh),
        grid=(m // tb,),
        in_specs=[
            pl.BlockSpec((2, tb, dv), lambda i: (0, i, 0)),
            pl.BlockSpec((tb, dv), lambda i: (i, 2)),
            pl.BlockSpec((1, GLA_DV), lambda i: (0, 0)),
        ],
        out_specs=pl.BlockSpec((tb, dv), lambda i: (i, 0)),
        out_shape=jax.ShapeDtypeStruct((m, dv), BF16),
        compiler_params=_params(("parallel",), 2 * tb * dv * 14),
        name="gla_out",
    )(o2, p, norm_g.reshape(1, GLA_DV))


def _swa_kernel(sink_ref, q_ref, kp, kc, kn, vp, vc, vn, kx, vx, o_ref, *, n_blk):
    hkv = pl.program_id(1)
    i = pl.program_id(2)
    kwin = jnp.concatenate([kp[...], kc[...], kn[...]], axis=0)
    vwin = jnp.concatenate([vp[...], vc[...], vn[...]], axis=0).astype(BF16)
    kctx = kx[...].astype(BF16)
    vctx = vx[...].astype(BF16)
    span = SWA_BLOCK + 2 * SWA_WINDOW
    gq = SWA_GROUP * SWA_BLOCK
    qs = jnp.concatenate([q_ref[:, g * HEAD_DIM:(g + 1) * HEAD_DIM] for g in range(SWA_GROUP)], axis=0)
    kj = lax.broadcasted_iota(jnp.int32, (span, gq), 0)
    lane = lax.broadcasted_iota(jnp.int32, (span, gq), 1)
    qi = lane & (SWA_BLOCK - 1)
    key_pos = (i - 1) * SWA_BLOCK + kj
    ok = jnp.logical_and(jnp.abs(qi + SWA_WINDOW - kj) <= SWA_WINDOW,
                         jnp.logical_and(key_pos >= 0, key_pos < n_blk * SWA_BLOCK))
    s_win = jnp.where(ok, _dot_nt(kwin, qs), -jnp.inf)
    s_ctx = _dot_nt(kctx, qs)
    head = lax.broadcasted_iota(jnp.int32, (1, gq), 1) // SWA_BLOCK
    sink = jnp.zeros((1, gq), F32)
    for g in range(SWA_GROUP):
        sink = jnp.where(head == g, sink_ref[0, hkv * SWA_GROUP + g], sink)
    m = jnp.maximum(jnp.maximum(jnp.max(s_win, axis=0, keepdims=True), jnp.max(s_ctx, axis=0, keepdims=True)), sink)
    p_win = jnp.exp(s_win - m)
    p_ctx = jnp.exp(s_ctx - m)
    den = jnp.sum(p_win, axis=0, keepdims=True) + jnp.sum(p_ctx, axis=0, keepdims=True) + jnp.exp(sink - m)
    ot = (_dot_tn(vctx, p_ctx.astype(BF16)) + _dot_tn(vwin, p_win.astype(BF16))) / den
    for g in range(SWA_GROUP):
        hs = slice(g * HEAD_DIM, (g + 1) * HEAD_DIM)
        o_ref[:, hs] = ot[:, g * SWA_BLOCK:(g + 1) * SWA_BLOCK].T.astype(o_ref.dtype)


def _swa(sink, q_rope, k_rope, p_lat, p_ctx, kcol, vcol, nb, n_tok, n_kv):
    n_blk = n_tok // SWA_BLOCK
    tb = SWA_BLOCK
    gw = SWA_GROUP * HEAD_DIM
    kc0 = kcol // HEAD_DIM
    vc0 = vcol // HEAD_DIM
    prev = lambda b, h, i: b * n_blk + jnp.maximum(i - 1, 0)
    cur = lambda b, h, i: b * n_blk + i
    nxt = lambda b, h, i: b * n_blk + jnp.minimum(i + 1, n_blk - 1)
    kspec = lambda f: pl.BlockSpec((tb, HEAD_DIM), lambda b, h, i: (f(b, h, i), h))
    vspec = lambda f: pl.BlockSpec((tb, HEAD_DIM), lambda b, h, i: (f(b, h, i), vc0 + h))
    return pl.pallas_call(
        functools.partial(_swa_kernel, n_blk=n_blk),
        grid=(nb, n_kv, n_blk),
        in_specs=[
            pl.BlockSpec(memory_space=pltpu.SMEM),
            pl.BlockSpec((tb, gw), lambda b, h, i: (cur(b, h, i), h)),
            kspec(prev), kspec(cur), kspec(nxt),
            vspec(prev), vspec(cur), vspec(nxt),
            pl.BlockSpec((CTX_BLOCK, HEAD_DIM), lambda b, h, i: (b, kc0 + h)),
            pl.BlockSpec((CTX_BLOCK, HEAD_DIM), lambda b, h, i: (b, vc0 + h)),
        ],
        out_specs=pl.BlockSpec((tb, gw), lambda b, h, i: (cur(b, h, i), h)),
        out_shape=jax.ShapeDtypeStruct((nb * n_tok, n_kv * gw), BF16),
        compiler_params=_params(("parallel", "parallel", "parallel"), 8 << 20),
        name="swa",
    )(sink, q_rope, k_rope, k_rope, k_rope, p_lat, p_lat, p_lat, p_ctx, p_ctx)


def _ctx_sink_kernel(sink_ref, q_ref, kx, vx, o_ref, *, scale):
    hkv = pl.program_id(1)
    kctx = kx[...].astype(BF16)
    vctx = vx[...].astype(BF16)
    for g in range(SWA_GROUP):
        hs = slice(g * HEAD_DIM, (g + 1) * HEAD_DIM)
        q = (q_ref[:, hs].astype(F32) * scale).astype(BF16)
        s = _dot_nt(q, kctx)
        sink = sink_ref[0, hkv * SWA_GROUP + g]
        m = jnp.maximum(jnp.max(s, axis=-1, keepdims=True), sink)
        p = jnp.exp(s - m)
        den = jnp.sum(p, axis=-1, keepdims=True) + jnp.exp(sink - m)
        o_ref[:, hs] = (_dot(p.astype(BF16), vctx) / den).astype(o_ref.dtype)


def _ctx_sink(sink, p_ctx, qcol, kcol, vcol, nb, n_kv):
    gw = SWA_GROUP * HEAD_DIM
    return pl.pallas_call(
        functools.partial(_ctx_sink_kernel, scale=HEAD_DIM ** -0.5),
        grid=(nb, n_kv),
        in_specs=[
            pl.BlockSpec(memory_space=pltpu.SMEM),
            pl.BlockSpec((CTX_BLOCK, gw), lambda b, h: (b, qcol // gw + h)),
            pl.BlockSpec((CTX_BLOCK, HEAD_DIM), lambda b, h: (b, kcol // HEAD_DIM + h)),
            pl.BlockSpec((CTX_BLOCK, HEAD_DIM), lambda b, h: (b, vcol // HEAD_DIM + h)),
        ],
        out_specs=pl.BlockSpec((CTX_BLOCK, gw), lambda b, h: (b, h)),
        out_shape=jax.ShapeDtypeStruct((nb * CTX_BLOCK, n_kv * gw), BF16),
        compiler_params=_params(("parallel", "parallel"), 4 << 20),
        name="ctx_sink_attention",
    )(sink, p_ctx, p_ctx, p_ctx)


def _na_bias_table(rpb, rows):
    qcol = np.arange(GRID_W)[:, None]
    kcol = np.arange(GRID_W)[None, :]
    col_start = np.clip(qcol - NA_COLS // 2, 0, GRID_W - NA_COLS)
    col_ok = (kcol >= col_start) & (kcol < col_start + NA_COLS)
    dc = np.clip(kcol - qcol + NA_COLS - 1, 0, 2 * NA_COLS - 2)
    n_dc = 2 * NA_COLS - 1
    onehot = (dc[None] == np.arange(n_dc)[:, None, None]).astype(np.float32)
    cols = jnp.einsum('hrd,dqk->hrqk', rpb.astype(F32), onehot, precision=lax.Precision.HIGHEST)
    cols = jnp.where(col_ok[None, None], cols, -jnp.inf).transpose(0, 1, 3, 2)
    masked = jnp.full(cols.shape[:1] + cols.shape[2:], -jnp.inf, F32)

    span = NA_ROWS + NA_PAIR - 1
    patterns = []
    for r0 in range(0, rows, NA_PAIR):
        rs = [min(max(r0 + i - NA_ROWS // 2, 0), rows - NA_ROWS) for i in range(NA_PAIR)]
        u = min(rs[0], rows - span)
        patterns.append(tuple((r0 + i - rs[i], rs[i] - u) for i in range(NA_PAIR)))
    variants = sorted(set(patterns), key=patterns.index)
    blocks = []
    for pat in variants:
        per_query_row = []
        for delta, off in pat:
            krows = [cols[:, kr - off - delta + NA_ROWS - 1] if 0 <= kr - off < NA_ROWS else masked
                     for kr in range(span)]
            per_query_row.append(jnp.stack(krows, axis=1))
        blocks.append(jnp.concatenate(per_query_row, axis=-1).reshape(rpb.shape[0], span * GRID_W, NA_PAIR * GRID_W))
    ids = np.array([variants.index(p) for p in patterns], dtype=np.int32)
    return jnp.stack(blocks, axis=1), ids


def _na_kernel(ids_ref, q_ref, k_ref, v_ref, kx, vx, bias_ref, o_ref, *, nh, scale):
    for h in range(nh):
        hs = slice(h * HEAD_DIM, (h + 1) * HEAD_DIM)
        q = (q_ref[:, hs].astype(F32) * scale).astype(BF16)
        s_nb = _dot_nt(k_ref[:, hs].astype(BF16), q) + bias_ref[h]
        s_ctx = _dot_nt(kx[:, hs].astype(BF16), q)
        m = jnp.maximum(jnp.max(s_nb, axis=0, keepdims=True), jnp.max(s_ctx, axis=0, keepdims=True))
        p_nb = jnp.exp(s_nb - m)
        p_ctx = jnp.exp(s_ctx - m)
        den = jnp.sum(p_nb, axis=0, keepdims=True) + jnp.sum(p_ctx, axis=0, keepdims=True)
        ot = (_dot_tn(vx[:, hs].astype(BF16), p_ctx.astype(BF16))
              + _dot_tn(v_ref[:, hs].astype(BF16), p_nb.astype(BF16)))
        o_ref[:, hs] = (ot / den).T.astype(o_ref.dtype)


def _na(p_lat, p_ctx, rpb, nb, n_tok, nh):
    rows = n_tok // GRID_W
    w = nh * HEAD_DIM
    span = NA_ROWS + NA_PAIR - 1
    win = span * GRID_W
    tq = NA_PAIR * GRID_W
    n_pair = rows // NA_PAIR
    bias_tab, ids = _na_bias_table(rpb, rows)
    u = lambda g: jnp.clip(g * NA_PAIR - NA_ROWS // 2, 0, rows - span)
    kv = lambda c: pl.BlockSpec((pl.Element(win), pl.Element(w)),
                                lambda b, g, ids: ((b * rows + u(g)) * GRID_W, c * w))
    vm = 2 * (tq * w * 4 + 2 * win * w * 4 + 2 * CTX_BLOCK * w * 4 + nh * tq * win * 4 + tq * w * 2)
    return pl.pallas_call(
        functools.partial(_na_kernel, nh=nh, scale=HEAD_DIM ** -0.5),
        grid_spec=pltpu.PrefetchScalarGridSpec(
            num_scalar_prefetch=1,
            grid=(nb, n_pair),
            in_specs=[
                pl.BlockSpec((tq, w), lambda b, g, ids: (b * n_pair + g, 0)),
                kv(1), kv(2),
                pl.BlockSpec((CTX_BLOCK, w), lambda b, g, ids: (b, 1)),
                pl.BlockSpec((CTX_BLOCK, w), lambda b, g, ids: (b, 2)),
                pl.BlockSpec((nh, None, win, tq), lambda b, g, ids: (0, ids[g], 0, 0)),
            ],
            out_specs=pl.BlockSpec((tq, w), lambda b, g, ids: (b * n_pair + g, 0)),
        ),
        out_shape=jax.ShapeDtypeStruct((nb * n_tok, w), BF16),
        compiler_params=_params(("parallel", "arbitrary"), vm),
        name="neighborhood_attention",
    )(jnp.asarray(ids), p_lat, p_lat, p_lat, p_ctx, p_ctx, bias_tab)


def _diff_kernel(q_ref, k_ref, vt_ref, kx, vxt, lq1, lk1, lq2, lk2, g_ref, o_ref, m_ref, l_ref, acc_ref,
                 sa_ref, sb_ref, *, n_blk, tk, lam_init):
    m_ref[...] = jnp.full_like(m_ref, -jnp.inf)
    l_ref[...] = jnp.zeros_like(l_ref)
    acc_ref[...] = jnp.zeros_like(acc_ref)

    def scores(keys):
        return [_dot_nt(keys[:, t * HEAD_DIM:(t + 1) * HEAD_DIM], q_ref[:, t * HEAD_DIM:(t + 1) * HEAD_DIM])
                for t in range(2)]

    def lat_scores(j, dst):
        s = scores(k_ref[pl.ds(pl.multiple_of(j * tk, tk), tk), :])
        dst[0] = s[0]
        dst[1] = s[1]

    def accumulate(s, vt):
        for t in range(2):
            m_old = m_ref[t]
            m_new = jnp.maximum(m_old, jnp.max(s[t], axis=0, keepdims=True))
            alpha = jnp.exp2(m_old - m_new)
            p = jnp.exp2(s[t] - m_new)
            l_ref[t] = alpha * l_ref[t] + jnp.sum(p, axis=0, keepdims=True)
            acc_ref[t] = alpha * acc_ref[t] + _dot(vt, p.astype(BF16))
            m_ref[t] = m_new

    s_ctx = scores(kx[...].astype(BF16))
    lat_scores(0, sa_ref)
    accumulate(s_ctx, vxt[...])

    def body(jj, carry):
        j = 2 * jj
        lat_scores(j + 1, sb_ref)
        accumulate((sa_ref[0], sa_ref[1]), vt_ref[j])
        lat_scores(jnp.minimum(j + 2, n_blk - 1), sa_ref)
        accumulate((sb_ref[0], sb_ref[1]), vt_ref[j + 1])
        return carry

    lax.fori_loop(0, n_blk // 2, body, 0)

    lam = (jnp.exp(jnp.sum(lq1[...] * lk1[...], axis=-1, keepdims=True))
           - jnp.exp(jnp.sum(lq2[...] * lk2[...], axis=-1, keepdims=True)) + lam_init)
    o = acc_ref[0] / l_ref[0] - lam * (acc_ref[1] / l_ref[1])
    y = o * lax.rsqrt(jnp.mean(o * o, axis=0, keepdims=True) + NORM_EPS) * (g_ref[...] * (1.0 - lam_init))
    o_ref[...] = y.T.astype(o_ref.dtype)


def _diff(q_rope, k_rope, vt_lat, p_ctx, vt_ctx, kcol, lq1, lk1, lq2, lk2, norm_g, lam_init, nb, n_tok, nh,
          tq=512):
    nq = n_tok // tq
    n_blk, tk = vt_lat.shape[2], vt_lat.shape[4]
    w = 2 * HEAD_DIM
    vec = lambda: pl.BlockSpec((1, HEAD_DIM), lambda b, h, i: (0, 0))
    vm = (2 * (tq * w * 2 + 2 * n_tok * w * 2 + CTX_BLOCK * w * 6 + tq * w * 2)
          + 2 * tq * w * 4 + 2 * 2 * tk * tq * 6 + tq * w * 8)
    return pl.pallas_call(
        functools.partial(_diff_kernel, n_blk=n_blk, tk=tk, lam_init=lam_init),
        grid=(nb, nh, nq),
        in_specs=[
            pl.BlockSpec((tq, w), lambda b, h, i: (b * nq + i, h)),
            pl.BlockSpec((n_tok, w), lambda b, h, i: (b, h)),
            pl.BlockSpec((None, None, n_blk, w, tk), lambda b, h, i: (b, h, 0, 0, 0)),
            pl.BlockSpec((CTX_BLOCK, w), lambda b, h, i: (b, kcol // w + h)),
            pl.BlockSpec((None, None, w, CTX_BLOCK), lambda b, h, i: (b, h, 0, 0)),
            vec(), vec(), vec(), vec(),
            pl.BlockSpec((DIFF_DV, 1), lambda b, h, i: (0, 0)),
        ],
        out_specs=pl.BlockSpec((tq, w), lambda b, h, i: (b * nq + i, h)),
        out_shape=jax.ShapeDtypeStruct((nb * n_tok, nh * w), BF16),
        scratch_shapes=[
            pltpu.VMEM((2, 1, tq), F32),
            pltpu.VMEM((2, 1, tq), F32),
            pltpu.VMEM((2, DIFF_DV, tq), F32),
            pltpu.VMEM((2, tk, tq), F32),
            pltpu.VMEM((2, tk, tq), F32),
        ],
        compiler_params=_params(("parallel", "parallel", "arbitrary"), vm),
        name="diff_attention",
    )(q_rope, k_rope, vt_lat, p_ctx, vt_ctx, lq1.reshape(1, -1), lk1.reshape(1, -1), lq2.reshape(1, -1),
      lk2.reshape(1, -1), norm_g.reshape(-1, 1))


def _pad_cols(w, n):
    return jnp.pad(w, ((0, 0), (0, n - w.shape[1])))


def kernel(x, c, ctx, c_ctx, ada_w, ada_b, norm_mix_g, norm_ffn_g, w_out, ffn_w_gate, ffn_w_up, ffn_w_down,
           ev_w_in, gla_gate_up_f, gla_gate_bias_f, gla_gate_up_b, gla_gate_bias_b, gla_norm_g, swa_sink,
           od_w_in, na_rpb, diff_lq1, diff_lk1, diff_lq2, diff_lk2, diff_norm_g, final_norm_g):
    nb, n_tok, d = x.shape
    lc = ctx.shape[1]
    depth = ada_w.shape[0]
    f_hidden = ffn_w_gate.shape[2]
    half = d // 2
    assert lc == CTX_BLOCK and nb < 8 and depth == 2, "layout assumes a 256-token context, batch < 8, two layers"
    gla_heads = half // GLA_DV
    gla_qk = gla_heads * GLA_DK
    n_q = half // HEAD_DIM
    n_kv = n_q // SWA_GROUP
    diff_heads = half // (2 * HEAD_DIM)

    xs = x.reshape(nb * n_tok, d)
    cs = ctx.reshape(nb * lc, d)
    cos, sin = _rope_tables(n_tok)

    s8 = jnp.zeros((8, d), F32).at[:nb].set(c).at[nb].set(c_ctx)
    mod5 = _mod_vectors(s8, ada_w, ada_b).reshape(depth, 8, 6, 1, d)
    lat_row = lambda i, tm: (i * tm) // n_tok
    ctx_row = lambda i, tm: nb

    f_pad = -(-f_hidden // FFN_PAD) * FFN_PAD

    zcols = jnp.zeros((depth, d, f_pad - f_hidden), BF16)
    wg = jnp.concatenate([ffn_w_gate.astype(BF16), zcols], axis=2)
    wu = jnp.concatenate([ffn_w_up.astype(BF16), zcols], axis=2)
    wd = jnp.concatenate([ffn_w_down.astype(BF16), zcols.transpose(0, 2, 1)], axis=1)
    wo = w_out.astype(BF16)

    def ffn(stream, layer, mod_row):
        h2 = _norm_mod(stream, norm_ffn_g[layer], mod5, layer, 3, 4, mod_row)
        a = _matmul_swiglu(h2, wg, wu, layer)
        return _matmul_residual([a], wd, stream, mod5, layer, 5, mod_row, nk=4, tn=2 * COL_TILE)

    layer = 0
    w_in = ev_w_in[0]
    g0 = 2 * gla_qk + 2 * half
    w_main = jnp.concatenate([w_in[:, :g0], w_in[:, g0 + 2 * GLA_RANK:]], axis=1).astype(BF16)
    w_dn = _pad_cols(w_in[:, g0:g0 + 2 * GLA_RANK], LANE).astype(BF16)
    swa_q0 = g0
    swa_k0 = swa_q0 + half
    swa_v0 = swa_k0 + n_kv * HEAD_DIM
    up2 = jnp.zeros((2, LANE, gla_qk), F32)
    up2 = up2.at[0, :GLA_RANK].set(gla_gate_up_f[0]).at[1, GLA_RANK:2 * GLA_RANK].set(gla_gate_up_b[0]).astype(BF16)
    bias2 = jnp.stack([gla_gate_bias_f[0], gla_gate_bias_b[0]]).reshape(2, 1, gla_qk)

    h = _norm_mod(xs, norm_mix_g[layer], mod5, layer, 0, 1, lat_row)
    hc = _norm_mod(cs, norm_mix_g[layer], mod5, layer, 0, 1, ctx_row)
    p_lat = _matmul(h, w_main)
    p_ctx = _matmul(hc, w_main)
    dn_lat = _matmul(h, w_dn)
    dn_ctx = _matmul(hc, w_dn)

    o_lat, o_ctx = _gla(p_lat, p_ctx, dn_lat, dn_ctx, up2, bias2, nb, n_tok, gla_heads)
    ya = _gla_out(o_lat, p_lat, gla_norm_g[0], gla_heads)
    yac = _gla_out(o_ctx, p_ctx, gla_norm_g[0], gla_heads)

    sink = swa_sink[0].reshape(1, n_q)
    q_rope = _rope(p_lat, swa_q0, half, cos, sin, HEAD_DIM ** -0.5)
    k_rope = _rope(p_lat, swa_k0, n_kv * HEAD_DIM, cos, sin, 1.0)
    yb = _swa(sink, q_rope, k_rope, p_lat, p_ctx, swa_k0, swa_v0, nb, n_tok, n_kv)
    ybc = _ctx_sink(sink, p_ctx, swa_q0, swa_k0, swa_v0, nb, n_kv)

    xs = _matmul_residual([ya, yb], wo, xs, mod5, layer, 2, lat_row)
    cs = _matmul_residual([yac, ybc], wo, cs, mod5, layer, 2, ctx_row)
    xs = ffn(xs, layer, lat_row)
    cs = ffn(cs, layer, ctx_row)

    layer = 1
    w_in = od_w_in[0].astype(BF16)
    h = _norm_mod(xs, norm_mix_g[layer], mod5, layer, 0, 1, lat_row)
    hc = _norm_mod(cs, norm_mix_g[layer], mod5, layer, 0, 1, ctx_row)
    p_lat = _matmul(h, w_in, BF16)
    p_ctx = _matmul(hc, w_in, BF16)

    yn =_na(p_lat, p_ctx, na_rpb[0], nb, n_tok, n_q)

    dq0 = 3 * half
    dk0 = dq0 + half
    dv0 = dk0 + half
    lam_init = 0.8 - 0.6 * math.exp(-0.3 * layer)
    qd = _rope(p_lat, dq0, half, cos, sin, HEAD_DIM ** -0.5 * math.log2(math.e))
    kd = _rope(p_lat, dk0, half, cos, sin, 1.0)
    tk = DIFF_KEY_BLOCK
    vt_lat = p_lat[:, dv0:].astype(BF16).reshape(nb, n_tok // tk, tk, diff_heads, DIFF_DV).transpose(0, 3, 1, 4, 2)
    vt_ctx = p_ctx[:, dv0:].astype(BF16).reshape(nb, lc, diff_heads, DIFF_DV).transpose(0, 2, 3, 1)
    yd = _diff(qd, kd, vt_lat, p_ctx, vt_ctx, dk0, diff_lq1[0], diff_lk1[0], diff_lq2[0], diff_lk2[0],
               diff_norm_g[0], lam_init, nb, n_tok, diff_heads)

    xs = _matmul_residual([yn, yd], wo, xs, mod5, layer, 2, lat_row)
    xs = ffn(xs, layer, lat_row)
    return _final_norm(xs, final_norm_g).reshape(nb, n_tok, d)
```

```python
import functools
import math

import numpy as np
import jax
import jax.numpy as jnp
from jax import lax
from jax.experimental import pallas as pl
from jax.experimental.pallas import tpu as pltpu

HEAD_DIM = 128
GRID_W = 64
ROPE_THETA = 10000.0
NORM_EPS = 1e-6
GLA_DK = 128
GLA_DV = 256
GLA_RANK = 16
GLA_NORMALIZER = 16.0
GLA_CHUNK = 64
SWA_GROUP = 4
SWA_WINDOW = 128
SWA_BLOCK = 128
NA_ROWS = 8
NA_COLS = 16
NA_PAIR = 2
DIFF_DV = 2 * HEAD_DIM
DIFF_KEY_BLOCK = 1024
CTX_BLOCK = 256

V7X_VMEM_BUDGET = 60 * 1024 * 1024
LANE = 128
ROW_TILE = 1024
COL_TILE = 512
FFN_PAD = 1024

F32 = jnp.float32
BF16 = jnp.bfloat16


def _params(sem, vmem_bytes):
    limit = int(min(V7X_VMEM_BUDGET, vmem_bytes * 5 // 4 + (4 << 20)))
    return pltpu.CompilerParams(dimension_semantics=sem, vmem_limit_bytes=limit)


def _dot(a, b):
    return jnp.dot(a, b, preferred_element_type=F32)


def _dot_nt(a, b):
    return lax.dot_general(a, b, (((1,), (1,)), ((), ())), preferred_element_type=F32)


def _dot_tn(a, b):
    return lax.dot_general(a, b, (((0,), (0,)), ((), ())), preferred_element_type=F32)


def _silu(t):
    return t * jax.nn.sigmoid(t)


def _mod_kernel(s_ref, w_ref, b_ref, o_ref):
    a = _silu(s_ref[...]).astype(BF16)
    o_ref[...] = _dot(a, w_ref[...].astype(BF16)) + b_ref[...]


def _mod_vectors(s8, ada_w, ada_b):
    depth, d, n = ada_w.shape
    tn = COL_TILE
    return pl.pallas_call(
        _mod_kernel,
        grid=(depth, n // tn),
        in_specs=[
            pl.BlockSpec((8, d), lambda l, j: (0, 0)),
            pl.BlockSpec((None, d, tn), lambda l, j: (l, 0, j)),
            pl.BlockSpec((None, 1, tn), lambda l, j: (l, 0, j)),
        ],
        out_specs=pl.BlockSpec((None, 8, tn), lambda l, j: (l, 0, j)),
        out_shape=jax.ShapeDtypeStruct((depth, 8, n), F32),
        compiler_params=_params(("parallel", "parallel"), 2 * d * tn * 4 + d * tn * 2),
        name="adaln_mod",
    )(s8, ada_w, ada_b.reshape(depth, 1, n))


def _norm_mod_kernel(x_ref, g_ref, sh_ref, sc_ref, o_ref):
    x = x_ref[...]
    r = lax.rsqrt(jnp.mean(x * x, axis=-1, keepdims=True) + NORM_EPS)
    y = x * r * g_ref[...]
    o_ref[...] = (y * (1.0 + sc_ref[...]) + sh_ref[...]).astype(o_ref.dtype)


def _norm_mod(x2, g, mod5, layer, k_shift, k_scale, mod_row, tm=256):
    m, d = x2.shape
    return pl.pallas_call(
        _norm_mod_kernel,
        grid=(m // tm,),
        in_specs=[
            pl.BlockSpec((tm, d), lambda i: (i, 0)),
            pl.BlockSpec((1, d), lambda i: (0, 0)),
            pl.BlockSpec((None, None, None, 1, d), lambda i: (layer, mod_row(i, tm), k_shift, 0, 0)),
            pl.BlockSpec((None, None, None, 1, d), lambda i: (layer, mod_row(i, tm), k_scale, 0, 0)),
        ],
        out_specs=pl.BlockSpec((tm, d), lambda i: (i, 0)),
        out_shape=jax.ShapeDtypeStruct((m, d), BF16),
        compiler_params=_params(("parallel",), 2 * tm * d * 6),
        name="norm_mod",
    )(x2, g.reshape(1, d), mod5, mod5)


def _final_norm_kernel(x_ref, g_ref, o_ref):
    x = x_ref[...]
    r = lax.rsqrt(jnp.mean(x * x, axis=-1, keepdims=True) + NORM_EPS)
    o_ref[...] = x * r * g_ref[...]


def _final_norm(x2, g, tm=256):
    m, d = x2.shape
    return pl.pallas_call(
        _final_norm_kernel,
        grid=(m // tm,),
        in_specs=[pl.BlockSpec((tm, d), lambda i: (i, 0)), pl.BlockSpec((1, d), lambda i: (0, 0))],
        out_specs=pl.BlockSpec((tm, d), lambda i: (i, 0)),
        out_shape=jax.ShapeDtypeStruct((m, d), F32),
        compiler_params=_params(("parallel",), 2 * tm * d * 8),
        name="final_norm",
    )(x2, g.reshape(1, d))


def _col_tile(n):
    return next(t for t in range(COL_TILE, 0, -LANE) if n % t == 0)


def _mm_kernel(a_ref, w_ref, o_ref):
    o_ref[...] = _dot(a_ref[...], w_ref[...]).astype(o_ref.dtype)


def _matmul(a, w, out_dtype=F32):
    m, k = a.shape
    n = w.shape[1]
    tm = min(ROW_TILE, m)
    tn = _col_tile(n)
    ob = jnp.dtype(out_dtype).itemsize
    return pl.pallas_call(
        _mm_kernel,
        grid=(m // tm, n // tn),
        in_specs=[pl.BlockSpec((tm, k), lambda i, j: (i, 0)), pl.BlockSpec((k, tn), lambda i, j: (0, j))],
        out_specs=pl.BlockSpec((tm, tn), lambda i, j: (i, j)),
        out_shape=jax.ShapeDtypeStruct((m, n), out_dtype),
        compiler_params=_params(("parallel", "parallel"), 2 * (tm * k * 2 + k * tn * 2 + tm * tn * ob)),
        name="matmul",
    )(a, w)


def _mm_swiglu_kernel(a_ref, wg_ref, wu_ref, o_ref):
    a = a_ref[...]
    g = _dot(a, wg_ref[...])
    u = _dot(a, wu_ref[...])
    o_ref[...] = (_silu(g) * u).astype(o_ref.dtype)


def _matmul_swiglu(a, wg, wu, layer):
    m, k = a.shape
    n = wg.shape[2]
    tm = min(ROW_TILE, m)
    tn = COL_TILE
    return pl.pallas_call(
        _mm_swiglu_kernel,
        grid=(m // tm, n // tn),
        in_specs=[
            pl.BlockSpec((tm, k), lambda i, j: (i, 0)),
            pl.BlockSpec((None, k, tn), lambda i, j: (layer, 0, j)),
            pl.BlockSpec((None, k, tn), lambda i, j: (layer, 0, j)),
        ],
        out_specs=pl.BlockSpec((tm, tn), lambda i, j: (i, j)),
        out_shape=jax.ShapeDtypeStruct((m, n), BF16),
        compiler_params=_params(("parallel", "parallel"), 2 * (tm * k * 2 + 2 * k * tn * 2 + tm * tn * 2)),
        name="matmul_swiglu",
    )(a, wg, wu)


def _mm_res_kernel(*refs, nk, n_a):
    a_refs, (w_ref, res_ref, gate_ref, o_ref), scratch = refs[:n_a], refs[n_a:n_a + 4], refs[n_a + 4:]
    k0 = 0
    p = None
    for a_ref in a_refs:
        ka = a_ref.shape[1]
        d = _dot(a_ref[...], w_ref[k0:k0 + ka, :])
        p = d if p is None else p + d
        k0 += ka
    if nk == 1:
        o_ref[...] = res_ref[...] + gate_ref[...] * p
        return
    acc_ref, = scratch
    kk = pl.program_id(2)

    @pl.when(kk == 0)
    def _():
        acc_ref[...] = p

    @pl.when(jnp.logical_and(kk > 0, kk < nk - 1))
    def _():
        acc_ref[...] += p

    @pl.when(kk == nk - 1)
    def _():
        o_ref[...] = res_ref[...] + gate_ref[...] * (acc_ref[...] + p)


def _matmul_residual(a_slabs, w, res, mod5, layer, k_gate, mod_row, nk=1, tn=COL_TILE):
    m = a_slabs[0].shape[0]
    k, n = w.shape[1], w.shape[2]
    tm = min(ROW_TILE, m)
    tk = k // nk
    assert nk == 1 or len(a_slabs) == 1
    a_specs = [pl.BlockSpec((tm, tk if nk > 1 else a.shape[1]), lambda i, j, q: (i, q)) for a in a_slabs]
    return pl.pallas_call(
        functools.partial(_mm_res_kernel, nk=nk, n_a=len(a_slabs)),
        grid=(m // tm, n // tn, nk),
        in_specs=a_specs + [
            pl.BlockSpec((None, tk, tn), lambda i, j, q: (layer, q, j)),
            pl.BlockSpec((tm, tn), lambda i, j, q: (i, j)),
            pl.BlockSpec((None, None, None, 1, tn), lambda i, j, q: (layer, mod_row(i, tm), k_gate, 0, j)),
        ],
        out_specs=pl.BlockSpec((tm, tn), lambda i, j, q: (i, j)),
        out_shape=jax.ShapeDtypeStruct((m, n), F32),
        scratch_shapes=[pltpu.VMEM((tm, tn), F32)] if nk > 1 else [],
        compiler_params=_params(
            ("parallel", "parallel", "arbitrary"), 2 * (tm * tk * 2 + tk * tn * 2 + 2 * tm * tn * 4) + tm * tn * 4
        ),
        name="matmul_residual",
    )(*a_slabs, w, res, mod5)


def _cast_pad_cols_kernel(w_ref, o_ref):
    n = w_ref.shape[1]
    o_ref[:, :n] = w_ref[...].astype(o_ref.dtype)
    o_ref[:, n:] = jnp.zeros((o_ref.shape[0], o_ref.shape[1] - n), o_ref.dtype)


def _cast_pad_cols(w, n_pad, tr=256):
    depth, k, n = w.shape
    return pl.pallas_call(
        _cast_pad_cols_kernel,
        grid=(depth, k // tr),
        in_specs=[pl.BlockSpec((None, tr, n), lambda l, i: (l, i, 0))],
        out_specs=pl.BlockSpec((None, tr, n_pad), lambda l, i: (l, i, 0)),
        out_shape=jax.ShapeDtypeStruct((depth, k, n_pad), BF16),
        compiler_params=_params(("parallel", "parallel"), 2 * tr * (n * 4 + n_pad * 2)),
        name="cast_pad_cols",
    )(w)


def _cast_pad_rows_kernel(w_ref, o_ref, *, n_valid):
    i = pl.program_id(1)

    @pl.when(i < n_valid)
    def _():
        o_ref[...] = w_ref[...].astype(o_ref.dtype)

    @pl.when(i >= n_valid)
    def _():
        o_ref[...] = jnp.zeros_like(o_ref)


def _cast_pad_rows(w, k_pad, tr=256):
    depth, k, n = w.shape
    assert k % tr == 0 and k_pad % tr == 0
    n_valid = k // tr
    return pl.pallas_call(
        functools.partial(_cast_pad_rows_kernel, n_valid=n_valid),
        grid=(depth, k_pad // tr),
        in_specs=[pl.BlockSpec((None, tr, n), lambda l, i: (l, jnp.minimum(i, n_valid - 1), 0))],
        out_specs=pl.BlockSpec((None, tr, n), lambda l, i: (l, i, 0)),
        out_shape=jax.ShapeDtypeStruct((depth, k_pad, n), BF16),
        compiler_params=_params(("parallel", "parallel"), 2 * tr * n * 6),
        name="cast_pad_rows",
    )(w)


def _rope_tables(n_tok):
    quarter = HEAD_DIM // 4
    inv = 1.0 / (ROPE_THETA ** (jnp.arange(quarter, dtype=F32) / quarter))
    pos = jnp.arange(n_tok, dtype=jnp.int32)
    row = (pos // GRID_W).astype(F32)[:, None] * inv
    col = (pos % GRID_W).astype(F32)[:, None] * inv
    cos = jnp.concatenate([jnp.cos(row), jnp.cos(row), jnp.cos(col), jnp.cos(col)], axis=-1)
    sin = jnp.concatenate([-jnp.sin(row), jnp.sin(row), -jnp.sin(col), jnp.sin(col)], axis=-1)
    return cos, sin


def _rope_kernel(x_ref, cos_ref, sin_ref, o_ref, *, scale, n_heads):
    cos = cos_ref[...] * scale
    sin = sin_ref[...] * scale
    lane = lax.broadcasted_iota(jnp.int32, cos.shape, 1)
    first = (lane & (HEAD_DIM // 4)) == 0
    for h in range(n_heads):
        hs = slice(h * HEAD_DIM, (h + 1) * HEAD_DIM)
        x = x_ref[:, hs].astype(F32)
        partner = jnp.where(first, pltpu.roll(x, HEAD_DIM - HEAD_DIM // 4, 1), pltpu.roll(x, HEAD_DIM // 4, 1))
        o_ref[:, hs] = (x * cos + partner * sin).astype(o_ref.dtype)


def _rope(p, col0, width, cos, sin, scale, tr=256):
    m = p.shape[0]
    n_tok = cos.shape[0]
    tpb = n_tok // tr
    assert col0 % width == 0
    return pl.pallas_call(
        functools.partial(_rope_kernel, scale=scale, n_heads=width // HEAD_DIM),
        grid=(m // tr,),
        in_specs=[
            pl.BlockSpec((tr, width), lambda i: (i, col0 // width)),
            pl.BlockSpec((tr, HEAD_DIM), lambda i: (i % tpb, 0)),
            pl.BlockSpec((tr, HEAD_DIM), lambda i: (i % tpb, 0)),
        ],
        out_specs=pl.BlockSpec((tr, width), lambda i: (i, 0)),
        out_shape=jax.ShapeDtypeStruct((m, width), BF16),
        compiler_params=_params(("parallel",), 2 * tr * (width * 6 + HEAD_DIM * 8)),
        name="rope",
    )(p, cos, sin)


def _gla_kernel(ql, kl, vl, dl, qc, kc, vc, dc, up_ref, bias_ref, ol, oc, st_ref, *, nh, scale):
    d = pl.program_id(0)
    s = pl.program_id(2)
    is_ctx = s == 0
    fwd = d == 0
    n_chunk = CTX_BLOCK // GLA_CHUNK

    @pl.when(s == 0)
    def _():
        st_ref[...] = jnp.zeros_like(st_ref)

    tb = CTX_BLOCK

    def in_scan_order(x):
        rev = jnp.concatenate([x[(n_chunk - 1 - p) * GLA_CHUNK:(n_chunk - p) * GLA_CHUNK] for p in range(n_chunk)],
                              axis=0)
        return jnp.where(fwd, x, rev)

    def load(rc, rl, cols):
        return in_scan_order(jnp.where(is_ctx, rc[:, cols], rl[:, cols]))

    row = lax.broadcasted_iota(jnp.int32, (tb, tb), 0)
    col = lax.broadcasted_iota(jnp.int32, (tb, tb), 1)
    incl = jnp.logical_and(row // GLA_CHUNK == col // GLA_CHUNK, jnp.where(fwd, row - col, col - row) >= 0)
    row_chunk = lax.broadcasted_iota(jnp.int32, (tb, GLA_DK), 0) // GLA_CHUNK

    def per_chunk(x):
        return jnp.concatenate([jnp.where(row_chunk == p, x, 0.0) for p in range(n_chunk)], axis=1)

    z = _dot(load(dc, dl, slice(None)).astype(BF16), up_ref[...]) + bias_ref[...]
    g = (jnp.minimum(z, 0.0) - jnp.log(1.0 + jnp.exp(-jnp.abs(z)))) * (1.0 / GLA_NORMALIZER)
    tmat = jnp.where(incl, 1.0, 0.0).astype(BF16)
    g_hi = g.astype(BF16)
    g_rest = g - g_hi.astype(F32)
    g_mid = g_rest.astype(BF16)
    g_lo = (g_rest - g_mid.astype(F32)).astype(BF16)
    cum = _dot(tmat, g_hi) + _dot(tmat, g_mid) + _dot(tmat, g_lo)
    tot = [jnp.sum(g[p * GLA_CHUNK:(p + 1) * GLA_CHUNK], axis=0, keepdims=True) for p in range(n_chunk)]
    tot_rows = jnp.concatenate([jnp.broadcast_to(t, (GLA_CHUNK, t.shape[1])) for t in tot], axis=0)

    for h in range(nh):
        ks = slice(h * GLA_DK, (h + 1) * GLA_DK)
        vs = slice(h * GLA_DV, (h + 1) * GLA_DV)
        bh = cum[:, ks]
        q = load(qc, ql, ks)
        k = load(kc, kl, ks)
        v = load(vc, vl, vs).astype(BF16)
        qd = q * (scale * jnp.exp(bh))
        ki = (k * jnp.exp(-bh)).astype(BF16)
        ku = k * jnp.exp(tot_rows[:, ks] - bh)
        a = jnp.where(incl, _dot_nt(qd.astype(BF16), ki), 0.0)
        inc = _dot_tn(v, per_chunk(ku).astype(BF16))
        st = st_ref[h]
        entering = []
        for p in range(n_chunk):
            entering.append(st)
            st = st * jnp.exp(tot[p][:, ks]) + inc[:, p * GLA_DK:(p + 1) * GLA_DK]
        st_ref[h] = st
        o = (_dot(a.astype(BF16), v)
             + _dot_nt(per_chunk(qd).astype(BF16), jnp.concatenate(entering, axis=1).astype(BF16)))
        o = in_scan_order(o)

        @pl.when(is_ctx)
        def _():
            oc[:, vs] = o

        @pl.when(jnp.logical_not(is_ctx))
        def _():
            ol[:, vs] = o


def _gla(p_lat, p_ctx, dn_lat, dn_ctx, up2, bias2, nb, n_tok, nh):
    qk = nh * GLA_DK
    dv = nh * GLA_DV
    tb = CTX_BLOCK
    nbl = n_tok // tb

    def lat_blk(d, b, s):
        t = jnp.maximum(s - 1, 0)
        return b * nbl + jnp.where(d == 0, t, nbl - 1 - t)

    lat = lambda w, c: pl.BlockSpec((tb, w), lambda d, b, s: (lat_blk(d, b, s), c))
    ctx = lambda w, c: pl.BlockSpec((tb, w), lambda d, b, s: (b, c))
    vm = 2 * (2 * (2 * tb * qk + tb * dv + tb * LANE) * 4 + 2 * tb * dv * 4) + nh * GLA_DV * GLA_DK * 4
    return pl.pallas_call(
        functools.partial(_gla_kernel, nh=nh, scale=GLA_DK ** -0.5),
        grid=(2, nb, nbl + 1),
        in_specs=[
            lat(qk, 0), lat(qk, 1), lat(dv, 1), lat(LANE, 0),
            ctx(qk, 0), ctx(qk, 1), ctx(dv, 1), ctx(LANE, 0),
            pl.BlockSpec((None, LANE, qk), lambda d, b, s: (d, 0, 0)),
            pl.BlockSpec((None, 1, qk), lambda d, b, s: (d, 0, 0)),
        ],
        out_specs=[
            pl.BlockSpec((None, tb, dv), lambda d, b, s: (d, lat_blk(d, b, s), 0)),
            pl.BlockSpec((None, tb, dv), lambda d, b, s: (d, b, 0)),
        ],
        out_shape=[
            jax.ShapeDtypeStruct((2, nb * n_tok, dv), F32),
            jax.ShapeDtypeStruct((2, nb * tb, dv), F32),
        ],
        scratch_shapes=[pltpu.VMEM((nh, GLA_DV, GLA_DK), F32)],
        compiler_params=_params(("parallel", "parallel", "arbitrary"), vm),
        name="gla_scan",
    )(p_lat, p_lat, p_lat, dn_lat, p_ctx, p_ctx, p_ctx, dn_ctx, up2, bias2)


def _gla_out_kernel(o_ref, r_ref, g_ref, y_ref, *, nh):
    for h in range(nh):
        sl = slice(h * GLA_DV, (h + 1) * GLA_DV)
        o = o_ref[0, :, sl] + o_ref[1, :, sl]
        y = o * lax.rsqrt(jnp.mean(o * o, axis=-1, keepdims=True) + NORM_EPS) * g_ref[...]
        y_ref[:, sl] = (y * _silu(r_ref[:, sl])).astype(y_ref.dtype)


def _gla_out(o2, p, norm_g, nh, tb=256):
    _, m, dv = o2.shape
    return pl.pallas_call(
        functools.partial(_gla_out_kernel, nh=nh),
        grid=(m // tb,),
        in_specs=[
            pl.BlockSpec((2, tb, dv), lambda i: (0, i, 0)),
            pl.BlockSpec((tb, dv), lambda i: (i, 2)),
            pl.BlockSpec((1, GLA_DV), lambda i: (0, 0)),
        ],
        out_specs=pl.BlockSpec((tb, dv), lambda i: (i, 0)),
        out_shape=jax.ShapeDtypeStruct((m, dv), BF16),
        compiler_params=_params(("parallel",), 2 * tb * dv * 14),
        name="gla_out",
    )(o2, p, norm_g.reshape(1, GLA_DV))


def _swa_kernel(sink_ref, q_ref, kp, kc, kn, vp, vc, vn, kx, vx, o_ref, *, n_blk):
    hkv = pl.program_id(1)
    i = pl.program_id(2)
    kwin = jnp.concatenate([kp[...], kc[...], kn[...]], axis=0)
    vwin = jnp.concatenate([vp[...], vc[...], vn[...]], axis=0).astype(BF16)
    kctx = kx[...].astype(BF16)
    vctx = vx[...].astype(BF16)
    span = SWA_BLOCK + 2 * SWA_WINDOW
    gq = SWA_GROUP * SWA_BLOCK
    qs = jnp.concatenate([q_ref[:, g * HEAD_DIM:(g + 1) * HEAD_DIM] for g in range(SWA_GROUP)], axis=0)
    kj = lax.broadcasted_iota(jnp.int32, (span, gq), 0)
    lane = lax.broadcasted_iota(jnp.int32, (span, gq), 1)
    qi = lane & (SWA_BLOCK - 1)
    key_pos = (i - 1) * SWA_BLOCK + kj
    ok = jnp.logical_and(jnp.abs(qi + SWA_WINDOW - kj) <= SWA_WINDOW,
                         jnp.logical_and(key_pos >= 0, key_pos < n_blk * SWA_BLOCK))
    s_win = jnp.where(ok, _dot_nt(kwin, qs), -jnp.inf)
    s_ctx = _dot_nt(kctx, qs)
    head = lax.broadcasted_iota(jnp.int32, (1, gq), 1) // SWA_BLOCK
    sink = jnp.zeros((1, gq), F32)
    for g in range(SWA_GROUP):
        sink = jnp.where(head == g, sink_ref[0, hkv * SWA_GROUP + g], sink)
    m = jnp.maximum(jnp.maximum(jnp.max(s_win, axis=0, keepdims=True), jnp.max(s_ctx, axis=0, keepdims=True)), sink)
    p_win = jnp.exp(s_win - m)
    p_ctx = jnp.exp(s_ctx - m)
    den = jnp.sum(p_win, axis=0, keepdims=True) + jnp.sum(p_ctx, axis=0, keepdims=True) + jnp.exp(sink - m)
    ot = (_dot_tn(vctx, p_ctx.astype(BF16)) + _dot_tn(vwin, p_win.astype(BF16))) / den
    for g in range(SWA_GROUP):
        hs = slice(g * HEAD_DIM, (g + 1) * HEAD_DIM)
        o_ref[:, hs] = ot[:, g * SWA_BLOCK:(g + 1) * SWA_BLOCK].T.astype(o_ref.dtype)


def _swa(sink, q_rope, k_rope, p_lat, p_ctx, kcol, vcol, nb, n_tok, n_kv):
    n_blk = n_tok // SWA_BLOCK
    tb = SWA_BLOCK
    gw = SWA_GROUP * HEAD_DIM
    kc0 = kcol // HEAD_DIM
    vc0 = vcol // HEAD_DIM
    prev = lambda b, h, i: b * n_blk + jnp.maximum(i - 1, 0)
    cur = lambda b, h, i: b * n_blk + i
    nxt = lambda b, h, i: b * n_blk + jnp.minimum(i + 1, n_blk - 1)
    kspec = lambda f: pl.BlockSpec((tb, HEAD_DIM), lambda b, h, i: (f(b, h, i), h))
    vspec = lambda f: pl.BlockSpec((tb, HEAD_DIM), lambda b, h, i: (f(b, h, i), vc0 + h))
    return pl.pallas_call(
        functools.partial(_swa_kernel, n_blk=n_blk),
        grid=(nb, n_kv, n_blk),
        in_specs=[
            pl.BlockSpec(memory_space=pltpu.SMEM),
            pl.BlockSpec((tb, gw), lambda b, h, i: (cur(b, h, i), h)),
            kspec(prev), kspec(cur), kspec(nxt),
            vspec(prev), vspec(cur), vspec(nxt),
            pl.BlockSpec((CTX_BLOCK, HEAD_DIM), lambda b, h, i: (b, kc0 + h)),
            pl.BlockSpec((CTX_BLOCK, HEAD_DIM), lambda b, h, i: (b, vc0 + h)),
        ],
        out_specs=pl.BlockSpec((tb, gw), lambda b, h, i: (cur(b, h, i), h)),
        out_shape=jax.ShapeDtypeStruct((nb * n_tok, n_kv * gw), BF16),
        compiler_params=_params(("parallel", "parallel", "parallel"), 8 << 20),
        name="swa",
    )(sink, q_rope, k_rope, k_rope, k_rope, p_lat, p_lat, p_lat, p_ctx, p_ctx)


def _ctx_sink_kernel(sink_ref, q_ref, kx, vx, o_ref, *, scale):
    hkv = pl.program_id(1)
    kctx = kx[...].astype(BF16)
    vctx = vx[...].astype(BF16)
    for g in range(SWA_GROUP):
        hs = slice(g * HEAD_DIM, (g + 1) * HEAD_DIM)
        q = (q_ref[:, hs].astype(F32) * scale).astype(BF16)
        s = _dot_nt(q, kctx)
        sink = sink_ref[0, hkv * SWA_GROUP + g]
        m = jnp.maximum(jnp.max(s, axis=-1, keepdims=True), sink)
        p = jnp.exp(s - m)
        den = jnp.sum(p, axis=-1, keepdims=True) + jnp.exp(sink - m)
        o_ref[:, hs] = (_dot(p.astype(BF16), vctx) / den).astype(o_ref.dtype)


def _ctx_sink(sink, p_ctx, qcol, kcol, vcol, nb, n_kv):
    gw = SWA_GROUP * HEAD_DIM
    return pl.pallas_call(
        functools.partial(_ctx_sink_kernel, scale=HEAD_DIM ** -0.5),
        grid=(nb, n_kv),
        in_specs=[
            pl.BlockSpec(memory_space=pltpu.SMEM),
            pl.BlockSpec((CTX_BLOCK, gw), lambda b, h: (b, qcol // gw + h)),
            pl.BlockSpec((CTX_BLOCK, HEAD_DIM), lambda b, h: (b, kcol // HEAD_DIM + h)),
            pl.BlockSpec((CTX_BLOCK, HEAD_DIM), lambda b, h: (b, vcol // HEAD_DIM + h)),
        ],
        out_specs=pl.BlockSpec((CTX_BLOCK, gw), lambda b, h: (b, h)),
        out_shape=jax.ShapeDtypeStruct((nb * CTX_BLOCK, n_kv * gw), BF16),
        compiler_params=_params(("parallel", "parallel"), 4 << 20),
        name="ctx_sink_attention",
    )(sink, p_ctx, p_ctx, p_ctx)


def _na_bias_table(rpb, rows):
    qcol = np.arange(GRID_W)[:, None]
    kcol = np.arange(GRID_W)[None, :]
    col_start = np.clip(qcol - NA_COLS // 2, 0, GRID_W - NA_COLS)
    col_ok = (kcol >= col_start) & (kcol < col_start + NA_COLS)
    dc = np.clip(kcol - qcol + NA_COLS - 1, 0, 2 * NA_COLS - 2)
    n_dc = 2 * NA_COLS - 1
    onehot = (dc[None] == np.arange(n_dc)[:, None, None]).astype(np.float32)
    cols = jnp.einsum('hrd,dqk->hrqk', rpb.astype(F32), onehot, precision=lax.Precision.HIGHEST)
    cols = jnp.where(col_ok[None, None], cols, -jnp.inf).transpose(0, 1, 3, 2)
    masked = jnp.full(cols.shape[:1] + cols.shape[2:], -jnp.inf, F32)

    span = NA_ROWS + NA_PAIR - 1
    patterns = []
    for r0 in range(0, rows, NA_PAIR):
        rs = [min(max(r0 + i - NA_ROWS // 2, 0), rows - NA_ROWS) for i in range(NA_PAIR)]
        u = min(rs[0], rows - span)
        patterns.append(tuple((r0 + i - rs[i], rs[i] - u) for i in range(NA_PAIR)))
    variants = sorted(set(patterns), key=patterns.index)
    blocks = []
    for pat in variants:
        per_query_row = []
        for delta, off in pat:
            krows = [cols[:, kr - off - delta + NA_ROWS - 1] if 0 <= kr - off < NA_ROWS else masked
                     for kr in range(span)]
            per_query_row.append(jnp.stack(krows, axis=1))
        blocks.append(jnp.concatenate(per_query_row, axis=-1).reshape(rpb.shape[0], span * GRID_W, NA_PAIR * GRID_W))
    ids = np.array([variants.index(p) for p in patterns], dtype=np.int32)
    return jnp.stack(blocks, axis=1), ids


def _na_kernel(ids_ref, q_ref, k_ref, v_ref, kx, vx, bias_ref, o_ref, *, nh, scale):
    for h in range(nh):
        hs = slice(h * HEAD_DIM, (h + 1) * HEAD_DIM)
        q = (q_ref[:, hs].astype(F32) * scale).astype(BF16)
        s_nb = _dot_nt(k_ref[:, hs].astype(BF16), q) + bias_ref[h]
        s_ctx = _dot_nt(kx[:, hs].astype(BF16), q)
        m = jnp.maximum(jnp.max(s_nb, axis=0, keepdims=True), jnp.max(s_ctx, axis=0, keepdims=True))
        p_nb = jnp.exp(s_nb - m)
        p_ctx = jnp.exp(s_ctx - m)
        den = jnp.sum(p_nb, axis=0, keepdims=True) + jnp.sum(p_ctx, axis=0, keepdims=True)
        ot = (_dot_tn(vx[:, hs].astype(BF16), p_ctx.astype(BF16))
              + _dot_tn(v_ref[:, hs].astype(BF16), p_nb.astype(BF16)))
        o_ref[:, hs] = (ot / den).T.astype(o_ref.dtype)


def _na(p_lat, p_ctx, rpb, nb, n_tok, nh):
    rows = n_tok // GRID_W
    w = nh * HEAD_DIM
    span = NA_ROWS + NA_PAIR - 1
    win = span * GRID_W
    tq = NA_PAIR * GRID_W
    n_pair = rows // NA_PAIR
    bias_tab, ids = _na_bias_table(rpb, rows)
    u = lambda g: jnp.clip(g * NA_PAIR - NA_ROWS // 2, 0, rows - span)
    kv = lambda c: pl.BlockSpec((pl.Element(win), pl.Element(w)),
                                lambda b, g, ids: ((b * rows + u(g)) * GRID_W, c * w))
    vm = 2 * (tq * w * 4 + 2 * win * w * 4 + 2 * CTX_BLOCK * w * 4 + nh * tq * win * 4 + tq * w * 2)
    return pl.pallas_call(
        functools.partial(_na_kernel, nh=nh, scale=HEAD_DIM ** -0.5),
        grid_spec=pltpu.PrefetchScalarGridSpec(
            num_scalar_prefetch=1,
            grid=(nb, n_pair),
            in_specs=[
                pl.BlockSpec((tq, w), lambda b, g, ids: (b * n_pair + g, 0)),
                kv(1), kv(2),
                pl.BlockSpec((CTX_BLOCK, w), lambda b, g, ids: (b, 1)),
                pl.BlockSpec((CTX_BLOCK, w), lambda b, g, ids: (b, 2)),
                pl.BlockSpec((nh, None, win, tq), lambda b, g, ids: (0, ids[g], 0, 0)),
            ],
            out_specs=pl.BlockSpec((tq, w), lambda b, g, ids: (b * n_pair + g, 0)),
        ),
        out_shape=jax.ShapeDtypeStruct((nb * n_tok, w), BF16),
        compiler_params=_params(("parallel", "arbitrary"), vm),
        name="neighborhood_attention",
    )(jnp.asarray(ids), p_lat, p_lat, p_lat, p_ctx, p_ctx, bias_tab)


def _diff_kernel(q_ref, k_ref, vt_ref, kx, vxt, lq1, lk1, lq2, lk2, g_ref, o_ref, m_ref, l_ref, acc_ref,
                 sa_ref, sb_ref, pa_ref, pb_ref, pc_ref, aa_ref, ab_ref, ac_ref, *, n_blk, tk, lam_init):
    m_ref[...] = jnp.full_like(m_ref, -jnp.inf)
    l_ref[...] = jnp.zeros_like(l_ref)
    acc_ref[...] = jnp.zeros_like(acc_ref)

    def scores(keys):
        return [_dot_nt(keys[:, t * HEAD_DIM:(t + 1) * HEAD_DIM], q_ref[:, t * HEAD_DIM:(t + 1) * HEAD_DIM])
                for t in range(2)]

    def lat_scores(j, dst):
        s = scores(k_ref[pl.ds(pl.multiple_of(j * tk, tk), tk), :])
        dst[0] = s[0]
        dst[1] = s[1]

    def softmax(s, p_dst, a_dst):
        for t in range(2):
            m_old = m_ref[t]
            m_new = jnp.maximum(m_old, jnp.max(s[t], axis=0, keepdims=True))
            alpha = jnp.exp2(m_old - m_new)
            p = jnp.exp2(s[t] - m_new)
            l_ref[t] = alpha * l_ref[t] + jnp.sum(p, axis=0, keepdims=True)
            m_ref[t] = m_new
            p_dst[t] = p.astype(BF16)
            a_dst[t] = alpha

    def add_values(p_src, a_src, vt):
        for t in range(2):
            acc_ref[t] = a_src[t] * acc_ref[t] + _dot(vt, p_src[t])

    def stage(s_cur, p_cur, a_cur, next_block, s_next, prev):
        if next_block is not None:
            lat_scores(next_block, s_next)
        add_values(*prev)
        softmax((s_cur[0], s_cur[1]), p_cur, a_cur)

    s_ctx = scores(kx[...].astype(BF16))
    lat_scores(0, sa_ref)
    softmax(s_ctx, pc_ref, ac_ref)
    stage(sa_ref, pa_ref, aa_ref, 1, sb_ref, (pc_ref, ac_ref, vxt[...]))

    def body(jj, carry):
        j = 2 * jj + 1
        stage(sb_ref, pb_ref, ab_ref, j + 1, sa_ref, (pa_ref, aa_ref, vt_ref[j - 1]))
        stage(sa_ref, pa_ref, aa_ref, j + 2, sb_ref, (pb_ref, ab_ref, vt_ref[j]))
        return carry

    lax.fori_loop(0, (n_blk - 2) // 2, body, 0)
    stage(sb_ref, pb_ref, ab_ref, None, None, (pa_ref, aa_ref, vt_ref[n_blk - 2]))
    add_values(pb_ref, ab_ref, vt_ref[n_blk - 1])

    lam = (jnp.exp(jnp.sum(lq1[...] * lk1[...], axis=-1, keepdims=True))
           - jnp.exp(jnp.sum(lq2[...] * lk2[...], axis=-1, keepdims=True)) + lam_init)
    o = acc_ref[0] / l_ref[0] - lam * (acc_ref[1] / l_ref[1])
    y = o * lax.rsqrt(jnp.mean(o * o, axis=0, keepdims=True) + NORM_EPS) * (g_ref[...] * (1.0 - lam_init))
    o_ref[...] = y.T.astype(o_ref.dtype)


def _diff(q_rope, k_rope, vt_lat, p_ctx, vt_ctx, kcol, lq1, lk1, lq2, lk2, norm_g, lam_init, nb, n_tok, nh,
          tq=512):
    nq = n_tok // tq
    n_blk, tk = vt_lat.shape[2], vt_lat.shape[4]
    w = 2 * HEAD_DIM
    vec = lambda: pl.BlockSpec((1, HEAD_DIM), lambda b, h, i: (0, 0))
    vm = (2 * (tq * w * 2 + 2 * n_tok * w * 2 + CTX_BLOCK * w * 6 + tq * w * 2)
          + 2 * tq * w * 4 + 2 * 2 * tk * tq * 6 + tq * w * 8)
    return pl.pallas_call(
        functools.partial(_diff_kernel, n_blk=n_blk, tk=tk, lam_init=lam_init),
        grid=(nb, nh, nq),
        in_specs=[
            pl.BlockSpec((tq, w), lambda b, h, i: (b * nq + i, h)),
            pl.BlockSpec((n_tok, w), lambda b, h, i: (b, h)),
            pl.BlockSpec((None, None, n_blk, w, tk), lambda b, h, i: (b, h, 0, 0, 0)),
            pl.BlockSpec((CTX_BLOCK, w), lambda b, h, i: (b, kcol // w + h)),
            pl.BlockSpec((None, None, w, CTX_BLOCK), lambda b, h, i: (b, h, 0, 0)),
            vec(), vec(), vec(), vec(),
            pl.BlockSpec((DIFF_DV, 1), lambda b, h, i: (0, 0)),
        ],
        out_specs=pl.BlockSpec((tq, w), lambda b, h, i: (b * nq + i, h)),
        out_shape=jax.ShapeDtypeStruct((nb * n_tok, nh * w), BF16),
        scratch_shapes=[
            pltpu.VMEM((2, 1, tq), F32),
            pltpu.VMEM((2, 1, tq), F32),
            pltpu.VMEM((2, DIFF_DV, tq), F32),
            pltpu.VMEM((2, tk, tq), F32),
            pltpu.VMEM((2, tk, tq), F32),
            pltpu.VMEM((2, tk, tq), BF16),
            pltpu.VMEM((2, tk, tq), BF16),
            pltpu.VMEM((2, CTX_BLOCK, tq), BF16),
            pltpu.VMEM((2, 1, tq), F32),
            pltpu.VMEM((2, 1, tq), F32),
            pltpu.VMEM((2, 1, tq), F32),
        ],
        compiler_params=_params(("parallel", "parallel", "arbitrary"), vm),
        name="diff_attention",
    )(q_rope, k_rope, vt_lat, p_ctx, vt_ctx, lq1.reshape(1, -1), lk1.reshape(1, -1), lq2.reshape(1, -1),
      lk2.reshape(1, -1), norm_g.reshape(-1, 1))


def _pad_cols(w, n):
    return jnp.pad(w, ((0, 0), (0, n - w.shape[1])))


def kernel(x, c, ctx, c_ctx, ada_w, ada_b, norm_mix_g, norm_ffn_g, w_out, ffn_w_gate, ffn_w_up, ffn_w_down,
           ev_w_in, gla_gate_up_f, gla_gate_bias_f, gla_gate_up_b, gla_gate_bias_b, gla_norm_g, swa_sink,
           od_w_in, na_rpb, diff_lq1, diff_lk1, diff_lq2, diff_lk2, diff_norm_g, final_norm_g):
    nb, n_tok, d = x.shape
    lc = ctx.shape[1]
    depth = ada_w.shape[0]
    f_hidden = ffn_w_gate.shape[2]
    half = d // 2
    assert lc == CTX_BLOCK and nb < 8 and depth == 2, "layout assumes a 256-token context, batch < 8, two layers"
    gla_heads = half // GLA_DV
    gla_qk = gla_heads * GLA_DK
    n_q = half // HEAD_DIM
    n_kv = n_q // SWA_GROUP
    diff_heads = half // (2 * HEAD_DIM)

    xs = x.reshape(nb * n_tok, d)
    cs = ctx.reshape(nb * lc, d)
    cos, sin = _rope_tables(n_tok)

    s8 = jnp.zeros((8, d), F32).at[:nb].set(c).at[nb].set(c_ctx)
    mod5 = _mod_vectors(s8, ada_w, ada_b).reshape(depth, 8, 6, 1, d)
    lat_row = lambda i, tm: (i * tm) // n_tok
    ctx_row = lambda i, tm: nb

    f_pad = -(-f_hidden // FFN_PAD) * FFN_PAD

    wg = _cast_pad_cols(ffn_w_gate, f_pad)
    wu = _cast_pad_cols(ffn_w_up, f_pad)
    wd = _cast_pad_rows(ffn_w_down, f_pad)
    wo = w_out.astype(BF16)

    def ffn(stream, layer, mod_row):
        h2 = _norm_mod(stream, norm_ffn_g[layer], mod5, layer, 3, 4, mod_row)
        a = _matmul_swiglu(h2, wg, wu, layer)
        return _matmul_residual([a], wd, stream, mod5, layer, 5, mod_row, nk=4, tn=2 * COL_TILE)

    layer = 0
    w_in = ev_w_in[0]
    g0 = 2 * gla_qk + 2 * half
    w_main = jnp.concatenate([w_in[:, :g0], w_in[:, g0 + 2 * GLA_RANK:]], axis=1).astype(BF16)
    w_dn = _pad_cols(w_in[:, g0:g0 + 2 * GLA_RANK], LANE).astype(BF16)
    swa_q0 = g0
    swa_k0 = swa_q0 + half
    swa_v0 = swa_k0 + n_kv * HEAD_DIM
    up2 = jnp.zeros((2, LANE, gla_qk), F32)
    up2 = up2.at[0, :GLA_RANK].set(gla_gate_up_f[0]).at[1, GLA_RANK:2 * GLA_RANK].set(gla_gate_up_b[0]).astype(BF16)
    bias2 = jnp.stack([gla_gate_bias_f[0], gla_gate_bias_b[0]]).reshape(2, 1, gla_qk)

    h = _norm_mod(xs, norm_mix_g[layer], mod5, layer, 0, 1, lat_row)
    hc = _norm_mod(cs, norm_mix_g[layer], mod5, layer, 0, 1, ctx_row)
    p_lat = _matmul(h, w_main)
    p_ctx = _matmul(hc, w_main)
    dn_lat = _matmul(h, w_dn)
    dn_ctx = _matmul(hc, w_dn)

    o_lat, o_ctx = _gla(p_lat, p_ctx, dn_lat, dn_ctx, up2, bias2, nb, n_tok, gla_heads)
    ya = _gla_out(o_lat, p_lat, gla_norm_g[0], gla_heads)
    yac = _gla_out(o_ctx, p_ctx, gla_norm_g[0], gla_heads)

    sink = swa_sink[0].reshape(1, n_q)
    q_rope = _rope(p_lat, swa_q0, half, cos, sin, HEAD_DIM ** -0.5)
    k_rope = _rope(p_lat, swa_k0, n_kv * HEAD_DIM, cos, sin, 1.0)
    yb = _swa(sink, q_rope, k_rope, p_lat, p_ctx, swa_k0, swa_v0, nb, n_tok, n_kv)
    ybc = _ctx_sink(sink, p_ctx, swa_q0, swa_k0, swa_v0, nb, n_kv)

    xs = _matmul_residual([ya, yb], wo, xs, mod5, layer, 2, lat_row)
    cs = _matmul_residual([yac, ybc], wo, cs, mod5, layer, 2, ctx_row)
    xs = ffn(xs, layer, lat_row)
    cs = ffn(cs, layer, ctx_row)

    layer = 1
    w_in = od_w_in[0].astype(BF16)
    h = _norm_mod(xs, norm_mix_g[layer], mod5, layer, 0, 1, lat_row)
    hc = _norm_mod(cs, norm_mix_g[layer], mod5, layer, 0, 1, ctx_row)
    p_lat = _matmul(h, w_in, BF16)
    p_ctx = _matmul(hc, w_in, BF16)

    yn =_na(p_lat, p_ctx, na_rpb[0], nb, n_tok, n_q)

    dq0 = 3 * half
    dk0 = dq0 + half
    dv0 = dk0 + half
    lam_init = 0.8 - 0.6 * math.exp(-0.3 * layer)
    qd = _rope(p_lat, dq0, half, cos, sin, HEAD_DIM ** -0.5 * math.log2(math.e))
    kd = _rope(p_lat, dk0, half, cos, sin, 1.0)
    tk = DIFF_KEY_BLOCK
    vt_lat = p_lat[:, dv0:].astype(BF16).reshape(nb, n_tok // tk, tk, diff_heads, DIFF_DV).transpose(0, 3, 1, 4, 2)
    vt_ctx = p_ctx[:, dv0:].astype(BF16).reshape(nb, lc, diff_heads, DIFF_DV).transpose(0, 2, 3, 1)
    yd = _diff(qd, kd, vt_lat, p_ctx, vt_ctx, dk0, diff_lq1[0], diff_lk1[0], diff_lq2[0], diff_lk2[0],
               diff_norm_g[0], lam_init, nb, n_tok, diff_heads)

    xs = _matmul_residual([yn, yd], wo, xs, mod5, layer, 2, lat_row)
    xs = ffn(xs, layer, lat_row)
    return _final_norm(xs, final_norm_g).reshape(nb, n_tok, d)
```

```python
import functools
import math

import numpy as np
import jax
import jax.numpy as jnp
from jax import lax
from jax.experimental import pallas as pl
from jax.experimental.pallas import tpu as pltpu

HEAD_DIM = 128
GRID_W = 64
ROPE_THETA = 10000.0
NORM_EPS = 1e-6
GLA_DK = 128
GLA_DV = 256
GLA_RANK = 16
GLA_NORMALIZER = 16.0
GLA_CHUNK = 64
SWA_GROUP = 4
SWA_WINDOW = 128
SWA_BLOCK = 128
NA_ROWS = 8
NA_COLS = 16
NA_PAIR = 4
DIFF_DV = 2 * HEAD_DIM
DIFF_KEY_BLOCK = 2048
CTX_BLOCK = 256

V7X_VMEM_BUDGET = 60 * 1024 * 1024
LANE = 128
ROW_TILE = 1024
COL_TILE = 512
FFN_PAD = 1024

F32 = jnp.float32
BF16 = jnp.bfloat16


def _params(sem, vmem_bytes):
    limit = int(min(V7X_VMEM_BUDGET, vmem_bytes * 5 // 4 + (4 << 20)))
    return pltpu.CompilerParams(dimension_semantics=sem, vmem_limit_bytes=limit)


def _dot(a, b):
    return jnp.dot(a, b, preferred_element_type=F32)


def _dot_nt(a, b):
    return lax.dot_general(a, b, (((1,), (1,)), ((), ())), preferred_element_type=F32)


def _dot_tn(a, b):
    return lax.dot_general(a, b, (((0,), (0,)), ((), ())), preferred_element_type=F32)


def _silu(t):
    return t * jax.nn.sigmoid(t)


def _mod_kernel(s_ref, w_ref, b_ref, o_ref):
    a = _silu(s_ref[...]).astype(BF16)
    o_ref[...] = _dot(a, w_ref[...].astype(BF16)) + b_ref[...]


def _mod_vectors(s8, ada_w, ada_b):
    depth, d, n = ada_w.shape
    tn = COL_TILE
    return pl.pallas_call(
        _mod_kernel,
        grid=(depth, n // tn),
        in_specs=[
            pl.BlockSpec((8, d), lambda l, j: (0, 0)),
            pl.BlockSpec((None, d, tn), lambda l, j: (l, 0, j)),
            pl.BlockSpec((None, 1, tn), lambda l, j: (l, 0, j)),
        ],
        out_specs=pl.BlockSpec((None, 8, tn), lambda l, j: (l, 0, j)),
        out_shape=jax.ShapeDtypeStruct((depth, 8, n), F32),
        compiler_params=_params(("parallel", "parallel"), 2 * d * tn * 4 + d * tn * 2),
        name="adaln_mod",
    )(s8, ada_w, ada_b.reshape(depth, 1, n))


def _norm_mod_kernel(x_ref, g_ref, sh_ref, sc_ref, o_ref):
    x = x_ref[...]
    r = lax.rsqrt(jnp.mean(x * x, axis=-1, keepdims=True) + NORM_EPS)
    y = x * r * g_ref[...]
    o_ref[...] = (y * (1.0 + sc_ref[...]) + sh_ref[...]).astype(o_ref.dtype)


def _norm_mod(x2, g, mod5, layer, k_shift, k_scale, mod_row, tm=256):
    m, d = x2.shape
    return pl.pallas_call(
        _norm_mod_kernel,
        grid=(m // tm,),
        in_specs=[
            pl.BlockSpec((tm, d), lambda i: (i, 0)),
            pl.BlockSpec((1, d), lambda i: (0, 0)),
            pl.BlockSpec((None, None, None, 1, d), lambda i: (layer, mod_row(i, tm), k_shift, 0, 0)),
            pl.BlockSpec((None, None, None, 1, d), lambda i: (layer, mod_row(i, tm), k_scale, 0, 0)),
        ],
        out_specs=pl.BlockSpec((tm, d), lambda i: (i, 0)),
        out_shape=jax.ShapeDtypeStruct((m, d), BF16),
        compiler_params=_params(("parallel",), 2 * tm * d * 6),
        name="norm_mod",
    )(x2, g.reshape(1, d), mod5, mod5)


def _final_norm_kernel(x_ref, g_ref, o_ref):
    x = x_ref[...]
    r = lax.rsqrt(jnp.mean(x * x, axis=-1, keepdims=True) + NORM_EPS)
    o_ref[...] = x * r * g_ref[...]


def _final_norm(x2, g, tm=256):
    m, d = x2.shape
    return pl.pallas_call(
        _final_norm_kernel,
        grid=(m // tm,),
        in_specs=[pl.BlockSpec((tm, d), lambda i: (i, 0)), pl.BlockSpec((1, d), lambda i: (0, 0))],
        out_specs=pl.BlockSpec((tm, d), lambda i: (i, 0)),
        out_shape=jax.ShapeDtypeStruct((m, d), F32),
        compiler_params=_params(("parallel",), 2 * tm * d * 8),
        name="final_norm",
    )(x2, g.reshape(1, d))


def _col_tile(n):
    return next(t for t in range(COL_TILE, 0, -LANE) if n % t == 0)


def _mm_kernel(a_ref, w_ref, o_ref):
    o_ref[...] = _dot(a_ref[...], w_ref[...]).astype(o_ref.dtype)


def _matmul(a, w, out_dtype=F32):
    m, k = a.shape
    n = w.shape[1]
    tm = min(ROW_TILE, m)
    tn = _col_tile(n)
    ob = jnp.dtype(out_dtype).itemsize
    return pl.pallas_call(
        _mm_kernel,
        grid=(m // tm, n // tn),
        in_specs=[pl.BlockSpec((tm, k), lambda i, j: (i, 0)), pl.BlockSpec((k, tn), lambda i, j: (0, j))],
        out_specs=pl.BlockSpec((tm, tn), lambda i, j: (i, j)),
        out_shape=jax.ShapeDtypeStruct((m, n), out_dtype),
        compiler_params=_params(("parallel", "parallel"), 2 * (tm * k * 2 + k * tn * 2 + tm * tn * ob)),
        name="matmul",
    )(a, w)


def _mm_swiglu_kernel(a_ref, wg_ref, wu_ref, o_ref):
    a = a_ref[...]
    g = _dot(a, wg_ref[...])
    u = _dot(a, wu_ref[...])
    o_ref[...] = (_silu(g) * u).astype(o_ref.dtype)


def _matmul_swiglu(a, wg, wu, layer):
    m, k = a.shape
    n = wg.shape[2]
    tm = min(ROW_TILE, m)
    tn = COL_TILE
    return pl.pallas_call(
        _mm_swiglu_kernel,
        grid=(m // tm, n // tn),
        in_specs=[
            pl.BlockSpec((tm, k), lambda i, j: (i, 0)),
            pl.BlockSpec((None, k, tn), lambda i, j: (layer, 0, j)),
            pl.BlockSpec((None, k, tn), lambda i, j: (layer, 0, j)),
        ],
        out_specs=pl.BlockSpec((tm, tn), lambda i, j: (i, j)),
        out_shape=jax.ShapeDtypeStruct((m, n), BF16),
        compiler_params=_params(("parallel", "parallel"), 2 * (tm * k * 2 + 2 * k * tn * 2 + tm * tn * 2)),
        name="matmul_swiglu",
    )(a, wg, wu)


def _mm_res_kernel(*refs, nk, n_a):
    a_refs, (w_ref, res_ref, gate_ref, o_ref), scratch = refs[:n_a], refs[n_a:n_a + 4], refs[n_a + 4:]
    k0 = 0
    p = None
    for a_ref in a_refs:
        ka = a_ref.shape[1]
        d = _dot(a_ref[...], w_ref[k0:k0 + ka, :])
        p = d if p is None else p + d
        k0 += ka
    if nk == 1:
        o_ref[...] = res_ref[...] + gate_ref[...] * p
        return
    acc_ref, = scratch
    kk = pl.program_id(2)

    @pl.when(kk == 0)
    def _():
        acc_ref[...] = p

    @pl.when(jnp.logical_and(kk > 0, kk < nk - 1))
    def _():
        acc_ref[...] += p

    @pl.when(kk == nk - 1)
    def _():
        o_ref[...] = res_ref[...] + gate_ref[...] * (acc_ref[...] + p)


def _matmul_residual(a_slabs, w, res, mod5, layer, k_gate, mod_row, nk=1, tn=COL_TILE):
    m = a_slabs[0].shape[0]
    k, n = w.shape[1], w.shape[2]
    tm = min(ROW_TILE, m)
    tk = k // nk
    assert nk == 1 or len(a_slabs) == 1
    a_specs = [pl.BlockSpec((tm, tk if nk > 1 else a.shape[1]), lambda i, j, q: (i, q)) for a in a_slabs]
    return pl.pallas_call(
        functools.partial(_mm_res_kernel, nk=nk, n_a=len(a_slabs)),
        grid=(m // tm, n // tn, nk),
        in_specs=a_specs + [
            pl.BlockSpec((None, tk, tn), lambda i, j, q: (layer, q, j)),
            pl.BlockSpec((tm, tn), lambda i, j, q: (i, j)),
            pl.BlockSpec((None, None, None, 1, tn), lambda i, j, q: (layer, mod_row(i, tm), k_gate, 0, j)),
        ],
        out_specs=pl.BlockSpec((tm, tn), lambda i, j, q: (i, j)),
        out_shape=jax.ShapeDtypeStruct((m, n), F32),
        scratch_shapes=[pltpu.VMEM((tm, tn), F32)] if nk > 1 else [],
        compiler_params=_params(
            ("parallel", "parallel", "arbitrary"), 2 * (tm * tk * 2 + tk * tn * 2 + 2 * tm * tn * 4) + tm * tn * 4
        ),
        name="matmul_residual",
    )(*a_slabs, w, res, mod5)


def _cast_pad_cols_kernel(w_ref, o_ref):
    n = w_ref.shape[1]
    o_ref[:, :n] = w_ref[...].astype(o_ref.dtype)
    o_ref[:, n:] = jnp.zeros((o_ref.shape[0], o_ref.shape[1] - n), o_ref.dtype)


def _cast_pad_cols(w, n_pad, tr=256):
    depth, k, n = w.shape
    return pl.pallas_call(
        _cast_pad_cols_kernel,
        grid=(depth, k // tr),
        in_specs=[pl.BlockSpec((None, tr, n), lambda l, i: (l, i, 0))],
        out_specs=pl.BlockSpec((None, tr, n_pad), lambda l, i: (l, i, 0)),
        out_shape=jax.ShapeDtypeStruct((depth, k, n_pad), BF16),
        compiler_params=_params(("parallel", "parallel"), 2 * tr * (n * 4 + n_pad * 2)),
        name="cast_pad_cols",
    )(w)


def _cast_pad_rows_kernel(w_ref, o_ref, *, n_valid):
    i = pl.program_id(1)

    @pl.when(i < n_valid)
    def _():
        o_ref[...] = w_ref[...].astype(o_ref.dtype)

    @pl.when(i >= n_valid)
    def _():
        o_ref[...] = jnp.zeros_like(o_ref)


def _cast_pad_rows(w, k_pad, tr=256):
    depth, k, n = w.shape
    assert k % tr == 0 and k_pad % tr == 0
    n_valid = k // tr
    return pl.pallas_call(
        functools.partial(_cast_pad_rows_kernel, n_valid=n_valid),
        grid=(depth, k_pad // tr),
        in_specs=[pl.BlockSpec((None, tr, n), lambda l, i: (l, jnp.minimum(i, n_valid - 1), 0))],
        out_specs=pl.BlockSpec((None, tr, n), lambda l, i: (l, i, 0)),
        out_shape=jax.ShapeDtypeStruct((depth, k_pad, n), BF16),
        compiler_params=_params(("parallel", "parallel"), 2 * tr * n * 6),
        name="cast_pad_rows",
    )(w)


def _rope_tables(n_tok):
    quarter = HEAD_DIM // 4
    inv = 1.0 / (ROPE_THETA ** (jnp.arange(quarter, dtype=F32) / quarter))
    pos = jnp.arange(n_tok, dtype=jnp.int32)
    row = (pos // GRID_W).astype(F32)[:, None] * inv
    col = (pos % GRID_W).astype(F32)[:, None] * inv
    cos = jnp.concatenate([jnp.cos(row), jnp.cos(row), jnp.cos(col), jnp.cos(col)], axis=-1)
    sin = jnp.concatenate([-jnp.sin(row), jnp.sin(row), -jnp.sin(col), jnp.sin(col)], axis=-1)
    return cos, sin


def _rope_kernel(x_ref, cos_ref, sin_ref, o_ref, *, scale, n_heads):
    cos = cos_ref[...] * scale
    sin = sin_ref[...] * scale
    lane = lax.broadcasted_iota(jnp.int32, cos.shape, 1)
    first = (lane & (HEAD_DIM // 4)) == 0
    for h in range(n_heads):
        hs = slice(h * HEAD_DIM, (h + 1) * HEAD_DIM)
        x = x_ref[:, hs].astype(F32)
        partner = jnp.where(first, pltpu.roll(x, HEAD_DIM - HEAD_DIM // 4, 1), pltpu.roll(x, HEAD_DIM // 4, 1))
        o_ref[:, hs] = (x * cos + partner * sin).astype(o_ref.dtype)


def _rope(p, col0, width, cos, sin, scale, tr=256):
    m = p.shape[0]
    n_tok = cos.shape[0]
    tpb = n_tok // tr
    assert col0 % width == 0
    return pl.pallas_call(
        functools.partial(_rope_kernel, scale=scale, n_heads=width // HEAD_DIM),
        grid=(m // tr,),
        in_specs=[
            pl.BlockSpec((tr, width), lambda i: (i, col0 // width)),
            pl.BlockSpec((tr, HEAD_DIM), lambda i: (i % tpb, 0)),
            pl.BlockSpec((tr, HEAD_DIM), lambda i: (i % tpb, 0)),
        ],
        out_specs=pl.BlockSpec((tr, width), lambda i: (i, 0)),
        out_shape=jax.ShapeDtypeStruct((m, width), BF16),
        compiler_params=_params(("parallel",), 2 * tr * (width * 6 + HEAD_DIM * 8)),
        name="rope",
    )(p, cos, sin)


def _gla_kernel(ql, kl, vl, dl, qc, kc, vc, dc, up_ref, bias_ref, ol, oc, st_ref, *, nh, scale):
    d = pl.program_id(0)
    s = pl.program_id(2)
    is_ctx = s == 0
    fwd = d == 0
    n_chunk = CTX_BLOCK // GLA_CHUNK

    @pl.when(s == 0)
    def _():
        st_ref[...] = jnp.zeros_like(st_ref)

    tb = CTX_BLOCK

    def in_scan_order(x):
        rev = jnp.concatenate([x[(n_chunk - 1 - p) * GLA_CHUNK:(n_chunk - p) * GLA_CHUNK] for p in range(n_chunk)],
                              axis=0)
        return jnp.where(fwd, x, rev)

    def load(rc, rl, cols):
        return in_scan_order(jnp.where(is_ctx, rc[:, cols], rl[:, cols]))

    row = lax.broadcasted_iota(jnp.int32, (tb, tb), 0)
    col = lax.broadcasted_iota(jnp.int32, (tb, tb), 1)
    incl = jnp.logical_and(row // GLA_CHUNK == col // GLA_CHUNK, jnp.where(fwd, row - col, col - row) >= 0)
    row_chunk = lax.broadcasted_iota(jnp.int32, (tb, GLA_DK), 0) // GLA_CHUNK

    def per_chunk(x):
        return jnp.concatenate([jnp.where(row_chunk == p, x, 0.0) for p in range(n_chunk)], axis=1)

    z = _dot(load(dc, dl, slice(None)).astype(BF16), up_ref[...]) + bias_ref[...]
    g = (jnp.minimum(z, 0.0) - jnp.log(1.0 + jnp.exp(-jnp.abs(z)))) * (1.0 / GLA_NORMALIZER)
    tmat = jnp.where(incl, 1.0, 0.0).astype(BF16)
    g_hi = g.astype(BF16)
    g_rest = g - g_hi.astype(F32)
    g_mid = g_rest.astype(BF16)
    g_lo = (g_rest - g_mid.astype(F32)).astype(BF16)
    cum = _dot(tmat, g_hi) + _dot(tmat, g_mid) + _dot(tmat, g_lo)
    tot = [jnp.sum(g[p * GLA_CHUNK:(p + 1) * GLA_CHUNK], axis=0, keepdims=True) for p in range(n_chunk)]
    tot_rows = jnp.concatenate([jnp.broadcast_to(t, (GLA_CHUNK, t.shape[1])) for t in tot], axis=0)

    for h in range(nh):
        ks = slice(h * GLA_DK, (h + 1) * GLA_DK)
        vs = slice(h * GLA_DV, (h + 1) * GLA_DV)
        bh = cum[:, ks]
        q = load(qc, ql, ks)
        k = load(kc, kl, ks)
        v = load(vc, vl, vs).astype(BF16)
        qd = q * (scale * jnp.exp(bh))
        ki = (k * jnp.exp(-bh)).astype(BF16)
        ku = k * jnp.exp(tot_rows[:, ks] - bh)
        a = jnp.where(incl, _dot_nt(qd.astype(BF16), ki), 0.0)
        inc = _dot_tn(v, per_chunk(ku).astype(BF16))
        st = st_ref[h]
        entering = []
        for p in range(n_chunk):
            entering.append(st)
            st = st * jnp.exp(tot[p][:, ks]) + inc[:, p * GLA_DK:(p + 1) * GLA_DK]
        st_ref[h] = st
        o = (_dot(a.astype(BF16), v)
             + _dot_nt(per_chunk(qd).astype(BF16), jnp.concatenate(entering, axis=1).astype(BF16)))
        o = in_scan_order(o)

        @pl.when(is_ctx)
        def _():
            oc[:, vs] = o

        @pl.when(jnp.logical_not(is_ctx))
        def _():
            ol[:, vs] = o


def _gla(p_lat, p_ctx, dn_lat, dn_ctx, up2, bias2, nb, n_tok, nh):
    qk = nh * GLA_DK
    dv = nh * GLA_DV
    tb = CTX_BLOCK
    nbl = n_tok // tb

    def lat_blk(d, b, s):
        t = jnp.maximum(s - 1, 0)
        return b * nbl + jnp.where(d == 0, t, nbl - 1 - t)

    lat = lambda w, c: pl.BlockSpec((tb, w), lambda d, b, s: (lat_blk(d, b, s), c))
    ctx = lambda w, c: pl.BlockSpec((tb, w), lambda d, b, s: (b, c))
    vm = 2 * (2 * (2 * tb * qk + tb * dv + tb * LANE) * 4 + 2 * tb * dv * 4) + nh * GLA_DV * GLA_DK * 4
    return pl.pallas_call(
        functools.partial(_gla_kernel, nh=nh, scale=GLA_DK ** -0.5),
        grid=(2, nb, nbl + 1),
        in_specs=[
            lat(qk, 0), lat(qk, 1), lat(dv, 1), lat(LANE, 0),
            ctx(qk, 0), ctx(qk, 1), ctx(dv, 1), ctx(LANE, 0),
            pl.BlockSpec((None, LANE, qk), lambda d, b, s: (d, 0, 0)),
            pl.BlockSpec((None, 1, qk), lambda d, b, s: (d, 0, 0)),
        ],
        out_specs=[
            pl.BlockSpec((None, tb, dv), lambda d, b, s: (d, lat_blk(d, b, s), 0)),
            pl.BlockSpec((None, tb, dv), lambda d, b, s: (d, b, 0)),
        ],
        out_shape=[
            jax.ShapeDtypeStruct((2, nb * n_tok, dv), F32),
            jax.ShapeDtypeStruct((2, nb * tb, dv), F32),
        ],
        scratch_shapes=[pltpu.VMEM((nh, GLA_DV, GLA_DK), F32)],
        compiler_params=_params(("parallel", "parallel", "arbitrary"), vm),
        name="gla_scan",
    )(p_lat, p_lat, p_lat, dn_lat, p_ctx, p_ctx, p_ctx, dn_ctx, up2, bias2)


def _gla_out_kernel(o_ref, r_ref, g_ref, y_ref, *, nh):
    for h in range(nh):
        sl = slice(h * GLA_DV, (h + 1) * GLA_DV)
        o = o_ref[0, :, sl] + o_ref[1, :, sl]
        y = o * lax.rsqrt(jnp.mean(o * o, axis=-1, keepdims=True) + NORM_EPS) * g_ref[...]
        y_ref[:, sl] = (y * _silu(r_ref[:, sl])).astype(y_ref.dtype)


def _gla_out(o2, p, norm_g, nh, tb=256):
    _, m, dv = o2.shape
    return pl.pallas_call(
        functools.partial(_gla_out_kernel, nh=nh),
        grid=(m // tb,),
        in_specs=[
            pl.BlockSpec((2, tb, dv), lambda i: (0, i, 0)),
            pl.BlockSpec((tb, dv), lambda i: (i, 2)),
            pl.BlockSpec((1, GLA_DV), lambda i: (0, 0)),
        ],
        out_specs=pl.BlockSpec((tb, dv), lambda i: (i, 0)),
        out_shape=jax.ShapeDtypeStruct((m, dv), BF16),
        compiler_params=_params(("parallel",), 2 * tb * dv * 14),
        name="gla_out",
    )(o2, p, norm_g.reshape(1, GLA_DV))


def _swa_kernel(sink_ref, q_ref, kp, kc, kn, vp, vc, vn, kx, vx, o_ref, *, n_blk, kvs):
    i = pl.program_id(2)
    span = SWA_BLOCK + 2 * SWA_WINDOW
    gq = SWA_GROUP * SWA_BLOCK
    kj = lax.broadcasted_iota(jnp.int32, (span, gq), 0)
    lane = lax.broadcasted_iota(jnp.int32, (span, gq), 1)
    qi = lane & (SWA_BLOCK - 1)
    key_pos = (i - 1) * SWA_BLOCK + kj
    ok = jnp.logical_and(jnp.abs(qi + SWA_WINDOW - kj) <= SWA_WINDOW,
                         jnp.logical_and(key_pos >= 0, key_pos < n_blk * SWA_BLOCK))
    head = lax.broadcasted_iota(jnp.int32, (1, gq), 1) // SWA_BLOCK
    for kv in range(kvs):
        hkv = pl.program_id(1) * kvs + kv
        ks = slice(kv * HEAD_DIM, (kv + 1) * HEAD_DIM)
        kwin = jnp.concatenate([kp[:, ks], kc[:, ks], kn[:, ks]], axis=0)
        vwin = jnp.concatenate([vp[:, ks], vc[:, ks], vn[:, ks]], axis=0).astype(BF16)
        kctx = kx[:, ks].astype(BF16)
        vctx = vx[:, ks].astype(BF16)
        q0 = kv * SWA_GROUP * HEAD_DIM
        qs = jnp.concatenate([q_ref[:, q0 + g * HEAD_DIM:q0 + (g + 1) * HEAD_DIM] for g in range(SWA_GROUP)], axis=0)
        s_win = jnp.where(ok, _dot_nt(kwin, qs), -jnp.inf)
        s_ctx = _dot_nt(kctx, qs)
        sink = jnp.zeros((1, gq), F32)
        for g in range(SWA_GROUP):
            sink = jnp.where(head == g, sink_ref[0, hkv * SWA_GROUP + g], sink)
        m = jnp.maximum(jnp.maximum(jnp.max(s_win, axis=0, keepdims=True), jnp.max(s_ctx, axis=0, keepdims=True)),
                        sink)
        p_win = jnp.exp(s_win - m)
        p_ctx = jnp.exp(s_ctx - m)
        den = jnp.sum(p_win, axis=0, keepdims=True) + jnp.sum(p_ctx, axis=0, keepdims=True) + jnp.exp(sink - m)
        ot = (_dot_tn(vctx, p_ctx.astype(BF16)) + _dot_tn(vwin, p_win.astype(BF16))) / den
        for g in range(SWA_GROUP):
            hs = slice(q0 + g * HEAD_DIM, q0 + (g + 1) * HEAD_DIM)
            o_ref[:, hs] = ot[:, g * SWA_BLOCK:(g + 1) * SWA_BLOCK].T.astype(o_ref.dtype)


def _swa(sink, q_rope, k_rope, p_lat, p_ctx, kcol, vcol, nb, n_tok, n_kv):
    n_blk = n_tok // SWA_BLOCK
    tb = SWA_BLOCK
    kvs = 2 if n_kv % 2 == 0 else 1
    kw = kvs * HEAD_DIM
    gw = kvs * SWA_GROUP * HEAD_DIM
    assert kcol % kw == 0 and vcol % kw == 0
    kc0 = kcol // kw
    vc0 = vcol // kw
    prev = lambda b, h, i: b * n_blk + jnp.maximum(i - 1, 0)
    cur = lambda b, h, i: b * n_blk + i
    nxt = lambda b, h, i: b * n_blk + jnp.minimum(i + 1, n_blk - 1)
    kspec = lambda f: pl.BlockSpec((tb, kw), lambda b, h, i: (f(b, h, i), h))
    vspec = lambda f: pl.BlockSpec((tb, kw), lambda b, h, i: (f(b, h, i), vc0 + h))
    return pl.pallas_call(
        functools.partial(_swa_kernel, n_blk=n_blk, kvs=kvs),
        grid=(nb, n_kv // kvs, n_blk),
        in_specs=[
            pl.BlockSpec(memory_space=pltpu.SMEM),
            pl.BlockSpec((tb, gw), lambda b, h, i: (cur(b, h, i), h)),
            kspec(prev), kspec(cur), kspec(nxt),
            vspec(prev), vspec(cur), vspec(nxt),
            pl.BlockSpec((CTX_BLOCK, kw), lambda b, h, i: (b, kc0 + h)),
            pl.BlockSpec((CTX_BLOCK, kw), lambda b, h, i: (b, vc0 + h)),
        ],
        out_specs=pl.BlockSpec((tb, gw), lambda b, h, i: (cur(b, h, i), h)),
        out_shape=jax.ShapeDtypeStruct((nb * n_tok, n_kv * SWA_GROUP * HEAD_DIM), BF16),
        compiler_params=_params(("parallel", "parallel", "parallel"), 12 << 20),
        name="swa",
    )(sink, q_rope, k_rope, k_rope, k_rope, p_lat, p_lat, p_lat, p_ctx, p_ctx)


def _ctx_sink_kernel(sink_ref, q_ref, kx, vx, o_ref, *, scale):
    hkv = pl.program_id(1)
    kctx = kx[...].astype(BF16)
    vctx = vx[...].astype(BF16)
    for g in range(SWA_GROUP):
        hs = slice(g * HEAD_DIM, (g + 1) * HEAD_DIM)
        q = (q_ref[:, hs].astype(F32) * scale).astype(BF16)
        s = _dot_nt(q, kctx)
        sink = sink_ref[0, hkv * SWA_GROUP + g]
        m = jnp.maximum(jnp.max(s, axis=-1, keepdims=True), sink)
        p = jnp.exp(s - m)
        den = jnp.sum(p, axis=-1, keepdims=True) + jnp.exp(sink - m)
        o_ref[:, hs] = (_dot(p.astype(BF16), vctx) / den).astype(o_ref.dtype)


def _ctx_sink(sink, p_ctx, qcol, kcol, vcol, nb, n_kv):
    gw = SWA_GROUP * HEAD_DIM
    return pl.pallas_call(
        functools.partial(_ctx_sink_kernel, scale=HEAD_DIM ** -0.5),
        grid=(nb, n_kv),
        in_specs=[
            pl.BlockSpec(memory_space=pltpu.SMEM),
            pl.BlockSpec((CTX_BLOCK, gw), lambda b, h: (b, qcol // gw + h)),
            pl.BlockSpec((CTX_BLOCK, HEAD_DIM), lambda b, h: (b, kcol // HEAD_DIM + h)),
            pl.BlockSpec((CTX_BLOCK, HEAD_DIM), lambda b, h: (b, vcol // HEAD_DIM + h)),
        ],
        out_specs=pl.BlockSpec((CTX_BLOCK, gw), lambda b, h: (b, h)),
        out_shape=jax.ShapeDtypeStruct((nb * CTX_BLOCK, n_kv * gw), BF16),
        compiler_params=_params(("parallel", "parallel"), 4 << 20),
        name="ctx_sink_attention",
    )(sink, p_ctx, p_ctx, p_ctx)


def _na_bias_table(rpb, rows):
    qcol = np.arange(GRID_W)[:, None]
    kcol = np.arange(GRID_W)[None, :]
    col_start = np.clip(qcol - NA_COLS // 2, 0, GRID_W - NA_COLS)
    col_ok = (kcol >= col_start) & (kcol < col_start + NA_COLS)
    dc = np.clip(kcol - qcol + NA_COLS - 1, 0, 2 * NA_COLS - 2)
    n_dc = 2 * NA_COLS - 1
    onehot = (dc[None] == np.arange(n_dc)[:, None, None]).astype(np.float32)
    cols = jnp.einsum('hrd,dqk->hrqk', rpb.astype(F32), onehot, precision=lax.Precision.HIGHEST)
    cols = jnp.where(col_ok[None, None], cols, -jnp.inf).transpose(0, 1, 3, 2)
    masked = jnp.full(cols.shape[:1] + cols.shape[2:], -jnp.inf, F32)

    span = NA_ROWS + NA_PAIR - 1
    patterns = []
    for r0 in range(0, rows, NA_PAIR):
        rs = [min(max(r0 + i - NA_ROWS // 2, 0), rows - NA_ROWS) for i in range(NA_PAIR)]
        u = min(rs[0], rows - span)
        patterns.append(tuple((r0 + i - rs[i], rs[i] - u) for i in range(NA_PAIR)))
    variants = sorted(set(patterns), key=patterns.index)
    blocks = []
    for pat in variants:
        per_query_row = []
        for delta, off in pat:
            krows = [cols[:, kr - off - delta + NA_ROWS - 1] if 0 <= kr - off < NA_ROWS else masked
                     for kr in range(span)]
            per_query_row.append(jnp.stack(krows, axis=1))
        blocks.append(jnp.concatenate(per_query_row, axis=-1).reshape(rpb.shape[0], span * GRID_W, NA_PAIR * GRID_W))
    ids = np.array([variants.index(p) for p in patterns], dtype=np.int32)
    return jnp.stack(blocks, axis=1), ids


def _na_kernel(ids_ref, q_ref, k_ref, v_ref, kx, vx, bias_ref, o_ref, *, nh, scale):
    for h in range(nh):
        hs = slice(h * HEAD_DIM, (h + 1) * HEAD_DIM)
        q = (q_ref[:, hs].astype(F32) * scale).astype(BF16)
        s_nb = _dot_nt(k_ref[:, hs].astype(BF16), q) + bias_ref[h]
        s_ctx = _dot_nt(kx[:, hs].astype(BF16), q)
        m = jnp.maximum(jnp.max(s_nb, axis=0, keepdims=True), jnp.max(s_ctx, axis=0, keepdims=True))
        p_nb = jnp.exp(s_nb - m)
        p_ctx = jnp.exp(s_ctx - m)
        den = jnp.sum(p_nb, axis=0, keepdims=True) + jnp.sum(p_ctx, axis=0, keepdims=True)
        ot = (_dot_tn(vx[:, hs].astype(BF16), p_ctx.astype(BF16))
              + _dot_tn(v_ref[:, hs].astype(BF16), p_nb.astype(BF16)))
        o_ref[:, hs] = (ot / den).T.astype(o_ref.dtype)


def _na(p_lat, p_ctx, rpb, nb, n_tok, nh):
    rows = n_tok // GRID_W
    w = nh * HEAD_DIM
    span = NA_ROWS + NA_PAIR - 1
    win = span * GRID_W
    tq = NA_PAIR * GRID_W
    n_pair = rows // NA_PAIR
    bias_tab, ids = _na_bias_table(rpb, rows)
    u = lambda g: jnp.clip(g * NA_PAIR - NA_ROWS // 2, 0, rows - span)
    kv = lambda c: pl.BlockSpec((pl.Element(win), pl.Element(w)),
                                lambda b, g, ids: ((b * rows + u(g)) * GRID_W, c * w))
    vm = 2 * (tq * w * 4 + 2 * win * w * 4 + 2 * CTX_BLOCK * w * 4 + nh * tq * win * 4 + tq * w * 2)
    return pl.pallas_call(
        functools.partial(_na_kernel, nh=nh, scale=HEAD_DIM ** -0.5),
        grid_spec=pltpu.PrefetchScalarGridSpec(
            num_scalar_prefetch=1,
            grid=(nb, n_pair),
            in_specs=[
                pl.BlockSpec((tq, w), lambda b, g, ids: (b * n_pair + g, 0)),
                kv(1), kv(2),
                pl.BlockSpec((CTX_BLOCK, w), lambda b, g, ids: (b, 1)),
                pl.BlockSpec((CTX_BLOCK, w), lambda b, g, ids: (b, 2)),
                pl.BlockSpec((nh, None, win, tq), lambda b, g, ids: (0, ids[g], 0, 0)),
            ],
            out_specs=pl.BlockSpec((tq, w), lambda b, g, ids: (b * n_pair + g, 0)),
        ),
        out_shape=jax.ShapeDtypeStruct((nb * n_tok, w), BF16),
        compiler_params=_params(("parallel", "arbitrary"), vm),
        name="neighborhood_attention",
    )(jnp.asarray(ids), p_lat, p_lat, p_lat, p_ctx, p_ctx, bias_tab)


def _diff_kernel(q_ref, k_ref, vt_ref, kx, vxt, lq1, lk1, lq2, lk2, g_ref, o_ref, m_ref, l_ref, acc_ref,
                 sa_ref, sb_ref, pa_ref, pb_ref, pc_ref, aa_ref, ab_ref, ac_ref, *, n_blk, tk, lam_init):
    m_ref[...] = jnp.full_like(m_ref, -jnp.inf)
    l_ref[...] = jnp.zeros_like(l_ref)
    acc_ref[...] = jnp.zeros_like(acc_ref)

    def scores(keys):
        return [_dot_nt(keys[:, t * HEAD_DIM:(t + 1) * HEAD_DIM], q_ref[:, t * HEAD_DIM:(t + 1) * HEAD_DIM])
                for t in range(2)]

    def lat_scores(j, dst):
        s = scores(k_ref[pl.ds(pl.multiple_of(j * tk, tk), tk), :])
        dst[0] = s[0]
        dst[1] = s[1]

    def softmax(s, p_dst, a_dst):
        for t in range(2):
            m_old = m_ref[t]
            m_new = jnp.maximum(m_old, jnp.max(s[t], axis=0, keepdims=True))
            alpha = jnp.exp2(m_old - m_new)
            p = jnp.exp2(s[t] - m_new)
            l_ref[t] = alpha * l_ref[t] + jnp.sum(p, axis=0, keepdims=True)
            m_ref[t] = m_new
            p_dst[t] = p.astype(BF16)
            a_dst[t] = alpha

    def add_values(p_src, a_src, vt):
        for t in range(2):
            acc_ref[t] = a_src[t] * acc_ref[t] + _dot(vt, p_src[t])

    def stage(s_cur, p_cur, a_cur, next_block, s_next, prev):
        if next_block is not None:
            lat_scores(next_block, s_next)
        add_values(*prev)
        softmax((s_cur[0], s_cur[1]), p_cur, a_cur)

    s_ctx = scores(kx[...].astype(BF16))
    lat_scores(0, sa_ref)
    softmax(s_ctx, pc_ref, ac_ref)
    stage(sa_ref, pa_ref, aa_ref, 1, sb_ref, (pc_ref, ac_ref, vxt[...]))

    def body(jj, carry):
        j = 2 * jj + 1
        stage(sb_ref, pb_ref, ab_ref, j + 1, sa_ref, (pa_ref, aa_ref, vt_ref[j - 1]))
        stage(sa_ref, pa_ref, aa_ref, j + 2, sb_ref, (pb_ref, ab_ref, vt_ref[j]))
        return carry

    lax.fori_loop(0, (n_blk - 2) // 2, body, 0)
    stage(sb_ref, pb_ref, ab_ref, None, None, (pa_ref, aa_ref, vt_ref[n_blk - 2]))
    add_values(pb_ref, ab_ref, vt_ref[n_blk - 1])

    lam = (jnp.exp(jnp.sum(lq1[...] * lk1[...], axis=-1, keepdims=True))
           - jnp.exp(jnp.sum(lq2[...] * lk2[...], axis=-1, keepdims=True)) + lam_init)
    o = acc_ref[0] / l_ref[0] - lam * (acc_ref[1] / l_ref[1])
    y = o * lax.rsqrt(jnp.mean(o * o, axis=0, keepdims=True) + NORM_EPS) * (g_ref[...] * (1.0 - lam_init))
    o_ref[...] = y.T.astype(o_ref.dtype)


def _diff(q_rope, k_rope, vt_lat, p_ctx, vt_ctx, kcol, lq1, lk1, lq2, lk2, norm_g, lam_init, nb, n_tok, nh,
          tq=512):
    nq = n_tok // tq
    n_blk, tk = vt_lat.shape[2], vt_lat.shape[4]
    assert n_blk >= 2 and n_blk % 2 == 0, "the key-block pipeline alternates two buffers"
    w = 2 * HEAD_DIM
    vec = lambda: pl.BlockSpec((1, HEAD_DIM), lambda b, h, i: (0, 0))
    vm = (2 * (tq * w * 2 + 2 * n_tok * w * 2 + CTX_BLOCK * w * 6 + tq * w * 2)
          + 2 * tq * w * 4 + 2 * 2 * tk * tq * 6 + tq * w * 8)
    return pl.pallas_call(
        functools.partial(_diff_kernel, n_blk=n_blk, tk=tk, lam_init=lam_init),
        grid=(nb, nh, nq),
        in_specs=[
            pl.BlockSpec((tq, w), lambda b, h, i: (b * nq + i, h)),
            pl.BlockSpec((n_tok, w), lambda b, h, i: (b, h)),
            pl.BlockSpec((None, None, n_blk, w, tk), lambda b, h, i: (b, h, 0, 0, 0)),
            pl.BlockSpec((CTX_BLOCK, w), lambda b, h, i: (b, kcol // w + h)),
            pl.BlockSpec((None, None, w, CTX_BLOCK), lambda b, h, i: (b, h, 0, 0)),
            vec(), vec(), vec(), vec(),
            pl.BlockSpec((DIFF_DV, 1), lambda b, h, i: (0, 0)),
        ],
        out_specs=pl.BlockSpec((tq, w), lambda b, h, i: (b * nq + i, h)),
        out_shape=jax.ShapeDtypeStruct((nb * n_tok, nh * w), BF16),
        scratch_shapes=[
            pltpu.VMEM((2, 1, tq), F32),
            pltpu.VMEM((2, 1, tq), F32),
            pltpu.VMEM((2, DIFF_DV, tq), F32),
            pltpu.VMEM((2, tk, tq), F32),
            pltpu.VMEM((2, tk, tq), F32),
            pltpu.VMEM((2, tk, tq), BF16),
            pltpu.VMEM((2, tk, tq), BF16),
            pltpu.VMEM((2, CTX_BLOCK, tq), BF16),
            pltpu.VMEM((2, 1, tq), F32),
            pltpu.VMEM((2, 1, tq), F32),
            pltpu.VMEM((2, 1, tq), F32),
        ],
        compiler_params=_params(("parallel", "parallel", "arbitrary"), vm),
        name="diff_attention",
    )(q_rope, k_rope, vt_lat, p_ctx, vt_ctx, lq1.reshape(1, -1), lk1.reshape(1, -1), lq2.reshape(1, -1),
      lk2.reshape(1, -1), norm_g.reshape(-1, 1))


def _pad_cols(w, n):
    return jnp.pad(w, ((0, 0), (0, n - w.shape[1])))


def kernel(x, c, ctx, c_ctx, ada_w, ada_b, norm_mix_g, norm_ffn_g, w_out, ffn_w_gate, ffn_w_up, ffn_w_down,
           ev_w_in, gla_gate_up_f, gla_gate_bias_f, gla_gate_up_b, gla_gate_bias_b, gla_norm_g, swa_sink,
           od_w_in, na_rpb, diff_lq1, diff_lk1, diff_lq2, diff_lk2, diff_norm_g, final_norm_g):
    nb, n_tok, d = x.shape
    lc = ctx.shape[1]
    depth = ada_w.shape[0]
    f_hidden = ffn_w_gate.shape[2]
    half = d // 2
    assert lc == CTX_BLOCK and nb < 8 and depth == 2, "layout assumes a 256-token context, batch < 8, two layers"
    gla_heads = half // GLA_DV
    gla_qk = gla_heads * GLA_DK
    n_q = half // HEAD_DIM
    n_kv = n_q // SWA_GROUP
    diff_heads = half // (2 * HEAD_DIM)

    xs = x.reshape(nb * n_tok, d)
    cs = ctx.reshape(nb * lc, d)
    cos, sin = _rope_tables(n_tok)

    s8 = jnp.zeros((8, d), F32).at[:nb].set(c).at[nb].set(c_ctx)
    mod5 = _mod_vectors(s8, ada_w, ada_b).reshape(depth, 8, 6, 1, d)
    lat_row = lambda i, tm: (i * tm) // n_tok
    ctx_row = lambda i, tm: nb

    f_pad = -(-f_hidden // FFN_PAD) * FFN_PAD

    wg = _cast_pad_cols(ffn_w_gate, f_pad)
    wu = _cast_pad_cols(ffn_w_up, f_pad)
    wd = _cast_pad_rows(ffn_w_down, f_pad)
    wo = w_out.astype(BF16)

    def ffn(stream, layer, mod_row):
        h2 = _norm_mod(stream, norm_ffn_g[layer], mod5, layer, 3, 4, mod_row)
        a = _matmul_swiglu(h2, wg, wu, layer)
        return _matmul_residual([a], wd, stream, mod5, layer, 5, mod_row, nk=4, tn=2 * COL_TILE)

    layer = 0
    w_in = ev_w_in[0]
    g0 = 2 * gla_qk + 2 * half
    w_main = jnp.concatenate([w_in[:, :g0], w_in[:, g0 + 2 * GLA_RANK:]], axis=1).astype(BF16)
    w_dn = _pad_cols(w_in[:, g0:g0 + 2 * GLA_RANK], LANE).astype(BF16)
    swa_q0 = g0
    swa_k0 = swa_q0 + half
    swa_v0 = swa_k0 + n_kv * HEAD_DIM
    up2 = jnp.zeros((2, LANE, gla_qk), F32)
    up2 = up2.at[0, :GLA_RANK].set(gla_gate_up_f[0]).at[1, GLA_RANK:2 * GLA_RANK].set(gla_gate_up_b[0]).astype(BF16)
    bias2 = jnp.stack([gla_gate_bias_f[0], gla_gate_bias_b[0]]).reshape(2, 1, gla_qk)

    h = _norm_mod(xs, norm_mix_g[layer], mod5, layer, 0, 1, lat_row)
    hc = _norm_mod(cs, norm_mix_g[layer], mod5, layer, 0, 1, ctx_row)
    p_lat = _matmul(h, w_main)
    p_ctx = _matmul(hc, w_main)
    dn_lat = _matmul(h, w_dn)
    dn_ctx = _matmul(hc, w_dn)

    o_lat, o_ctx = _gla(p_lat, p_ctx, dn_lat, dn_ctx, up2, bias2, nb, n_tok, gla_heads)
    ya = _gla_out(o_lat, p_lat, gla_norm_g[0], gla_heads)
    yac = _gla_out(o_ctx, p_ctx, gla_norm_g[0], gla_heads)

    sink = swa_sink[0].reshape(1, n_q)
    q_rope = _rope(p_lat, swa_q0, half, cos, sin, HEAD_DIM ** -0.5)
    k_rope = _rope(p_lat, swa_k0, n_kv * HEAD_DIM, cos, sin, 1.0)
    yb = _swa(sink, q_rope, k_rope, p_lat, p_ctx, swa_k0, swa_v0, nb, n_tok, n_kv)
    ybc = _ctx_sink(sink, p_ctx, swa_q0, swa_k0, swa_v0, nb, n_kv)

    xs = _matmul_residual([ya, yb], wo, xs, mod5, layer, 2, lat_row)
    cs = _matmul_residual([yac, ybc], wo, cs, mod5, layer, 2, ctx_row)
    xs = ffn(xs, layer, lat_row)
    cs = ffn(cs, layer, ctx_row)

    layer = 1
    w_in = od_w_in[0].astype(BF16)
    h = _norm_mod(xs, norm_mix_g[layer], mod5, layer, 0, 1, lat_row)
    hc = _norm_mod(cs, norm_mix_g[layer], mod5, layer, 0, 1, ctx_row)
    p_lat = _matmul(h, w_in, BF16)
    p_ctx = _matmul(hc, w_in, BF16)

    yn =_na(p_lat, p_ctx, na_rpb[0], nb, n_tok, n_q)

    dq0 = 3 * half
    dk0 = dq0 + half
    dv0 = dk0 + half
    lam_init = 0.8 - 0.6 * math.exp(-0.3 * layer)
    qd = _rope(p_lat, dq0, half, cos, sin, HEAD_DIM ** -0.5 * math.log2(math.e))
    kd = _rope(p_lat, dk0, half, cos, sin, 1.0)
    tk = DIFF_KEY_BLOCK
    vt_lat = p_lat[:, dv0:].astype(BF16).reshape(nb, n_tok // tk, tk, diff_heads, DIFF_DV).transpose(0, 3, 1, 4, 2)
    vt_ctx = p_ctx[:, dv0:].astype(BF16).reshape(nb, lc, diff_heads, DIFF_DV).transpose(0, 2, 3, 1)
    yd = _diff(qd, kd, vt_lat, p_ctx, vt_ctx, dk0, diff_lq1[0], diff_lk1[0], diff_lq2[0], diff_lk2[0],
               diff_norm_g[0], lam_init, nb, n_tok, diff_heads)

    xs = _matmul_residual([yn, yd], wo, xs, mod5, layer, 2, lat_row)
    xs = ffn(xs, layer, lat_row)
    return _final_norm(xs, final_norm_g).reshape(nb, n_tok, d)
```

```python
import functools
import math

import numpy as np
import jax
import jax.numpy as jnp
from jax import lax
from jax.experimental import pallas as pl
from jax.experimental.pallas import tpu as pltpu

HEAD_DIM = 128
GRID_W = 64
ROPE_THETA = 10000.0
NORM_EPS = 1e-6
GLA_DK = 128
GLA_DV = 256
GLA_RANK = 16
GLA_NORMALIZER = 16.0
GLA_CHUNK = 64
SWA_GROUP = 4
SWA_WINDOW = 128
SWA_BLOCK = 128
NA_ROWS = 8
NA_COLS = 16
NA_PAIR = 4
DIFF_DV = 2 * HEAD_DIM
DIFF_KEY_BLOCK = 2048
CTX_BLOCK = 256

V7X_VMEM_BUDGET = 60 * 1024 * 1024
LANE = 128
ROW_TILE = 1024
COL_TILE = 512
FFN_PAD = 1024

F32 = jnp.float32
BF16 = jnp.bfloat16


def _params(sem, vmem_bytes):
    limit = int(min(V7X_VMEM_BUDGET, vmem_bytes * 5 // 4 + (4 << 20)))
    return pltpu.CompilerParams(dimension_semantics=sem, vmem_limit_bytes=limit)


def _dot(a, b):
    return jnp.dot(a, b, preferred_element_type=F32)


def _dot_nt(a, b):
    return lax.dot_general(a, b, (((1,), (1,)), ((), ())), preferred_element_type=F32)


def _dot_tn(a, b):
    return lax.dot_general(a, b, (((0,), (0,)), ((), ())), preferred_element_type=F32)


def _silu(t):
    return t * jax.nn.sigmoid(t)


def _mod_kernel(s_ref, w_ref, b_ref, o_ref):
    a = _silu(s_ref[...]).astype(BF16)
    o_ref[...] = _dot(a, w_ref[...].astype(BF16)) + b_ref[...]


def _mod_vectors(s8, ada_w, ada_b):
    depth, d, n = ada_w.shape
    tn = COL_TILE
    return pl.pallas_call(
        _mod_kernel,
        grid=(depth, n // tn),
        in_specs=[
            pl.BlockSpec((8, d), lambda l, j: (0, 0)),
            pl.BlockSpec((None, d, tn), lambda l, j: (l, 0, j)),
            pl.BlockSpec((None, 1, tn), lambda l, j: (l, 0, j)),
        ],
        out_specs=pl.BlockSpec((None, 8, tn), lambda l, j: (l, 0, j)),
        out_shape=jax.ShapeDtypeStruct((depth, 8, n), F32),
        compiler_params=_params(("parallel", "parallel"), 2 * d * tn * 4 + d * tn * 2),
        name="adaln_mod",
    )(s8, ada_w, ada_b.reshape(depth, 1, n))


def _norm_mod_kernel(x_ref, g_ref, sh_ref, sc_ref, o_ref):
    x = x_ref[...]
    r = lax.rsqrt(jnp.mean(x * x, axis=-1, keepdims=True) + NORM_EPS)
    y = x * r * g_ref[...]
    o_ref[...] = (y * (1.0 + sc_ref[...]) + sh_ref[...]).astype(o_ref.dtype)


def _norm_mod(x2, g, mod5, layer, k_shift, k_scale, mod_row, tm=256):
    m, d = x2.shape
    return pl.pallas_call(
        _norm_mod_kernel,
        grid=(m // tm,),
        in_specs=[
            pl.BlockSpec((tm, d), lambda i: (i, 0)),
            pl.BlockSpec((1, d), lambda i: (0, 0)),
            pl.BlockSpec((None, None, None, 1, d), lambda i: (layer, mod_row(i, tm), k_shift, 0, 0)),
            pl.BlockSpec((None, None, None, 1, d), lambda i: (layer, mod_row(i, tm), k_scale, 0, 0)),
        ],
        out_specs=pl.BlockSpec((tm, d), lambda i: (i, 0)),
        out_shape=jax.ShapeDtypeStruct((m, d), BF16),
        compiler_params=_params(("parallel",), 2 * tm * d * 6),
        name="norm_mod",
    )(x2, g.reshape(1, d), mod5, mod5)


def _final_norm_kernel(x_ref, g_ref, o_ref):
    x = x_ref[...]
    r = lax.rsqrt(jnp.mean(x * x, axis=-1, keepdims=True) + NORM_EPS)
    o_ref[...] = x * r * g_ref[...]


def _final_norm(x2, g, tm=256):
    m, d = x2.shape
    return pl.pallas_call(
        _final_norm_kernel,
        grid=(m // tm,),
        in_specs=[pl.BlockSpec((tm, d), lambda i: (i, 0)), pl.BlockSpec((1, d), lambda i: (0, 0))],
        out_specs=pl.BlockSpec((tm, d), lambda i: (i, 0)),
        out_shape=jax.ShapeDtypeStruct((m, d), F32),
        compiler_params=_params(("parallel",), 2 * tm * d * 8),
        name="final_norm",
    )(x2, g.reshape(1, d))


def _col_tile(n):
    return next(t for t in range(COL_TILE, 0, -LANE) if n % t == 0)


def _mm_kernel(a_ref, w_ref, o_ref):
    o_ref[...] = _dot(a_ref[...], w_ref[...]).astype(o_ref.dtype)


def _matmul(a, w, out_dtype=F32):
    m, k = a.shape
    n = w.shape[1]
    tm = min(ROW_TILE, m)
    tn = _col_tile(n)
    ob = jnp.dtype(out_dtype).itemsize
    return pl.pallas_call(
        _mm_kernel,
        grid=(m // tm, n // tn),
        in_specs=[pl.BlockSpec((tm, k), lambda i, j: (i, 0)), pl.BlockSpec((k, tn), lambda i, j: (0, j))],
        out_specs=pl.BlockSpec((tm, tn), lambda i, j: (i, j)),
        out_shape=jax.ShapeDtypeStruct((m, n), out_dtype),
        compiler_params=_params(("parallel", "parallel"), 2 * (tm * k * 2 + k * tn * 2 + tm * tn * ob)),
        name="matmul",
    )(a, w)


def _mm_swiglu_kernel(a_ref, wg_ref, wu_ref, o_ref):
    a = a_ref[...]
    g = _dot(a, wg_ref[...])
    u = _dot(a, wu_ref[...])
    o_ref[...] = (_silu(g) * u).astype(o_ref.dtype)


def _matmul_swiglu(a, wg, wu, layer):
    m, k = a.shape
    n = wg.shape[2]
    tm = min(ROW_TILE, m)
    tn = COL_TILE
    return pl.pallas_call(
        _mm_swiglu_kernel,
        grid=(m // tm, n // tn),
        in_specs=[
            pl.BlockSpec((tm, k), lambda i, j: (i, 0)),
            pl.BlockSpec((None, k, tn), lambda i, j: (layer, 0, j)),
            pl.BlockSpec((None, k, tn), lambda i, j: (layer, 0, j)),
        ],
        out_specs=pl.BlockSpec((tm, tn), lambda i, j: (i, j)),
        out_shape=jax.ShapeDtypeStruct((m, n), BF16),
        compiler_params=_params(("parallel", "parallel"), 2 * (tm * k * 2 + 2 * k * tn * 2 + tm * tn * 2)),
        name="matmul_swiglu",
    )(a, wg, wu)


def _mm_res_kernel(*refs, nk, n_a):
    a_refs, (w_ref, res_ref, gate_ref, o_ref), scratch = refs[:n_a], refs[n_a:n_a + 4], refs[n_a + 4:]
    k0 = 0
    p = None
    for a_ref in a_refs:
        ka = a_ref.shape[1]
        d = _dot(a_ref[...], w_ref[k0:k0 + ka, :])
        p = d if p is None else p + d
        k0 += ka
    if nk == 1:
        o_ref[...] = res_ref[...] + gate_ref[...] * p
        return
    acc_ref, = scratch
    kk = pl.program_id(2)

    @pl.when(kk == 0)
    def _():
        acc_ref[...] = p

    @pl.when(jnp.logical_and(kk > 0, kk < nk - 1))
    def _():
        acc_ref[...] += p

    @pl.when(kk == nk - 1)
    def _():
        o_ref[...] = res_ref[...] + gate_ref[...] * (acc_ref[...] + p)


def _matmul_residual(a_slabs, w, res, mod5, layer, k_gate, mod_row, nk=1, tn=COL_TILE):
    m = a_slabs[0].shape[0]
    k, n = w.shape[1], w.shape[2]
    tm = min(ROW_TILE, m)
    tk = k // nk
    assert nk == 1 or len(a_slabs) == 1
    a_specs = [pl.BlockSpec((tm, tk if nk > 1 else a.shape[1]), lambda i, j, q: (i, q)) for a in a_slabs]
    return pl.pallas_call(
        functools.partial(_mm_res_kernel, nk=nk, n_a=len(a_slabs)),
        grid=(m // tm, n // tn, nk),
        in_specs=a_specs + [
            pl.BlockSpec((None, tk, tn), lambda i, j, q: (layer, q, j)),
            pl.BlockSpec((tm, tn), lambda i, j, q: (i, j)),
            pl.BlockSpec((None, None, None, 1, tn), lambda i, j, q: (layer, mod_row(i, tm), k_gate, 0, j)),
        ],
        out_specs=pl.BlockSpec((tm, tn), lambda i, j, q: (i, j)),
        out_shape=jax.ShapeDtypeStruct((m, n), F32),
        scratch_shapes=[pltpu.VMEM((tm, tn), F32)] if nk > 1 else [],
        compiler_params=_params(
            ("parallel", "parallel", "arbitrary"), 2 * (tm * tk * 2 + tk * tn * 2 + 2 * tm * tn * 4) + tm * tn * 4
        ),
        name="matmul_residual",
    )(*a_slabs, w, res, mod5)


def _cast_pad_cols_kernel(w_ref, o_ref):
    n = w_ref.shape[1]
    o_ref[:, :n] = w_ref[...].astype(o_ref.dtype)
    o_ref[:, n:] = jnp.zeros((o_ref.shape[0], o_ref.shape[1] - n), o_ref.dtype)


def _cast_pad_cols(w, n_pad, tr=256):
    depth, k, n = w.shape
    return pl.pallas_call(
        _cast_pad_cols_kernel,
        grid=(depth, k // tr),
        in_specs=[pl.BlockSpec((None, tr, n), lambda l, i: (l, i, 0))],
        out_specs=pl.BlockSpec((None, tr, n_pad), lambda l, i: (l, i, 0)),
        out_shape=jax.ShapeDtypeStruct((depth, k, n_pad), BF16),
        compiler_params=_params(("parallel", "parallel"), 2 * tr * (n * 4 + n_pad * 2)),
        name="cast_pad_cols",
    )(w)


def _cast_pad_rows_kernel(w_ref, o_ref, *, n_valid):
    i = pl.program_id(1)

    @pl.when(i < n_valid)
    def _():
        o_ref[...] = w_ref[...].astype(o_ref.dtype)

    @pl.when(i >= n_valid)
    def _():
        o_ref[...] = jnp.zeros_like(o_ref)


def _cast_pad_rows(w, k_pad, tr=256):
    depth, k, n = w.shape
    assert k % tr == 0 and k_pad % tr == 0
    n_valid = k // tr
    return pl.pallas_call(
        functools.partial(_cast_pad_rows_kernel, n_valid=n_valid),
        grid=(depth, k_pad // tr),
        in_specs=[pl.BlockSpec((None, tr, n), lambda l, i: (l, jnp.minimum(i, n_valid - 1), 0))],
        out_specs=pl.BlockSpec((None, tr, n), lambda l, i: (l, i, 0)),
        out_shape=jax.ShapeDtypeStruct((depth, k_pad, n), BF16),
        compiler_params=_params(("parallel", "parallel"), 2 * tr * n * 6),
        name="cast_pad_rows",
    )(w)


def _rope_tables(n_tok):
    quarter = HEAD_DIM // 4
    inv = 1.0 / (ROPE_THETA ** (jnp.arange(quarter, dtype=F32) / quarter))
    pos = jnp.arange(n_tok, dtype=jnp.int32)
    row = (pos // GRID_W).astype(F32)[:, None] * inv
    col = (pos % GRID_W).astype(F32)[:, None] * inv
    cos = jnp.concatenate([jnp.cos(row), jnp.cos(row), jnp.cos(col), jnp.cos(col)], axis=-1)
    sin = jnp.concatenate([-jnp.sin(row), jnp.sin(row), -jnp.sin(col), jnp.sin(col)], axis=-1)
    return cos, sin


def _rope_kernel(x_ref, cos_ref, sin_ref, o_ref, *, scale, n_heads):
    cos = cos_ref[...] * scale
    sin = sin_ref[...] * scale
    lane = lax.broadcasted_iota(jnp.int32, cos.shape, 1)
    first = (lane & (HEAD_DIM // 4)) == 0
    for h in range(n_heads):
        hs = slice(h * HEAD_DIM, (h + 1) * HEAD_DIM)
        x = x_ref[:, hs].astype(F32)
        partner = jnp.where(first, pltpu.roll(x, HEAD_DIM - HEAD_DIM // 4, 1), pltpu.roll(x, HEAD_DIM // 4, 1))
        o_ref[:, hs] = (x * cos + partner * sin).astype(o_ref.dtype)


def _rope(p, col0, width, cos, sin, scale, tr=256):
    m = p.shape[0]
    n_tok = cos.shape[0]
    tpb = n_tok // tr
    assert col0 % width == 0
    return pl.pallas_call(
        functools.partial(_rope_kernel, scale=scale, n_heads=width // HEAD_DIM),
        grid=(m // tr,),
        in_specs=[
            pl.BlockSpec((tr, width), lambda i: (i, col0 // width)),
            pl.BlockSpec((tr, HEAD_DIM), lambda i: (i % tpb, 0)),
            pl.BlockSpec((tr, HEAD_DIM), lambda i: (i % tpb, 0)),
        ],
        out_specs=pl.BlockSpec((tr, width), lambda i: (i, 0)),
        out_shape=jax.ShapeDtypeStruct((m, width), BF16),
        compiler_params=_params(("parallel",), 2 * tr * (width * 6 + HEAD_DIM * 8)),
        name="rope",
    )(p, cos, sin)


def _gla_kernel(*refs, nh, scale, fwd):
    ql, kl, vl, dl, qc, kc, vc, dc, up_ref, bias_ref = refs[:10]
    if fwd:
        ol, oc, st_ref = refs[10:]
    else:
        fl, fc, rl, rc, gn_ref, ol, oc, st_ref = refs[10:]
    s = pl.program_id(1)
    is_ctx = s == 0
    n_chunk = CTX_BLOCK // GLA_CHUNK

    @pl.when(s == 0)
    def _():
        st_ref[...] = jnp.zeros_like(st_ref)

    tb = CTX_BLOCK

    def in_scan_order(x):
        if fwd:
            return x
        return jnp.concatenate([x[(n_chunk - 1 - p) * GLA_CHUNK:(n_chunk - p) * GLA_CHUNK] for p in range(n_chunk)],
                               axis=0)

    def pick(rc_, rl_, cols):
        return jnp.where(is_ctx, rc_[:, cols], rl_[:, cols])

    def load(rc_, rl_, cols):
        return in_scan_order(pick(rc_, rl_, cols))

    row = lax.broadcasted_iota(jnp.int32, (tb, tb), 0)
    col = lax.broadcasted_iota(jnp.int32, (tb, tb), 1)
    incl = jnp.logical_and(row // GLA_CHUNK == col // GLA_CHUNK, (row >= col) if fwd else (col >= row))
    row_chunk = lax.broadcasted_iota(jnp.int32, (tb, GLA_DK), 0) // GLA_CHUNK

    def per_chunk(x):
        return jnp.concatenate([jnp.where(row_chunk == p, x, 0.0) for p in range(n_chunk)], axis=1)

    z = _dot(load(dc, dl, slice(None)).astype(BF16), up_ref[...]) + bias_ref[...]
    g = (jnp.minimum(z, 0.0) - jnp.log(1.0 + jnp.exp(-jnp.abs(z)))) * (1.0 / GLA_NORMALIZER)
    tmat = jnp.where(incl, 1.0, 0.0).astype(BF16)
    g_hi = g.astype(BF16)
    g_rest = g - g_hi.astype(F32)
    g_mid = g_rest.astype(BF16)
    g_lo = (g_rest - g_mid.astype(F32)).astype(BF16)
    cum = _dot(tmat, g_hi) + _dot(tmat, g_mid) + _dot(tmat, g_lo)
    tot = [jnp.sum(g[p * GLA_CHUNK:(p + 1) * GLA_CHUNK], axis=0, keepdims=True) for p in range(n_chunk)]
    tot_rows = jnp.concatenate([jnp.broadcast_to(t, (GLA_CHUNK, t.shape[1])) for t in tot], axis=0)

    for h in range(nh):
        ks = slice(h * GLA_DK, (h + 1) * GLA_DK)
        vs = slice(h * GLA_DV, (h + 1) * GLA_DV)
        bh = cum[:, ks]
        q = load(qc, ql, ks)
        k = load(kc, kl, ks)
        v = load(vc, vl, vs).astype(BF16)
        qd = q * (scale * jnp.exp(bh))
        ki = (k * jnp.exp(-bh)).astype(BF16)
        ku = k * jnp.exp(tot_rows[:, ks] - bh)
        a = jnp.where(incl, _dot_nt(qd.astype(BF16), ki), 0.0)
        inc = _dot_tn(v, per_chunk(ku).astype(BF16))
        st = st_ref[h]
        entering = []
        for p in range(n_chunk):
            entering.append(st)
            st = st * jnp.exp(tot[p][:, ks]) + inc[:, p * GLA_DK:(p + 1) * GLA_DK]
        st_ref[h] = st
        o = (_dot(a.astype(BF16), v)
             + _dot_nt(per_chunk(qd).astype(BF16), jnp.concatenate(entering, axis=1).astype(BF16)))
        o = in_scan_order(o)
        if not fwd:
            o = o + pick(fc, fl, vs)
            o = o * lax.rsqrt(jnp.mean(o * o, axis=-1, keepdims=True) + NORM_EPS) * gn_ref[...]
            o = o * _silu(pick(rc, rl, vs))
        o = o.astype(ol.dtype)

        @pl.when(is_ctx)
        def _():
            oc[:, vs] = o

        @pl.when(jnp.logical_not(is_ctx))
        def _():
            ol[:, vs] = o


def _gla(p_lat, p_ctx, dn_lat, dn_ctx, up2, bias2, norm_g, nb, n_tok, nh):
    qk = nh * GLA_DK
    dv = nh * GLA_DV
    tb = CTX_BLOCK
    nbl = n_tok // tb
    m_lat, m_ctx = nb * n_tok, nb * tb

    def scan(fwd, extra_in, extra_specs, out_dtype):
        def lat_blk(b, s):
            t = jnp.maximum(s - 1, 0)
            return b * nbl + (t if fwd else nbl - 1 - t)

        lat = lambda w, c: pl.BlockSpec((tb, w), lambda b, s: (lat_blk(b, s), c))
        ctx = lambda w, c: pl.BlockSpec((tb, w), lambda b, s: (b, c))
        d = 0 if fwd else 1
        vm = (2 * (2 * (2 * tb * qk + tb * dv + tb * LANE) * 4 + 2 * tb * dv * 4 + len(extra_in) * tb * dv * 4)
              + nh * GLA_DV * GLA_DK * 4)
        return pl.pallas_call(
            functools.partial(_gla_kernel, nh=nh, scale=GLA_DK ** -0.5, fwd=fwd),
            grid=(nb, nbl + 1),
            in_specs=[
                lat(qk, 0), lat(qk, 1), lat(dv, 1), lat(LANE, 0),
                ctx(qk, 0), ctx(qk, 1), ctx(dv, 1), ctx(LANE, 0),
                pl.BlockSpec((None, LANE, qk), lambda b, s: (d, 0, 0)),
                pl.BlockSpec((None, 1, qk), lambda b, s: (d, 0, 0)),
            ] + extra_specs(lat, ctx),
            out_specs=[lat(dv, 0), ctx(dv, 0)],
            out_shape=[jax.ShapeDtypeStruct((m_lat, dv), out_dtype), jax.ShapeDtypeStruct((m_ctx, dv), out_dtype)],
            scratch_shapes=[pltpu.VMEM((nh, GLA_DV, GLA_DK), F32)],
            compiler_params=_params(("parallel", "arbitrary"), vm),
            name="gla_scan_fwd" if fwd else "gla_scan_bwd",
        )(p_lat, p_lat, p_lat, dn_lat, p_ctx, p_ctx, p_ctx, dn_ctx, up2, bias2, *extra_in)

    of_lat, of_ctx = scan(True, (), lambda lat, ctx: [], F32)
    return scan(
        False,
        (of_lat, of_ctx, p_lat, p_ctx, norm_g.reshape(1, GLA_DV)),
        lambda lat, ctx: [lat(dv, 0), ctx(dv, 0), lat(dv, 2), ctx(dv, 2),
                          pl.BlockSpec((1, GLA_DV), lambda b, s: (0, 0))],
        BF16,
    )


def _swa_kernel(sink_ref, q_ref, kp, kc, kn, vp, vc, vn, kx, vx, o_ref, *, n_blk, kvs):
    i = pl.program_id(2)
    span = SWA_BLOCK + 2 * SWA_WINDOW
    gq = SWA_GROUP * SWA_BLOCK
    kj = lax.broadcasted_iota(jnp.int32, (span, gq), 0)
    lane = lax.broadcasted_iota(jnp.int32, (span, gq), 1)
    qi = lane & (SWA_BLOCK - 1)
    key_pos = (i - 1) * SWA_BLOCK + kj
    ok = jnp.logical_and(jnp.abs(qi + SWA_WINDOW - kj) <= SWA_WINDOW,
                         jnp.logical_and(key_pos >= 0, key_pos < n_blk * SWA_BLOCK))
    head = lax.broadcasted_iota(jnp.int32, (1, gq), 1) // SWA_BLOCK
    for kv in range(kvs):
        hkv = pl.program_id(1) * kvs + kv
        ks = slice(kv * HEAD_DIM, (kv + 1) * HEAD_DIM)
        kwin = jnp.concatenate([kp[:, ks], kc[:, ks], kn[:, ks]], axis=0)
        vwin = jnp.concatenate([vp[:, ks], vc[:, ks], vn[:, ks]], axis=0).astype(BF16)
        kctx = kx[:, ks].astype(BF16)
        vctx = vx[:, ks].astype(BF16)
        q0 = kv * SWA_GROUP * HEAD_DIM
        qs = jnp.concatenate([q_ref[:, q0 + g * HEAD_DIM:q0 + (g + 1) * HEAD_DIM] for g in range(SWA_GROUP)], axis=0)
        s_win = jnp.where(ok, _dot_nt(kwin, qs), -jnp.inf)
        s_ctx = _dot_nt(kctx, qs)
        sink = jnp.zeros((1, gq), F32)
        for g in range(SWA_GROUP):
            sink = jnp.where(head == g, sink_ref[0, hkv * SWA_GROUP + g], sink)
        m = jnp.maximum(jnp.maximum(jnp.max(s_win, axis=0, keepdims=True), jnp.max(s_ctx, axis=0, keepdims=True)),
                        sink)
        p_win = jnp.exp(s_win - m)
        p_ctx = jnp.exp(s_ctx - m)
        den = jnp.sum(p_win, axis=0, keepdims=True) + jnp.sum(p_ctx, axis=0, keepdims=True) + jnp.exp(sink - m)
        ot = (_dot_tn(vctx, p_ctx.astype(BF16)) + _dot_tn(vwin, p_win.astype(BF16))) / den
        for g in range(SWA_GROUP):
            hs = slice(q0 + g * HEAD_DIM, q0 + (g + 1) * HEAD_DIM)
            o_ref[:, hs] = ot[:, g * SWA_BLOCK:(g + 1) * SWA_BLOCK].T.astype(o_ref.dtype)


def _swa(sink, q_rope, k_rope, p_lat, p_ctx, kcol, vcol, nb, n_tok, n_kv):
    n_blk = n_tok // SWA_BLOCK
    tb = SWA_BLOCK
    kvs = 2 if n_kv % 2 == 0 else 1
    kw = kvs * HEAD_DIM
    gw = kvs * SWA_GROUP * HEAD_DIM
    assert kcol % kw == 0 and vcol % kw == 0
    kc0 = kcol // kw
    vc0 = vcol // kw
    prev = lambda b, h, i: b * n_blk + jnp.maximum(i - 1, 0)
    cur = lambda b, h, i: b * n_blk + i
    nxt = lambda b, h, i: b * n_blk + jnp.minimum(i + 1, n_blk - 1)
    kspec = lambda f: pl.BlockSpec((tb, kw), lambda b, h, i: (f(b, h, i), h))
    vspec = lambda f: pl.BlockSpec((tb, kw), lambda b, h, i: (f(b, h, i), vc0 + h))
    return pl.pallas_call(
        functools.partial(_swa_kernel, n_blk=n_blk, kvs=kvs),
        grid=(nb, n_kv // kvs, n_blk),
        in_specs=[
            pl.BlockSpec(memory_space=pltpu.SMEM),
            pl.BlockSpec((tb, gw), lambda b, h, i: (cur(b, h, i), h)),
            kspec(prev), kspec(cur), kspec(nxt),
            vspec(prev), vspec(cur), vspec(nxt),
            pl.BlockSpec((CTX_BLOCK, kw), lambda b, h, i: (b, kc0 + h)),
            pl.BlockSpec((CTX_BLOCK, kw), lambda b, h, i: (b, vc0 + h)),
        ],
        out_specs=pl.BlockSpec((tb, gw), lambda b, h, i: (cur(b, h, i), h)),
        out_shape=jax.ShapeDtypeStruct((nb * n_tok, n_kv * SWA_GROUP * HEAD_DIM), BF16),
        compiler_params=_params(("parallel", "parallel", "parallel"), 12 << 20),
        name="swa",
    )(sink, q_rope, k_rope, k_rope, k_rope, p_lat, p_lat, p_lat, p_ctx, p_ctx)


def _ctx_sink_kernel(sink_ref, q_ref, kx, vx, o_ref, *, scale):
    hkv = pl.program_id(1)
    kctx = kx[...].astype(BF16)
    vctx = vx[...].astype(BF16)
    for g in range(SWA_GROUP):
        hs = slice(g * HEAD_DIM, (g + 1) * HEAD_DIM)
        q = (q_ref[:, hs].astype(F32) * scale).astype(BF16)
        s = _dot_nt(q, kctx)
        sink = sink_ref[0, hkv * SWA_GROUP + g]
        m = jnp.maximum(jnp.max(s, axis=-1, keepdims=True), sink)
        p = jnp.exp(s - m)
        den = jnp.sum(p, axis=-1, keepdims=True) + jnp.exp(sink - m)
        o_ref[:, hs] = (_dot(p.astype(BF16), vctx) / den).astype(o_ref.dtype)


def _ctx_sink(sink, p_ctx, qcol, kcol, vcol, nb, n_kv):
    gw = SWA_GROUP * HEAD_DIM
    return pl.pallas_call(
        functools.partial(_ctx_sink_kernel, scale=HEAD_DIM ** -0.5),
        grid=(nb, n_kv),
        in_specs=[
            pl.BlockSpec(memory_space=pltpu.SMEM),
            pl.BlockSpec((CTX_BLOCK, gw), lambda b, h: (b, qcol // gw + h)),
            pl.BlockSpec((CTX_BLOCK, HEAD_DIM), lambda b, h: (b, kcol // HEAD_DIM + h)),
            pl.BlockSpec((CTX_BLOCK, HEAD_DIM), lambda b, h: (b, vcol // HEAD_DIM + h)),
        ],
        out_specs=pl.BlockSpec((CTX_BLOCK, gw), lambda b, h: (b, h)),
        out_shape=jax.ShapeDtypeStruct((nb * CTX_BLOCK, n_kv * gw), BF16),
        compiler_params=_params(("parallel", "parallel"), 4 << 20),
        name="ctx_sink_attention",
    )(sink, p_ctx, p_ctx, p_ctx)


def _na_bias_table(rpb, rows):
    qcol = np.arange(GRID_W)[:, None]
    kcol = np.arange(GRID_W)[None, :]
    col_start = np.clip(qcol - NA_COLS // 2, 0, GRID_W - NA_COLS)
    col_ok = (kcol >= col_start) & (kcol < col_start + NA_COLS)
    dc = np.clip(kcol - qcol + NA_COLS - 1, 0, 2 * NA_COLS - 2)
    n_dc = 2 * NA_COLS - 1
    onehot = (dc[None] == np.arange(n_dc)[:, None, None]).astype(np.float32)
    cols = jnp.einsum('hrd,dqk->hrqk', rpb.astype(F32), onehot, precision=lax.Precision.HIGHEST)
    cols = jnp.where(col_ok[None, None], cols, -jnp.inf).transpose(0, 1, 3, 2)
    masked = jnp.full(cols.shape[:1] + cols.shape[2:], -jnp.inf, F32)

    span = NA_ROWS + NA_PAIR - 1
    patterns = []
    for r0 in range(0, rows, NA_PAIR):
        rs = [min(max(r0 + i - NA_ROWS // 2, 0), rows - NA_ROWS) for i in range(NA_PAIR)]
        u = min(rs[0], rows - span)
        patterns.append(tuple((r0 + i - rs[i], rs[i] - u) for i in range(NA_PAIR)))
    variants = sorted(set(patterns), key=patterns.index)
    blocks = []
    for pat in variants:
        per_query_row = []
        for delta, off in pat:
            krows = [cols[:, kr - off - delta + NA_ROWS - 1] if 0 <= kr - off < NA_ROWS else masked
                     for kr in range(span)]
            per_query_row.append(jnp.stack(krows, axis=1))
        blocks.append(jnp.concatenate(per_query_row, axis=-1).reshape(rpb.shape[0], span * GRID_W, NA_PAIR * GRID_W))
    ids = np.array([variants.index(p) for p in patterns], dtype=np.int32)
    return jnp.stack(blocks, axis=1), ids


def _na_kernel(ids_ref, q_ref, k_ref, v_ref, kx, vx, bias_ref, o_ref, *, nh, scale):
    for h in range(nh):
        hs = slice(h * HEAD_DIM, (h + 1) * HEAD_DIM)
        q = (q_ref[:, hs].astype(F32) * scale).astype(BF16)
        s_nb = _dot_nt(k_ref[:, hs].astype(BF16), q) + bias_ref[h]
        s_ctx = _dot_nt(kx[:, hs].astype(BF16), q)
        m = jnp.maximum(jnp.max(s_nb, axis=0, keepdims=True), jnp.max(s_ctx, axis=0, keepdims=True))
        p_nb = jnp.exp(s_nb - m)
        p_ctx = jnp.exp(s_ctx - m)
        den = jnp.sum(p_nb, axis=0, keepdims=True) + jnp.sum(p_ctx, axis=0, keepdims=True)
        ot = (_dot_tn(vx[:, hs].astype(BF16), p_ctx.astype(BF16))
              + _dot_tn(v_ref[:, hs].astype(BF16), p_nb.astype(BF16)))
        o_ref[:, hs] = (ot / den).T.astype(o_ref.dtype)


def _na(p_lat, p_ctx, rpb, nb, n_tok, nh):
    rows = n_tok // GRID_W
    w = nh * HEAD_DIM
    span = NA_ROWS + NA_PAIR - 1
    win = span * GRID_W
    tq = NA_PAIR * GRID_W
    n_pair = rows // NA_PAIR
    bias_tab, ids = _na_bias_table(rpb, rows)
    u = lambda g: jnp.clip(g * NA_PAIR - NA_ROWS // 2, 0, rows - span)
    kv = lambda c: pl.BlockSpec((pl.Element(win), pl.Element(w)),
                                lambda b, g, ids: ((b * rows + u(g)) * GRID_W, c * w))
    vm = 2 * (tq * w * 4 + 2 * win * w * 4 + 2 * CTX_BLOCK * w * 4 + nh * tq * win * 4 + tq * w * 2)
    return pl.pallas_call(
        functools.partial(_na_kernel, nh=nh, scale=HEAD_DIM ** -0.5),
        grid_spec=pltpu.PrefetchScalarGridSpec(
            num_scalar_prefetch=1,
            grid=(nb, n_pair),
            in_specs=[
                pl.BlockSpec((tq, w), lambda b, g, ids: (b * n_pair + g, 0)),
                kv(1), kv(2),
                pl.BlockSpec((CTX_BLOCK, w), lambda b, g, ids: (b, 1)),
                pl.BlockSpec((CTX_BLOCK, w), lambda b, g, ids: (b, 2)),
                pl.BlockSpec((nh, None, win, tq), lambda b, g, ids: (0, ids[g], 0, 0)),
            ],
            out_specs=pl.BlockSpec((tq, w), lambda b, g, ids: (b * n_pair + g, 0)),
        ),
        out_shape=jax.ShapeDtypeStruct((nb * n_tok, w), BF16),
        compiler_params=_params(("parallel", "arbitrary"), vm),
        name="neighborhood_attention",
    )(jnp.asarray(ids), p_lat, p_lat, p_lat, p_ctx, p_ctx, bias_tab)


def _diff_kernel(q_ref, k_ref, v_ref, kx, vx, lq1, lk1, lq2, lk2, g_ref, o_ref, m_ref, l_ref, acc_ref,
                 sa_ref, sb_ref, pa_ref, pb_ref, pc_ref, aa_ref, ab_ref, ac_ref, vt_ref, vxt_ref,
                 *, n_blk, tk, lam_init):

    @pl.when(pl.program_id(2) == 0)
    def _():
        for j in range(n_blk):
            vt_ref[j] = v_ref[j * tk:(j + 1) * tk, :].T
        vxt_ref[...] = vx[...].astype(BF16).T

    vxt = vxt_ref
    m_ref[...] = jnp.full_like(m_ref, -jnp.inf)
    l_ref[...] = jnp.zeros_like(l_ref)
    acc_ref[...] = jnp.zeros_like(acc_ref)

    def scores(keys):
        return [_dot_nt(keys[:, t * HEAD_DIM:(t + 1) * HEAD_DIM], q_ref[:, t * HEAD_DIM:(t + 1) * HEAD_DIM])
                for t in range(2)]

    def lat_scores(j, dst):
        s = scores(k_ref[pl.ds(pl.multiple_of(j * tk, tk), tk), :])
        dst[0] = s[0]
        dst[1] = s[1]

    def softmax(s, p_dst, a_dst):
        for t in range(2):
            m_old = m_ref[t]
            m_new = jnp.maximum(m_old, jnp.max(s[t], axis=0, keepdims=True))
            alpha = jnp.exp2(m_old - m_new)
            p = jnp.exp2(s[t] - m_new)
            l_ref[t] = alpha * l_ref[t] + jnp.sum(p, axis=0, keepdims=True)
            m_ref[t] = m_new
            p_dst[t] = p.astype(BF16)
            a_dst[t] = alpha

    def add_values(p_src, a_src, vt):
        for t in range(2):
            acc_ref[t] = a_src[t] * acc_ref[t] + _dot(vt, p_src[t])

    def stage(s_cur, p_cur, a_cur, next_block, s_next, prev):
        if next_block is not None:
            lat_scores(next_block, s_next)
        add_values(*prev)
        softmax((s_cur[0], s_cur[1]), p_cur, a_cur)

    s_ctx = scores(kx[...].astype(BF16))
    lat_scores(0, sa_ref)
    softmax(s_ctx, pc_ref, ac_ref)
    stage(sa_ref, pa_ref, aa_ref, 1, sb_ref, (pc_ref, ac_ref, vxt[...]))

    def body(jj, carry):
        j = 2 * jj + 1
        stage(sb_ref, pb_ref, ab_ref, j + 1, sa_ref, (pa_ref, aa_ref, vt_ref[j - 1]))
        stage(sa_ref, pa_ref, aa_ref, j + 2, sb_ref, (pb_ref, ab_ref, vt_ref[j]))
        return carry

    lax.fori_loop(0, (n_blk - 2) // 2, body, 0)
    stage(sb_ref, pb_ref, ab_ref, None, None, (pa_ref, aa_ref, vt_ref[n_blk - 2]))
    add_values(pb_ref, ab_ref, vt_ref[n_blk - 1])

    lam = (jnp.exp(jnp.sum(lq1[...] * lk1[...], axis=-1, keepdims=True))
           - jnp.exp(jnp.sum(lq2[...] * lk2[...], axis=-1, keepdims=True)) + lam_init)
    o = acc_ref[0] / l_ref[0] - lam * (acc_ref[1] / l_ref[1])
    y = o * lax.rsqrt(jnp.mean(o * o, axis=0, keepdims=True) + NORM_EPS) * (g_ref[...] * (1.0 - lam_init))
    o_ref[...] = y.T.astype(o_ref.dtype)


def _diff(q_rope, k_rope, p_lat, p_ctx, kcol, vcol, lq1, lk1, lq2, lk2, norm_g, lam_init, nb, n_tok, nh,
          tq=512, tk=DIFF_KEY_BLOCK):
    nq = n_tok // tq
    n_blk = n_tok // tk
    assert n_blk >= 2 and n_blk % 2 == 0, "the key-block pipeline alternates two buffers"
    w = 2 * HEAD_DIM
    vec = lambda: pl.BlockSpec((1, HEAD_DIM), lambda b, h, i: (0, 0))
    vm = (2 * (tq * w * 2 + 2 * n_tok * w * 2 + 2 * CTX_BLOCK * w * 2 + tq * w * 2)
          + n_tok * w * 2 + 2 * tq * w * 4 + 2 * 2 * tk * tq * 6 + tq * w * 8)
    return pl.pallas_call(
        functools.partial(_diff_kernel, n_blk=n_blk, tk=tk, lam_init=lam_init),
        grid=(nb, nh, nq),
        in_specs=[
            pl.BlockSpec((tq, w), lambda b, h, i: (b * nq + i, h)),
            pl.BlockSpec((n_tok, w), lambda b, h, i: (b, h)),
            pl.BlockSpec((n_tok, w), lambda b, h, i: (b, vcol // w + h)),
            pl.BlockSpec((CTX_BLOCK, w), lambda b, h, i: (b, kcol // w + h)),
            pl.BlockSpec((CTX_BLOCK, w), lambda b, h, i: (b, vcol // w + h)),
            vec(), vec(), vec(), vec(),
            pl.BlockSpec((DIFF_DV, 1), lambda b, h, i: (0, 0)),
        ],
        out_specs=pl.BlockSpec((tq, w), lambda b, h, i: (b * nq + i, h)),
        out_shape=jax.ShapeDtypeStruct((nb * n_tok, nh * w), BF16),
        scratch_shapes=[
            pltpu.VMEM((2, 1, tq), F32),
            pltpu.VMEM((2, 1, tq), F32),
            pltpu.VMEM((2, DIFF_DV, tq), F32),
            pltpu.VMEM((2, tk, tq), F32),
            pltpu.VMEM((2, tk, tq), F32),
            pltpu.VMEM((2, tk, tq), BF16),
            pltpu.VMEM((2, tk, tq), BF16),
            pltpu.VMEM((2, CTX_BLOCK, tq), BF16),
            pltpu.VMEM((2, 1, tq), F32),
            pltpu.VMEM((2, 1, tq), F32),
            pltpu.VMEM((2, 1, tq), F32),
            pltpu.VMEM((n_blk, w, tk), BF16),
            pltpu.VMEM((w, CTX_BLOCK), BF16),
        ],
        compiler_params=_params(("parallel", "parallel", "arbitrary"), vm),
        name="diff_attention",
    )(q_rope, k_rope, p_lat, p_ctx, p_ctx, lq1.reshape(1, -1), lk1.reshape(1, -1), lq2.reshape(1, -1),
      lk2.reshape(1, -1), norm_g.reshape(-1, 1))


def _pad_cols(w, n):
    return jnp.pad(w, ((0, 0), (0, n - w.shape[1])))


def kernel(x, c, ctx, c_ctx, ada_w, ada_b, norm_mix_g, norm_ffn_g, w_out, ffn_w_gate, ffn_w_up, ffn_w_down,
           ev_w_in, gla_gate_up_f, gla_gate_bias_f, gla_gate_up_b, gla_gate_bias_b, gla_norm_g, swa_sink,
           od_w_in, na_rpb, diff_lq1, diff_lk1, diff_lq2, diff_lk2, diff_norm_g, final_norm_g):
    nb, n_tok, d = x.shape
    lc = ctx.shape[1]
    depth = ada_w.shape[0]
    f_hidden = ffn_w_gate.shape[2]
    half = d // 2
    assert lc == CTX_BLOCK and nb < 8 and depth == 2, "layout assumes a 256-token context, batch < 8, two layers"
    gla_heads = half // GLA_DV
    gla_qk = gla_heads * GLA_DK
    n_q = half // HEAD_DIM
    n_kv = n_q // SWA_GROUP
    diff_heads = half // (2 * HEAD_DIM)

    xs = x.reshape(nb * n_tok, d)
    cs = ctx.reshape(nb * lc, d)
    cos, sin = _rope_tables(n_tok)

    s8 = jnp.zeros((8, d), F32).at[:nb].set(c).at[nb].set(c_ctx)
    mod5 = _mod_vectors(s8, ada_w, ada_b).reshape(depth, 8, 6, 1, d)
    lat_row = lambda i, tm: (i * tm) // n_tok
    ctx_row = lambda i, tm: nb

    f_pad = -(-f_hidden // FFN_PAD) * FFN_PAD

    wg = _cast_pad_cols(ffn_w_gate, f_pad)
    wu = _cast_pad_cols(ffn_w_up, f_pad)
    wd = _cast_pad_rows(ffn_w_down, f_pad)
    wo = w_out.astype(BF16)

    def ffn(stream, layer, mod_row):
        h2 = _norm_mod(stream, norm_ffn_g[layer], mod5, layer, 3, 4, mod_row)
        a = _matmul_swiglu(h2, wg, wu, layer)
        return _matmul_residual([a], wd, stream, mod5, layer, 5, mod_row, nk=4, tn=2 * COL_TILE)

    layer = 0
    w_in = ev_w_in[0]
    g0 = 2 * gla_qk + 2 * half
    w_main = jnp.concatenate([w_in[:, :g0], w_in[:, g0 + 2 * GLA_RANK:]], axis=1).astype(BF16)
    w_dn = _pad_cols(w_in[:, g0:g0 + 2 * GLA_RANK], LANE).astype(BF16)
    swa_q0 = g0
    swa_k0 = swa_q0 + half
    swa_v0 = swa_k0 + n_kv * HEAD_DIM
    up2 = jnp.zeros((2, LANE, gla_qk), F32)
    up2 = up2.at[0, :GLA_RANK].set(gla_gate_up_f[0]).at[1, GLA_RANK:2 * GLA_RANK].set(gla_gate_up_b[0]).astype(BF16)
    bias2 = jnp.stack([gla_gate_bias_f[0], gla_gate_bias_b[0]]).reshape(2, 1, gla_qk)

    h = _norm_mod(xs, norm_mix_g[layer], mod5, layer, 0, 1, lat_row)
    hc = _norm_mod(cs, norm_mix_g[layer], mod5, layer, 0, 1, ctx_row)
    p_lat = _matmul(h, w_main)
    p_ctx = _matmul(hc, w_main)
    dn_lat = _matmul(h, w_dn)
    dn_ctx = _matmul(hc, w_dn)

    ya, yac = _gla(p_lat, p_ctx, dn_lat, dn_ctx, up2, bias2, gla_norm_g[0], nb, n_tok, gla_heads)

    sink = swa_sink[0].reshape(1, n_q)
    q_rope = _rope(p_lat, swa_q0, half, cos, sin, HEAD_DIM ** -0.5)
    k_rope = _rope(p_lat, swa_k0, n_kv * HEAD_DIM, cos, sin, 1.0)
    yb = _swa(sink, q_rope, k_rope, p_lat, p_ctx, swa_k0, swa_v0, nb, n_tok, n_kv)
    ybc = _ctx_sink(sink, p_ctx, swa_q0, swa_k0, swa_v0, nb, n_kv)

    xs = _matmul_residual([ya, yb], wo, xs, mod5, layer, 2, lat_row, tn=2 * COL_TILE)
    cs = _matmul_residual([yac, ybc], wo, cs, mod5, layer, 2, ctx_row, tn=2 * COL_TILE)
    xs = ffn(xs, layer, lat_row)
    cs = ffn(cs, layer, ctx_row)

    layer = 1
    w_in = od_w_in[0].astype(BF16)
    h = _norm_mod(xs, norm_mix_g[layer], mod5, layer, 0, 1, lat_row)
    hc = _norm_mod(cs, norm_mix_g[layer], mod5, layer, 0, 1, ctx_row)
    p_lat = _matmul(h, w_in, BF16)
    p_ctx = _matmul(hc, w_in, BF16)

    yn =_na(p_lat, p_ctx, na_rpb[0], nb, n_tok, n_q)

    dq0 = 3 * half
    dk0 = dq0 + half
    dv0 = dk0 + half
    lam_init = 0.8 - 0.6 * math.exp(-0.3 * layer)
    qd = _rope(p_lat, dq0, half, cos, sin, HEAD_DIM ** -0.5 * math.log2(math.e))
    kd = _rope(p_lat, dk0, half, cos, sin, 1.0)
    yd = _diff(qd, kd, p_lat, p_ctx, dk0, dv0, diff_lq1[0], diff_lk1[0], diff_lq2[0], diff_lk2[0],
               diff_norm_g[0], lam_init, nb, n_tok, diff_heads)

    xs = _matmul_residual([yn, yd], wo, xs, mod5, layer, 2, lat_row, tn=2 * COL_TILE)
    xs = ffn(xs, layer, lat_row)
    return _final_norm(xs, final_norm_g).reshape(nb, n_tok, d)
```

```python
import functools
import math

import numpy as np
import jax
import jax.numpy as jnp
from jax import lax
from jax.experimental import pallas as pl
from jax.experimental.pallas import tpu as pltpu

HEAD_DIM = 128
GRID_W = 64
ROPE_THETA = 10000.0
NORM_EPS = 1e-6
GLA_DK = 128
GLA_DV = 256
GLA_RANK = 16
GLA_NORMALIZER = 16.0
GLA_CHUNK = 64
SWA_GROUP = 4
SWA_WINDOW = 128
SWA_BLOCK = 128
NA_ROWS = 8
NA_COLS = 16
NA_PAIR = 4
DIFF_DV = 2 * HEAD_DIM
DIFF_KEY_BLOCK = 2048
CTX_BLOCK = 256

V7X_VMEM_BUDGET = 60 * 1024 * 1024
LANE = 128
ROW_TILE = 1024
COL_TILE = 512
FFN_PAD = 1024

F32 = jnp.float32
BF16 = jnp.bfloat16


def _params(sem, vmem_bytes):
    limit = int(min(V7X_VMEM_BUDGET, vmem_bytes * 5 // 4 + (4 << 20)))
    return pltpu.CompilerParams(dimension_semantics=sem, vmem_limit_bytes=limit)


def _dot(a, b):
    return jnp.dot(a, b, preferred_element_type=F32)


def _dot_nt(a, b):
    return lax.dot_general(a, b, (((1,), (1,)), ((), ())), preferred_element_type=F32)


def _dot_tn(a, b):
    return lax.dot_general(a, b, (((0,), (0,)), ((), ())), preferred_element_type=F32)


def _silu(t):
    return t * jax.nn.sigmoid(t)


def _mod_kernel(s_ref, w_ref, b_ref, o_ref):
    a = _silu(s_ref[...]).astype(BF16)
    o_ref[...] = _dot(a, w_ref[...].astype(BF16)) + b_ref[...]


def _mod_vectors(s8, ada_w, ada_b):
    depth, d, n = ada_w.shape
    tn = COL_TILE
    return pl.pallas_call(
        _mod_kernel,
        grid=(depth, n // tn),
        in_specs=[
            pl.BlockSpec((8, d), lambda l, j: (0, 0)),
            pl.BlockSpec((None, d, tn), lambda l, j: (l, 0, j)),
            pl.BlockSpec((None, 1, tn), lambda l, j: (l, 0, j)),
        ],
        out_specs=pl.BlockSpec((None, 8, tn), lambda l, j: (l, 0, j)),
        out_shape=jax.ShapeDtypeStruct((depth, 8, n), F32),
        compiler_params=_params(("parallel", "parallel"), 2 * d * tn * 4 + d * tn * 2),
        name="adaln_mod",
    )(s8, ada_w, ada_b.reshape(depth, 1, n))


def _norm_mod_kernel(x_ref, g_ref, sh_ref, sc_ref, o_ref):
    x = x_ref[...]
    r = lax.rsqrt(jnp.mean(x * x, axis=-1, keepdims=True) + NORM_EPS)
    y = x * r * g_ref[...]
    o_ref[...] = (y * (1.0 + sc_ref[...]) + sh_ref[...]).astype(o_ref.dtype)


def _norm_mod(x2, g, mod5, layer, k_shift, k_scale, mod_row, tm=512):
    m, d = x2.shape
    return pl.pallas_call(
        _norm_mod_kernel,
        grid=(m // tm,),
        in_specs=[
            pl.BlockSpec((tm, d), lambda i: (i, 0)),
            pl.BlockSpec((1, d), lambda i: (0, 0)),
            pl.BlockSpec((None, None, None, 1, d), lambda i: (layer, mod_row(i, tm), k_shift, 0, 0)),
            pl.BlockSpec((None, None, None, 1, d), lambda i: (layer, mod_row(i, tm), k_scale, 0, 0)),
        ],
        out_specs=pl.BlockSpec((tm, d), lambda i: (i, 0)),
        out_shape=jax.ShapeDtypeStruct((m, d), BF16),
        compiler_params=_params(("parallel",), 2 * tm * d * 6),
        name="norm_mod",
    )(x2, g.reshape(1, d), mod5, mod5)


def _final_norm_kernel(x_ref, g_ref, o_ref):
    x = x_ref[...]
    r = lax.rsqrt(jnp.mean(x * x, axis=-1, keepdims=True) + NORM_EPS)
    o_ref[...] = x * r * g_ref[...]


def _final_norm(x2, g, tm=512):
    m, d = x2.shape
    return pl.pallas_call(
        _final_norm_kernel,
        grid=(m // tm,),
        in_specs=[pl.BlockSpec((tm, d), lambda i: (i, 0)), pl.BlockSpec((1, d), lambda i: (0, 0))],
        out_specs=pl.BlockSpec((tm, d), lambda i: (i, 0)),
        out_shape=jax.ShapeDtypeStruct((m, d), F32),
        compiler_params=_params(("parallel",), 2 * tm * d * 8),
        name="final_norm",
    )(x2, g.reshape(1, d))


def _col_tile(n):
    return next(t for t in range(COL_TILE, 0, -LANE) if n % t == 0)


def _mm_kernel(a_ref, w_ref, o_ref):
    o_ref[...] = _dot(a_ref[...], w_ref[...]).astype(o_ref.dtype)


def _matmul(a, w, out_dtype=F32):
    m, k = a.shape
    n = w.shape[1]
    tm = min(ROW_TILE, m)
    tn = _col_tile(n)
    ob = jnp.dtype(out_dtype).itemsize
    return pl.pallas_call(
        _mm_kernel,
        grid=(m // tm, n // tn),
        in_specs=[pl.BlockSpec((tm, k), lambda i, j: (i, 0)), pl.BlockSpec((k, tn), lambda i, j: (0, j))],
        out_specs=pl.BlockSpec((tm, tn), lambda i, j: (i, j)),
        out_shape=jax.ShapeDtypeStruct((m, n), out_dtype),
        compiler_params=_params(("parallel", "parallel"), 2 * (tm * k * 2 + k * tn * 2 + tm * tn * ob)),
        name="matmul",
    )(a, w)


def _mm_swiglu_kernel(a_ref, wg_ref, wu_ref, o_ref):
    a = a_ref[...]
    g = _dot(a, wg_ref[...])
    u = _dot(a, wu_ref[...])
    o_ref[...] = (_silu(g) * u).astype(o_ref.dtype)


def _matmul_swiglu(a, wg, wu, layer):
    m, k = a.shape
    n = wg.shape[2]
    tm = min(ROW_TILE, m)
    tn = COL_TILE
    return pl.pallas_call(
        _mm_swiglu_kernel,
        grid=(m // tm, n // tn),
        in_specs=[
            pl.BlockSpec((tm, k), lambda i, j: (i, 0)),
            pl.BlockSpec((None, k, tn), lambda i, j: (layer, 0, j)),
            pl.BlockSpec((None, k, tn), lambda i, j: (layer, 0, j)),
        ],
        out_specs=pl.BlockSpec((tm, tn), lambda i, j: (i, j)),
        out_shape=jax.ShapeDtypeStruct((m, n), BF16),
        compiler_params=_params(("parallel", "parallel"), 2 * (tm * k * 2 + 2 * k * tn * 2 + tm * tn * 2)),
        name="matmul_swiglu",
    )(a, wg, wu)


def _mm_res_kernel(*refs, nk, n_a):
    a_refs, (w_ref, res_ref, gate_ref, o_ref), scratch = refs[:n_a], refs[n_a:n_a + 4], refs[n_a + 4:]
    k0 = 0
    p = None
    for a_ref in a_refs:
        ka = a_ref.shape[1]
        d = _dot(a_ref[...], w_ref[k0:k0 + ka, :])
        p = d if p is None else p + d
        k0 += ka
    if nk == 1:
        o_ref[...] = res_ref[...] + gate_ref[...] * p
        return
    acc_ref, = scratch
    kk = pl.program_id(2)

    @pl.when(kk == 0)
    def _():
        acc_ref[...] = p

    @pl.when(jnp.logical_and(kk > 0, kk < nk - 1))
    def _():
        acc_ref[...] += p

    @pl.when(kk == nk - 1)
    def _():
        o_ref[...] = res_ref[...] + gate_ref[...] * (acc_ref[...] + p)


def _matmul_residual(a_slabs, w, res, mod5, layer, k_gate, mod_row, nk=1, tn=COL_TILE):
    m = a_slabs[0].shape[0]
    k, n = w.shape[1], w.shape[2]
    tm = min(ROW_TILE, m)
    tk = k // nk
    assert nk == 1 or len(a_slabs) == 1
    a_specs = [pl.BlockSpec((tm, tk if nk > 1 else a.shape[1]), lambda i, j, q: (i, q)) for a in a_slabs]
    return pl.pallas_call(
        functools.partial(_mm_res_kernel, nk=nk, n_a=len(a_slabs)),
        grid=(m // tm, n // tn, nk),
        in_specs=a_specs + [
            pl.BlockSpec((None, tk, tn), lambda i, j, q: (layer, q, j)),
            pl.BlockSpec((tm, tn), lambda i, j, q: (i, j)),
            pl.BlockSpec((None, None, None, 1, tn), lambda i, j, q: (layer, mod_row(i, tm), k_gate, 0, j)),
        ],
        out_specs=pl.BlockSpec((tm, tn), lambda i, j, q: (i, j)),
        out_shape=jax.ShapeDtypeStruct((m, n), F32),
        scratch_shapes=[pltpu.VMEM((tm, tn), F32)] if nk > 1 else [],
        compiler_params=_params(
            ("parallel", "parallel", "arbitrary"), 2 * (tm * tk * 2 + tk * tn * 2 + 2 * tm * tn * 4) + tm * tn * 4
        ),
        name="matmul_residual",
    )(*a_slabs, w, res, mod5)


def _cast_pad_cols_kernel(w_ref, o_ref):
    n = w_ref.shape[1]
    o_ref[:, :n] = w_ref[...].astype(o_ref.dtype)
    o_ref[:, n:] = jnp.zeros((o_ref.shape[0], o_ref.shape[1] - n), o_ref.dtype)


def _cast_pad_cols(w, n_pad, tr=256):
    depth, k, n = w.shape
    return pl.pallas_call(
        _cast_pad_cols_kernel,
        grid=(depth, k // tr),
        in_specs=[pl.BlockSpec((None, tr, n), lambda l, i: (l, i, 0))],
        out_specs=pl.BlockSpec((None, tr, n_pad), lambda l, i: (l, i, 0)),
        out_shape=jax.ShapeDtypeStruct((depth, k, n_pad), BF16),
        compiler_params=_params(("parallel", "parallel"), 2 * tr * (n * 4 + n_pad * 2)),
        name="cast_pad_cols",
    )(w)


def _cast_pad_rows_kernel(w_ref, o_ref, *, n_valid):
    i = pl.program_id(1)

    @pl.when(i < n_valid)
    def _():
        o_ref[...] = w_ref[...].astype(o_ref.dtype)

    @pl.when(i >= n_valid)
    def _():
        o_ref[...] = jnp.zeros_like(o_ref)


def _cast_pad_rows(w, k_pad, tr=256):
    depth, k, n = w.shape
    assert k % tr == 0 and k_pad % tr == 0
    n_valid = k // tr
    return pl.pallas_call(
        functools.partial(_cast_pad_rows_kernel, n_valid=n_valid),
        grid=(depth, k_pad // tr),
        in_specs=[pl.BlockSpec((None, tr, n), lambda l, i: (l, jnp.minimum(i, n_valid - 1), 0))],
        out_specs=pl.BlockSpec((None, tr, n), lambda l, i: (l, i, 0)),
        out_shape=jax.ShapeDtypeStruct((depth, k_pad, n), BF16),
        compiler_params=_params(("parallel", "parallel"), 2 * tr * n * 6),
        name="cast_pad_rows",
    )(w)


def _rope_tables(n_tok):
    quarter = HEAD_DIM // 4
    inv = 1.0 / (ROPE_THETA ** (jnp.arange(quarter, dtype=F32) / quarter))
    pos = jnp.arange(n_tok, dtype=jnp.int32)
    row = (pos // GRID_W).astype(F32)[:, None] * inv
    col = (pos % GRID_W).astype(F32)[:, None] * inv
    cos = jnp.concatenate([jnp.cos(row), jnp.cos(row), jnp.cos(col), jnp.cos(col)], axis=-1)
    sin = jnp.concatenate([-jnp.sin(row), jnp.sin(row), -jnp.sin(col), jnp.sin(col)], axis=-1)
    return cos, sin


def _rotary(cos_ref, sin_ref, scale):
    cos = cos_ref[...] * scale
    sin = sin_ref[...] * scale
    lane = lax.broadcasted_iota(jnp.int32, cos.shape, 1)
    first = (lane & (HEAD_DIM // 4)) == 0

    def rotate(x):
        x = x.astype(F32)
        partner = jnp.where(first, pltpu.roll(x, HEAD_DIM - HEAD_DIM // 4, 1), pltpu.roll(x, HEAD_DIM // 4, 1))
        return x * cos + partner * sin

    return rotate


def _rope_kernel(x_ref, cos_ref, sin_ref, o_ref, *, scale, n_heads):
    rotate = _rotary(cos_ref, sin_ref, scale)
    for h in range(n_heads):
        hs = slice(h * HEAD_DIM, (h + 1) * HEAD_DIM)
        o_ref[:, hs] = rotate(x_ref[:, hs]).astype(o_ref.dtype)


def _rope(p, col0, width, cos, sin, scale, tr=256):
    m = p.shape[0]
    n_tok = cos.shape[0]
    tpb = n_tok // tr
    assert col0 % width == 0
    return pl.pallas_call(
        functools.partial(_rope_kernel, scale=scale, n_heads=width // HEAD_DIM),
        grid=(m // tr,),
        in_specs=[
            pl.BlockSpec((tr, width), lambda i: (i, col0 // width)),
            pl.BlockSpec((tr, HEAD_DIM), lambda i: (i % tpb, 0)),
            pl.BlockSpec((tr, HEAD_DIM), lambda i: (i % tpb, 0)),
        ],
        out_specs=pl.BlockSpec((tr, width), lambda i: (i, 0)),
        out_shape=jax.ShapeDtypeStruct((m, width), BF16),
        compiler_params=_params(("parallel",), 2 * tr * (width * 6 + HEAD_DIM * 8)),
        name="rope",
    )(p, cos, sin)


def _gla_kernel(*refs, nh, scale, fwd):
    ql, kl, vl, dl, qc, kc, vc, dc, up_ref, bias_ref = refs[:10]
    if fwd:
        ol, oc, st_ref = refs[10:]
    else:
        fl, fc, rl, rc, gn_ref, ol, oc, st_ref = refs[10:]
    s = pl.program_id(1)
    is_ctx = s == 0
    n_chunk = CTX_BLOCK // GLA_CHUNK

    @pl.when(s == 0)
    def _():
        st_ref[...] = jnp.zeros_like(st_ref)

    tb = CTX_BLOCK

    def in_scan_order(x):
        if fwd:
            return x
        return jnp.concatenate([x[(n_chunk - 1 - p) * GLA_CHUNK:(n_chunk - p) * GLA_CHUNK] for p in range(n_chunk)],
                               axis=0)

    def pick(rc_, rl_, cols):
        return jnp.where(is_ctx, rc_[:, cols], rl_[:, cols])

    def load(rc_, rl_, cols):
        return in_scan_order(pick(rc_, rl_, cols))

    row = lax.broadcasted_iota(jnp.int32, (tb, tb), 0)
    col = lax.broadcasted_iota(jnp.int32, (tb, tb), 1)
    incl = jnp.logical_and(row // GLA_CHUNK == col // GLA_CHUNK, (row >= col) if fwd else (col >= row))
    row_chunk = lax.broadcasted_iota(jnp.int32, (tb, GLA_DK), 0) // GLA_CHUNK

    def per_chunk(x):
        return jnp.concatenate([jnp.where(row_chunk == p, x, 0.0) for p in range(n_chunk)], axis=1)

    z = _dot(load(dc, dl, slice(None)).astype(BF16), up_ref[...]) + bias_ref[...]
    g = (jnp.minimum(z, 0.0) - jnp.log(1.0 + jnp.exp(-jnp.abs(z)))) * (1.0 / GLA_NORMALIZER)
    tmat = jnp.where(incl, 1.0, 0.0).astype(BF16)
    g_hi = g.astype(BF16)
    g_rest = g - g_hi.astype(F32)
    g_mid = g_rest.astype(BF16)
    g_lo = (g_rest - g_mid.astype(F32)).astype(BF16)
    cum = _dot(tmat, g_hi) + _dot(tmat, g_mid) + _dot(tmat, g_lo)
    tot = [jnp.sum(g[p * GLA_CHUNK:(p + 1) * GLA_CHUNK], axis=0, keepdims=True) for p in range(n_chunk)]
    tot_rows = jnp.concatenate([jnp.broadcast_to(t, (GLA_CHUNK, t.shape[1])) for t in tot], axis=0)

    for h in range(nh):
        ks = slice(h * GLA_DK, (h + 1) * GLA_DK)
        vs = slice(h * GLA_DV, (h + 1) * GLA_DV)
        bh = cum[:, ks]
        q = load(qc, ql, ks)
        k = load(kc, kl, ks)
        v = load(vc, vl, vs).astype(BF16)
        qd = q * (scale * jnp.exp(bh))
        ki = (k * jnp.exp(-bh)).astype(BF16)
        ku = k * jnp.exp(tot_rows[:, ks] - bh)
        a = jnp.where(incl, _dot_nt(qd.astype(BF16), ki), 0.0)
        inc = _dot_tn(v, per_chunk(ku).astype(BF16))
        st = st_ref[h]
        entering = []
        for p in range(n_chunk):
            entering.append(st)
            st = st * jnp.exp(tot[p][:, ks]) + inc[:, p * GLA_DK:(p + 1) * GLA_DK]
        st_ref[h] = st
        o = (_dot(a.astype(BF16), v)
             + _dot_nt(per_chunk(qd).astype(BF16), jnp.concatenate(entering, axis=1).astype(BF16)))
        o = in_scan_order(o)
        if not fwd:
            o = o + pick(fc, fl, vs)
            o = o * lax.rsqrt(jnp.mean(o * o, axis=-1, keepdims=True) + NORM_EPS) * gn_ref[...]
            o = o * _silu(pick(rc, rl, vs))
        o = o.astype(ol.dtype)

        @pl.when(is_ctx)
        def _():
            oc[:, vs] = o

        @pl.when(jnp.logical_not(is_ctx))
        def _():
            ol[:, vs] = o


def _gla(p_lat, p_ctx, dn_lat, dn_ctx, up2, bias2, norm_g, nb, n_tok, nh):
    qk = nh * GLA_DK
    dv = nh * GLA_DV
    tb = CTX_BLOCK
    nbl = n_tok // tb
    m_lat, m_ctx = nb * n_tok, nb * tb

    def scan(fwd, extra_in, extra_specs, out_dtype):
        def lat_blk(b, s):
            t = jnp.maximum(s - 1, 0)
            return b * nbl + (t if fwd else nbl - 1 - t)

        lat = lambda w, c: pl.BlockSpec((tb, w), lambda b, s: (lat_blk(b, s), c))
        ctx = lambda w, c: pl.BlockSpec((tb, w), lambda b, s: (b, c))
        d = 0 if fwd else 1
        vm = (2 * (2 * (2 * tb * qk + tb * dv + tb * LANE) * 4 + 2 * tb * dv * 4 + len(extra_in) * tb * dv * 4)
              + nh * GLA_DV * GLA_DK * 4)
        return pl.pallas_call(
            functools.partial(_gla_kernel, nh=nh, scale=GLA_DK ** -0.5, fwd=fwd),
            grid=(nb, nbl + 1),
            in_specs=[
                lat(qk, 0), lat(qk, 1), lat(dv, 1), lat(LANE, 0),
                ctx(qk, 0), ctx(qk, 1), ctx(dv, 1), ctx(LANE, 0),
                pl.BlockSpec((None, LANE, qk), lambda b, s: (d, 0, 0)),
                pl.BlockSpec((None, 1, qk), lambda b, s: (d, 0, 0)),
            ] + extra_specs(lat, ctx),
            out_specs=[lat(dv, 0), ctx(dv, 0)],
            out_shape=[jax.ShapeDtypeStruct((m_lat, dv), out_dtype), jax.ShapeDtypeStruct((m_ctx, dv), out_dtype)],
            scratch_shapes=[pltpu.VMEM((nh, GLA_DV, GLA_DK), F32)],
            compiler_params=_params(("parallel", "arbitrary"), vm),
            name="gla_scan_fwd" if fwd else "gla_scan_bwd",
        )(p_lat, p_lat, p_lat, dn_lat, p_ctx, p_ctx, p_ctx, dn_ctx, up2, bias2, *extra_in)

    of_lat, of_ctx = scan(True, (), lambda lat, ctx: [], F32)
    return scan(
        False,
        (of_lat, of_ctx, p_lat, p_ctx, norm_g.reshape(1, GLA_DV)),
        lambda lat, ctx: [lat(dv, 0), ctx(dv, 0), lat(dv, 2), ctx(dv, 2),
                          pl.BlockSpec((1, GLA_DV), lambda b, s: (0, 0))],
        BF16,
    )


def _swa_kernel(sink_ref, q_ref, cos_ref, sin_ref, kp, kc, kn, vp, vc, vn, kx, vx, o_ref, *, n_blk, kvs):
    i = pl.program_id(2)
    rotate = _rotary(cos_ref, sin_ref, HEAD_DIM ** -0.5)
    span = SWA_BLOCK + 2 * SWA_WINDOW
    gq = SWA_GROUP * SWA_BLOCK
    kj = lax.broadcasted_iota(jnp.int32, (span, gq), 0)
    lane = lax.broadcasted_iota(jnp.int32, (span, gq), 1)
    qi = lane & (SWA_BLOCK - 1)
    key_pos = (i - 1) * SWA_BLOCK + kj
    ok = jnp.logical_and(jnp.abs(qi + SWA_WINDOW - kj) <= SWA_WINDOW,
                         jnp.logical_and(key_pos >= 0, key_pos < n_blk * SWA_BLOCK))
    head = lax.broadcasted_iota(jnp.int32, (1, gq), 1) // SWA_BLOCK
    for kv in range(kvs):
        hkv = pl.program_id(1) * kvs + kv
        ks = slice(kv * HEAD_DIM, (kv + 1) * HEAD_DIM)
        kwin = jnp.concatenate([kp[:, ks], kc[:, ks], kn[:, ks]], axis=0)
        vwin = jnp.concatenate([vp[:, ks], vc[:, ks], vn[:, ks]], axis=0).astype(BF16)
        kctx = kx[:, ks].astype(BF16)
        vctx = vx[:, ks].astype(BF16)
        q0 = kv * SWA_GROUP * HEAD_DIM
        qs = jnp.concatenate([rotate(q_ref[:, q0 + g * HEAD_DIM:q0 + (g + 1) * HEAD_DIM]).astype(BF16)
                              for g in range(SWA_GROUP)], axis=0)
        s_win = jnp.where(ok, _dot_nt(kwin, qs), -jnp.inf)
        s_ctx = _dot_nt(kctx, qs)
        sink = jnp.zeros((1, gq), F32)
        for g in range(SWA_GROUP):
            sink = jnp.where(head == g, sink_ref[0, hkv * SWA_GROUP + g], sink)
        m = jnp.maximum(jnp.maximum(jnp.max(s_win, axis=0, keepdims=True), jnp.max(s_ctx, axis=0, keepdims=True)),
                        sink)
        p_win = jnp.exp(s_win - m)
        p_ctx = jnp.exp(s_ctx - m)
        den = jnp.sum(p_win, axis=0, keepdims=True) + jnp.sum(p_ctx, axis=0, keepdims=True) + jnp.exp(sink - m)
        ot = (_dot_tn(vctx, p_ctx.astype(BF16)) + _dot_tn(vwin, p_win.astype(BF16))) / den
        for g in range(SWA_GROUP):
            hs = slice(q0 + g * HEAD_DIM, q0 + (g + 1) * HEAD_DIM)
            o_ref[:, hs] = ot[:, g * SWA_BLOCK:(g + 1) * SWA_BLOCK].T.astype(o_ref.dtype)


def _swa(sink, cos, sin, k_rope, p_lat, p_ctx, qcol, kcol, vcol, nb, n_tok, n_kv):
    n_blk = n_tok // SWA_BLOCK
    tb = SWA_BLOCK
    kvs = 2 if n_kv % 2 == 0 else 1
    kw = kvs * HEAD_DIM
    gw = kvs * SWA_GROUP * HEAD_DIM
    assert kcol % kw == 0 and vcol % kw == 0 and qcol % gw == 0
    kc0 = kcol // kw
    vc0 = vcol // kw
    qc0 = qcol // gw
    prev = lambda b, h, i: b * n_blk + jnp.maximum(i - 1, 0)
    cur = lambda b, h, i: b * n_blk + i
    nxt = lambda b, h, i: b * n_blk + jnp.minimum(i + 1, n_blk - 1)
    kspec = lambda f: pl.BlockSpec((tb, kw), lambda b, h, i: (f(b, h, i), h))
    vspec = lambda f: pl.BlockSpec((tb, kw), lambda b, h, i: (f(b, h, i), vc0 + h))
    return pl.pallas_call(
        functools.partial(_swa_kernel, n_blk=n_blk, kvs=kvs),
        grid=(nb, n_kv // kvs, n_blk),
        in_specs=[
            pl.BlockSpec(memory_space=pltpu.SMEM),
            pl.BlockSpec((tb, gw), lambda b, h, i: (cur(b, h, i), qc0 + h)),
            pl.BlockSpec((tb, HEAD_DIM), lambda b, h, i: (i, 0)),
            pl.BlockSpec((tb, HEAD_DIM), lambda b, h, i: (i, 0)),
            kspec(prev), kspec(cur), kspec(nxt),
            vspec(prev), vspec(cur), vspec(nxt),
            pl.BlockSpec((CTX_BLOCK, kw), lambda b, h, i: (b, kc0 + h)),
            pl.BlockSpec((CTX_BLOCK, kw), lambda b, h, i: (b, vc0 + h)),
        ],
        out_specs=pl.BlockSpec((tb, gw), lambda b, h, i: (cur(b, h, i), h)),
        out_shape=jax.ShapeDtypeStruct((nb * n_tok, n_kv * SWA_GROUP * HEAD_DIM), BF16),
        compiler_params=_params(("parallel", "parallel", "parallel"), 16 << 20),
        name="swa",
    )(sink, p_lat, cos, sin, k_rope, k_rope, k_rope, p_lat, p_lat, p_lat, p_ctx, p_ctx)


def _ctx_sink_kernel(sink_ref, q_ref, kx, vx, o_ref, *, scale):
    hkv = pl.program_id(1)
    kctx = kx[...].astype(BF16)
    vctx = vx[...].astype(BF16)
    for g in range(SWA_GROUP):
        hs = slice(g * HEAD_DIM, (g + 1) * HEAD_DIM)
        q = (q_ref[:, hs].astype(F32) * scale).astype(BF16)
        s = _dot_nt(q, kctx)
        sink = sink_ref[0, hkv * SWA_GROUP + g]
        m = jnp.maximum(jnp.max(s, axis=-1, keepdims=True), sink)
        p = jnp.exp(s - m)
        den = jnp.sum(p, axis=-1, keepdims=True) + jnp.exp(sink - m)
        o_ref[:, hs] = (_dot(p.astype(BF16), vctx) / den).astype(o_ref.dtype)


def _ctx_sink(sink, p_ctx, qcol, kcol, vcol, nb, n_kv):
    gw = SWA_GROUP * HEAD_DIM
    return pl.pallas_call(
        functools.partial(_ctx_sink_kernel, scale=HEAD_DIM ** -0.5),
        grid=(nb, n_kv),
        in_specs=[
            pl.BlockSpec(memory_space=pltpu.SMEM),
            pl.BlockSpec((CTX_BLOCK, gw), lambda b, h: (b, qcol // gw + h)),
            pl.BlockSpec((CTX_BLOCK, HEAD_DIM), lambda b, h: (b, kcol // HEAD_DIM + h)),
            pl.BlockSpec((CTX_BLOCK, HEAD_DIM), lambda b, h: (b, vcol // HEAD_DIM + h)),
        ],
        out_specs=pl.BlockSpec((CTX_BLOCK, gw), lambda b, h: (b, h)),
        out_shape=jax.ShapeDtypeStruct((nb * CTX_BLOCK, n_kv * gw), BF16),
        compiler_params=_params(("parallel", "parallel"), 4 << 20),
        name="ctx_sink_attention",
    )(sink, p_ctx, p_ctx, p_ctx)


def _na_bias_table(rpb, rows):
    qcol = np.arange(GRID_W)[:, None]
    kcol = np.arange(GRID_W)[None, :]
    col_start = np.clip(qcol - NA_COLS // 2, 0, GRID_W - NA_COLS)
    col_ok = (kcol >= col_start) & (kcol < col_start + NA_COLS)
    dc = np.clip(kcol - qcol + NA_COLS - 1, 0, 2 * NA_COLS - 2)
    n_dc = 2 * NA_COLS - 1
    onehot = (dc[None] == np.arange(n_dc)[:, None, None]).astype(np.float32)
    cols = jnp.einsum('hrd,dqk->hrqk', rpb.astype(F32), onehot, precision=lax.Precision.HIGHEST)
    cols = jnp.where(col_ok[None, None], cols, -jnp.inf).transpose(0, 1, 3, 2)
    masked = jnp.full(cols.shape[:1] + cols.shape[2:], -jnp.inf, F32)

    span = NA_ROWS + NA_PAIR - 1
    patterns = []
    for r0 in range(0, rows, NA_PAIR):
        rs = [min(max(r0 + i - NA_ROWS // 2, 0), rows - NA_ROWS) for i in range(NA_PAIR)]
        u = min(rs[0], rows - span)
        patterns.append(tuple((r0 + i - rs[i], rs[i] - u) for i in range(NA_PAIR)))
    variants = sorted(set(patterns), key=patterns.index)
    blocks = []
    for pat in variants:
        per_query_row = []
        for delta, off in pat:
            krows = [cols[:, kr - off - delta + NA_ROWS - 1] if 0 <= kr - off < NA_ROWS else masked
                     for kr in range(span)]
            per_query_row.append(jnp.stack(krows, axis=1))
        blocks.append(jnp.concatenate(per_query_row, axis=-1).reshape(rpb.shape[0], span * GRID_W, NA_PAIR * GRID_W))
    ids = np.array([variants.index(p) for p in patterns], dtype=np.int32)
    return jnp.stack(blocks, axis=1), ids


def _na_kernel(ids_ref, q_ref, k_ref, v_ref, kx, vx, bias_ref, o_ref, *, nh, scale):
    for h in range(nh):
        hs = slice(h * HEAD_DIM, (h + 1) * HEAD_DIM)
        q = (q_ref[:, hs].astype(F32) * scale).astype(BF16)
        s_nb = _dot_nt(k_ref[:, hs].astype(BF16), q) + bias_ref[h]
        s_ctx = _dot_nt(kx[:, hs].astype(BF16), q)
        m = jnp.maximum(jnp.max(s_nb, axis=0, keepdims=True), jnp.max(s_ctx, axis=0, keepdims=True))
        p_nb = jnp.exp(s_nb - m)
        p_ctx = jnp.exp(s_ctx - m)
        den = jnp.sum(p_nb, axis=0, keepdims=True) + jnp.sum(p_ctx, axis=0, keepdims=True)
        ot = (_dot_tn(vx[:, hs].astype(BF16), p_ctx.astype(BF16))
              + _dot_tn(v_ref[:, hs].astype(BF16), p_nb.astype(BF16)))
        o_ref[:, hs] = (ot / den).T.astype(o_ref.dtype)


def _na(p_lat, p_ctx, rpb, nb, n_tok, nh):
    rows = n_tok // GRID_W
    w = nh * HEAD_DIM
    span = NA_ROWS + NA_PAIR - 1
    win = span * GRID_W
    tq = NA_PAIR * GRID_W
    n_pair = rows // NA_PAIR
    bias_tab, ids = _na_bias_table(rpb, rows)
    u = lambda g: jnp.clip(g * NA_PAIR - NA_ROWS // 2, 0, rows - span)
    kv = lambda c: pl.BlockSpec((pl.Element(win), pl.Element(w)),
                                lambda b, g, ids: ((b * rows + u(g)) * GRID_W, c * w))
    vm = 2 * (tq * w * 4 + 2 * win * w * 4 + 2 * CTX_BLOCK * w * 4 + nh * tq * win * 4 + tq * w * 2)
    return pl.pallas_call(
        functools.partial(_na_kernel, nh=nh, scale=HEAD_DIM ** -0.5),
        grid_spec=pltpu.PrefetchScalarGridSpec(
            num_scalar_prefetch=1,
            grid=(nb, n_pair),
            in_specs=[
                pl.BlockSpec((tq, w), lambda b, g, ids: (b * n_pair + g, 0)),
                kv(1), kv(2),
                pl.BlockSpec((CTX_BLOCK, w), lambda b, g, ids: (b, 1)),
                pl.BlockSpec((CTX_BLOCK, w), lambda b, g, ids: (b, 2)),
                pl.BlockSpec((nh, None, win, tq), lambda b, g, ids: (0, ids[g], 0, 0)),
            ],
            out_specs=pl.BlockSpec((tq, w), lambda b, g, ids: (b * n_pair + g, 0)),
        ),
        out_shape=jax.ShapeDtypeStruct((nb * n_tok, w), BF16),
        compiler_params=_params(("parallel", "arbitrary"), vm),
        name="neighborhood_attention",
    )(jnp.asarray(ids), p_lat, p_lat, p_lat, p_ctx, p_ctx, bias_tab)


def _diff_kernel(qraw_ref, cos_ref, sin_ref, k_ref, v_ref, kx, vx, lq1, lk1, lq2, lk2, g_ref, o_ref,
                 m_ref, l_ref, acc_ref, sa_ref, sb_ref, pa_ref, pb_ref, pc_ref, aa_ref, ab_ref, ac_ref,
                 vt_ref, vxt_ref, q_ref, *, n_blk, tk, lam_init):

    rotate = _rotary(cos_ref, sin_ref, HEAD_DIM ** -0.5 * math.log2(math.e))
    for t in range(2):
        hs = slice(t * HEAD_DIM, (t + 1) * HEAD_DIM)
        q_ref[:, hs] = rotate(qraw_ref[:, hs]).astype(q_ref.dtype)

    @pl.when(pl.program_id(2) == 0)
    def _():
        for j in range(n_blk):
            vt_ref[j] = v_ref[j * tk:(j + 1) * tk, :].T
        vxt_ref[...] = vx[...].astype(BF16).T

    vxt = vxt_ref
    m_ref[...] = jnp.full_like(m_ref, -jnp.inf)
    l_ref[...] = jnp.zeros_like(l_ref)
    acc_ref[...] = jnp.zeros_like(acc_ref)

    def scores(keys):
        return [_dot_nt(keys[:, t * HEAD_DIM:(t + 1) * HEAD_DIM], q_ref[:, t * HEAD_DIM:(t + 1) * HEAD_DIM])
                for t in range(2)]

    def lat_scores(j, dst):
        s = scores(k_ref[pl.ds(pl.multiple_of(j * tk, tk), tk), :])
        dst[0] = s[0]
        dst[1] = s[1]

    def softmax(s, p_dst, a_dst):
        for t in range(2):
            m_old = m_ref[t]
            m_new = jnp.maximum(m_old, jnp.max(s[t], axis=0, keepdims=True))
            alpha = jnp.exp2(m_old - m_new)
            p = jnp.exp2(s[t] - m_new)
            l_ref[t] = alpha * l_ref[t] + jnp.sum(p, axis=0, keepdims=True)
            m_ref[t] = m_new
            p_dst[t] = p.astype(BF16)
            a_dst[t] = alpha

    def add_values(p_src, a_src, vt):
        for t in range(2):
            acc_ref[t] = a_src[t] * acc_ref[t] + _dot(vt, p_src[t])

    def stage(s_cur, p_cur, a_cur, next_block, s_next, prev):
        if next_block is not None:
            lat_scores(next_block, s_next)
        add_values(*prev)
        softmax((s_cur[0], s_cur[1]), p_cur, a_cur)

    s_ctx = scores(kx[...].astype(BF16))
    lat_scores(0, sa_ref)
    softmax(s_ctx, pc_ref, ac_ref)
    stage(sa_ref, pa_ref, aa_ref, 1, sb_ref, (pc_ref, ac_ref, vxt[...]))

    def body(jj, carry):
        j = 2 * jj + 1
        stage(sb_ref, pb_ref, ab_ref, j + 1, sa_ref, (pa_ref, aa_ref, vt_ref[j - 1]))
        stage(sa_ref, pa_ref, aa_ref, j + 2, sb_ref, (pb_ref, ab_ref, vt_ref[j]))
        return carry

    lax.fori_loop(0, (n_blk - 2) // 2, body, 0)
    stage(sb_ref, pb_ref, ab_ref, None, None, (pa_ref, aa_ref, vt_ref[n_blk - 2]))
    add_values(pb_ref, ab_ref, vt_ref[n_blk - 1])

    lam = (jnp.exp(jnp.sum(lq1[...] * lk1[...], axis=-1, keepdims=True))
           - jnp.exp(jnp.sum(lq2[...] * lk2[...], axis=-1, keepdims=True)) + lam_init)
    o = acc_ref[0] / l_ref[0] - lam * (acc_ref[1] / l_ref[1])
    y = o * lax.rsqrt(jnp.mean(o * o, axis=0, keepdims=True) + NORM_EPS) * (g_ref[...] * (1.0 - lam_init))
    o_ref[...] = y.T.astype(o_ref.dtype)


def _diff(cos, sin, k_rope, p_lat, p_ctx, qcol, kcol, vcol, lq1, lk1, lq2, lk2, norm_g, lam_init, nb, n_tok, nh,
          tq=512, tk=DIFF_KEY_BLOCK):
    nq = n_tok // tq
    n_blk = n_tok // tk
    assert n_blk >= 2 and n_blk % 2 == 0, "the key-block pipeline alternates two buffers"
    w = 2 * HEAD_DIM
    vec = lambda: pl.BlockSpec((1, HEAD_DIM), lambda b, h, i: (0, 0))
    vm = (2 * (tq * w * 2 + 2 * n_tok * w * 2 + 2 * CTX_BLOCK * w * 2 + tq * w * 2)
          + n_tok * w * 2 + 2 * tq * w * 4 + 2 * 2 * tk * tq * 6 + tq * w * 8)
    return pl.pallas_call(
        functools.partial(_diff_kernel, n_blk=n_blk, tk=tk, lam_init=lam_init),
        grid=(nb, nh, nq),
        in_specs=[
            pl.BlockSpec((tq, w), lambda b, h, i: (b * nq + i, qcol // w + h)),
            pl.BlockSpec((tq, HEAD_DIM), lambda b, h, i: (i, 0)),
            pl.BlockSpec((tq, HEAD_DIM), lambda b, h, i: (i, 0)),
            pl.BlockSpec((n_tok, w), lambda b, h, i: (b, h)),
            pl.BlockSpec((n_tok, w), lambda b, h, i: (b, vcol // w + h)),
            pl.BlockSpec((CTX_BLOCK, w), lambda b, h, i: (b, kcol // w + h)),
            pl.BlockSpec((CTX_BLOCK, w), lambda b, h, i: (b, vcol // w + h)),
            vec(), vec(), vec(), vec(),
            pl.BlockSpec((DIFF_DV, 1), lambda b, h, i: (0, 0)),
        ],
        out_specs=pl.BlockSpec((tq, w), lambda b, h, i: (b * nq + i, h)),
        out_shape=jax.ShapeDtypeStruct((nb * n_tok, nh * w), BF16),
        scratch_shapes=[
            pltpu.VMEM((2, 1, tq), F32),
            pltpu.VMEM((2, 1, tq), F32),
            pltpu.VMEM((2, DIFF_DV, tq), F32),
            pltpu.VMEM((2, tk, tq), F32),
            pltpu.VMEM((2, tk, tq), F32),
            pltpu.VMEM((2, tk, tq), BF16),
            pltpu.VMEM((2, tk, tq), BF16),
            pltpu.VMEM((2, CTX_BLOCK, tq), BF16),
            pltpu.VMEM((2, 1, tq), F32),
            pltpu.VMEM((2, 1, tq), F32),
            pltpu.VMEM((2, 1, tq), F32),
            pltpu.VMEM((n_blk, w, tk), BF16),
            pltpu.VMEM((w, CTX_BLOCK), BF16),
            pltpu.VMEM((tq, w), BF16),
        ],
        compiler_params=_params(("parallel", "parallel", "arbitrary"), vm),
        name="diff_attention",
    )(p_lat, cos, sin, k_rope, p_lat, p_ctx, p_ctx, lq1.reshape(1, -1), lk1.reshape(1, -1), lq2.reshape(1, -1),
      lk2.reshape(1, -1), norm_g.reshape(-1, 1))


def _pad_cols(w, n):
    return jnp.pad(w, ((0, 0), (0, n - w.shape[1])))


def kernel(x, c, ctx, c_ctx, ada_w, ada_b, norm_mix_g, norm_ffn_g, w_out, ffn_w_gate, ffn_w_up, ffn_w_down,
           ev_w_in, gla_gate_up_f, gla_gate_bias_f, gla_gate_up_b, gla_gate_bias_b, gla_norm_g, swa_sink,
           od_w_in, na_rpb, diff_lq1, diff_lk1, diff_lq2, diff_lk2, diff_norm_g, final_norm_g):
    nb, n_tok, d = x.shape
    lc = ctx.shape[1]
    depth = ada_w.shape[0]
    f_hidden = ffn_w_gate.shape[2]
    half = d // 2
    assert lc == CTX_BLOCK and nb < 8 and depth == 2, "layout assumes a 256-token context, batch < 8, two layers"
    gla_heads = half // GLA_DV
    gla_qk = gla_heads * GLA_DK
    n_q = half // HEAD_DIM
    n_kv = n_q // SWA_GROUP
    diff_heads = half // (2 * HEAD_DIM)

    xs = x.reshape(nb * n_tok, d)
    cs = ctx.reshape(nb * lc, d)
    cos, sin = _rope_tables(n_tok)

    s8 = jnp.zeros((8, d), F32).at[:nb].set(c).at[nb].set(c_ctx)
    mod5 = _mod_vectors(s8, ada_w, ada_b).reshape(depth, 8, 6, 1, d)
    lat_row = lambda i, tm: (i * tm) // n_tok
    ctx_row = lambda i, tm: nb

    f_pad = -(-f_hidden // FFN_PAD) * FFN_PAD

    wg = _cast_pad_cols(ffn_w_gate, f_pad)
    wu = _cast_pad_cols(ffn_w_up, f_pad)
    wd = _cast_pad_rows(ffn_w_down, f_pad)
    wo = w_out.astype(BF16)

    def ffn(stream, layer, mod_row):
        h2 = _norm_mod(stream, norm_ffn_g[layer], mod5, layer, 3, 4, mod_row)
        a = _matmul_swiglu(h2, wg, wu, layer)
        return _matmul_residual([a], wd, stream, mod5, layer, 5, mod_row, nk=4, tn=2 * COL_TILE)

    layer = 0
    w_in = ev_w_in[0]
    g0 = 2 * gla_qk + 2 * half
    w_main = jnp.concatenate([w_in[:, :g0], w_in[:, g0 + 2 * GLA_RANK:]], axis=1).astype(BF16)
    w_dn = _pad_cols(w_in[:, g0:g0 + 2 * GLA_RANK], LANE).astype(BF16)
    swa_q0 = g0
    swa_k0 = swa_q0 + half
    swa_v0 = swa_k0 + n_kv * HEAD_DIM
    up2 = jnp.zeros((2, LANE, gla_qk), F32)
    up2 = up2.at[0, :GLA_RANK].set(gla_gate_up_f[0]).at[1, GLA_RANK:2 * GLA_RANK].set(gla_gate_up_b[0]).astype(BF16)
    bias2 = jnp.stack([gla_gate_bias_f[0], gla_gate_bias_b[0]]).reshape(2, 1, gla_qk)

    h = _norm_mod(xs, norm_mix_g[layer], mod5, layer, 0, 1, lat_row)
    hc = _norm_mod(cs, norm_mix_g[layer], mod5, layer, 0, 1, ctx_row)
    p_lat = _matmul(h, w_main)
    p_ctx = _matmul(hc, w_main)
    dn_lat = _matmul(h, w_dn)
    dn_ctx = _matmul(hc, w_dn)

    ya, yac = _gla(p_lat, p_ctx, dn_lat, dn_ctx, up2, bias2, gla_norm_g[0], nb, n_tok, gla_heads)

    sink = swa_sink[0].reshape(1, n_q)
    k_rope = _rope(p_lat, swa_k0, n_kv * HEAD_DIM, cos, sin, 1.0)
    yb = _swa(sink, cos, sin, k_rope, p_lat, p_ctx, swa_q0, swa_k0, swa_v0, nb, n_tok, n_kv)
    ybc = _ctx_sink(sink, p_ctx, swa_q0, swa_k0, swa_v0, nb, n_kv)

    xs = _matmul_residual([ya, yb], wo, xs, mod5, layer, 2, lat_row, tn=2 * COL_TILE)
    cs = _matmul_residual([yac, ybc], wo, cs, mod5, layer, 2, ctx_row, tn=2 * COL_TILE)
    xs = ffn(xs, layer, lat_row)
    cs = ffn(cs, layer, ctx_row)

    layer = 1
    w_in = od_w_in[0].astype(BF16)
    h = _norm_mod(xs, norm_mix_g[layer], mod5, layer, 0, 1, lat_row)
    hc = _norm_mod(cs, norm_mix_g[layer], mod5, layer, 0, 1, ctx_row)
    p_lat = _matmul(h, w_in, BF16)
    p_ctx = _matmul(hc, w_in, BF16)

    yn =_na(p_lat, p_ctx, na_rpb[0], nb, n_tok, n_q)

    dq0 = 3 * half
    dk0 = dq0 + half
    dv0 = dk0 + half
    lam_init = 0.8 - 0.6 * math.exp(-0.3 * layer)
    kd = _rope(p_lat, dk0, half, cos, sin, 1.0)
    yd = _diff(cos, sin, kd, p_lat, p_ctx, dq0, dk0, dv0, diff_lq1[0], diff_lk1[0], diff_lq2[0], diff_lk2[0],
               diff_norm_g[0], lam_init, nb, n_tok, diff_heads)

    xs = _matmul_residual([yn, yd], wo, xs, mod5, layer, 2, lat_row, tn=2 * COL_TILE)
    xs = ffn(xs, layer, lat_row)
    return _final_norm(xs, final_norm_g).reshape(nb, n_tok, d)
```

```python
import functools
import math
from typing import NamedTuple

import numpy as np
import jax
import jax.numpy as jnp
from jax import lax
from jax.experimental import pallas as pl
from jax.experimental.pallas import tpu as pltpu

HEAD_DIM = 128
GRID_W = 64
ROPE_THETA = 10000.0
NORM_EPS = 1e-6
GLA_DK = 128
GLA_DV = 256
GLA_RANK = 16
GLA_NORMALIZER = 16.0
GLA_CHUNK = 64
SWA_GROUP = 4
SWA_WINDOW = 128
SWA_BLOCK = 128
NA_ROWS = 8
NA_COLS = 16
NA_QROWS = 4
DIFF_DV = 2 * HEAD_DIM
DIFF_KEY_BLOCK = 2048
CTX_BLOCK = 256

V7X_VMEM_BUDGET = 60 * 1024 * 1024
LANE = 128
ROW_TILE = 1024
COL_TILE = 512
FFN_PAD = 1024

F32 = jnp.float32
BF16 = jnp.bfloat16


def _params(sem, vmem_bytes):
    limit = int(min(V7X_VMEM_BUDGET, vmem_bytes * 5 // 4 + (4 << 20)))
    return pltpu.CompilerParams(dimension_semantics=sem, vmem_limit_bytes=limit)


def _dot(a, b):
    return jnp.dot(a, b, preferred_element_type=F32)


def _dot_nt(a, b):
    return lax.dot_general(a, b, (((1,), (1,)), ((), ())), preferred_element_type=F32)


def _dot_tn(a, b):
    return lax.dot_general(a, b, (((0,), (0,)), ((), ())), preferred_element_type=F32)


def _silu(t):
    return t * jax.nn.sigmoid(t)


def _mod_kernel(s_ref, w_ref, b_ref, o_ref):
    a = _silu(s_ref[...]).astype(BF16)
    o_ref[...] = _dot(a, w_ref[...].astype(BF16)) + b_ref[...]


def _mod_vectors(s8, ada_w, ada_b):
    depth, d, n = ada_w.shape
    tn = COL_TILE
    return pl.pallas_call(
        _mod_kernel,
        grid=(depth, n // tn),
        in_specs=[
            pl.BlockSpec((8, d), lambda l, j: (0, 0)),
            pl.BlockSpec((None, d, tn), lambda l, j: (l, 0, j)),
            pl.BlockSpec((None, 1, tn), lambda l, j: (l, 0, j)),
        ],
        out_specs=pl.BlockSpec((None, 8, tn), lambda l, j: (l, 0, j)),
        out_shape=jax.ShapeDtypeStruct((depth, 8, n), F32),
        compiler_params=_params(("parallel", "parallel"), 2 * d * tn * 4 + d * tn * 2),
        name="adaln_mod",
    )(s8, ada_w, ada_b.reshape(depth, 1, n))


def _norm_mod_kernel(x_ref, g_ref, sh_ref, sc_ref, o_ref):
    x = x_ref[...]
    r = lax.rsqrt(jnp.mean(x * x, axis=-1, keepdims=True) + NORM_EPS)
    y = x * r * g_ref[...]
    o_ref[...] = (y * (1.0 + sc_ref[...]) + sh_ref[...]).astype(o_ref.dtype)


def _norm_mod(x2, g, mod5, layer, k_shift, k_scale, mod_row, tm=512):
    m, d = x2.shape
    return pl.pallas_call(
        _norm_mod_kernel,
        grid=(m // tm,),
        in_specs=[
            pl.BlockSpec((tm, d), lambda i: (i, 0)),
            pl.BlockSpec((1, d), lambda i: (0, 0)),
            pl.BlockSpec((None, None, None, 1, d), lambda i: (layer, mod_row(i, tm), k_shift, 0, 0)),
            pl.BlockSpec((None, None, None, 1, d), lambda i: (layer, mod_row(i, tm), k_scale, 0, 0)),
        ],
        out_specs=pl.BlockSpec((tm, d), lambda i: (i, 0)),
        out_shape=jax.ShapeDtypeStruct((m, d), BF16),
        compiler_params=_params(("parallel",), 2 * tm * d * 6),
        name="norm_mod",
    )(x2, g.reshape(1, d), mod5, mod5)


def _final_norm_kernel(x_ref, g_ref, o_ref):
    x = x_ref[...]
    r = lax.rsqrt(jnp.mean(x * x, axis=-1, keepdims=True) + NORM_EPS)
    o_ref[...] = x * r * g_ref[...]


def _final_norm(x2, g, tm=512):
    m, d = x2.shape
    return pl.pallas_call(
        _final_norm_kernel,
        grid=(m // tm,),
        in_specs=[pl.BlockSpec((tm, d), lambda i: (i, 0)), pl.BlockSpec((1, d), lambda i: (0, 0))],
        out_specs=pl.BlockSpec((tm, d), lambda i: (i, 0)),
        out_shape=jax.ShapeDtypeStruct((m, d), F32),
        compiler_params=_params(("parallel",), 2 * tm * d * 8),
        name="final_norm",
    )(x2, g.reshape(1, d))


def _col_tile(n):
    return next(t for t in range(COL_TILE, 0, -LANE) if n % t == 0)


class _CastJob(NamedTuple):
    src: jax.Array
    r_pad: int
    c_pad: int


def _cast_plan(job, n_steps):
    depth, r, _ = job.src.shape
    for rc in range(16, r + 1, 16):
        if r % rc == 0 and job.r_pad % rc == 0 and depth * (job.r_pad // rc) <= n_steps:
            return rc
    return None


def _cast_specs(job, rc, n_j):
    depth, r, c = job.src.shape
    cpl, valid = job.r_pad // rc, r // rc
    n_total = depth * cpl

    def pos(i, j):
        t = jnp.minimum(i * n_j + j, n_total - 1)
        return t // cpl, t % cpl

    in_spec = pl.BlockSpec((None, rc, c), lambda i, j: (pos(i, j)[0], jnp.minimum(pos(i, j)[1], valid - 1), 0))
    out_spec = pl.BlockSpec((None, rc, job.c_pad), lambda i, j: (pos(i, j)[0], pos(i, j)[1], 0))
    out_shape = jax.ShapeDtypeStruct((depth, job.r_pad, job.c_pad), BF16)
    return in_spec, out_spec, out_shape, (cpl, valid, n_total)


def _run_casts(metas, src_refs, dst_refs, n_j):
    t = pl.program_id(0) * n_j + pl.program_id(1)
    for (cpl, valid, n_total), s, d in zip(metas, src_refs, dst_refs):
        chunk = jnp.minimum(t, n_total - 1) % cpl
        n = s.shape[-1]

        @pl.when(jnp.logical_and(t < n_total, chunk < valid))
        def _():
            d[:, :n] = s[...].astype(d.dtype)
            if d.shape[-1] > n:
                d[:, n:] = jnp.zeros((d.shape[0], d.shape[-1] - n), d.dtype)

        if cpl > valid:
            @pl.when(jnp.logical_and(t < n_total, chunk >= valid))
            def _():
                d[...] = jnp.zeros_like(d)


def _cast_only_kernel(*refs, metas, n_j):
    _run_casts(metas, refs[:len(metas)], refs[len(metas):], n_j)


def _attach_casts(jobs, n_i, n_j):
    plans = [_cast_plan(job, n_i * n_j) for job in jobs]
    riders = [(job, rc) for job, rc in zip(jobs, plans) if rc is not None]
    specs = [_cast_specs(job, rc, n_j) for job, rc in riders]

    def finish(rider_outs):
        rider_outs = list(rider_outs)
        outs = []
        for job, rc in zip(jobs, plans):
            if rc is not None:
                outs.append(rider_outs.pop(0))
                continue
            rc = next(c for c in (256, 128, 64, 32, 16) if job.src.shape[1] % c == 0 and job.r_pad % c == 0)
            n_steps = job.src.shape[0] * (job.r_pad // rc)
            i_s, o_s, o_shape, meta = _cast_specs(job, rc, n_steps)
            outs.append(pl.pallas_call(
                functools.partial(_cast_only_kernel, metas=[meta], n_j=n_steps),
                grid=(1, n_steps), in_specs=[i_s], out_specs=o_s, out_shape=o_shape,
                compiler_params=_params(("arbitrary", "arbitrary"), 2 * rc * (job.src.shape[2] * 4 + job.c_pad * 2)),
                name="cast_weights",
            )(job.src))
        return outs

    vmem = sum(2 * rc * (job.src.shape[2] * 4 + job.c_pad * 2) for job, rc in riders)
    return riders, specs, finish, vmem


def _mm_kernel(*refs, n_cast, metas, n_j):
    a_ref, w_ref = refs[:2]
    o_ref = refs[2 + n_cast]
    o_ref[...] = _dot(a_ref[...], w_ref[...]).astype(o_ref.dtype)
    _run_casts(metas, refs[2:2 + n_cast], refs[3 + n_cast:], n_j)


def _matmul(a, w, out_dtype=F32, cast_jobs=()):
    m, k = a.shape
    n = w.shape[1]
    tm = min(ROW_TILE, m)
    tn = _col_tile(n)
    ob = jnp.dtype(out_dtype).itemsize
    n_i, n_j = m // tm, n // tn
    riders, specs, finish, cast_vmem = _attach_casts(cast_jobs, n_i, n_j)
    outs = pl.pallas_call(
        functools.partial(_mm_kernel, n_cast=len(riders), metas=[s[3] for s in specs], n_j=n_j),
        grid=(n_i, n_j),
        in_specs=[pl.BlockSpec((tm, k), lambda i, j: (i, 0)), pl.BlockSpec((k, tn), lambda i, j: (0, j))]
        + [s[0] for s in specs],
        out_specs=[pl.BlockSpec((tm, tn), lambda i, j: (i, j))] + [s[1] for s in specs],
        out_shape=[jax.ShapeDtypeStruct((m, n), out_dtype)] + [s[2] for s in specs],
        compiler_params=_params(("arbitrary", "arbitrary") if riders else ("parallel", "parallel"),
                                2 * (tm * k * 2 + k * tn * 2 + tm * tn * ob) + cast_vmem),
        name="matmul",
    )(a, w, *[job.src for job, _ in riders])
    return outs[0], finish(outs[1:])


def _mm_swiglu_kernel(*refs, n_cast, metas, n_j):
    a_ref, wg_ref, wu_ref = refs[:3]
    o_ref = refs[3 + n_cast]
    a = a_ref[...]
    g = _dot(a, wg_ref[...])
    u = _dot(a, wu_ref[...])
    o_ref[...] = (_silu(g) * u).astype(o_ref.dtype)
    _run_casts(metas, refs[3:3 + n_cast], refs[4 + n_cast:], n_j)


def _matmul_swiglu(a, wg, wu, layer, cast_jobs=()):
    m, k = a.shape
    n = wg.shape[2]
    tm = min(ROW_TILE, m)
    tn = COL_TILE
    n_i, n_j = m // tm, n // tn
    riders, specs, finish, cast_vmem = _attach_casts(cast_jobs, n_i, n_j)
    outs = pl.pallas_call(
        functools.partial(_mm_swiglu_kernel, n_cast=len(riders), metas=[s[3] for s in specs], n_j=n_j),
        grid=(n_i, n_j),
        in_specs=[
            pl.BlockSpec((tm, k), lambda i, j: (i, 0)),
            pl.BlockSpec((None, k, tn), lambda i, j: (layer, 0, j)),
            pl.BlockSpec((None, k, tn), lambda i, j: (layer, 0, j)),
        ] + [s[0] for s in specs],
        out_specs=[pl.BlockSpec((tm, tn), lambda i, j: (i, j))] + [s[1] for s in specs],
        out_shape=[jax.ShapeDtypeStruct((m, n), BF16)] + [s[2] for s in specs],
        compiler_params=_params(("arbitrary", "arbitrary") if riders else ("parallel", "parallel"),
                                2 * (tm * k * 2 + 2 * k * tn * 2 + tm * tn * 2) + cast_vmem),
        name="matmul_swiglu",
    )(a, wg, wu, *[job.src for job, _ in riders])
    return outs[0], finish(outs[1:])


def _mm_res_kernel(*refs, nk, n_a):
    a_refs, (w_ref, res_ref, gate_ref, o_ref), scratch = refs[:n_a], refs[n_a:n_a + 4], refs[n_a + 4:]
    k0 = 0
    p = None
    for a_ref in a_refs:
        ka = a_ref.shape[1]
        d = _dot(a_ref[...], w_ref[k0:k0 + ka, :])
        p = d if p is None else p + d
        k0 += ka
    if nk == 1:
        o_ref[...] = res_ref[...] + gate_ref[...] * p
        return
    acc_ref, = scratch
    kk = pl.program_id(2)

    @pl.when(kk == 0)
    def _():
        acc_ref[...] = p

    @pl.when(jnp.logical_and(kk > 0, kk < nk - 1))
    def _():
        acc_ref[...] += p

    @pl.when(kk == nk - 1)
    def _():
        o_ref[...] = res_ref[...] + gate_ref[...] * (acc_ref[...] + p)


def _matmul_residual(a_slabs, w, res, mod5, layer, k_gate, mod_row, nk=1, tn=COL_TILE):
    m = a_slabs[0].shape[0]
    k, n = w.shape[1], w.shape[2]
    tm = min(ROW_TILE, m)
    tk = k // nk
    assert nk == 1 or len(a_slabs) == 1
    a_specs = [pl.BlockSpec((tm, tk if nk > 1 else a.shape[1]), lambda i, j, q: (i, q)) for a in a_slabs]
    return pl.pallas_call(
        functools.partial(_mm_res_kernel, nk=nk, n_a=len(a_slabs)),
        grid=(m // tm, n // tn, nk),
        in_specs=a_specs + [
            pl.BlockSpec((None, tk, tn), lambda i, j, q: (layer, q, j)),
            pl.BlockSpec((tm, tn), lambda i, j, q: (i, j)),
            pl.BlockSpec((None, None, None, 1, tn), lambda i, j, q: (layer, mod_row(i, tm), k_gate, 0, j)),
        ],
        out_specs=pl.BlockSpec((tm, tn), lambda i, j, q: (i, j)),
        out_shape=jax.ShapeDtypeStruct((m, n), F32),
        scratch_shapes=[pltpu.VMEM((tm, tn), F32)] if nk > 1 else [],
        compiler_params=_params(
            ("parallel", "parallel", "arbitrary"), 2 * (tm * tk * 2 + tk * tn * 2 + 2 * tm * tn * 4) + tm * tn * 4
        ),
        name="matmul_residual",
    )(*a_slabs, w, res, mod5)


def _rope_tables(n_tok):
    quarter = HEAD_DIM // 4
    inv = 1.0 / (ROPE_THETA ** (jnp.arange(quarter, dtype=F32) / quarter))
    pos = jnp.arange(n_tok, dtype=jnp.int32)
    row = (pos // GRID_W).astype(F32)[:, None] * inv
    col = (pos % GRID_W).astype(F32)[:, None] * inv
    cos = jnp.concatenate([jnp.cos(row), jnp.cos(row), jnp.cos(col), jnp.cos(col)], axis=-1)
    sin = jnp.concatenate([-jnp.sin(row), jnp.sin(row), -jnp.sin(col), jnp.sin(col)], axis=-1)
    return cos, sin


def _rotary(cos_ref, sin_ref, scale):
    cos = cos_ref[...] * scale
    sin = sin_ref[...] * scale
    lane = lax.broadcasted_iota(jnp.int32, cos.shape, 1)
    first = (lane & (HEAD_DIM // 4)) == 0

    def rotate(x):
        x = x.astype(F32)
        partner = jnp.where(first, pltpu.roll(x, HEAD_DIM - HEAD_DIM // 4, 1), pltpu.roll(x, HEAD_DIM // 4, 1))
        return x * cos + partner * sin

    return rotate


def _rope_kernel(x_ref, cos_ref, sin_ref, o_ref, *, scale, n_heads):
    rotate = _rotary(cos_ref, sin_ref, scale)
    for h in range(n_heads):
        hs = slice(h * HEAD_DIM, (h + 1) * HEAD_DIM)
        o_ref[:, hs] = rotate(x_ref[:, hs]).astype(o_ref.dtype)


def _rope(p, col0, width, cos, sin, scale, tr=256):
    m = p.shape[0]
    n_tok = cos.shape[0]
    tpb = n_tok // tr
    assert col0 % width == 0
    return pl.pallas_call(
        functools.partial(_rope_kernel, scale=scale, n_heads=width // HEAD_DIM),
        grid=(m // tr,),
        in_specs=[
            pl.BlockSpec((tr, width), lambda i: (i, col0 // width)),
            pl.BlockSpec((tr, HEAD_DIM), lambda i: (i % tpb, 0)),
            pl.BlockSpec((tr, HEAD_DIM), lambda i: (i % tpb, 0)),
        ],
        out_specs=pl.BlockSpec((tr, width), lambda i: (i, 0)),
        out_shape=jax.ShapeDtypeStruct((m, width), BF16),
        compiler_params=_params(("parallel",), 2 * tr * (width * 6 + HEAD_DIM * 8)),
        name="rope",
    )(p, cos, sin)


def _gla_kernel(*refs, nh, scale, fwd):
    ql, kl, vl, dl, qc, kc, vc, dc, up_ref, bias_ref = refs[:10]
    if fwd:
        ol, oc, st_ref = refs[10:]
    else:
        fl, fc, rl, rc, gn_ref, ol, oc, st_ref = refs[10:]
    s = pl.program_id(1)
    is_ctx = s == 0
    n_chunk = CTX_BLOCK // GLA_CHUNK

    @pl.when(s == 0)
    def _():
        st_ref[...] = jnp.zeros_like(st_ref)

    tb = CTX_BLOCK

    def in_scan_order(x):
        if fwd:
            return x
        return jnp.concatenate([x[(n_chunk - 1 - p) * GLA_CHUNK:(n_chunk - p) * GLA_CHUNK] for p in range(n_chunk)],
                               axis=0)

    def pick(rc_, rl_, cols):
        return jnp.where(is_ctx, rc_[:, cols], rl_[:, cols])

    def load(rc_, rl_, cols):
        return in_scan_order(pick(rc_, rl_, cols))

    row = lax.broadcasted_iota(jnp.int32, (tb, tb), 0)
    col = lax.broadcasted_iota(jnp.int32, (tb, tb), 1)
    incl = jnp.logical_and(row // GLA_CHUNK == col // GLA_CHUNK, (row >= col) if fwd else (col >= row))
    row_chunk = lax.broadcasted_iota(jnp.int32, (tb, GLA_DK), 0) // GLA_CHUNK

    def per_chunk(x):
        return jnp.concatenate([jnp.where(row_chunk == p, x, 0.0) for p in range(n_chunk)], axis=1)

    z = _dot(load(dc, dl, slice(None)).astype(BF16), up_ref[...]) + bias_ref[...]
    g = (jnp.minimum(z, 0.0) - jnp.log(1.0 + jnp.exp(-jnp.abs(z)))) * (1.0 / GLA_NORMALIZER)
    tmat = jnp.where(incl, 1.0, 0.0).astype(BF16)
    g_hi = g.astype(BF16)
    g_rest = g - g_hi.astype(F32)
    g_mid = g_rest.astype(BF16)
    g_lo = (g_rest - g_mid.astype(F32)).astype(BF16)
    cum = _dot(tmat, g_hi) + _dot(tmat, g_mid) + _dot(tmat, g_lo)
    tot = [jnp.sum(g[p * GLA_CHUNK:(p + 1) * GLA_CHUNK], axis=0, keepdims=True) for p in range(n_chunk)]
    tot_rows = jnp.concatenate([jnp.broadcast_to(t, (GLA_CHUNK, t.shape[1])) for t in tot], axis=0)

    for h in range(nh):
        ks = slice(h * GLA_DK, (h + 1) * GLA_DK)
        vs = slice(h * GLA_DV, (h + 1) * GLA_DV)
        bh = cum[:, ks]
        q = load(qc, ql, ks)
        k = load(kc, kl, ks)
        v = load(vc, vl, vs).astype(BF16)
        qd = q * (scale * jnp.exp(bh))
        ki = (k * jnp.exp(-bh)).astype(BF16)
        ku = k * jnp.exp(tot_rows[:, ks] - bh)
        a = jnp.where(incl, _dot_nt(qd.astype(BF16), ki), 0.0)
        inc = _dot_tn(v, per_chunk(ku).astype(BF16))
        st = st_ref[h]
        entering = []
        for p in range(n_chunk):
            entering.append(st)
            st = st * jnp.exp(tot[p][:, ks]) + inc[:, p * GLA_DK:(p + 1) * GLA_DK]
        st_ref[h] = st
        o = (_dot(a.astype(BF16), v)
             + _dot_nt(per_chunk(qd).astype(BF16), jnp.concatenate(entering, axis=1).astype(BF16)))
        o = in_scan_order(o)
        if not fwd:
            o = o + pick(fc, fl, vs)
            o = o * lax.rsqrt(jnp.mean(o * o, axis=-1, keepdims=True) + NORM_EPS) * gn_ref[...]
            o = o * _silu(pick(rc, rl, vs))
        o = o.astype(ol.dtype)

        @pl.when(is_ctx)
        def _():
            oc[:, vs] = o

        @pl.when(jnp.logical_not(is_ctx))
        def _():
            ol[:, vs] = o


def _gla(p_lat, p_ctx, dn_lat, dn_ctx, up2, bias2, norm_g, nb, n_tok, nh):
    qk = nh * GLA_DK
    dv = nh * GLA_DV
    tb = CTX_BLOCK
    nbl = n_tok // tb
    m_lat, m_ctx = nb * n_tok, nb * tb

    def scan(fwd, extra_in, extra_specs, out_dtype):
        def lat_blk(b, s):
            t = jnp.maximum(s - 1, 0)
            return b * nbl + (t if fwd else nbl - 1 - t)

        lat = lambda w, c: pl.BlockSpec((tb, w), lambda b, s: (lat_blk(b, s), c))
        ctx = lambda w, c: pl.BlockSpec((tb, w), lambda b, s: (b, c))
        d = 0 if fwd else 1
        vm = (2 * (2 * (2 * tb * qk + tb * dv + tb * LANE) * 4 + 2 * tb * dv * 4 + len(extra_in) * tb * dv * 4)
              + nh * GLA_DV * GLA_DK * 4)
        return pl.pallas_call(
            functools.partial(_gla_kernel, nh=nh, scale=GLA_DK ** -0.5, fwd=fwd),
            grid=(nb, nbl + 1),
            in_specs=[
                lat(qk, 0), lat(qk, 1), lat(dv, 1), lat(LANE, 0),
                ctx(qk, 0), ctx(qk, 1), ctx(dv, 1), ctx(LANE, 0),
                pl.BlockSpec((None, LANE, qk), lambda b, s: (d, 0, 0)),
                pl.BlockSpec((None, 1, qk), lambda b, s: (d, 0, 0)),
            ] + extra_specs(lat, ctx),
            out_specs=[lat(dv, 0), ctx(dv, 0)],
            out_shape=[jax.ShapeDtypeStruct((m_lat, dv), out_dtype), jax.ShapeDtypeStruct((m_ctx, dv), out_dtype)],
            scratch_shapes=[pltpu.VMEM((nh, GLA_DV, GLA_DK), F32)],
            compiler_params=_params(("parallel", "arbitrary"), vm),
            name="gla_scan_fwd" if fwd else "gla_scan_bwd",
        )(p_lat, p_lat, p_lat, dn_lat, p_ctx, p_ctx, p_ctx, dn_ctx, up2, bias2, *extra_in)

    of_lat, of_ctx = scan(True, (), lambda lat, ctx: [], F32)
    return scan(
        False,
        (of_lat, of_ctx, p_lat, p_ctx, norm_g.reshape(1, GLA_DV)),
        lambda lat, ctx: [lat(dv, 0), ctx(dv, 0), lat(dv, 2), ctx(dv, 2),
                          pl.BlockSpec((1, GLA_DV), lambda b, s: (0, 0))],
        BF16,
    )


def _swa_kernel(sink_ref, q_ref, cos_ref, sin_ref, kp, kc, kn, vp, vc, vn, kx, vx, o_ref, *, n_blk, kvs):
    i = pl.program_id(2)
    rotate = _rotary(cos_ref, sin_ref, HEAD_DIM ** -0.5)
    span = SWA_BLOCK + 2 * SWA_WINDOW
    gq = SWA_GROUP * SWA_BLOCK
    kj = lax.broadcasted_iota(jnp.int32, (span, gq), 0)
    lane = lax.broadcasted_iota(jnp.int32, (span, gq), 1)
    qi = lane & (SWA_BLOCK - 1)
    key_pos = (i - 1) * SWA_BLOCK + kj
    ok = jnp.logical_and(jnp.abs(qi + SWA_WINDOW - kj) <= SWA_WINDOW,
                         jnp.logical_and(key_pos >= 0, key_pos < n_blk * SWA_BLOCK))
    head = lax.broadcasted_iota(jnp.int32, (1, gq), 1) // SWA_BLOCK
    for kv in range(kvs):
        hkv = pl.program_id(1) * kvs + kv
        ks = slice(kv * HEAD_DIM, (kv + 1) * HEAD_DIM)
        kwin = jnp.concatenate([kp[:, ks], kc[:, ks], kn[:, ks]], axis=0)
        vwin = jnp.concatenate([vp[:, ks], vc[:, ks], vn[:, ks]], axis=0).astype(BF16)
        kctx = kx[:, ks].astype(BF16)
        vctx = vx[:, ks].astype(BF16)
        q0 = kv * SWA_GROUP * HEAD_DIM
        qs = jnp.concatenate([rotate(q_ref[:, q0 + g * HEAD_DIM:q0 + (g + 1) * HEAD_DIM]).astype(BF16)
                              for g in range(SWA_GROUP)], axis=0)
        s_win = jnp.where(ok, _dot_nt(kwin, qs), -jnp.inf)
        s_ctx = _dot_nt(kctx, qs)
        sink = jnp.zeros((1, gq), F32)
        for g in range(SWA_GROUP):
            sink = jnp.where(head == g, sink_ref[0, hkv * SWA_GROUP + g], sink)
        m = jnp.maximum(jnp.maximum(jnp.max(s_win, axis=0, keepdims=True), jnp.max(s_ctx, axis=0, keepdims=True)),
                        sink)
        p_win = jnp.exp(s_win - m)
        p_ctx = jnp.exp(s_ctx - m)
        den = jnp.sum(p_win, axis=0, keepdims=True) + jnp.sum(p_ctx, axis=0, keepdims=True) + jnp.exp(sink - m)
        ot = (_dot_tn(vctx, p_ctx.astype(BF16)) + _dot_tn(vwin, p_win.astype(BF16))) / den
        for g in range(SWA_GROUP):
            hs = slice(q0 + g * HEAD_DIM, q0 + (g + 1) * HEAD_DIM)
            o_ref[:, hs] = ot[:, g * SWA_BLOCK:(g + 1) * SWA_BLOCK].T.astype(o_ref.dtype)


def _swa(sink, cos, sin, k_rope, p_lat, p_ctx, qcol, kcol, vcol, nb, n_tok, n_kv):
    n_blk = n_tok // SWA_BLOCK
    tb = SWA_BLOCK
    kvs = 2 if n_kv % 2 == 0 else 1
    kw = kvs * HEAD_DIM
    gw = kvs * SWA_GROUP * HEAD_DIM
    assert kcol % kw == 0 and vcol % kw == 0 and qcol % gw == 0
    kc0 = kcol // kw
    vc0 = vcol // kw
    qc0 = qcol // gw
    prev = lambda b, h, i: b * n_blk + jnp.maximum(i - 1, 0)
    cur = lambda b, h, i: b * n_blk + i
    nxt = lambda b, h, i: b * n_blk + jnp.minimum(i + 1, n_blk - 1)
    kspec = lambda f: pl.BlockSpec((tb, kw), lambda b, h, i: (f(b, h, i), h))
    vspec = lambda f: pl.BlockSpec((tb, kw), lambda b, h, i: (f(b, h, i), vc0 + h))
    return pl.pallas_call(
        functools.partial(_swa_kernel, n_blk=n_blk, kvs=kvs),
        grid=(nb, n_kv // kvs, n_blk),
        in_specs=[
            pl.BlockSpec(memory_space=pltpu.SMEM),
            pl.BlockSpec((tb, gw), lambda b, h, i: (cur(b, h, i), qc0 + h)),
            pl.BlockSpec((tb, HEAD_DIM), lambda b, h, i: (i, 0)),
            pl.BlockSpec((tb, HEAD_DIM), lambda b, h, i: (i, 0)),
            kspec(prev), kspec(cur), kspec(nxt),
            vspec(prev), vspec(cur), vspec(nxt),
            pl.BlockSpec((CTX_BLOCK, kw), lambda b, h, i: (b, kc0 + h)),
            pl.BlockSpec((CTX_BLOCK, kw), lambda b, h, i: (b, vc0 + h)),
        ],
        out_specs=pl.BlockSpec((tb, gw), lambda b, h, i: (cur(b, h, i), h)),
        out_shape=jax.ShapeDtypeStruct((nb * n_tok, n_kv * SWA_GROUP * HEAD_DIM), BF16),
        compiler_params=_params(("parallel", "parallel", "parallel"), 16 << 20),
        name="swa",
    )(sink, p_lat, cos, sin, k_rope, k_rope, k_rope, p_lat, p_lat, p_lat, p_ctx, p_ctx)


def _ctx_sink_kernel(sink_ref, q_ref, kx, vx, o_ref, *, scale):
    hkv = pl.program_id(1)
    kctx = kx[...].astype(BF16)
    vctx = vx[...].astype(BF16)
    for g in range(SWA_GROUP):
        hs = slice(g * HEAD_DIM, (g + 1) * HEAD_DIM)
        q = (q_ref[:, hs].astype(F32) * scale).astype(BF16)
        s = _dot_nt(q, kctx)
        sink = sink_ref[0, hkv * SWA_GROUP + g]
        m = jnp.maximum(jnp.max(s, axis=-1, keepdims=True), sink)
        p = jnp.exp(s - m)
        den = jnp.sum(p, axis=-1, keepdims=True) + jnp.exp(sink - m)
        o_ref[:, hs] = (_dot(p.astype(BF16), vctx) / den).astype(o_ref.dtype)


def _ctx_sink(sink, p_ctx, qcol, kcol, vcol, nb, n_kv):
    gw = SWA_GROUP * HEAD_DIM
    return pl.pallas_call(
        functools.partial(_ctx_sink_kernel, scale=HEAD_DIM ** -0.5),
        grid=(nb, n_kv),
        in_specs=[
            pl.BlockSpec(memory_space=pltpu.SMEM),
            pl.BlockSpec((CTX_BLOCK, gw), lambda b, h: (b, qcol // gw + h)),
            pl.BlockSpec((CTX_BLOCK, HEAD_DIM), lambda b, h: (b, kcol // HEAD_DIM + h)),
            pl.BlockSpec((CTX_BLOCK, HEAD_DIM), lambda b, h: (b, vcol // HEAD_DIM + h)),
        ],
        out_specs=pl.BlockSpec((CTX_BLOCK, gw), lambda b, h: (b, h)),
        out_shape=jax.ShapeDtypeStruct((nb * CTX_BLOCK, n_kv * gw), BF16),
        compiler_params=_params(("parallel", "parallel"), 4 << 20),
        name="ctx_sink_attention",
    )(sink, p_ctx, p_ctx, p_ctx)


def _na_bias_table(rpb, rows):
    qcol = np.arange(GRID_W)[:, None]
    kcol = np.arange(GRID_W)[None, :]
    col_start = np.clip(qcol - NA_COLS // 2, 0, GRID_W - NA_COLS)
    col_ok = (kcol >= col_start) & (kcol < col_start + NA_COLS)
    dc = np.clip(kcol - qcol + NA_COLS - 1, 0, 2 * NA_COLS - 2)
    n_dc = 2 * NA_COLS - 1
    onehot = (dc[None] == np.arange(n_dc)[:, None, None]).astype(np.float32)
    cols = jnp.einsum('hrd,dqk->hrqk', rpb.astype(F32), onehot, precision=lax.Precision.HIGHEST)
    cols = jnp.where(col_ok[None, None], cols, -jnp.inf).transpose(0, 1, 3, 2)
    masked = jnp.full(cols.shape[:1] + cols.shape[2:], -jnp.inf, F32)

    span = NA_ROWS + NA_QROWS - 1
    patterns = []
    for r0 in range(0, rows, NA_QROWS):
        rs = [min(max(r0 + i - NA_ROWS // 2, 0), rows - NA_ROWS) for i in range(NA_QROWS)]
        u = min(rs[0], rows - span)
        patterns.append(tuple((r0 + i - rs[i], rs[i] - u) for i in range(NA_QROWS)))
    variants = sorted(set(patterns), key=patterns.index)
    blocks = []
    for pat in variants:
        per_query_row = []
        for delta, off in pat:
            krows = [cols[:, kr - off - delta + NA_ROWS - 1] if 0 <= kr - off < NA_ROWS else masked
                     for kr in range(span)]
            per_query_row.append(jnp.stack(krows, axis=1))
        blocks.append(jnp.concatenate(per_query_row, axis=-1).reshape(rpb.shape[0], span * GRID_W, NA_QROWS * GRID_W))
    ids = np.array([variants.index(p) for p in patterns], dtype=np.int32)
    return jnp.stack(blocks, axis=1), ids


def _na_kernel(ids_ref, q_ref, k_ref, v_ref, kx, vx, bias_ref, o_ref, *, nh, scale):
    for h in range(nh):
        hs = slice(h * HEAD_DIM, (h + 1) * HEAD_DIM)
        q = (q_ref[:, hs].astype(F32) * scale).astype(BF16)
        s_nb = _dot_nt(k_ref[:, hs].astype(BF16), q) + bias_ref[h]
        s_ctx = _dot_nt(kx[:, hs].astype(BF16), q)
        m = jnp.maximum(jnp.max(s_nb, axis=0, keepdims=True), jnp.max(s_ctx, axis=0, keepdims=True))
        p_nb = jnp.exp(s_nb - m)
        p_ctx = jnp.exp(s_ctx - m)
        den = jnp.sum(p_nb, axis=0, keepdims=True) + jnp.sum(p_ctx, axis=0, keepdims=True)
        ot = (_dot_tn(vx[:, hs].astype(BF16), p_ctx.astype(BF16))
              + _dot_tn(v_ref[:, hs].astype(BF16), p_nb.astype(BF16)))
        o_ref[:, hs] = (ot / den).T.astype(o_ref.dtype)


def _na(p_lat, p_ctx, rpb, nb, n_tok, nh):
    rows = n_tok // GRID_W
    w = nh * HEAD_DIM
    span = NA_ROWS + NA_QROWS - 1
    win = span * GRID_W
    tq = NA_QROWS * GRID_W
    n_step = rows // NA_QROWS
    bias_tab, ids = _na_bias_table(rpb, rows)
    u = lambda g: jnp.clip(g * NA_QROWS - NA_ROWS // 2, 0, rows - span)
    kv = lambda c: pl.BlockSpec((pl.Element(win), pl.Element(w)),
                                lambda b, g, ids: ((b * rows + u(g)) * GRID_W, c * w))
    vm = 2 * (tq * w * 4 + 2 * win * w * 4 + 2 * CTX_BLOCK * w * 4 + nh * tq * win * 4 + tq * w * 2)
    return pl.pallas_call(
        functools.partial(_na_kernel, nh=nh, scale=HEAD_DIM ** -0.5),
        grid_spec=pltpu.PrefetchScalarGridSpec(
            num_scalar_prefetch=1,
            grid=(nb, n_step),
            in_specs=[
                pl.BlockSpec((tq, w), lambda b, g, ids: (b * n_step + g, 0)),
                kv(1), kv(2),
                pl.BlockSpec((CTX_BLOCK, w), lambda b, g, ids: (b, 1)),
                pl.BlockSpec((CTX_BLOCK, w), lambda b, g, ids: (b, 2)),
                pl.BlockSpec((nh, None, win, tq), lambda b, g, ids: (0, ids[g], 0, 0)),
            ],
            out_specs=pl.BlockSpec((tq, w), lambda b, g, ids: (b * n_step + g, 0)),
        ),
        out_shape=jax.ShapeDtypeStruct((nb * n_tok, w), BF16),
        compiler_params=_params(("parallel", "arbitrary"), vm),
        name="neighborhood_attention",
    )(jnp.asarray(ids), p_lat, p_lat, p_lat, p_ctx, p_ctx, bias_tab)


def _diff_kernel(qraw_ref, cos_ref, sin_ref, k_ref, v_ref, kx, vx, lq1, lk1, lq2, lk2, g_ref, o_ref,
                 m_ref, l_ref, acc_ref, sa_ref, sb_ref, pa_ref, pb_ref, pc_ref, aa_ref, ab_ref, ac_ref,
                 vt_ref, vxt_ref, q_ref, *, n_blk, tk, lam_init):

    rotate = _rotary(cos_ref, sin_ref, HEAD_DIM ** -0.5 * math.log2(math.e))
    for t in range(2):
        hs = slice(t * HEAD_DIM, (t + 1) * HEAD_DIM)
        q_ref[:, hs] = rotate(qraw_ref[:, hs]).astype(q_ref.dtype)

    @pl.when(pl.program_id(2) == 0)
    def _():
        for j in range(n_blk):
            vt_ref[j] = v_ref[j * tk:(j + 1) * tk, :].T
        vxt_ref[...] = vx[...].astype(BF16).T

    vxt = vxt_ref
    m_ref[...] = jnp.full_like(m_ref, -jnp.inf)
    l_ref[...] = jnp.zeros_like(l_ref)
    acc_ref[...] = jnp.zeros_like(acc_ref)

    def scores(keys):
        return [_dot_nt(keys[:, t * HEAD_DIM:(t + 1) * HEAD_DIM], q_ref[:, t * HEAD_DIM:(t + 1) * HEAD_DIM])
                for t in range(2)]

    def lat_scores(j, dst):
        s = scores(k_ref[pl.ds(pl.multiple_of(j * tk, tk), tk), :])
        dst[0] = s[0]
        dst[1] = s[1]

    def softmax(s, p_dst, a_dst):
        for t in range(2):
            m_old = m_ref[t]
            m_new = jnp.maximum(m_old, jnp.max(s[t], axis=0, keepdims=True))
            alpha = jnp.exp2(m_old - m_new)
            p = jnp.exp2(s[t] - m_new)
            l_ref[t] = alpha * l_ref[t] + jnp.sum(p, axis=0, keepdims=True)
            m_ref[t] = m_new
            p_dst[t] = p.astype(BF16)
            a_dst[t] = alpha

    def add_values(p_src, a_src, vt):
        for t in range(2):
            acc_ref[t] = a_src[t] * acc_ref[t] + _dot(vt, p_src[t])

    def stage(s_cur, p_cur, a_cur, next_block, s_next, prev):
        if next_block is not None:
            lat_scores(next_block, s_next)
        add_values(*prev)
        softmax((s_cur[0], s_cur[1]), p_cur, a_cur)

    s_ctx = scores(kx[...].astype(BF16))
    lat_scores(0, sa_ref)
    softmax(s_ctx, pc_ref, ac_ref)
    stage(sa_ref, pa_ref, aa_ref, 1, sb_ref, (pc_ref, ac_ref, vxt[...]))

    def body(jj, carry):
        j = 2 * jj + 1
        stage(sb_ref, pb_ref, ab_ref, j + 1, sa_ref, (pa_ref, aa_ref, vt_ref[j - 1]))
        stage(sa_ref, pa_ref, aa_ref, j + 2, sb_ref, (pb_ref, ab_ref, vt_ref[j]))
        return carry

    lax.fori_loop(0, (n_blk - 2) // 2, body, 0)
    stage(sb_ref, pb_ref, ab_ref, None, None, (pa_ref, aa_ref, vt_ref[n_blk - 2]))
    add_values(pb_ref, ab_ref, vt_ref[n_blk - 1])

    lam = (jnp.exp(jnp.sum(lq1[...] * lk1[...], axis=-1, keepdims=True))
           - jnp.exp(jnp.sum(lq2[...] * lk2[...], axis=-1, keepdims=True)) + lam_init)
    o = acc_ref[0] / l_ref[0] - lam * (acc_ref[1] / l_ref[1])
    y = o * lax.rsqrt(jnp.mean(o * o, axis=0, keepdims=True) + NORM_EPS) * (g_ref[...] * (1.0 - lam_init))
    o_ref[...] = y.T.astype(o_ref.dtype)


def _diff(cos, sin, k_rope, p_lat, p_ctx, qcol, kcol, vcol, lq1, lk1, lq2, lk2, norm_g, lam_init, nb, n_tok, nh,
          tq=512, tk=DIFF_KEY_BLOCK):
    nq = n_tok // tq
    n_blk = n_tok // tk
    assert n_blk >= 2 and n_blk % 2 == 0, "the key-block pipeline alternates two buffers"
    w = 2 * HEAD_DIM
    vec = lambda: pl.BlockSpec((1, HEAD_DIM), lambda b, h, i: (0, 0))
    vm = (2 * (tq * w * 2 + 2 * n_tok * w * 2 + 2 * CTX_BLOCK * w * 2 + tq * w * 2)
          + n_tok * w * 2 + 2 * tq * w * 4 + 2 * 2 * tk * tq * 6 + tq * w * 8)
    return pl.pallas_call(
        functools.partial(_diff_kernel, n_blk=n_blk, tk=tk, lam_init=lam_init),
        grid=(nb, nh, nq),
        in_specs=[
            pl.BlockSpec((tq, w), lambda b, h, i: (b * nq + i, qcol // w + h)),
            pl.BlockSpec((tq, HEAD_DIM), lambda b, h, i: (i, 0)),
            pl.BlockSpec((tq, HEAD_DIM), lambda b, h, i: (i, 0)),
            pl.BlockSpec((n_tok, w), lambda b, h, i: (b, h)),
            pl.BlockSpec((n_tok, w), lambda b, h, i: (b, vcol // w + h)),
            pl.BlockSpec((CTX_BLOCK, w), lambda b, h, i: (b, kcol // w + h)),
            pl.BlockSpec((CTX_BLOCK, w), lambda b, h, i: (b, vcol // w + h)),
            vec(), vec(), vec(), vec(),
            pl.BlockSpec((DIFF_DV, 1), lambda b, h, i: (0, 0)),
        ],
        out_specs=pl.BlockSpec((tq, w), lambda b, h, i: (b * nq + i, h)),
        out_shape=jax.ShapeDtypeStruct((nb * n_tok, nh * w), BF16),
        scratch_shapes=[
            pltpu.VMEM((2, 1, tq), F32),
            pltpu.VMEM((2, 1, tq), F32),
            pltpu.VMEM((2, DIFF_DV, tq), F32),
            pltpu.VMEM((2, tk, tq), F32),
            pltpu.VMEM((2, tk, tq), F32),
            pltpu.VMEM((2, tk, tq), BF16),
            pltpu.VMEM((2, tk, tq), BF16),
            pltpu.VMEM((2, CTX_BLOCK, tq), BF16),
            pltpu.VMEM((2, 1, tq), F32),
            pltpu.VMEM((2, 1, tq), F32),
            pltpu.VMEM((2, 1, tq), F32),
            pltpu.VMEM((n_blk, w, tk), BF16),
            pltpu.VMEM((w, CTX_BLOCK), BF16),
            pltpu.VMEM((tq, w), BF16),
        ],
        compiler_params=_params(("parallel", "parallel", "arbitrary"), vm),
        name="diff_attention",
    )(p_lat, cos, sin, k_rope, p_lat, p_ctx, p_ctx, lq1.reshape(1, -1), lk1.reshape(1, -1), lq2.reshape(1, -1),
      lk2.reshape(1, -1), norm_g.reshape(-1, 1))


def _pad_cols(w, n):
    return jnp.pad(w, ((0, 0), (0, n - w.shape[1])))


def kernel(x, c, ctx, c_ctx, ada_w, ada_b, norm_mix_g, norm_ffn_g, w_out, ffn_w_gate, ffn_w_up, ffn_w_down,
           ev_w_in, gla_gate_up_f, gla_gate_bias_f, gla_gate_up_b, gla_gate_bias_b, gla_norm_g, swa_sink,
           od_w_in, na_rpb, diff_lq1, diff_lk1, diff_lq2, diff_lk2, diff_norm_g, final_norm_g):
    nb, n_tok, d = x.shape
    lc = ctx.shape[1]
    depth = ada_w.shape[0]
    f_hidden = ffn_w_gate.shape[2]
    half = d // 2
    assert lc == CTX_BLOCK and nb < 8 and depth == 2, "layout assumes a 256-token context, batch < 8, two layers"
    gla_heads = half // GLA_DV
    gla_qk = gla_heads * GLA_DK
    n_q = half // HEAD_DIM
    n_kv = n_q // SWA_GROUP
    diff_heads = half // (2 * HEAD_DIM)

    xs = x.reshape(nb * n_tok, d)
    cs = ctx.reshape(nb * lc, d)
    cos, sin = _rope_tables(n_tok)

    s8 = jnp.zeros((8, d), F32).at[:nb].set(c).at[nb].set(c_ctx)
    mod5 = _mod_vectors(s8, ada_w, ada_b).reshape(depth, 8, 6, 1, d)
    lat_row = lambda i, tm: (i * tm) // n_tok
    ctx_row = lambda i, tm: nb

    f_pad = -(-f_hidden // FFN_PAD) * FFN_PAD

    def ffn_up(stream, layer, mod_row, wg, wu, cast_jobs=()):
        h2 = _norm_mod(stream, norm_ffn_g[layer], mod5, layer, 3, 4, mod_row)
        return _matmul_swiglu(h2, wg, wu, layer, cast_jobs)

    def ffn_down(a, stream, layer, mod_row, wd):
        return _matmul_residual([a], wd, stream, mod5, layer, 5, mod_row, nk=4, tn=2 * COL_TILE)

    layer = 0
    w_in = ev_w_in[0]
    g0 = 2 * gla_qk + 2 * half
    w_main = jnp.concatenate([w_in[:, :g0], w_in[:, g0 + 2 * GLA_RANK:]], axis=1).astype(BF16)
    w_dn = _pad_cols(w_in[:, g0:g0 + 2 * GLA_RANK], LANE).astype(BF16)
    swa_q0 = g0
    swa_k0 = swa_q0 + half
    swa_v0 = swa_k0 + n_kv * HEAD_DIM
    up2 = jnp.zeros((2, LANE, gla_qk), F32)
    up2 = up2.at[0, :GLA_RANK].set(gla_gate_up_f[0]).at[1, GLA_RANK:2 * GLA_RANK].set(gla_gate_up_b[0]).astype(BF16)
    bias2 = jnp.stack([gla_gate_bias_f[0], gla_gate_bias_b[0]]).reshape(2, 1, gla_qk)

    h = _norm_mod(xs, norm_mix_g[layer], mod5, layer, 0, 1, lat_row)
    hc = _norm_mod(cs, norm_mix_g[layer], mod5, layer, 0, 1, ctx_row)
    p_lat, (wg, wu, wo) = _matmul(h, w_main, cast_jobs=[
        _CastJob(ffn_w_gate, d, f_pad), _CastJob(ffn_w_up, d, f_pad), _CastJob(w_out, d, d)])
    p_ctx, _ = _matmul(hc, w_main)
    dn_lat, _ = _matmul(h, w_dn)
    dn_ctx, _ = _matmul(hc, w_dn)

    ya, yac = _gla(p_lat, p_ctx, dn_lat, dn_ctx, up2, bias2, gla_norm_g[0], nb, n_tok, gla_heads)

    sink = swa_sink[0].reshape(1, n_q)
    k_rope = _rope(p_lat, swa_k0, n_kv * HEAD_DIM, cos, sin, 1.0)
    yb = _swa(sink, cos, sin, k_rope, p_lat, p_ctx, swa_q0, swa_k0, swa_v0, nb, n_tok, n_kv)
    ybc = _ctx_sink(sink, p_ctx, swa_q0, swa_k0, swa_v0, nb, n_kv)

    xs = _matmul_residual([ya, yb], wo, xs, mod5, layer, 2, lat_row, tn=2 * COL_TILE)
    cs = _matmul_residual([yac, ybc], wo, cs, mod5, layer, 2, ctx_row, tn=2 * COL_TILE)
    a, (wd, w_in1) = ffn_up(xs, layer, lat_row, wg, wu, cast_jobs=[
        _CastJob(ffn_w_down, f_pad, d), _CastJob(od_w_in, d, od_w_in.shape[2])])
    xs = ffn_down(a, xs, layer, lat_row, wd)
    cs = ffn_down(ffn_up(cs, layer, ctx_row, wg, wu)[0], cs, layer, ctx_row, wd)

    layer = 1
    w_in = w_in1[0]
    h = _norm_mod(xs, norm_mix_g[layer], mod5, layer, 0, 1, lat_row)
    hc = _norm_mod(cs, norm_mix_g[layer], mod5, layer, 0, 1, ctx_row)
    p_lat, _ = _matmul(h, w_in, BF16)
    p_ctx, _ = _matmul(hc, w_in, BF16)

    yn =_na(p_lat, p_ctx, na_rpb[0], nb, n_tok, n_q)

    dq0 = 3 * half
    dk0 = dq0 + half
    dv0 = dk0 + half
    lam_init = 0.8 - 0.6 * math.exp(-0.3 * layer)
    kd = _rope(p_lat, dk0, half, cos, sin, 1.0)
    yd = _diff(cos, sin, kd, p_lat, p_ctx, dq0, dk0, dv0, diff_lq1[0], diff_lk1[0], diff_lq2[0], diff_lk2[0],
               diff_norm_g[0], lam_init, nb, n_tok, diff_heads)

    xs = _matmul_residual([yn, yd], wo, xs, mod5, layer, 2, lat_row, tn=2 * COL_TILE)
    xs = ffn_down(ffn_up(xs, layer, lat_row, wg, wu)[0], xs, layer, lat_row, wd)
    return _final_norm(xs, final_norm_g).reshape(nb, n_tok, d)
```

```python
import functools
import math
from typing import NamedTuple

import numpy as np
import jax
import jax.numpy as jnp
from jax import lax
from jax.experimental import pallas as pl
from jax.experimental.pallas import tpu as pltpu

HEAD_DIM = 128
GRID_W = 64
ROPE_THETA = 10000.0
NORM_EPS = 1e-6
GLA_DK = 128
GLA_DV = 256
GLA_RANK = 16
GLA_NORMALIZER = 16.0
GLA_CHUNK = 64
SWA_GROUP = 4
SWA_WINDOW = 128
SWA_BLOCK = 128
NA_ROWS = 8
NA_COLS = 16
NA_QROWS = 4
DIFF_DV = 2 * HEAD_DIM
DIFF_KEY_BLOCK = 2048
CTX_BLOCK = 256

V7X_VMEM_BUDGET = 60 * 1024 * 1024
LANE = 128
ROW_TILE = 1024
COL_TILE = 512
FFN_PAD = 1024

F32 = jnp.float32
BF16 = jnp.bfloat16


def _params(sem, vmem_bytes):
    limit = int(min(V7X_VMEM_BUDGET, vmem_bytes * 5 // 4 + (4 << 20)))
    return pltpu.CompilerParams(dimension_semantics=sem, vmem_limit_bytes=limit)


def _dot(a, b):
    return jnp.dot(a, b, preferred_element_type=F32)


def _dot_nt(a, b):
    return lax.dot_general(a, b, (((1,), (1,)), ((), ())), preferred_element_type=F32)


def _dot_tn(a, b):
    return lax.dot_general(a, b, (((0,), (0,)), ((), ())), preferred_element_type=F32)


def _silu(t):
    return t * jax.nn.sigmoid(t)


def _mod_kernel(s_ref, w_ref, b_ref, o_ref):
    a = _silu(s_ref[...]).astype(BF16)
    o_ref[...] = _dot(a, w_ref[...].astype(BF16)) + b_ref[...]


def _mod_vectors(s8, ada_w, ada_b):
    depth, d, n = ada_w.shape
    tn = COL_TILE
    return pl.pallas_call(
        _mod_kernel,
        grid=(depth, n // tn),
        in_specs=[
            pl.BlockSpec((8, d), lambda l, j: (0, 0)),
            pl.BlockSpec((None, d, tn), lambda l, j: (l, 0, j)),
            pl.BlockSpec((None, 1, tn), lambda l, j: (l, 0, j)),
        ],
        out_specs=pl.BlockSpec((None, 8, tn), lambda l, j: (l, 0, j)),
        out_shape=jax.ShapeDtypeStruct((depth, 8, n), F32),
        compiler_params=_params(("parallel", "parallel"), 2 * d * tn * 4 + d * tn * 2),
        name="adaln_mod",
    )(s8, ada_w, ada_b.reshape(depth, 1, n))


def _norm_mod_kernel(x_ref, g_ref, sh_ref, sc_ref, o_ref):
    x = x_ref[...]
    r = lax.rsqrt(jnp.mean(x * x, axis=-1, keepdims=True) + NORM_EPS)
    y = x * r * g_ref[...]
    o_ref[...] = (y * (1.0 + sc_ref[...]) + sh_ref[...]).astype(o_ref.dtype)


def _norm_mod(x2, g, mod5, layer, k_shift, k_scale, mod_row, tm=512):
    m, d = x2.shape
    return pl.pallas_call(
        _norm_mod_kernel,
        grid=(m // tm,),
        in_specs=[
            pl.BlockSpec((tm, d), lambda i: (i, 0)),
            pl.BlockSpec((1, d), lambda i: (0, 0)),
            pl.BlockSpec((None, None, None, 1, d), lambda i: (layer, mod_row(i, tm), k_shift, 0, 0)),
            pl.BlockSpec((None, None, None, 1, d), lambda i: (layer, mod_row(i, tm), k_scale, 0, 0)),
        ],
        out_specs=pl.BlockSpec((tm, d), lambda i: (i, 0)),
        out_shape=jax.ShapeDtypeStruct((m, d), BF16),
        compiler_params=_params(("parallel",), 2 * tm * d * 6),
        name="norm_mod",
    )(x2, g.reshape(1, d), mod5, mod5)


def _final_norm_kernel(x_ref, g_ref, o_ref):
    x = x_ref[...]
    r = lax.rsqrt(jnp.mean(x * x, axis=-1, keepdims=True) + NORM_EPS)
    o_ref[...] = x * r * g_ref[...]


def _final_norm(x2, g, tm=512):
    m, d = x2.shape
    return pl.pallas_call(
        _final_norm_kernel,
        grid=(m // tm,),
        in_specs=[pl.BlockSpec((tm, d), lambda i: (i, 0)), pl.BlockSpec((1, d), lambda i: (0, 0))],
        out_specs=pl.BlockSpec((tm, d), lambda i: (i, 0)),
        out_shape=jax.ShapeDtypeStruct((m, d), F32),
        compiler_params=_params(("parallel",), 2 * tm * d * 8),
        name="final_norm",
    )(x2, g.reshape(1, d))


def _col_tile(n):
    return next(t for t in range(COL_TILE, 0, -LANE) if n % t == 0)


class _CastJob(NamedTuple):
    src: jax.Array
    r_pad: int
    c_pad: int


def _cast_plan(job, n_steps):
    depth, r, _ = job.src.shape
    for rc in range(16, r + 1, 16):
        if r % rc == 0 and job.r_pad % rc == 0 and depth * (job.r_pad // rc) <= n_steps:
            return rc
    return None


def _cast_specs(job, rc, n_j):
    depth, r, c = job.src.shape
    cpl, valid = job.r_pad // rc, r // rc
    n_total = depth * cpl

    def pos(i, j):
        t = jnp.minimum(i * n_j + j, n_total - 1)
        return t // cpl, t % cpl

    in_spec = pl.BlockSpec((None, rc, c), lambda i, j: (pos(i, j)[0], jnp.minimum(pos(i, j)[1], valid - 1), 0))
    out_spec = pl.BlockSpec((None, rc, job.c_pad), lambda i, j: (pos(i, j)[0], pos(i, j)[1], 0))
    out_shape = jax.ShapeDtypeStruct((depth, job.r_pad, job.c_pad), BF16)
    return in_spec, out_spec, out_shape, (cpl, valid, n_total)


def _run_casts(metas, src_refs, dst_refs, n_j):
    t = pl.program_id(0) * n_j + pl.program_id(1)
    for (cpl, valid, n_total), s, d in zip(metas, src_refs, dst_refs):
        chunk = jnp.minimum(t, n_total - 1) % cpl
        n = s.shape[-1]

        @pl.when(jnp.logical_and(t < n_total, chunk < valid))
        def _():
            d[:, :n] = s[...].astype(d.dtype)
            if d.shape[-1] > n:
                d[:, n:] = jnp.zeros((d.shape[0], d.shape[-1] - n), d.dtype)

        if cpl > valid:
            @pl.when(jnp.logical_and(t < n_total, chunk >= valid))
            def _():
                d[...] = jnp.zeros_like(d)


def _cast_only_kernel(*refs, metas, n_j):
    _run_casts(metas, refs[:len(metas)], refs[len(metas):], n_j)


def _attach_casts(jobs, n_i, n_j):
    plans = [_cast_plan(job, n_i * n_j) for job in jobs]
    riders = [(job, rc) for job, rc in zip(jobs, plans) if rc is not None]
    specs = [_cast_specs(job, rc, n_j) for job, rc in riders]

    def finish(rider_outs):
        rider_outs = list(rider_outs)
        outs = []
        for job, rc in zip(jobs, plans):
            if rc is not None:
                outs.append(rider_outs.pop(0))
                continue
            rc = next(c for c in (256, 128, 64, 32, 16) if job.src.shape[1] % c == 0 and job.r_pad % c == 0)
            n_steps = job.src.shape[0] * (job.r_pad // rc)
            i_s, o_s, o_shape, meta = _cast_specs(job, rc, n_steps)
            outs.append(pl.pallas_call(
                functools.partial(_cast_only_kernel, metas=[meta], n_j=n_steps),
                grid=(1, n_steps), in_specs=[i_s], out_specs=o_s, out_shape=o_shape,
                compiler_params=_params(("arbitrary", "arbitrary"), 2 * rc * (job.src.shape[2] * 4 + job.c_pad * 2)),
                name="cast_weights",
            )(job.src))
        return outs

    vmem = sum(2 * rc * (job.src.shape[2] * 4 + job.c_pad * 2) for job, rc in riders)
    return riders, specs, finish, vmem


def _mm_kernel(*refs, n_narrow, n_cast, metas, n_j):
    n_in = 2 + n_narrow + n_cast
    a_ref, w_ref = refs[:2]
    o_ref = refs[n_in]
    a = a_ref[...]
    o_ref[...] = _dot(a, w_ref[...]).astype(o_ref.dtype)
    if n_narrow:
        @pl.when(pl.program_id(1) == 0)
        def _():
            refs[n_in + 1][...] = _dot(a, refs[2][...])

    _run_casts(metas, refs[2 + n_narrow:n_in], refs[n_in + 1 + n_narrow:], n_j)


def _matmul(a, w, out_dtype=F32, w_narrow=None, cast_jobs=()):
    m, k = a.shape
    n = w.shape[1]
    tm = min(ROW_TILE, m)
    tn = _col_tile(n)
    ob = jnp.dtype(out_dtype).itemsize
    n_i, n_j = m // tm, n // tn
    riders, specs, finish, cast_vmem = _attach_casts(cast_jobs, n_i, n_j)
    narrow = [] if w_narrow is None else [w_narrow]
    outs = pl.pallas_call(
        functools.partial(_mm_kernel, n_narrow=len(narrow), n_cast=len(riders), metas=[s[3] for s in specs], n_j=n_j),
        grid=(n_i, n_j),
        in_specs=[pl.BlockSpec((tm, k), lambda i, j: (i, 0)), pl.BlockSpec((k, tn), lambda i, j: (0, j))]
        + [pl.BlockSpec((k, LANE), lambda i, j: (0, 0)) for _ in narrow] + [s[0] for s in specs],
        out_specs=[pl.BlockSpec((tm, tn), lambda i, j: (i, j))]
        + [pl.BlockSpec((tm, LANE), lambda i, j: (i, 0)) for _ in narrow] + [s[1] for s in specs],
        out_shape=[jax.ShapeDtypeStruct((m, n), out_dtype)]
        + [jax.ShapeDtypeStruct((m, LANE), F32) for _ in narrow] + [s[2] for s in specs],
        compiler_params=_params(("arbitrary", "arbitrary") if riders or narrow else ("parallel", "parallel"),
                                2 * (tm * k * 2 + k * tn * 2 + tm * tn * ob + k * LANE * 2 + tm * LANE * 4)
                                + cast_vmem),
        name="matmul",
    )(a, w, *narrow, *[job.src for job, _ in riders])
    n_main = 1 + len(narrow)
    return outs[0], (outs[1] if narrow else None), finish(outs[n_main:])


def _mm_swiglu_kernel(*refs, n_cast, metas, n_j):
    a_ref, wg_ref, wu_ref = refs[:3]
    o_ref = refs[3 + n_cast]
    a = a_ref[...]
    g = _dot(a, wg_ref[...])
    u = _dot(a, wu_ref[...])
    o_ref[...] = (_silu(g) * u).astype(o_ref.dtype)
    _run_casts(metas, refs[3:3 + n_cast], refs[4 + n_cast:], n_j)


def _matmul_swiglu(a, wg, wu, layer, cast_jobs=()):
    m, k = a.shape
    n = wg.shape[2]
    tm = min(ROW_TILE, m)
    tn = COL_TILE
    n_i, n_j = m // tm, n // tn
    riders, specs, finish, cast_vmem = _attach_casts(cast_jobs, n_i, n_j)
    outs = pl.pallas_call(
        functools.partial(_mm_swiglu_kernel, n_cast=len(riders), metas=[s[3] for s in specs], n_j=n_j),
        grid=(n_i, n_j),
        in_specs=[
            pl.BlockSpec((tm, k), lambda i, j: (i, 0)),
            pl.BlockSpec((None, k, tn), lambda i, j: (layer, 0, j)),
            pl.BlockSpec((None, k, tn), lambda i, j: (layer, 0, j)),
        ] + [s[0] for s in specs],
        out_specs=[pl.BlockSpec((tm, tn), lambda i, j: (i, j))] + [s[1] for s in specs],
        out_shape=[jax.ShapeDtypeStruct((m, n), BF16)] + [s[2] for s in specs],
        compiler_params=_params(("arbitrary", "arbitrary") if riders else ("parallel", "parallel"),
                                2 * (tm * k * 2 + 2 * k * tn * 2 + tm * tn * 2) + cast_vmem),
        name="matmul_swiglu",
    )(a, wg, wu, *[job.src for job, _ in riders])
    return outs[0], finish(outs[1:])


def _mm_res_kernel(*refs, nk, n_a):
    a_refs, (w_ref, res_ref, gate_ref, o_ref), scratch = refs[:n_a], refs[n_a:n_a + 4], refs[n_a + 4:]
    k0 = 0
    p = None
    for a_ref in a_refs:
        ka = a_ref.shape[1]
        d = _dot(a_ref[...], w_ref[k0:k0 + ka, :])
        p = d if p is None else p + d
        k0 += ka
    if nk == 1:
        o_ref[...] = res_ref[...] + gate_ref[...] * p
        return
    acc_ref, = scratch
    kk = pl.program_id(2)

    @pl.when(kk == 0)
    def _():
        acc_ref[...] = p

    @pl.when(jnp.logical_and(kk > 0, kk < nk - 1))
    def _():
        acc_ref[...] += p

    @pl.when(kk == nk - 1)
    def _():
        o_ref[...] = res_ref[...] + gate_ref[...] * (acc_ref[...] + p)


def _matmul_residual(a_slabs, w, res, mod5, layer, k_gate, mod_row, nk=1, tn=COL_TILE):
    m = a_slabs[0].shape[0]
    k, n = w.shape[1], w.shape[2]
    tm = min(ROW_TILE, m)
    tk = k // nk
    assert nk == 1 or len(a_slabs) == 1
    a_specs = [pl.BlockSpec((tm, tk if nk > 1 else a.shape[1]), lambda i, j, q: (i, q)) for a in a_slabs]
    return pl.pallas_call(
        functools.partial(_mm_res_kernel, nk=nk, n_a=len(a_slabs)),
        grid=(m // tm, n // tn, nk),
        in_specs=a_specs + [
            pl.BlockSpec((None, tk, tn), lambda i, j, q: (layer, q, j)),
            pl.BlockSpec((tm, tn), lambda i, j, q: (i, j)),
            pl.BlockSpec((None, None, None, 1, tn), lambda i, j, q: (layer, mod_row(i, tm), k_gate, 0, j)),
        ],
        out_specs=pl.BlockSpec((tm, tn), lambda i, j, q: (i, j)),
        out_shape=jax.ShapeDtypeStruct((m, n), F32),
        scratch_shapes=[pltpu.VMEM((tm, tn), F32)] if nk > 1 else [],
        compiler_params=_params(
            ("parallel", "parallel", "arbitrary"), 2 * (tm * tk * 2 + tk * tn * 2 + 2 * tm * tn * 4) + tm * tn * 4
        ),
        name="matmul_residual",
    )(*a_slabs, w, res, mod5)


def _rope_tables(n_tok):
    quarter = HEAD_DIM // 4
    inv = 1.0 / (ROPE_THETA ** (jnp.arange(quarter, dtype=F32) / quarter))
    pos = jnp.arange(n_tok, dtype=jnp.int32)
    row = (pos // GRID_W).astype(F32)[:, None] * inv
    col = (pos % GRID_W).astype(F32)[:, None] * inv
    cos = jnp.concatenate([jnp.cos(row), jnp.cos(row), jnp.cos(col), jnp.cos(col)], axis=-1)
    sin = jnp.concatenate([-jnp.sin(row), jnp.sin(row), -jnp.sin(col), jnp.sin(col)], axis=-1)
    return cos, sin


def _rotary(cos_ref, sin_ref, scale):
    cos = cos_ref[...] * scale
    sin = sin_ref[...] * scale
    lane = lax.broadcasted_iota(jnp.int32, cos.shape, 1)
    first = (lane & (HEAD_DIM // 4)) == 0

    def rotate(x):
        x = x.astype(F32)
        partner = jnp.where(first, pltpu.roll(x, HEAD_DIM - HEAD_DIM // 4, 1), pltpu.roll(x, HEAD_DIM // 4, 1))
        return x * cos + partner * sin

    return rotate


def _rope_kernel(x_ref, cos_ref, sin_ref, o_ref, *, scale, n_heads):
    rotate = _rotary(cos_ref, sin_ref, scale)
    for h in range(n_heads):
        hs = slice(h * HEAD_DIM, (h + 1) * HEAD_DIM)
        o_ref[:, hs] = rotate(x_ref[:, hs]).astype(o_ref.dtype)


def _rope(p, col0, width, cos, sin, scale, tr=256):
    m = p.shape[0]
    n_tok = cos.shape[0]
    tpb = n_tok // tr
    assert col0 % width == 0
    return pl.pallas_call(
        functools.partial(_rope_kernel, scale=scale, n_heads=width // HEAD_DIM),
        grid=(m // tr,),
        in_specs=[
            pl.BlockSpec((tr, width), lambda i: (i, col0 // width)),
            pl.BlockSpec((tr, HEAD_DIM), lambda i: (i % tpb, 0)),
            pl.BlockSpec((tr, HEAD_DIM), lambda i: (i % tpb, 0)),
        ],
        out_specs=pl.BlockSpec((tr, width), lambda i: (i, 0)),
        out_shape=jax.ShapeDtypeStruct((m, width), BF16),
        compiler_params=_params(("parallel",), 2 * tr * (width * 6 + HEAD_DIM * 8)),
        name="rope",
    )(p, cos, sin)


def _gla_kernel(*refs, nh, scale, fwd):
    ql, kl, vl, dl, qc, kc, vc, dc, up_ref, bias_ref = refs[:10]
    if fwd:
        ol, oc, st_ref = refs[10:]
    else:
        fl, fc, rl, rc, gn_ref, ol, oc, st_ref = refs[10:]
    s = pl.program_id(1)
    is_ctx = s == 0
    n_chunk = CTX_BLOCK // GLA_CHUNK

    @pl.when(s == 0)
    def _():
        st_ref[...] = jnp.zeros_like(st_ref)

    tb = CTX_BLOCK

    def in_scan_order(x):
        if fwd:
            return x
        return jnp.concatenate([x[(n_chunk - 1 - p) * GLA_CHUNK:(n_chunk - p) * GLA_CHUNK] for p in range(n_chunk)],
                               axis=0)

    def pick(rc_, rl_, cols):
        return jnp.where(is_ctx, rc_[:, cols], rl_[:, cols])

    def load(rc_, rl_, cols):
        return in_scan_order(pick(rc_, rl_, cols))

    row = lax.broadcasted_iota(jnp.int32, (tb, tb), 0)
    col = lax.broadcasted_iota(jnp.int32, (tb, tb), 1)
    incl = jnp.logical_and(row // GLA_CHUNK == col // GLA_CHUNK, (row >= col) if fwd else (col >= row))
    row_chunk = lax.broadcasted_iota(jnp.int32, (tb, GLA_DK), 0) // GLA_CHUNK

    def per_chunk(x):
        return jnp.concatenate([jnp.where(row_chunk == p, x, 0.0) for p in range(n_chunk)], axis=1)

    z = _dot(load(dc, dl, slice(None)).astype(BF16), up_ref[...]) + bias_ref[...]
    g = (jnp.minimum(z, 0.0) - jnp.log(1.0 + jnp.exp(-jnp.abs(z)))) * (1.0 / GLA_NORMALIZER)
    tmat = jnp.where(incl, 1.0, 0.0).astype(BF16)
    g_hi = g.astype(BF16)
    g_rest = g - g_hi.astype(F32)
    g_mid = g_rest.astype(BF16)
    g_lo = (g_rest - g_mid.astype(F32)).astype(BF16)
    cum = _dot(tmat, g_hi) + _dot(tmat, g_mid) + _dot(tmat, g_lo)
    tot = [jnp.sum(g[p * GLA_CHUNK:(p + 1) * GLA_CHUNK], axis=0, keepdims=True) for p in range(n_chunk)]
    tot_rows = jnp.concatenate([jnp.broadcast_to(t, (GLA_CHUNK, t.shape[1])) for t in tot], axis=0)

    staged = []
    for h in range(nh):
        ks = slice(h * GLA_DK, (h + 1) * GLA_DK)
        vs = slice(h * GLA_DV, (h + 1) * GLA_DV)
        bh = cum[:, ks]
        q = load(qc, ql, ks)
        k = load(kc, kl, ks)
        v = load(vc, vl, vs).astype(BF16)
        qd = q * (scale * jnp.exp(bh))
        ki = (k * jnp.exp(-bh)).astype(BF16)
        ku = k * jnp.exp(tot_rows[:, ks] - bh)
        a = jnp.where(incl, _dot_nt(qd.astype(BF16), ki), 0.0)
        inc = _dot_tn(v, per_chunk(ku).astype(BF16))
        staged.append((a.astype(BF16), inc, per_chunk(qd).astype(BF16), v))

    for h in range(nh):
        ks = slice(h * GLA_DK, (h + 1) * GLA_DK)
        vs = slice(h * GLA_DV, (h + 1) * GLA_DV)
        a, inc, qd_chunks, v = staged[h]
        st = st_ref[h]
        entering = []
        for p in range(n_chunk):
            entering.append(st)
            st = st * jnp.exp(tot[p][:, ks]) + inc[:, p * GLA_DK:(p + 1) * GLA_DK]
        st_ref[h] = st
        o = _dot(a, v) + _dot_nt(qd_chunks, jnp.concatenate(entering, axis=1).astype(BF16))
        o = in_scan_order(o)
        if not fwd:
            o = o + pick(fc, fl, vs)
            o = o * lax.rsqrt(jnp.mean(o * o, axis=-1, keepdims=True) + NORM_EPS) * gn_ref[...]
            o = o * _silu(pick(rc, rl, vs))
        o = o.astype(ol.dtype)

        @pl.when(is_ctx)
        def _():
            oc[:, vs] = o

        @pl.when(jnp.logical_not(is_ctx))
        def _():
            ol[:, vs] = o


def _gla(p_lat, p_ctx, dn_lat, dn_ctx, up2, bias2, norm_g, nb, n_tok, nh):
    qk = nh * GLA_DK
    dv = nh * GLA_DV
    tb = CTX_BLOCK
    nbl = n_tok // tb
    m_lat, m_ctx = nb * n_tok, nb * tb

    def scan(fwd, extra_in, extra_specs, out_dtype):
        def lat_blk(b, s):
            t = jnp.maximum(s - 1, 0)
            return b * nbl + (t if fwd else nbl - 1 - t)

        lat = lambda w, c: pl.BlockSpec((tb, w), lambda b, s: (lat_blk(b, s), c))
        ctx = lambda w, c: pl.BlockSpec((tb, w), lambda b, s: (b, c))
        d = 0 if fwd else 1
        vm = (2 * (2 * (2 * tb * qk + tb * dv + tb * LANE) * 4 + 2 * tb * dv * 4 + len(extra_in) * tb * dv * 4)
              + nh * GLA_DV * GLA_DK * 4)
        return pl.pallas_call(
            functools.partial(_gla_kernel, nh=nh, scale=GLA_DK ** -0.5, fwd=fwd),
            grid=(nb, nbl + 1),
            in_specs=[
                lat(qk, 0), lat(qk, 1), lat(dv, 1), lat(LANE, 0),
                ctx(qk, 0), ctx(qk, 1), ctx(dv, 1), ctx(LANE, 0),
                pl.BlockSpec((None, LANE, qk), lambda b, s: (d, 0, 0)),
                pl.BlockSpec((None, 1, qk), lambda b, s: (d, 0, 0)),
            ] + extra_specs(lat, ctx),
            out_specs=[lat(dv, 0), ctx(dv, 0)],
            out_shape=[jax.ShapeDtypeStruct((m_lat, dv), out_dtype), jax.ShapeDtypeStruct((m_ctx, dv), out_dtype)],
            scratch_shapes=[pltpu.VMEM((nh, GLA_DV, GLA_DK), F32)],
            compiler_params=_params(("parallel", "arbitrary"), vm),
            name="gla_scan_fwd" if fwd else "gla_scan_bwd",
        )(p_lat, p_lat, p_lat, dn_lat, p_ctx, p_ctx, p_ctx, dn_ctx, up2, bias2, *extra_in)

    of_lat, of_ctx = scan(True, (), lambda lat, ctx: [], F32)
    return scan(
        False,
        (of_lat, of_ctx, p_lat, p_ctx, norm_g.reshape(1, GLA_DV)),
        lambda lat, ctx: [lat(dv, 0), ctx(dv, 0), lat(dv, 2), ctx(dv, 2),
                          pl.BlockSpec((1, GLA_DV), lambda b, s: (0, 0))],
        BF16,
    )


def _swa_kernel(sink_ref, q_ref, cos_ref, sin_ref, kp, kc, kn, vp, vc, vn, kx, vx, o_ref, *, n_blk, kvs):
    i = pl.program_id(2)
    rotate = _rotary(cos_ref, sin_ref, HEAD_DIM ** -0.5)
    span = SWA_BLOCK + 2 * SWA_WINDOW
    gq = SWA_GROUP * SWA_BLOCK
    kj = lax.broadcasted_iota(jnp.int32, (span, gq), 0)
    lane = lax.broadcasted_iota(jnp.int32, (span, gq), 1)
    qi = lane & (SWA_BLOCK - 1)
    key_pos = (i - 1) * SWA_BLOCK + kj
    ok = jnp.logical_and(jnp.abs(qi + SWA_WINDOW - kj) <= SWA_WINDOW,
                         jnp.logical_and(key_pos >= 0, key_pos < n_blk * SWA_BLOCK))
    head = lax.broadcasted_iota(jnp.int32, (1, gq), 1) // SWA_BLOCK
    for kv in range(kvs):
        hkv = pl.program_id(1) * kvs + kv
        ks = slice(kv * HEAD_DIM, (kv + 1) * HEAD_DIM)
        kwin = jnp.concatenate([kp[:, ks], kc[:, ks], kn[:, ks]], axis=0)
        vwin = jnp.concatenate([vp[:, ks], vc[:, ks], vn[:, ks]], axis=0).astype(BF16)
        kctx = kx[:, ks].astype(BF16)
        vctx = vx[:, ks].astype(BF16)
        q0 = kv * SWA_GROUP * HEAD_DIM
        qs = jnp.concatenate([rotate(q_ref[:, q0 + g * HEAD_DIM:q0 + (g + 1) * HEAD_DIM]).astype(BF16)
                              for g in range(SWA_GROUP)], axis=0)
        s_win = jnp.where(ok, _dot_nt(kwin, qs), -jnp.inf)
        s_ctx = _dot_nt(kctx, qs)
        sink = jnp.zeros((1, gq), F32)
        for g in range(SWA_GROUP):
            sink = jnp.where(head == g, sink_ref[0, hkv * SWA_GROUP + g], sink)
        m = jnp.maximum(jnp.maximum(jnp.max(s_win, axis=0, keepdims=True), jnp.max(s_ctx, axis=0, keepdims=True)),
                        sink)
        p_win = jnp.exp(s_win - m)
        p_ctx = jnp.exp(s_ctx - m)
        den = jnp.sum(p_win, axis=0, keepdims=True) + jnp.sum(p_ctx, axis=0, keepdims=True) + jnp.exp(sink - m)
        ot = (_dot_tn(vctx, p_ctx.astype(BF16)) + _dot_tn(vwin, p_win.astype(BF16))) / den
        for g in range(SWA_GROUP):
            hs = slice(q0 + g * HEAD_DIM, q0 + (g + 1) * HEAD_DIM)
            o_ref[:, hs] = ot[:, g * SWA_BLOCK:(g + 1) * SWA_BLOCK].T.astype(o_ref.dtype)


def _swa(sink, cos, sin, k_rope, p_lat, p_ctx, qcol, kcol, vcol, nb, n_tok, n_kv):
    n_blk = n_tok // SWA_BLOCK
    tb = SWA_BLOCK
    kvs = 2 if n_kv % 2 == 0 else 1
    kw = kvs * HEAD_DIM
    gw = kvs * SWA_GROUP * HEAD_DIM
    assert kcol % kw == 0 and vcol % kw == 0 and qcol % gw == 0
    kc0 = kcol // kw
    vc0 = vcol // kw
    qc0 = qcol // gw
    prev = lambda b, h, i: b * n_blk + jnp.maximum(i - 1, 0)
    cur = lambda b, h, i: b * n_blk + i
    nxt = lambda b, h, i: b * n_blk + jnp.minimum(i + 1, n_blk - 1)
    kspec = lambda f: pl.BlockSpec((tb, kw), lambda b, h, i: (f(b, h, i), h))
    vspec = lambda f: pl.BlockSpec((tb, kw), lambda b, h, i: (f(b, h, i), vc0 + h))
    return pl.pallas_call(
        functools.partial(_swa_kernel, n_blk=n_blk, kvs=kvs),
        grid=(nb, n_kv // kvs, n_blk),
        in_specs=[
            pl.BlockSpec(memory_space=pltpu.SMEM),
            pl.BlockSpec((tb, gw), lambda b, h, i: (cur(b, h, i), qc0 + h)),
            pl.BlockSpec((tb, HEAD_DIM), lambda b, h, i: (i, 0)),
            pl.BlockSpec((tb, HEAD_DIM), lambda b, h, i: (i, 0)),
            kspec(prev), kspec(cur), kspec(nxt),
            vspec(prev), vspec(cur), vspec(nxt),
            pl.BlockSpec((CTX_BLOCK, kw), lambda b, h, i: (b, kc0 + h)),
            pl.BlockSpec((CTX_BLOCK, kw), lambda b, h, i: (b, vc0 + h)),
        ],
        out_specs=pl.BlockSpec((tb, gw), lambda b, h, i: (cur(b, h, i), h)),
        out_shape=jax.ShapeDtypeStruct((nb * n_tok, n_kv * SWA_GROUP * HEAD_DIM), BF16),
        compiler_params=_params(("parallel", "parallel", "parallel"), 16 << 20),
        name="swa",
    )(sink, p_lat, cos, sin, k_rope, k_rope, k_rope, p_lat, p_lat, p_lat, p_ctx, p_ctx)


def _ctx_sink_kernel(sink_ref, q_ref, kx, vx, o_ref, *, scale):
    hkv = pl.program_id(1)
    kctx = kx[...].astype(BF16)
    vctx = vx[...].astype(BF16)
    for g in range(SWA_GROUP):
        hs = slice(g * HEAD_DIM, (g + 1) * HEAD_DIM)
        q = (q_ref[:, hs].astype(F32) * scale).astype(BF16)
        s = _dot_nt(q, kctx)
        sink = sink_ref[0, hkv * SWA_GROUP + g]
        m = jnp.maximum(jnp.max(s, axis=-1, keepdims=True), sink)
        p = jnp.exp(s - m)
        den = jnp.sum(p, axis=-1, keepdims=True) + jnp.exp(sink - m)
        o_ref[:, hs] = (_dot(p.astype(BF16), vctx) / den).astype(o_ref.dtype)


def _ctx_sink(sink, p_ctx, qcol, kcol, vcol, nb, n_kv):
    gw = SWA_GROUP * HEAD_DIM
    return pl.pallas_call(
        functools.partial(_ctx_sink_kernel, scale=HEAD_DIM ** -0.5),
        grid=(nb, n_kv),
        in_specs=[
            pl.BlockSpec(memory_space=pltpu.SMEM),
            pl.BlockSpec((CTX_BLOCK, gw), lambda b, h: (b, qcol // gw + h)),
            pl.BlockSpec((CTX_BLOCK, HEAD_DIM), lambda b, h: (b, kcol // HEAD_DIM + h)),
            pl.BlockSpec((CTX_BLOCK, HEAD_DIM), lambda b, h: (b, vcol // HEAD_DIM + h)),
        ],
        out_specs=pl.BlockSpec((CTX_BLOCK, gw), lambda b, h: (b, h)),
        out_shape=jax.ShapeDtypeStruct((nb * CTX_BLOCK, n_kv * gw), BF16),
        compiler_params=_params(("parallel", "parallel"), 4 << 20),
        name="ctx_sink_attention",
    )(sink, p_ctx, p_ctx, p_ctx)


def _na_bias_table(rpb, rows):
    qcol = np.arange(GRID_W)[:, None]
    kcol = np.arange(GRID_W)[None, :]
    col_start = np.clip(qcol - NA_COLS // 2, 0, GRID_W - NA_COLS)
    col_ok = (kcol >= col_start) & (kcol < col_start + NA_COLS)
    dc = np.clip(kcol - qcol + NA_COLS - 1, 0, 2 * NA_COLS - 2)
    n_dc = 2 * NA_COLS - 1
    onehot = (dc[None] == np.arange(n_dc)[:, None, None]).astype(np.float32)
    cols = jnp.einsum('hrd,dqk->hrqk', rpb.astype(F32), onehot, precision=lax.Precision.HIGHEST)
    cols = jnp.where(col_ok[None, None], cols, -jnp.inf).transpose(0, 1, 3, 2)
    masked = jnp.full(cols.shape[:1] + cols.shape[2:], -jnp.inf, F32)

    span = NA_ROWS + NA_QROWS - 1
    patterns = []
    for r0 in range(0, rows, NA_QROWS):
        rs = [min(max(r0 + i - NA_ROWS // 2, 0), rows - NA_ROWS) for i in range(NA_QROWS)]
        u = min(rs[0], rows - span)
        patterns.append(tuple((r0 + i - rs[i], rs[i] - u) for i in range(NA_QROWS)))
    variants = sorted(set(patterns), key=patterns.index)
    blocks = []
    for pat in variants:
        per_query_row = []
        for delta, off in pat:
            krows = [cols[:, kr - off - delta + NA_ROWS - 1] if 0 <= kr - off < NA_ROWS else masked
                     for kr in range(span)]
            per_query_row.append(jnp.stack(krows, axis=1))
        blocks.append(jnp.concatenate(per_query_row, axis=-1).reshape(rpb.shape[0], span * GRID_W, NA_QROWS * GRID_W))
    ids = np.array([variants.index(p) for p in patterns], dtype=np.int32)
    return jnp.stack(blocks, axis=1), ids


def _na_kernel(ids_ref, q_ref, k_ref, v_ref, kx, vx, bias_ref, o_ref, *, nh, scale):
    for h in range(nh):
        hs = slice(h * HEAD_DIM, (h + 1) * HEAD_DIM)
        q = (q_ref[:, hs].astype(F32) * scale).astype(BF16)
        s_nb = _dot_nt(k_ref[:, hs].astype(BF16), q) + bias_ref[h]
        s_ctx = _dot_nt(kx[:, hs].astype(BF16), q)
        m = jnp.maximum(jnp.max(s_nb, axis=0, keepdims=True), jnp.max(s_ctx, axis=0, keepdims=True))
        p_nb = jnp.exp(s_nb - m)
        p_ctx = jnp.exp(s_ctx - m)
        den = jnp.sum(p_nb, axis=0, keepdims=True) + jnp.sum(p_ctx, axis=0, keepdims=True)
        ot = (_dot_tn(vx[:, hs].astype(BF16), p_ctx.astype(BF16))
              + _dot_tn(v_ref[:, hs].astype(BF16), p_nb.astype(BF16)))
        o_ref[:, hs] = (ot / den).T.astype(o_ref.dtype)


def _na(p_lat, p_ctx, rpb, nb, n_tok, nh):
    rows = n_tok // GRID_W
    w = nh * HEAD_DIM
    span = NA_ROWS + NA_QROWS - 1
    win = span * GRID_W
    tq = NA_QROWS * GRID_W
    n_step = rows // NA_QROWS
    bias_tab, ids = _na_bias_table(rpb, rows)
    u = lambda g: jnp.clip(g * NA_QROWS - NA_ROWS // 2, 0, rows - span)
    kv = lambda c: pl.BlockSpec((pl.Element(win), pl.Element(w)),
                                lambda b, g, ids: ((b * rows + u(g)) * GRID_W, c * w))
    vm = 2 * (tq * w * 4 + 2 * win * w * 4 + 2 * CTX_BLOCK * w * 4 + nh * tq * win * 4 + tq * w * 2)
    return pl.pallas_call(
        functools.partial(_na_kernel, nh=nh, scale=HEAD_DIM ** -0.5),
        grid_spec=pltpu.PrefetchScalarGridSpec(
            num_scalar_prefetch=1,
            grid=(nb, n_step),
            in_specs=[
                pl.BlockSpec((tq, w), lambda b, g, ids: (b * n_step + g, 0)),
                kv(1), kv(2),
                pl.BlockSpec((CTX_BLOCK, w), lambda b, g, ids: (b, 1)),
                pl.BlockSpec((CTX_BLOCK, w), lambda b, g, ids: (b, 2)),
                pl.BlockSpec((nh, None, win, tq), lambda b, g, ids: (0, ids[g], 0, 0)),
            ],
            out_specs=pl.BlockSpec((tq, w), lambda b, g, ids: (b * n_step + g, 0)),
        ),
        out_shape=jax.ShapeDtypeStruct((nb * n_tok, w), BF16),
        compiler_params=_params(("parallel", "arbitrary"), vm),
        name="neighborhood_attention",
    )(jnp.asarray(ids), p_lat, p_lat, p_lat, p_ctx, p_ctx, bias_tab)


def _diff_kernel(qraw_ref, cos_ref, sin_ref, k_ref, v_ref, kx, vx, lq1, lk1, lq2, lk2, g_ref, o_ref,
                 m_ref, l_ref, acc_ref, sa_ref, sb_ref, pa_ref, pb_ref, pc_ref, aa_ref, ab_ref, ac_ref,
                 vt_ref, vxt_ref, q_ref, *, n_blk, tk, lam_init):

    rotate = _rotary(cos_ref, sin_ref, HEAD_DIM ** -0.5 * math.log2(math.e))
    for t in range(2):
        hs = slice(t * HEAD_DIM, (t + 1) * HEAD_DIM)
        q_ref[:, hs] = rotate(qraw_ref[:, hs]).astype(q_ref.dtype)

    @pl.when(pl.program_id(2) == 0)
    def _():
        for j in range(n_blk):
            vt_ref[j] = v_ref[j * tk:(j + 1) * tk, :].T
        vxt_ref[...] = vx[...].astype(BF16).T

    vxt = vxt_ref
    m_ref[...] = jnp.full_like(m_ref, -jnp.inf)
    l_ref[...] = jnp.zeros_like(l_ref)
    acc_ref[...] = jnp.zeros_like(acc_ref)

    def scores(keys):
        return [_dot_nt(keys[:, t * HEAD_DIM:(t + 1) * HEAD_DIM], q_ref[:, t * HEAD_DIM:(t + 1) * HEAD_DIM])
                for t in range(2)]

    def lat_scores(j, dst):
        s = scores(k_ref[pl.ds(pl.multiple_of(j * tk, tk), tk), :])
        dst[0] = s[0]
        dst[1] = s[1]

    def softmax(s, p_dst, a_dst):
        for t in range(2):
            m_old = m_ref[t]
            m_new = jnp.maximum(m_old, jnp.max(s[t], axis=0, keepdims=True))
            alpha = jnp.exp2(m_old - m_new)
            p = jnp.exp2(s[t] - m_new)
            l_ref[t] = alpha * l_ref[t] + jnp.sum(p, axis=0, keepdims=True)
            m_ref[t] = m_new
            p_dst[t] = p.astype(BF16)
            a_dst[t] = alpha

    def add_values(p_src, a_src, vt):
        for t in range(2):
            acc_ref[t] = a_src[t] * acc_ref[t] + _dot(vt, p_src[t])

    def stage(s_cur, p_cur, a_cur, next_block, s_next, prev):
        if next_block is not None:
            lat_scores(next_block, s_next)
        add_values(*prev)
        softmax((s_cur[0], s_cur[1]), p_cur, a_cur)

    s_ctx = scores(kx[...].astype(BF16))
    lat_scores(0, sa_ref)
    softmax(s_ctx, pc_ref, ac_ref)
    stage(sa_ref, pa_ref, aa_ref, 1, sb_ref, (pc_ref, ac_ref, vxt[...]))

    def body(jj, carry):
        j = 2 * jj + 1
        stage(sb_ref, pb_ref, ab_ref, j + 1, sa_ref, (pa_ref, aa_ref, vt_ref[j - 1]))
        stage(sa_ref, pa_ref, aa_ref, j + 2, sb_ref, (pb_ref, ab_ref, vt_ref[j]))
        return carry

    lax.fori_loop(0, (n_blk - 2) // 2, body, 0)
    stage(sb_ref, pb_ref, ab_ref, None, None, (pa_ref, aa_ref, vt_ref[n_blk - 2]))
    add_values(pb_ref, ab_ref, vt_ref[n_blk - 1])

    lam = (jnp.exp(jnp.sum(lq1[...] * lk1[...], axis=-1, keepdims=True))
           - jnp.exp(jnp.sum(lq2[...] * lk2[...], axis=-1, keepdims=True)) + lam_init)
    o = acc_ref[0] / l_ref[0] - lam * (acc_ref[1] / l_ref[1])
    y = o * lax.rsqrt(jnp.mean(o * o, axis=0, keepdims=True) + NORM_EPS) * (g_ref[...] * (1.0 - lam_init))
    o_ref[...] = y.T.astype(o_ref.dtype)


def _diff(cos, sin, k_rope, p_lat, p_ctx, qcol, kcol, vcol, lq1, lk1, lq2, lk2, norm_g, lam_init, nb, n_tok, nh,
          tq=512, tk=DIFF_KEY_BLOCK):
    nq = n_tok // tq
    n_blk = n_tok // tk
    assert n_blk >= 2 and n_blk % 2 == 0, "the key-block pipeline alternates two buffers"
    w = 2 * HEAD_DIM
    vec = lambda: pl.BlockSpec((1, HEAD_DIM), lambda b, h, i: (0, 0))
    vm = (2 * (tq * w * 2 + 2 * n_tok * w * 2 + 2 * CTX_BLOCK * w * 2 + tq * w * 2)
          + n_tok * w * 2 + 2 * tq * w * 4 + 2 * 2 * tk * tq * 6 + tq * w * 8)
    return pl.pallas_call(
        functools.partial(_diff_kernel, n_blk=n_blk, tk=tk, lam_init=lam_init),
        grid=(nb, nh, nq),
        in_specs=[
            pl.BlockSpec((tq, w), lambda b, h, i: (b * nq + i, qcol // w + h)),
            pl.BlockSpec((tq, HEAD_DIM), lambda b, h, i: (i, 0)),
            pl.BlockSpec((tq, HEAD_DIM), lambda b, h, i: (i, 0)),
            pl.BlockSpec((n_tok, w), lambda b, h, i: (b, h)),
            pl.BlockSpec((n_tok, w), lambda b, h, i: (b, vcol // w + h)),
            pl.BlockSpec((CTX_BLOCK, w), lambda b, h, i: (b, kcol // w + h)),
            pl.BlockSpec((CTX_BLOCK, w), lambda b, h, i: (b, vcol // w + h)),
            vec(), vec(), vec(), vec(),
            pl.BlockSpec((DIFF_DV, 1), lambda b, h, i: (0, 0)),
        ],
        out_specs=pl.BlockSpec((tq, w), lambda b, h, i: (b * nq + i, h)),
        out_shape=jax.ShapeDtypeStruct((nb * n_tok, nh * w), BF16),
        scratch_shapes=[
            pltpu.VMEM((2, 1, tq), F32),
            pltpu.VMEM((2, 1, tq), F32),
            pltpu.VMEM((2, DIFF_DV, tq), F32),
            pltpu.VMEM((2, tk, tq), F32),
            pltpu.VMEM((2, tk, tq), F32),
            pltpu.VMEM((2, tk, tq), BF16),
            pltpu.VMEM((2, tk, tq), BF16),
            pltpu.VMEM((2, CTX_BLOCK, tq), BF16),
            pltpu.VMEM((2, 1, tq), F32),
            pltpu.VMEM((2, 1, tq), F32),
            pltpu.VMEM((2, 1, tq), F32),
            pltpu.VMEM((n_blk, w, tk), BF16),
            pltpu.VMEM((w, CTX_BLOCK), BF16),
            pltpu.VMEM((tq, w), BF16),
        ],
        compiler_params=_params(("parallel", "parallel", "arbitrary"), vm),
        name="diff_attention",
    )(p_lat, cos, sin, k_rope, p_lat, p_ctx, p_ctx, lq1.reshape(1, -1), lk1.reshape(1, -1), lq2.reshape(1, -1),
      lk2.reshape(1, -1), norm_g.reshape(-1, 1))


def _pad_cols(w, n):
    return jnp.pad(w, ((0, 0), (0, n - w.shape[1])))


def kernel(x, c, ctx, c_ctx, ada_w, ada_b, norm_mix_g, norm_ffn_g, w_out, ffn_w_gate, ffn_w_up, ffn_w_down,
           ev_w_in, gla_gate_up_f, gla_gate_bias_f, gla_gate_up_b, gla_gate_bias_b, gla_norm_g, swa_sink,
           od_w_in, na_rpb, diff_lq1, diff_lk1, diff_lq2, diff_lk2, diff_norm_g, final_norm_g):
    nb, n_tok, d = x.shape
    lc = ctx.shape[1]
    depth = ada_w.shape[0]
    f_hidden = ffn_w_gate.shape[2]
    half = d // 2
    assert lc == CTX_BLOCK and nb < 8 and depth == 2, "layout assumes a 256-token context, batch < 8, two layers"
    gla_heads = half // GLA_DV
    gla_qk = gla_heads * GLA_DK
    n_q = half // HEAD_DIM
    n_kv = n_q // SWA_GROUP
    diff_heads = half // (2 * HEAD_DIM)

    xs = x.reshape(nb * n_tok, d)
    cs = ctx.reshape(nb * lc, d)
    cos, sin = _rope_tables(n_tok)

    s8 = jnp.zeros((8, d), F32).at[:nb].set(c).at[nb].set(c_ctx)
    mod5 = _mod_vectors(s8, ada_w, ada_b).reshape(depth, 8, 6, 1, d)
    lat_row = lambda i, tm: (i * tm) // n_tok
    ctx_row = lambda i, tm: nb

    f_pad = -(-f_hidden // FFN_PAD) * FFN_PAD

    def ffn_up(stream, layer, mod_row, wg, wu, cast_jobs=()):
        h2 = _norm_mod(stream, norm_ffn_g[layer], mod5, layer, 3, 4, mod_row)
        return _matmul_swiglu(h2, wg, wu, layer, cast_jobs)

    def ffn_down(a, stream, layer, mod_row, wd):
        return _matmul_residual([a], wd, stream, mod5, layer, 5, mod_row, nk=4, tn=2 * COL_TILE)

    layer = 0
    w_in = ev_w_in[0]
    g0 = 2 * gla_qk + 2 * half
    w_main = jnp.concatenate([w_in[:, :g0], w_in[:, g0 + 2 * GLA_RANK:]], axis=1).astype(BF16)
    w_dn = _pad_cols(w_in[:, g0:g0 + 2 * GLA_RANK], LANE).astype(BF16)
    swa_q0 = g0
    swa_k0 = swa_q0 + half
    swa_v0 = swa_k0 + n_kv * HEAD_DIM
    up2 = jnp.zeros((2, LANE, gla_qk), F32)
    up2 = up2.at[0, :GLA_RANK].set(gla_gate_up_f[0]).at[1, GLA_RANK:2 * GLA_RANK].set(gla_gate_up_b[0]).astype(BF16)
    bias2 = jnp.stack([gla_gate_bias_f[0], gla_gate_bias_b[0]]).reshape(2, 1, gla_qk)

    h = _norm_mod(xs, norm_mix_g[layer], mod5, layer, 0, 1, lat_row)
    hc = _norm_mod(cs, norm_mix_g[layer], mod5, layer, 0, 1, ctx_row)
    p_lat, dn_lat, (wg, wu, wo) = _matmul(h, w_main, w_narrow=w_dn, cast_jobs=[
        _CastJob(ffn_w_gate, d, f_pad), _CastJob(ffn_w_up, d, f_pad), _CastJob(w_out, d, d)])
    p_ctx, dn_ctx, _ = _matmul(hc, w_main, w_narrow=w_dn)

    ya, yac = _gla(p_lat, p_ctx, dn_lat, dn_ctx, up2, bias2, gla_norm_g[0], nb, n_tok, gla_heads)

    sink = swa_sink[0].reshape(1, n_q)
    k_rope = _rope(p_lat, swa_k0, n_kv * HEAD_DIM, cos, sin, 1.0)
    yb = _swa(sink, cos, sin, k_rope, p_lat, p_ctx, swa_q0, swa_k0, swa_v0, nb, n_tok, n_kv)
    ybc = _ctx_sink(sink, p_ctx, swa_q0, swa_k0, swa_v0, nb, n_kv)

    xs = _matmul_residual([ya, yb], wo, xs, mod5, layer, 2, lat_row, tn=2 * COL_TILE)
    cs = _matmul_residual([yac, ybc], wo, cs, mod5, layer, 2, ctx_row, tn=2 * COL_TILE)
    a, (wd, w_in1) = ffn_up(xs, layer, lat_row, wg, wu, cast_jobs=[
        _CastJob(ffn_w_down, f_pad, d), _CastJob(od_w_in, d, od_w_in.shape[2])])
    xs = ffn_down(a, xs, layer, lat_row, wd)
    cs = ffn_down(ffn_up(cs, layer, ctx_row, wg, wu)[0], cs, layer, ctx_row, wd)

    layer = 1
    w_in = w_in1[0]
    h = _norm_mod(xs, norm_mix_g[layer], mod5, layer, 0, 1, lat_row)
    hc = _norm_mod(cs, norm_mix_g[layer], mod5, layer, 0, 1, ctx_row)
    p_lat = _matmul(h, w_in, BF16)[0]
    p_ctx = _matmul(hc, w_in, BF16)[0]

    yn =_na(p_lat, p_ctx, na_rpb[0], nb, n_tok, n_q)

    dq0 = 3 * half
    dk0 = dq0 + half
    dv0 = dk0 + half
    lam_init = 0.8 - 0.6 * math.exp(-0.3 * layer)
    kd = _rope(p_lat, dk0, half, cos, sin, 1.0)
    yd = _diff(cos, sin, kd, p_lat, p_ctx, dq0, dk0, dv0, diff_lq1[0], diff_lk1[0], diff_lq2[0], diff_lk2[0],
               diff_norm_g[0], lam_init, nb, n_tok, diff_heads)

    xs = _matmul_residual([yn, yd], wo, xs, mod5, layer, 2, lat_row, tn=2 * COL_TILE)
    xs = ffn_down(ffn_up(xs, layer, lat_row, wg, wu)[0], xs, layer, lat_row, wd)
    return _final_norm(xs, final_norm_g).reshape(nb, n_tok, d)
```

```python
import functools
import math
from typing import NamedTuple

import numpy as np
import jax
import jax.numpy as jnp
from jax import lax
from jax.experimental import pallas as pl
from jax.experimental.pallas import tpu as pltpu

HEAD_DIM = 128
GRID_W = 64
ROPE_THETA = 10000.0
NORM_EPS = 1e-6
GLA_DK = 128
GLA_DV = 256
GLA_RANK = 16
GLA_NORMALIZER = 16.0
GLA_CHUNK = 64
SWA_GROUP = 4
SWA_WINDOW = 128
SWA_BLOCK = 128
NA_ROWS = 8
NA_COLS = 16
NA_QROWS = 4
DIFF_DV = 2 * HEAD_DIM
DIFF_KEY_BLOCK = 2048
CTX_BLOCK = 256

V7X_VMEM_BUDGET = 60 * 1024 * 1024
LANE = 128
ROW_TILE = 1024
COL_TILE = 512
FFN_PAD = 1024

F32 = jnp.float32
BF16 = jnp.bfloat16


def _params(sem, vmem_bytes):
    limit = int(min(V7X_VMEM_BUDGET, vmem_bytes * 5 // 4 + (4 << 20)))
    return pltpu.CompilerParams(dimension_semantics=sem, vmem_limit_bytes=limit)


def _dot(a, b):
    return jnp.dot(a, b, preferred_element_type=F32)


def _dot_nt(a, b):
    return lax.dot_general(a, b, (((1,), (1,)), ((), ())), preferred_element_type=F32)


def _dot_tn(a, b):
    return lax.dot_general(a, b, (((0,), (0,)), ((), ())), preferred_element_type=F32)


def _silu(t):
    return t * jax.nn.sigmoid(t)


def _mod_kernel(s_ref, w_ref, b_ref, o_ref):
    a = _silu(s_ref[...]).astype(BF16)
    o_ref[...] = _dot(a, w_ref[...].astype(BF16)) + b_ref[...]


def _mod_vectors(s8, ada_w, ada_b):
    depth, d, n = ada_w.shape
    tn = COL_TILE
    return pl.pallas_call(
        _mod_kernel,
        grid=(depth, n // tn),
        in_specs=[
            pl.BlockSpec((8, d), lambda l, j: (0, 0)),
            pl.BlockSpec((None, d, tn), lambda l, j: (l, 0, j)),
            pl.BlockSpec((None, 1, tn), lambda l, j: (l, 0, j)),
        ],
        out_specs=pl.BlockSpec((None, 8, tn), lambda l, j: (l, 0, j)),
        out_shape=jax.ShapeDtypeStruct((depth, 8, n), F32),
        compiler_params=_params(("parallel", "parallel"), 2 * d * tn * 4 + d * tn * 2),
        name="adaln_mod",
    )(s8, ada_w, ada_b.reshape(depth, 1, n))


def _norm_mod_kernel(x_ref, g_ref, sh_ref, sc_ref, o_ref):
    x = x_ref[...]
    r = lax.rsqrt(jnp.mean(x * x, axis=-1, keepdims=True) + NORM_EPS)
    y = x * r * g_ref[...]
    o_ref[...] = (y * (1.0 + sc_ref[...]) + sh_ref[...]).astype(o_ref.dtype)


def _norm_mod(x2, g, mod5, layer, k_shift, k_scale, mod_row, tm=512):
    m, d = x2.shape
    return pl.pallas_call(
        _norm_mod_kernel,
        grid=(m // tm,),
        in_specs=[
            pl.BlockSpec((tm, d), lambda i: (i, 0)),
            pl.BlockSpec((1, d), lambda i: (0, 0)),
            pl.BlockSpec((None, None, None, 1, d), lambda i: (layer, mod_row(i, tm), k_shift, 0, 0)),
            pl.BlockSpec((None, None, None, 1, d), lambda i: (layer, mod_row(i, tm), k_scale, 0, 0)),
        ],
        out_specs=pl.BlockSpec((tm, d), lambda i: (i, 0)),
        out_shape=jax.ShapeDtypeStruct((m, d), BF16),
        compiler_params=_params(("parallel",), 2 * tm * d * 6),
        name="norm_mod",
    )(x2, g.reshape(1, d), mod5, mod5)


def _final_norm_kernel(x_ref, g_ref, o_ref):
    x = x_ref[...]
    r = lax.rsqrt(jnp.mean(x * x, axis=-1, keepdims=True) + NORM_EPS)
    o_ref[...] = x * r * g_ref[...]


def _final_norm(x2, g, tm=512):
    m, d = x2.shape
    return pl.pallas_call(
        _final_norm_kernel,
        grid=(m // tm,),
        in_specs=[pl.BlockSpec((tm, d), lambda i: (i, 0)), pl.BlockSpec((1, d), lambda i: (0, 0))],
        out_specs=pl.BlockSpec((tm, d), lambda i: (i, 0)),
        out_shape=jax.ShapeDtypeStruct((m, d), F32),
        compiler_params=_params(("parallel",), 2 * tm * d * 8),
        name="final_norm",
    )(x2, g.reshape(1, d))


def _col_tile(n):
    return next(t for t in range(COL_TILE, 0, -LANE) if n % t == 0)


class _CastJob(NamedTuple):
    src: jax.Array
    r_pad: int
    c_pad: int


def _cast_plan(job, n_steps):
    depth, r, _ = job.src.shape
    for rc in range(16, r + 1, 16):
        if r % rc == 0 and job.r_pad % rc == 0 and depth * (job.r_pad // rc) <= n_steps:
            return rc
    return None


def _cast_specs(job, rc, n_j):
    depth, r, c = job.src.shape
    cpl, valid = job.r_pad // rc, r // rc
    n_total = depth * cpl

    def pos(i, j):
        t = jnp.minimum(i * n_j + j, n_total - 1)
        return t // cpl, t % cpl

    in_spec = pl.BlockSpec((None, rc, c), lambda i, j: (pos(i, j)[0], jnp.minimum(pos(i, j)[1], valid - 1), 0))
    out_spec = pl.BlockSpec((None, rc, job.c_pad), lambda i, j: (pos(i, j)[0], pos(i, j)[1], 0))
    out_shape = jax.ShapeDtypeStruct((depth, job.r_pad, job.c_pad), BF16)
    return in_spec, out_spec, out_shape, (cpl, valid, n_total)


def _run_casts(metas, src_refs, dst_refs, n_j):
    t = pl.program_id(0) * n_j + pl.program_id(1)
    for (cpl, valid, n_total), s, d in zip(metas, src_refs, dst_refs):
        chunk = jnp.minimum(t, n_total - 1) % cpl
        n = s.shape[-1]

        @pl.when(jnp.logical_and(t < n_total, chunk < valid))
        def _():
            d[:, :n] = s[...].astype(d.dtype)
            if d.shape[-1] > n:
                d[:, n:] = jnp.zeros((d.shape[0], d.shape[-1] - n), d.dtype)

        if cpl > valid:
            @pl.when(jnp.logical_and(t < n_total, chunk >= valid))
            def _():
                d[...] = jnp.zeros_like(d)


def _cast_only_kernel(*refs, metas, n_j):
    _run_casts(metas, refs[:len(metas)], refs[len(metas):], n_j)


def _attach_casts(jobs, n_i, n_j):
    plans = [_cast_plan(job, n_i * n_j) for job in jobs]
    riders = [(job, rc) for job, rc in zip(jobs, plans) if rc is not None]
    specs = [_cast_specs(job, rc, n_j) for job, rc in riders]

    def finish(rider_outs):
        rider_outs = list(rider_outs)
        outs = []
        for job, rc in zip(jobs, plans):
            if rc is not None:
                outs.append(rider_outs.pop(0))
                continue
            rc = next(c for c in (256, 128, 64, 32, 16) if job.src.shape[1] % c == 0 and job.r_pad % c == 0)
            n_steps = job.src.shape[0] * (job.r_pad // rc)
            i_s, o_s, o_shape, meta = _cast_specs(job, rc, n_steps)
            outs.append(pl.pallas_call(
                functools.partial(_cast_only_kernel, metas=[meta], n_j=n_steps),
                grid=(1, n_steps), in_specs=[i_s], out_specs=o_s, out_shape=o_shape,
                compiler_params=_params(("arbitrary", "arbitrary"), 2 * rc * (job.src.shape[2] * 4 + job.c_pad * 2)),
                name="cast_weights",
            )(job.src))
        return outs

    vmem = sum(2 * rc * (job.src.shape[2] * 4 + job.c_pad * 2) for job, rc in riders)
    return riders, specs, finish, vmem


def _mm_kernel(*refs, n_narrow, n_cast, metas, n_j):
    n_in = 2 + n_narrow + n_cast
    a_ref, w_ref = refs[:2]
    o_ref = refs[n_in]
    a = a_ref[...]
    o_ref[...] = _dot(a, w_ref[...]).astype(o_ref.dtype)
    if n_narrow:
        @pl.when(pl.program_id(1) == 0)
        def _():
            refs[n_in + 1][...] = _dot(a, refs[2][...])

    _run_casts(metas, refs[2 + n_narrow:n_in], refs[n_in + 1 + n_narrow:], n_j)


def _matmul(a, w, out_dtype=F32, w_narrow=None, cast_jobs=(), tm=ROW_TILE):
    m, k = a.shape
    n = w.shape[1]
    tm = min(tm, m)
    tn = _col_tile(n)
    ob = jnp.dtype(out_dtype).itemsize
    n_i, n_j = m // tm, n // tn
    riders, specs, finish, cast_vmem = _attach_casts(cast_jobs, n_i, n_j)
    narrow = [] if w_narrow is None else [w_narrow]
    outs = pl.pallas_call(
        functools.partial(_mm_kernel, n_narrow=len(narrow), n_cast=len(riders), metas=[s[3] for s in specs], n_j=n_j),
        grid=(n_i, n_j),
        in_specs=[pl.BlockSpec((tm, k), lambda i, j: (i, 0)), pl.BlockSpec((k, tn), lambda i, j: (0, j))]
        + [pl.BlockSpec((k, LANE), lambda i, j: (0, 0)) for _ in narrow] + [s[0] for s in specs],
        out_specs=[pl.BlockSpec((tm, tn), lambda i, j: (i, j))]
        + [pl.BlockSpec((tm, LANE), lambda i, j: (i, 0)) for _ in narrow] + [s[1] for s in specs],
        out_shape=[jax.ShapeDtypeStruct((m, n), out_dtype)]
        + [jax.ShapeDtypeStruct((m, LANE), F32) for _ in narrow] + [s[2] for s in specs],
        compiler_params=_params(("arbitrary", "arbitrary") if riders or narrow else ("parallel", "parallel"),
                                2 * (tm * k * 2 + k * tn * 2 + tm * tn * ob + k * LANE * 2 + tm * LANE * 4)
                                + cast_vmem),
        name="matmul",
    )(a, w, *narrow, *[job.src for job, _ in riders])
    n_main = 1 + len(narrow)
    return outs[0], (outs[1] if narrow else None), finish(outs[n_main:])


def _mm_swiglu_kernel(*refs, n_cast, metas, n_j):
    a_ref, wg_ref, wu_ref = refs[:3]
    o_ref = refs[3 + n_cast]
    a = a_ref[...]
    g = _dot(a, wg_ref[...])
    u = _dot(a, wu_ref[...])
    o_ref[...] = (_silu(g) * u).astype(o_ref.dtype)
    _run_casts(metas, refs[3:3 + n_cast], refs[4 + n_cast:], n_j)


def _matmul_swiglu(a, wg, wu, layer, cast_jobs=()):
    m, k = a.shape
    n = wg.shape[2]
    tm = min(ROW_TILE, m)
    tn = COL_TILE
    n_i, n_j = m // tm, n // tn
    riders, specs, finish, cast_vmem = _attach_casts(cast_jobs, n_i, n_j)
    outs = pl.pallas_call(
        functools.partial(_mm_swiglu_kernel, n_cast=len(riders), metas=[s[3] for s in specs], n_j=n_j),
        grid=(n_i, n_j),
        in_specs=[
            pl.BlockSpec((tm, k), lambda i, j: (i, 0)),
            pl.BlockSpec((None, k, tn), lambda i, j: (layer, 0, j)),
            pl.BlockSpec((None, k, tn), lambda i, j: (layer, 0, j)),
        ] + [s[0] for s in specs],
        out_specs=[pl.BlockSpec((tm, tn), lambda i, j: (i, j))] + [s[1] for s in specs],
        out_shape=[jax.ShapeDtypeStruct((m, n), BF16)] + [s[2] for s in specs],
        compiler_params=_params(("arbitrary", "arbitrary") if riders else ("parallel", "parallel"),
                                2 * (tm * k * 2 + 2 * k * tn * 2 + tm * tn * 2) + cast_vmem),
        name="matmul_swiglu",
    )(a, wg, wu, *[job.src for job, _ in riders])
    return outs[0], finish(outs[1:])


def _mm_res_kernel(*refs, nk, n_a):
    a_refs, (w_ref, res_ref, gate_ref, o_ref), scratch = refs[:n_a], refs[n_a:n_a + 4], refs[n_a + 4:]
    k0 = 0
    p = None
    for a_ref in a_refs:
        ka = a_ref.shape[1]
        d = _dot(a_ref[...], w_ref[k0:k0 + ka, :])
        p = d if p is None else p + d
        k0 += ka
    if nk == 1:
        o_ref[...] = res_ref[...] + gate_ref[...] * p
        return
    acc_ref, = scratch
    kk = pl.program_id(2)

    @pl.when(kk == 0)
    def _():
        acc_ref[...] = p

    @pl.when(jnp.logical_and(kk > 0, kk < nk - 1))
    def _():
        acc_ref[...] += p

    @pl.when(kk == nk - 1)
    def _():
        o_ref[...] = res_ref[...] + gate_ref[...] * (acc_ref[...] + p)


def _matmul_residual(a_slabs, w, res, mod5, layer, k_gate, mod_row, nk=1, tn=COL_TILE):
    m = a_slabs[0].shape[0]
    k, n = w.shape[1], w.shape[2]
    tm = min(ROW_TILE, m)
    tk = k // nk
    assert nk == 1 or len(a_slabs) == 1
    a_specs = [pl.BlockSpec((tm, tk if nk > 1 else a.shape[1]), lambda i, j, q: (i, q)) for a in a_slabs]
    return pl.pallas_call(
        functools.partial(_mm_res_kernel, nk=nk, n_a=len(a_slabs)),
        grid=(m // tm, n // tn, nk),
        in_specs=a_specs + [
            pl.BlockSpec((None, tk, tn), lambda i, j, q: (layer, q, j)),
            pl.BlockSpec((tm, tn), lambda i, j, q: (i, j)),
            pl.BlockSpec((None, None, None, 1, tn), lambda i, j, q: (layer, mod_row(i, tm), k_gate, 0, j)),
        ],
        out_specs=pl.BlockSpec((tm, tn), lambda i, j, q: (i, j)),
        out_shape=jax.ShapeDtypeStruct((m, n), F32),
        scratch_shapes=[pltpu.VMEM((tm, tn), F32)] if nk > 1 else [],
        compiler_params=_params(
            ("parallel", "parallel", "arbitrary"), 2 * (tm * tk * 2 + tk * tn * 2 + 2 * tm * tn * 4) + tm * tn * 4
        ),
        name="matmul_residual",
    )(*a_slabs, w, res, mod5)


def _rope_tables(n_tok):
    quarter = HEAD_DIM // 4
    inv = 1.0 / (ROPE_THETA ** (jnp.arange(quarter, dtype=F32) / quarter))
    pos = jnp.arange(n_tok, dtype=jnp.int32)
    row = (pos // GRID_W).astype(F32)[:, None] * inv
    col = (pos % GRID_W).astype(F32)[:, None] * inv
    cos = jnp.concatenate([jnp.cos(row), jnp.cos(row), jnp.cos(col), jnp.cos(col)], axis=-1)
    sin = jnp.concatenate([-jnp.sin(row), jnp.sin(row), -jnp.sin(col), jnp.sin(col)], axis=-1)
    return cos, sin


def _rotary(cos_ref, sin_ref, scale):
    cos = cos_ref[...] * scale
    sin = sin_ref[...] * scale
    lane = lax.broadcasted_iota(jnp.int32, cos.shape, 1)
    first = (lane & (HEAD_DIM // 4)) == 0

    def rotate(x):
        x = x.astype(F32)
        partner = jnp.where(first, pltpu.roll(x, HEAD_DIM - HEAD_DIM // 4, 1), pltpu.roll(x, HEAD_DIM // 4, 1))
        return x * cos + partner * sin

    return rotate


def _rope_kernel(x_ref, cos_ref, sin_ref, o_ref, *, scale, n_heads):
    rotate = _rotary(cos_ref, sin_ref, scale)
    for h in range(n_heads):
        hs = slice(h * HEAD_DIM, (h + 1) * HEAD_DIM)
        o_ref[:, hs] = rotate(x_ref[:, hs]).astype(o_ref.dtype)


def _rope(p, col0, width, cos, sin, scale, tr=256):
    m = p.shape[0]
    n_tok = cos.shape[0]
    tpb = n_tok // tr
    assert col0 % width == 0
    return pl.pallas_call(
        functools.partial(_rope_kernel, scale=scale, n_heads=width // HEAD_DIM),
        grid=(m // tr,),
        in_specs=[
            pl.BlockSpec((tr, width), lambda i: (i, col0 // width)),
            pl.BlockSpec((tr, HEAD_DIM), lambda i: (i % tpb, 0)),
            pl.BlockSpec((tr, HEAD_DIM), lambda i: (i % tpb, 0)),
        ],
        out_specs=pl.BlockSpec((tr, width), lambda i: (i, 0)),
        out_shape=jax.ShapeDtypeStruct((m, width), BF16),
        compiler_params=_params(("parallel",), 2 * tr * (width * 6 + HEAD_DIM * 8)),
        name="rope",
    )(p, cos, sin)


def _gla_kernel(*refs, nh, scale, fwd):
    ql, kl, vl, dl, qc, kc, vc, dc, up_ref, bias_ref = refs[:10]
    if fwd:
        ol, oc, st_ref = refs[10:]
    else:
        fl, fc, rl, rc, gn_ref, ol, oc, st_ref = refs[10:]
    s = pl.program_id(1)
    is_ctx = s == 0
    n_chunk = CTX_BLOCK // GLA_CHUNK

    @pl.when(s == 0)
    def _():
        st_ref[...] = jnp.zeros_like(st_ref)

    tb = CTX_BLOCK

    def in_scan_order(x):
        if fwd:
            return x
        return jnp.concatenate([x[(n_chunk - 1 - p) * GLA_CHUNK:(n_chunk - p) * GLA_CHUNK] for p in range(n_chunk)],
                               axis=0)

    def pick(rc_, rl_, cols):
        return jnp.where(is_ctx, rc_[:, cols], rl_[:, cols])

    def load(rc_, rl_, cols):
        return in_scan_order(pick(rc_, rl_, cols))

    row = lax.broadcasted_iota(jnp.int32, (tb, tb), 0)
    col = lax.broadcasted_iota(jnp.int32, (tb, tb), 1)
    incl = jnp.logical_and(row // GLA_CHUNK == col // GLA_CHUNK, (row >= col) if fwd else (col >= row))
    row_chunk = lax.broadcasted_iota(jnp.int32, (tb, GLA_DK), 0) // GLA_CHUNK

    def per_chunk(x):
        return jnp.concatenate([jnp.where(row_chunk == p, x, 0.0) for p in range(n_chunk)], axis=1)

    z = _dot(load(dc, dl, slice(None)).astype(BF16), up_ref[...]) + bias_ref[...]
    g = (jnp.minimum(z, 0.0) - jnp.log(1.0 + jnp.exp(-jnp.abs(z)))) * (1.0 / GLA_NORMALIZER)
    tmat = jnp.where(incl, 1.0, 0.0).astype(BF16)
    g_hi = g.astype(BF16)
    g_rest = g - g_hi.astype(F32)
    g_mid = g_rest.astype(BF16)
    g_lo = (g_rest - g_mid.astype(F32)).astype(BF16)
    cum = _dot(tmat, g_hi) + _dot(tmat, g_mid) + _dot(tmat, g_lo)
    tot = [jnp.sum(g[p * GLA_CHUNK:(p + 1) * GLA_CHUNK], axis=0, keepdims=True) for p in range(n_chunk)]
    tot_rows = jnp.concatenate([jnp.broadcast_to(t, (GLA_CHUNK, t.shape[1])) for t in tot], axis=0)

    staged = []
    for h in range(nh):
        ks = slice(h * GLA_DK, (h + 1) * GLA_DK)
        vs = slice(h * GLA_DV, (h + 1) * GLA_DV)
        bh = cum[:, ks]
        q = load(qc, ql, ks)
        k = load(kc, kl, ks)
        v = load(vc, vl, vs).astype(BF16)
        qd = q * (scale * jnp.exp(bh))
        ki = (k * jnp.exp(-bh)).astype(BF16)
        ku = k * jnp.exp(tot_rows[:, ks] - bh)
        a = jnp.where(incl, _dot_nt(qd.astype(BF16), ki), 0.0)
        inc = _dot_tn(v, per_chunk(ku).astype(BF16))
        staged.append((a.astype(BF16), inc, per_chunk(qd).astype(BF16), v))

    for h in range(nh):
        ks = slice(h * GLA_DK, (h + 1) * GLA_DK)
        vs = slice(h * GLA_DV, (h + 1) * GLA_DV)
        a, inc, qd_chunks, v = staged[h]
        st = st_ref[h]
        entering = []
        for p in range(n_chunk):
            entering.append(st)
            st = st * jnp.exp(tot[p][:, ks]) + inc[:, p * GLA_DK:(p + 1) * GLA_DK]
        st_ref[h] = st
        o = _dot(a, v) + _dot_nt(qd_chunks, jnp.concatenate(entering, axis=1).astype(BF16))
        o = in_scan_order(o)
        if not fwd:
            o = o + pick(fc, fl, vs)
            o = o * lax.rsqrt(jnp.mean(o * o, axis=-1, keepdims=True) + NORM_EPS) * gn_ref[...]
            o = o * _silu(pick(rc, rl, vs))
        o = o.astype(ol.dtype)

        @pl.when(is_ctx)
        def _():
            oc[:, vs] = o

        @pl.when(jnp.logical_not(is_ctx))
        def _():
            ol[:, vs] = o


def _gla(p_lat, p_ctx, dn_lat, dn_ctx, up2, bias2, norm_g, nb, n_tok, nh):
    qk = nh * GLA_DK
    dv = nh * GLA_DV
    tb = CTX_BLOCK
    nbl = n_tok // tb
    m_lat, m_ctx = nb * n_tok, nb * tb

    def scan(fwd, extra_in, extra_specs, out_dtype):
        def lat_blk(b, s):
            t = jnp.maximum(s - 1, 0)
            return b * nbl + (t if fwd else nbl - 1 - t)

        lat = lambda w, c: pl.BlockSpec((tb, w), lambda b, s: (lat_blk(b, s), c))
        ctx = lambda w, c: pl.BlockSpec((tb, w), lambda b, s: (b, c))
        d = 0 if fwd else 1
        vm = (2 * (2 * (2 * tb * qk + tb * dv + tb * LANE) * 4 + 2 * tb * dv * 4 + len(extra_in) * tb * dv * 4)
              + nh * GLA_DV * GLA_DK * 4)
        return pl.pallas_call(
            functools.partial(_gla_kernel, nh=nh, scale=GLA_DK ** -0.5, fwd=fwd),
            grid=(nb, nbl + 1),
            in_specs=[
                lat(qk, 0), lat(qk, 1), lat(dv, 1), lat(LANE, 0),
                ctx(qk, 0), ctx(qk, 1), ctx(dv, 1), ctx(LANE, 0),
                pl.BlockSpec((None, LANE, qk), lambda b, s: (d, 0, 0)),
                pl.BlockSpec((None, 1, qk), lambda b, s: (d, 0, 0)),
            ] + extra_specs(lat, ctx),
            out_specs=[lat(dv, 0), ctx(dv, 0)],
            out_shape=[jax.ShapeDtypeStruct((m_lat, dv), out_dtype), jax.ShapeDtypeStruct((m_ctx, dv), out_dtype)],
            scratch_shapes=[pltpu.VMEM((nh, GLA_DV, GLA_DK), F32)],
            compiler_params=_params(("parallel", "arbitrary"), vm),
            name="gla_scan_fwd" if fwd else "gla_scan_bwd",
        )(p_lat, p_lat, p_lat, dn_lat, p_ctx, p_ctx, p_ctx, dn_ctx, up2, bias2, *extra_in)

    of_lat, of_ctx = scan(True, (), lambda lat, ctx: [], F32)
    return scan(
        False,
        (of_lat, of_ctx, p_lat, p_ctx, norm_g.reshape(1, GLA_DV)),
        lambda lat, ctx: [lat(dv, 0), ctx(dv, 0), lat(dv, 2), ctx(dv, 2),
                          pl.BlockSpec((1, GLA_DV), lambda b, s: (0, 0))],
        BF16,
    )


def _swa_kernel(sink_ref, q_ref, cos_ref, sin_ref, kp, kc, kn, vp, vc, vn, kx, vx, o_ref, *, n_blk, kvs):
    i = pl.program_id(2)
    log2e = math.log2(math.e)
    rotate = _rotary(cos_ref, sin_ref, HEAD_DIM ** -0.5 * log2e)
    span = SWA_BLOCK + 2 * SWA_WINDOW
    gq = SWA_GROUP * SWA_BLOCK
    kj = lax.broadcasted_iota(jnp.int32, (span, gq), 0)
    lane = lax.broadcasted_iota(jnp.int32, (span, gq), 1)
    qi = lane & (SWA_BLOCK - 1)
    key_pos = (i - 1) * SWA_BLOCK + kj
    ok = jnp.logical_and(jnp.abs(qi + SWA_WINDOW - kj) <= SWA_WINDOW,
                         jnp.logical_and(key_pos >= 0, key_pos < n_blk * SWA_BLOCK))
    head = lax.broadcasted_iota(jnp.int32, (1, gq), 1) // SWA_BLOCK
    for kv in range(kvs):
        hkv = pl.program_id(1) * kvs + kv
        ks = slice(kv * HEAD_DIM, (kv + 1) * HEAD_DIM)
        kwin = jnp.concatenate([kp[:, ks], kc[:, ks], kn[:, ks]], axis=0)
        vwin = jnp.concatenate([vp[:, ks], vc[:, ks], vn[:, ks]], axis=0).astype(BF16)
        kctx = kx[:, ks].astype(BF16)
        vctx = vx[:, ks].astype(BF16)
        q0 = kv * SWA_GROUP * HEAD_DIM
        qs = jnp.concatenate([rotate(q_ref[:, q0 + g * HEAD_DIM:q0 + (g + 1) * HEAD_DIM]).astype(BF16)
                              for g in range(SWA_GROUP)], axis=0)
        s_win = jnp.where(ok, _dot_nt(kwin, qs), -jnp.inf)
        s_ctx = _dot_nt(kctx, qs)
        sink = jnp.zeros((1, gq), F32)
        for g in range(SWA_GROUP):
            sink = jnp.where(head == g, sink_ref[0, hkv * SWA_GROUP + g] * log2e, sink)
        m = jnp.maximum(jnp.maximum(jnp.max(s_win, axis=0, keepdims=True), jnp.max(s_ctx, axis=0, keepdims=True)),
                        sink)
        p_win = jnp.exp2(s_win - m)
        p_ctx = jnp.exp2(s_ctx - m)
        den = jnp.sum(p_win, axis=0, keepdims=True) + jnp.sum(p_ctx, axis=0, keepdims=True) + jnp.exp2(sink - m)
        ot = (_dot_tn(vctx, p_ctx.astype(BF16)) + _dot_tn(vwin, p_win.astype(BF16))) / den
        for g in range(SWA_GROUP):
            hs = slice(q0 + g * HEAD_DIM, q0 + (g + 1) * HEAD_DIM)
            o_ref[:, hs] = ot[:, g * SWA_BLOCK:(g + 1) * SWA_BLOCK].T.astype(o_ref.dtype)


def _swa(sink, cos, sin, k_rope, p_lat, p_ctx, qcol, kcol, vcol, nb, n_tok, n_kv):
    n_blk = n_tok // SWA_BLOCK
    tb = SWA_BLOCK
    kvs = next(c for c in (4, 2, 1) if n_kv % c == 0 and kcol % (c * HEAD_DIM) == 0 and vcol % (c * HEAD_DIM) == 0
               and qcol % (c * SWA_GROUP * HEAD_DIM) == 0)
    kw = kvs * HEAD_DIM
    gw = kvs * SWA_GROUP * HEAD_DIM
    assert kcol % kw == 0 and vcol % kw == 0 and qcol % gw == 0
    kc0 = kcol // kw
    vc0 = vcol // kw
    qc0 = qcol // gw
    prev = lambda b, h, i: b * n_blk + jnp.maximum(i - 1, 0)
    cur = lambda b, h, i: b * n_blk + i
    nxt = lambda b, h, i: b * n_blk + jnp.minimum(i + 1, n_blk - 1)
    kspec = lambda f: pl.BlockSpec((tb, kw), lambda b, h, i: (f(b, h, i), h))
    vspec = lambda f: pl.BlockSpec((tb, kw), lambda b, h, i: (f(b, h, i), vc0 + h))
    return pl.pallas_call(
        functools.partial(_swa_kernel, n_blk=n_blk, kvs=kvs),
        grid=(nb, n_kv // kvs, n_blk),
        in_specs=[
            pl.BlockSpec(memory_space=pltpu.SMEM),
            pl.BlockSpec((tb, gw), lambda b, h, i: (cur(b, h, i), qc0 + h)),
            pl.BlockSpec((tb, HEAD_DIM), lambda b, h, i: (i, 0)),
            pl.BlockSpec((tb, HEAD_DIM), lambda b, h, i: (i, 0)),
            kspec(prev), kspec(cur), kspec(nxt),
            vspec(prev), vspec(cur), vspec(nxt),
            pl.BlockSpec((CTX_BLOCK, kw), lambda b, h, i: (b, kc0 + h)),
            pl.BlockSpec((CTX_BLOCK, kw), lambda b, h, i: (b, vc0 + h)),
        ],
        out_specs=pl.BlockSpec((tb, gw), lambda b, h, i: (cur(b, h, i), h)),
        out_shape=jax.ShapeDtypeStruct((nb * n_tok, n_kv * SWA_GROUP * HEAD_DIM), BF16),
        compiler_params=_params(("parallel", "parallel", "parallel"), 16 << 20),
        name="swa",
    )(sink, p_lat, cos, sin, k_rope, k_rope, k_rope, p_lat, p_lat, p_lat, p_ctx, p_ctx)


def _ctx_sink_kernel(sink_ref, q_ref, kx, vx, o_ref, *, scale):
    hkv = pl.program_id(1)
    kctx = kx[...].astype(BF16)
    vctx = vx[...].astype(BF16)
    for g in range(SWA_GROUP):
        hs = slice(g * HEAD_DIM, (g + 1) * HEAD_DIM)
        q = (q_ref[:, hs].astype(F32) * scale).astype(BF16)
        s = _dot_nt(q, kctx)
        sink = sink_ref[0, hkv * SWA_GROUP + g]
        m = jnp.maximum(jnp.max(s, axis=-1, keepdims=True), sink)
        p = jnp.exp(s - m)
        den = jnp.sum(p, axis=-1, keepdims=True) + jnp.exp(sink - m)
        o_ref[:, hs] = (_dot(p.astype(BF16), vctx) / den).astype(o_ref.dtype)


def _ctx_sink(sink, p_ctx, qcol, kcol, vcol, nb, n_kv):
    gw = SWA_GROUP * HEAD_DIM
    return pl.pallas_call(
        functools.partial(_ctx_sink_kernel, scale=HEAD_DIM ** -0.5),
        grid=(nb, n_kv),
        in_specs=[
            pl.BlockSpec(memory_space=pltpu.SMEM),
            pl.BlockSpec((CTX_BLOCK, gw), lambda b, h: (b, qcol // gw + h)),
            pl.BlockSpec((CTX_BLOCK, HEAD_DIM), lambda b, h: (b, kcol // HEAD_DIM + h)),
            pl.BlockSpec((CTX_BLOCK, HEAD_DIM), lambda b, h: (b, vcol // HEAD_DIM + h)),
        ],
        out_specs=pl.BlockSpec((CTX_BLOCK, gw), lambda b, h: (b, h)),
        out_shape=jax.ShapeDtypeStruct((nb * CTX_BLOCK, n_kv * gw), BF16),
        compiler_params=_params(("parallel", "parallel"), 4 << 20),
        name="ctx_sink_attention",
    )(sink, p_ctx, p_ctx, p_ctx)


def _na_bias_table(rpb, rows):
    qcol = np.arange(GRID_W)[:, None]
    kcol = np.arange(GRID_W)[None, :]
    col_start = np.clip(qcol - NA_COLS // 2, 0, GRID_W - NA_COLS)
    col_ok = (kcol >= col_start) & (kcol < col_start + NA_COLS)
    dc = np.clip(kcol - qcol + NA_COLS - 1, 0, 2 * NA_COLS - 2)
    n_dc = 2 * NA_COLS - 1
    onehot = (dc[None] == np.arange(n_dc)[:, None, None]).astype(np.float32)
    cols = jnp.einsum('hrd,dqk->hrqk', rpb.astype(F32), onehot, precision=lax.Precision.HIGHEST)
    cols = jnp.where(col_ok[None, None], cols, -jnp.inf).transpose(0, 1, 3, 2)
    masked = jnp.full(cols.shape[:1] + cols.shape[2:], -jnp.inf, F32)

    span = NA_ROWS + NA_QROWS - 1
    patterns = []
    for r0 in range(0, rows, NA_QROWS):
        rs = [min(max(r0 + i - NA_ROWS // 2, 0), rows - NA_ROWS) for i in range(NA_QROWS)]
        u = min(rs[0], rows - span)
        patterns.append(tuple((r0 + i - rs[i], rs[i] - u) for i in range(NA_QROWS)))
    variants = sorted(set(patterns), key=patterns.index)
    blocks = []
    for pat in variants:
        per_query_row = []
        for delta, off in pat:
            krows = [cols[:, kr - off - delta + NA_ROWS - 1] if 0 <= kr - off < NA_ROWS else masked
                     for kr in range(span)]
            per_query_row.append(jnp.stack(krows, axis=1))
        blocks.append(jnp.concatenate(per_query_row, axis=-1).reshape(rpb.shape[0], span * GRID_W, NA_QROWS * GRID_W))
    ids = np.array([variants.index(p) for p in patterns], dtype=np.int32)
    return jnp.stack(blocks, axis=1), ids


def _na_kernel(ids_ref, q_ref, k_ref, v_ref, kx, vx, bias_ref, o_ref, *, nh, scale):
    for h in range(nh):
        hs = slice(h * HEAD_DIM, (h + 1) * HEAD_DIM)
        q = (q_ref[:, hs].astype(F32) * scale).astype(BF16)
        s_nb = _dot_nt(k_ref[:, hs].astype(BF16), q) + bias_ref[h]
        s_ctx = _dot_nt(kx[:, hs].astype(BF16), q)
        m = jnp.maximum(jnp.max(s_nb, axis=0, keepdims=True), jnp.max(s_ctx, axis=0, keepdims=True))
        p_nb = jnp.exp(s_nb - m)
        p_ctx = jnp.exp(s_ctx - m)
        den = jnp.sum(p_nb, axis=0, keepdims=True) + jnp.sum(p_ctx, axis=0, keepdims=True)
        ot = (_dot_tn(vx[:, hs].astype(BF16), p_ctx.astype(BF16))
              + _dot_tn(v_ref[:, hs].astype(BF16), p_nb.astype(BF16)))
        o_ref[:, hs] = (ot / den).T.astype(o_ref.dtype)


def _na(p_lat, p_ctx, rpb, nb, n_tok, nh):
    rows = n_tok // GRID_W
    w = nh * HEAD_DIM
    span = NA_ROWS + NA_QROWS - 1
    win = span * GRID_W
    tq = NA_QROWS * GRID_W
    n_step = rows // NA_QROWS
    bias_tab, ids = _na_bias_table(rpb, rows)
    u = lambda g: jnp.clip(g * NA_QROWS - NA_ROWS // 2, 0, rows - span)
    kv = lambda c: pl.BlockSpec((pl.Element(win), pl.Element(w)),
                                lambda b, g, ids: ((b * rows + u(g)) * GRID_W, c * w))
    vm = 2 * (tq * w * 4 + 2 * win * w * 4 + 2 * CTX_BLOCK * w * 4 + nh * tq * win * 4 + tq * w * 2)
    return pl.pallas_call(
        functools.partial(_na_kernel, nh=nh, scale=HEAD_DIM ** -0.5),
        grid_spec=pltpu.PrefetchScalarGridSpec(
            num_scalar_prefetch=1,
            grid=(nb, n_step),
            in_specs=[
                pl.BlockSpec((tq, w), lambda b, g, ids: (b * n_step + g, 0)),
                kv(1), kv(2),
                pl.BlockSpec((CTX_BLOCK, w), lambda b, g, ids: (b, 1)),
                pl.BlockSpec((CTX_BLOCK, w), lambda b, g, ids: (b, 2)),
                pl.BlockSpec((nh, None, win, tq), lambda b, g, ids: (0, ids[g], 0, 0)),
            ],
            out_specs=pl.BlockSpec((tq, w), lambda b, g, ids: (b * n_step + g, 0)),
        ),
        out_shape=jax.ShapeDtypeStruct((nb * n_tok, w), BF16),
        compiler_params=_params(("parallel", "arbitrary"), vm),
        name="neighborhood_attention",
    )(jnp.asarray(ids), p_lat, p_lat, p_lat, p_ctx, p_ctx, bias_tab)


def _diff_kernel(qraw_ref, cos_ref, sin_ref, k_ref, v_ref, kx, vx, lq1, lk1, lq2, lk2, g_ref, o_ref,
                 m_ref, l_ref, acc_ref, sa_ref, sb_ref, pa_ref, pb_ref, pc_ref, aa_ref, ab_ref, ac_ref,
                 vt_ref, vxt_ref, q_ref, *, n_blk, tk, lam_init):

    rotate = _rotary(cos_ref, sin_ref, HEAD_DIM ** -0.5 * math.log2(math.e))
    for t in range(2):
        hs = slice(t * HEAD_DIM, (t + 1) * HEAD_DIM)
        q_ref[:, hs] = rotate(qraw_ref[:, hs]).astype(q_ref.dtype)

    @pl.when(pl.program_id(2) == 0)
    def _():
        for j in range(n_blk):
            vt_ref[j] = v_ref[j * tk:(j + 1) * tk, :].T
        vxt_ref[...] = vx[...].astype(BF16).T

    vxt = vxt_ref
    m_ref[...] = jnp.full_like(m_ref, -jnp.inf)
    l_ref[...] = jnp.zeros_like(l_ref)
    acc_ref[...] = jnp.zeros_like(acc_ref)

    def scores(keys):
        return [_dot_nt(keys[:, t * HEAD_DIM:(t + 1) * HEAD_DIM], q_ref[:, t * HEAD_DIM:(t + 1) * HEAD_DIM])
                for t in range(2)]

    def lat_scores(j, dst):
        s = scores(k_ref[pl.ds(pl.multiple_of(j * tk, tk), tk), :])
        dst[0] = s[0]
        dst[1] = s[1]

    def softmax(s, p_dst, a_dst):
        for t in range(2):
            m_old = m_ref[t]
            m_new = jnp.maximum(m_old, jnp.max(s[t], axis=0, keepdims=True))
            alpha = jnp.exp2(m_old - m_new)
            p = jnp.exp2(s[t] - m_new)
            l_ref[t] = alpha * l_ref[t] + jnp.sum(p, axis=0, keepdims=True)
            m_ref[t] = m_new
            p_dst[t] = p.astype(BF16)
            a_dst[t] = alpha

    def add_values(p_src, a_src, vt):
        for t in range(2):
            acc_ref[t] = a_src[t] * acc_ref[t] + _dot(vt, p_src[t])

    def stage(s_cur, p_cur, a_cur, next_block, s_next, prev):
        if next_block is not None:
            lat_scores(next_block, s_next)
        add_values(*prev)
        softmax((s_cur[0], s_cur[1]), p_cur, a_cur)

    s_ctx = scores(kx[...].astype(BF16))
    lat_scores(0, sa_ref)
    softmax(s_ctx, pc_ref, ac_ref)
    stage(sa_ref, pa_ref, aa_ref, 1, sb_ref, (pc_ref, ac_ref, vxt[...]))

    def body(jj, carry):
        j = 2 * jj + 1
        stage(sb_ref, pb_ref, ab_ref, j + 1, sa_ref, (pa_ref, aa_ref, vt_ref[j - 1]))
        stage(sa_ref, pa_ref, aa_ref, j + 2, sb_ref, (pb_ref, ab_ref, vt_ref[j]))
        return carry

    lax.fori_loop(0, (n_blk - 2) // 2, body, 0)
    stage(sb_ref, pb_ref, ab_ref, None, None, (pa_ref, aa_ref, vt_ref[n_blk - 2]))
    add_values(pb_ref, ab_ref, vt_ref[n_blk - 1])

    lam = (jnp.exp(jnp.sum(lq1[...] * lk1[...], axis=-1, keepdims=True))
           - jnp.exp(jnp.sum(lq2[...] * lk2[...], axis=-1, keepdims=True)) + lam_init)
    o = acc_ref[0] / l_ref[0] - lam * (acc_ref[1] / l_ref[1])
    y = o * lax.rsqrt(jnp.mean(o * o, axis=0, keepdims=True) + NORM_EPS) * (g_ref[...] * (1.0 - lam_init))
    o_ref[...] = y.T.astype(o_ref.dtype)


def _diff(cos, sin, k_rope, p_lat, p_ctx, qcol, kcol, vcol, lq1, lk1, lq2, lk2, norm_g, lam_init, nb, n_tok, nh,
          tq=512, tk=DIFF_KEY_BLOCK):
    nq = n_tok // tq
    n_blk = n_tok // tk
    assert n_blk >= 2 and n_blk % 2 == 0, "the key-block pipeline alternates two buffers"
    w = 2 * HEAD_DIM
    vec = lambda: pl.BlockSpec((1, HEAD_DIM), lambda b, h, i: (0, 0))
    vm = (2 * (tq * w * 2 + 2 * n_tok * w * 2 + 2 * CTX_BLOCK * w * 2 + tq * w * 2)
          + n_tok * w * 2 + 2 * tq * w * 4 + 2 * 2 * tk * tq * 6 + tq * w * 8)
    return pl.pallas_call(
        functools.partial(_diff_kernel, n_blk=n_blk, tk=tk, lam_init=lam_init),
        grid=(nb, nh, nq),
        in_specs=[
            pl.BlockSpec((tq, w), lambda b, h, i: (b * nq + i, qcol // w + h)),
            pl.BlockSpec((tq, HEAD_DIM), lambda b, h, i: (i, 0)),
            pl.BlockSpec((tq, HEAD_DIM), lambda b, h, i: (i, 0)),
            pl.BlockSpec((n_tok, w), lambda b, h, i: (b, h)),
            pl.BlockSpec((n_tok, w), lambda b, h, i: (b, vcol // w + h)),
            pl.BlockSpec((CTX_BLOCK, w), lambda b, h, i: (b, kcol // w + h)),
            pl.BlockSpec((CTX_BLOCK, w), lambda b, h, i: (b, vcol // w + h)),
            vec(), vec(), vec(), vec(),
            pl.BlockSpec((DIFF_DV, 1), lambda b, h, i: (0, 0)),
        ],
        out_specs=pl.BlockSpec((tq, w), lambda b, h, i: (b * nq + i, h)),
        out_shape=jax.ShapeDtypeStruct((nb * n_tok, nh * w), BF16),
        scratch_shapes=[
            pltpu.VMEM((2, 1, tq), F32),
            pltpu.VMEM((2, 1, tq), F32),
            pltpu.VMEM((2, DIFF_DV, tq), F32),
            pltpu.VMEM((2, tk, tq), F32),
            pltpu.VMEM((2, tk, tq), F32),
            pltpu.VMEM((2, tk, tq), BF16),
            pltpu.VMEM((2, tk, tq), BF16),
            pltpu.VMEM((2, CTX_BLOCK, tq), BF16),
            pltpu.VMEM((2, 1, tq), F32),
            pltpu.VMEM((2, 1, tq), F32),
            pltpu.VMEM((2, 1, tq), F32),
            pltpu.VMEM((n_blk, w, tk), BF16),
            pltpu.VMEM((w, CTX_BLOCK), BF16),
            pltpu.VMEM((tq, w), BF16),
        ],
        compiler_params=_params(("parallel", "parallel", "arbitrary"), vm),
        name="diff_attention",
    )(p_lat, cos, sin, k_rope, p_lat, p_ctx, p_ctx, lq1.reshape(1, -1), lk1.reshape(1, -1), lq2.reshape(1, -1),
      lk2.reshape(1, -1), norm_g.reshape(-1, 1))


def _pad_cols(w, n):
    return jnp.pad(w, ((0, 0), (0, n - w.shape[1])))


def kernel(x, c, ctx, c_ctx, ada_w, ada_b, norm_mix_g, norm_ffn_g, w_out, ffn_w_gate, ffn_w_up, ffn_w_down,
           ev_w_in, gla_gate_up_f, gla_gate_bias_f, gla_gate_up_b, gla_gate_bias_b, gla_norm_g, swa_sink,
           od_w_in, na_rpb, diff_lq1, diff_lk1, diff_lq2, diff_lk2, diff_norm_g, final_norm_g):
    nb, n_tok, d = x.shape
    lc = ctx.shape[1]
    depth = ada_w.shape[0]
    f_hidden = ffn_w_gate.shape[2]
    half = d // 2
    assert lc == CTX_BLOCK and nb < 8 and depth == 2, "layout assumes a 256-token context, batch < 8, two layers"
    gla_heads = half // GLA_DV
    gla_qk = gla_heads * GLA_DK
    n_q = half // HEAD_DIM
    n_kv = n_q // SWA_GROUP
    diff_heads = half // (2 * HEAD_DIM)

    xs = x.reshape(nb * n_tok, d)
    cs = ctx.reshape(nb * lc, d)
    cos, sin = _rope_tables(n_tok)

    s8 = jnp.zeros((8, d), F32).at[:nb].set(c).at[nb].set(c_ctx)
    mod5 = _mod_vectors(s8, ada_w, ada_b).reshape(depth, 8, 6, 1, d)
    lat_row = lambda i, tm: (i * tm) // n_tok
    ctx_row = lambda i, tm: nb

    f_pad = -(-f_hidden // FFN_PAD) * FFN_PAD

    def ffn_up(stream, layer, mod_row, wg, wu, cast_jobs=()):
        h2 = _norm_mod(stream, norm_ffn_g[layer], mod5, layer, 3, 4, mod_row)
        return _matmul_swiglu(h2, wg, wu, layer, cast_jobs)

    def ffn_down(a, stream, layer, mod_row, wd):
        return _matmul_residual([a], wd, stream, mod5, layer, 5, mod_row, nk=4, tn=2 * COL_TILE)

    layer = 0
    w_in = ev_w_in[0]
    g0 = 2 * gla_qk + 2 * half
    w_main = jnp.concatenate([w_in[:, :g0], w_in[:, g0 + 2 * GLA_RANK:]], axis=1).astype(BF16)
    w_dn = _pad_cols(w_in[:, g0:g0 + 2 * GLA_RANK], LANE).astype(BF16)
    swa_q0 = g0
    swa_k0 = swa_q0 + half
    swa_v0 = swa_k0 + n_kv * HEAD_DIM
    up2 = jnp.zeros((2, LANE, gla_qk), F32)
    up2 = up2.at[0, :GLA_RANK].set(gla_gate_up_f[0]).at[1, GLA_RANK:2 * GLA_RANK].set(gla_gate_up_b[0]).astype(BF16)
    bias2 = jnp.stack([gla_gate_bias_f[0], gla_gate_bias_b[0]]).reshape(2, 1, gla_qk)

    h = _norm_mod(xs, norm_mix_g[layer], mod5, layer, 0, 1, lat_row)
    hc = _norm_mod(cs, norm_mix_g[layer], mod5, layer, 0, 1, ctx_row)
    p_lat, dn_lat, (wg, wu, wo) = _matmul(h, w_main, w_narrow=w_dn, cast_jobs=[
        _CastJob(ffn_w_gate, d, f_pad), _CastJob(ffn_w_up, d, f_pad), _CastJob(w_out, d, d)])
    p_ctx, dn_ctx, _ = _matmul(hc, w_main, w_narrow=w_dn)

    ya, yac = _gla(p_lat, p_ctx, dn_lat, dn_ctx, up2, bias2, gla_norm_g[0], nb, n_tok, gla_heads)

    sink = swa_sink[0].reshape(1, n_q)
    k_rope = _rope(p_lat, swa_k0, n_kv * HEAD_DIM, cos, sin, 1.0)
    yb = _swa(sink, cos, sin, k_rope, p_lat, p_ctx, swa_q0, swa_k0, swa_v0, nb, n_tok, n_kv)
    ybc = _ctx_sink(sink, p_ctx, swa_q0, swa_k0, swa_v0, nb, n_kv)

    xs = _matmul_residual([ya, yb], wo, xs, mod5, layer, 2, lat_row, tn=2 * COL_TILE)
    cs = _matmul_residual([yac, ybc], wo, cs, mod5, layer, 2, ctx_row, tn=2 * COL_TILE)
    a, (wd, w_in1) = ffn_up(xs, layer, lat_row, wg, wu, cast_jobs=[
        _CastJob(ffn_w_down, f_pad, d), _CastJob(od_w_in, d, od_w_in.shape[2])])
    xs = ffn_down(a, xs, layer, lat_row, wd)
    cs = ffn_down(ffn_up(cs, layer, ctx_row, wg, wu)[0], cs, layer, ctx_row, wd)

    layer = 1
    w_in = w_in1[0]
    h = _norm_mod(xs, norm_mix_g[layer], mod5, layer, 0, 1, lat_row)
    hc = _norm_mod(cs, norm_mix_g[layer], mod5, layer, 0, 1, ctx_row)
    p_lat = _matmul(h, w_in, BF16, tm=2 * ROW_TILE)[0]
    p_ctx = _matmul(hc, w_in, BF16)[0]

    yn =_na(p_lat, p_ctx, na_rpb[0], nb, n_tok, n_q)

    dq0 = 3 * half
    dk0 = dq0 + half
    dv0 = dk0 + half
    lam_init = 0.8 - 0.6 * math.exp(-0.3 * layer)
    kd = _rope(p_lat, dk0, half, cos, sin, 1.0)
    yd = _diff(cos, sin, kd, p_lat, p_ctx, dq0, dk0, dv0, diff_lq1[0], diff_lk1[0], diff_lq2[0], diff_lk2[0],
               diff_norm_g[0], lam_init, nb, n_tok, diff_heads)

    xs = _matmul_residual([yn, yd], wo, xs, mod5, layer, 2, lat_row, tn=2 * COL_TILE)
    xs = ffn_down(ffn_up(xs, layer, lat_row, wg, wu)[0], xs, layer, lat_row, wd)
    return _final_norm(xs, final_norm_g).reshape(nb, n_tok, d)
```

```python
import functools
import math
from typing import NamedTuple

import numpy as np
import jax
import jax.numpy as jnp
from jax import lax
from jax.experimental import pallas as pl
from jax.experimental.pallas import tpu as pltpu

HEAD_DIM = 128
GRID_W = 64
ROPE_THETA = 10000.0
NORM_EPS = 1e-6
GLA_DK = 128
GLA_DV = 256
GLA_RANK = 16
GLA_NORMALIZER = 16.0
GLA_CHUNK = 64
SWA_GROUP = 4
SWA_WINDOW = 128
SWA_BLOCK = 128
NA_ROWS = 8
NA_COLS = 16
NA_QROWS = 4
DIFF_DV = 2 * HEAD_DIM
DIFF_KEY_BLOCK = 2048
CTX_BLOCK = 256

V7X_VMEM_BUDGET = 60 * 1024 * 1024
LANE = 128
ROW_TILE = 1024
COL_TILE = 512
FFN_PAD = 1024

F32 = jnp.float32
BF16 = jnp.bfloat16


def _params(sem, vmem_bytes):
    limit = int(min(V7X_VMEM_BUDGET, vmem_bytes * 5 // 4 + (4 << 20)))
    return pltpu.CompilerParams(dimension_semantics=sem, vmem_limit_bytes=limit)


def _dot(a, b):
    return jnp.dot(a, b, preferred_element_type=F32)


def _dot_nt(a, b):
    return lax.dot_general(a, b, (((1,), (1,)), ((), ())), preferred_element_type=F32)


def _dot_tn(a, b):
    return lax.dot_general(a, b, (((0,), (0,)), ((), ())), preferred_element_type=F32)


def _silu(t):
    return t * jax.nn.sigmoid(t)


def _mod_kernel(s_ref, w_ref, b_ref, o_ref):
    a = _silu(s_ref[...]).astype(BF16)
    o_ref[...] = _dot(a, w_ref[...].astype(BF16)) + b_ref[...]


def _mod_vectors(s8, ada_w, ada_b, first_layer, n_layers):
    depth, d, n = ada_w.shape
    tn = COL_TILE
    return pl.pallas_call(
        _mod_kernel,
        grid=(n_layers, n // tn),
        in_specs=[
            pl.BlockSpec((8, d), lambda l, j: (0, 0)),
            pl.BlockSpec((None, d, tn), lambda l, j: (first_layer + l, 0, j)),
            pl.BlockSpec((None, 1, tn), lambda l, j: (first_layer + l, 0, j)),
        ],
        out_specs=pl.BlockSpec((None, 8, tn), lambda l, j: (l, 0, j)),
        out_shape=jax.ShapeDtypeStruct((n_layers, 8, n), F32),
        compiler_params=_params(("parallel", "parallel"), 2 * d * tn * 4 + d * tn * 2),
        name="adaln_mod",
    )(s8, ada_w, ada_b.reshape(depth, 1, n))


def _norm_mod_kernel(x_ref, g_ref, sh_ref, sc_ref, o_ref):
    x = x_ref[...]
    r = lax.rsqrt(jnp.mean(x * x, axis=-1, keepdims=True) + NORM_EPS)
    y = x * r * g_ref[...]
    o_ref[...] = (y * (1.0 + sc_ref[...]) + sh_ref[...]).astype(o_ref.dtype)


def _norm_mod(x2, g, mod5, layer, k_shift, k_scale, mod_row, tm=512):
    m, d = x2.shape
    return pl.pallas_call(
        _norm_mod_kernel,
        grid=(m // tm,),
        in_specs=[
            pl.BlockSpec((tm, d), lambda i: (i, 0)),
            pl.BlockSpec((1, d), lambda i: (0, 0)),
            pl.BlockSpec((None, None, None, 1, d), lambda i: (layer, mod_row(i, tm), k_shift, 0, 0)),
            pl.BlockSpec((None, None, None, 1, d), lambda i: (layer, mod_row(i, tm), k_scale, 0, 0)),
        ],
        out_specs=pl.BlockSpec((tm, d), lambda i: (i, 0)),
        out_shape=jax.ShapeDtypeStruct((m, d), BF16),
        compiler_params=_params(("parallel",), 2 * tm * d * 6),
        name="norm_mod",
    )(x2, g.reshape(1, d), mod5, mod5)


def _final_norm_kernel(x_ref, g_ref, o_ref):
    x = x_ref[...]
    r = lax.rsqrt(jnp.mean(x * x, axis=-1, keepdims=True) + NORM_EPS)
    o_ref[...] = x * r * g_ref[...]


def _final_norm(x2, g, tm=512):
    m, d = x2.shape
    return pl.pallas_call(
        _final_norm_kernel,
        grid=(m // tm,),
        in_specs=[pl.BlockSpec((tm, d), lambda i: (i, 0)), pl.BlockSpec((1, d), lambda i: (0, 0))],
        out_specs=pl.BlockSpec((tm, d), lambda i: (i, 0)),
        out_shape=jax.ShapeDtypeStruct((m, d), F32),
        compiler_params=_params(("parallel",), 2 * tm * d * 8),
        name="final_norm",
    )(x2, g.reshape(1, d))


def _col_tile(n):
    return next(t for t in range(COL_TILE, 0, -LANE) if n % t == 0)


class _CastJob(NamedTuple):
    src: jax.Array
    r_pad: int
    c_pad: int


def _cast_plan(job, n_steps):
    depth, r, _ = job.src.shape
    for rc in range(16, r + 1, 16):
        if r % rc == 0 and job.r_pad % rc == 0 and depth * (job.r_pad // rc) <= n_steps:
            return rc
    return None


def _cast_specs(job, rc, n_j):
    depth, r, c = job.src.shape
    cpl, valid = job.r_pad // rc, r // rc
    n_total = depth * cpl

    def pos(i, j):
        t = jnp.minimum(i * n_j + j, n_total - 1)
        return t // cpl, t % cpl

    in_spec = pl.BlockSpec((None, rc, c), lambda i, j: (pos(i, j)[0], jnp.minimum(pos(i, j)[1], valid - 1), 0))
    out_spec = pl.BlockSpec((None, rc, job.c_pad), lambda i, j: (pos(i, j)[0], pos(i, j)[1], 0))
    out_shape = jax.ShapeDtypeStruct((depth, job.r_pad, job.c_pad), BF16)
    return in_spec, out_spec, out_shape, (cpl, valid, n_total)


def _run_casts(metas, src_refs, dst_refs, n_j):
    t = pl.program_id(0) * n_j + pl.program_id(1)
    for (cpl, valid, n_total), s, d in zip(metas, src_refs, dst_refs):
        chunk = jnp.minimum(t, n_total - 1) % cpl
        n = s.shape[-1]

        @pl.when(jnp.logical_and(t < n_total, chunk < valid))
        def _():
            d[:, :n] = s[...].astype(d.dtype)
            if d.shape[-1] > n:
                d[:, n:] = jnp.zeros((d.shape[0], d.shape[-1] - n), d.dtype)

        if cpl > valid:
            @pl.when(jnp.logical_and(t < n_total, chunk >= valid))
            def _():
                d[...] = jnp.zeros_like(d)


def _cast_only_kernel(*refs, metas, n_j):
    _run_casts(metas, refs[:len(metas)], refs[len(metas):], n_j)


def _attach_casts(jobs, n_i, n_j):
    plans = [_cast_plan(job, n_i * n_j) for job in jobs]
    riders = [(job, rc) for job, rc in zip(jobs, plans) if rc is not None]
    specs = [_cast_specs(job, rc, n_j) for job, rc in riders]

    def finish(rider_outs):
        rider_outs = list(rider_outs)
        outs = []
        for job, rc in zip(jobs, plans):
            if rc is not None:
                outs.append(rider_outs.pop(0))
                continue
            rc = next(c for c in (256, 128, 64, 32, 16) if job.src.shape[1] % c == 0 and job.r_pad % c == 0)
            n_steps = job.src.shape[0] * (job.r_pad // rc)
            i_s, o_s, o_shape, meta = _cast_specs(job, rc, n_steps)
            outs.append(pl.pallas_call(
                functools.partial(_cast_only_kernel, metas=[meta], n_j=n_steps),
                grid=(1, n_steps), in_specs=[i_s], out_specs=o_s, out_shape=o_shape,
                compiler_params=_params(("arbitrary", "arbitrary"), 2 * rc * (job.src.shape[2] * 4 + job.c_pad * 2)),
                name="cast_weights",
            )(job.src))
        return outs

    vmem = sum(2 * rc * (job.src.shape[2] * 4 + job.c_pad * 2) for job, rc in riders)
    return riders, specs, finish, vmem


def _mm_kernel(*refs, n_narrow, n_cast, metas, n_j):
    n_in = 2 + n_narrow + n_cast
    a_ref, w_ref = refs[:2]
    o_ref = refs[n_in]
    a = a_ref[...]
    o_ref[...] = _dot(a, w_ref[...]).astype(o_ref.dtype)
    if n_narrow:
        @pl.when(pl.program_id(1) == 0)
        def _():
            refs[n_in + 1][...] = _dot(a, refs[2][...])

    _run_casts(metas, refs[2 + n_narrow:n_in], refs[n_in + 1 + n_narrow:], n_j)


def _matmul(a, w, out_dtype=F32, w_narrow=None, cast_jobs=(), tm=ROW_TILE):
    m, k = a.shape
    n = w.shape[1]
    tm = min(tm, m)
    tn = _col_tile(n)
    ob = jnp.dtype(out_dtype).itemsize
    n_i, n_j = m // tm, n // tn
    riders, specs, finish, cast_vmem = _attach_casts(cast_jobs, n_i, n_j)
    narrow = [] if w_narrow is None else [w_narrow]
    outs = pl.pallas_call(
        functools.partial(_mm_kernel, n_narrow=len(narrow), n_cast=len(riders), metas=[s[3] for s in specs], n_j=n_j),
        grid=(n_i, n_j),
        in_specs=[pl.BlockSpec((tm, k), lambda i, j: (i, 0)), pl.BlockSpec((k, tn), lambda i, j: (0, j))]
        + [pl.BlockSpec((k, LANE), lambda i, j: (0, 0)) for _ in narrow] + [s[0] for s in specs],
        out_specs=[pl.BlockSpec((tm, tn), lambda i, j: (i, j))]
        + [pl.BlockSpec((tm, LANE), lambda i, j: (i, 0)) for _ in narrow] + [s[1] for s in specs],
        out_shape=[jax.ShapeDtypeStruct((m, n), out_dtype)]
        + [jax.ShapeDtypeStruct((m, LANE), F32) for _ in narrow] + [s[2] for s in specs],
        compiler_params=_params(("arbitrary", "arbitrary") if riders or narrow else ("parallel", "parallel"),
                                2 * (tm * k * 2 + k * tn * 2 + tm * tn * ob + k * LANE * 2 + tm * LANE * 4)
                                + cast_vmem),
        name="matmul",
    )(a, w, *narrow, *[job.src for job, _ in riders])
    n_main = 1 + len(narrow)
    return outs[0], (outs[1] if narrow else None), finish(outs[n_main:])


def _mm_swiglu_kernel(*refs, n_mod, n_cast, metas, n_j):
    n_in = 3 + 3 * n_mod + n_cast
    a_ref, wg_ref, wu_ref = refs[:3]
    o_ref = refs[n_in]
    a = a_ref[...]
    g = _dot(a, wg_ref[...])
    u = _dot(a, wu_ref[...])
    o_ref[...] = (_silu(g) * u).astype(o_ref.dtype)
    if n_mod:
        @pl.when(pl.program_id(0) * n_j + pl.program_id(1) < metas[-1])
        def _():
            _mod_kernel(refs[3], refs[4], refs[5], refs[n_in + 1])

    _run_casts(metas[:n_cast], refs[3 + 3 * n_mod:n_in], refs[n_in + 1 + n_mod:], n_j)


def _matmul_swiglu(a, wg, wu, layer, cast_jobs=(), mod_job=None):
    m, k = a.shape
    n = wg.shape[2]
    tm = min(ROW_TILE, m)
    tn = COL_TILE
    n_i, n_j = m // tm, n // tn
    riders, specs, finish, cast_vmem = _attach_casts(cast_jobs, n_i, n_j)
    mod_in, mod_specs, mod_out_spec, mod_out_shape, n_chunks = [], [], [], [], 0
    if mod_job is not None:
        s8, ada_w, ada_b, ml = mod_job
        d_mod, n_mod_cols = ada_w.shape[1], ada_w.shape[2]
        n_chunks = n_mod_cols // LANE
        if n_chunks <= n_i * n_j:
            chunk = lambda i, j: jnp.minimum(i * n_j + j, n_chunks - 1)
            mod_in = [s8, ada_w, ada_b.reshape(ada_w.shape[0], 1, n_mod_cols)]
            mod_specs = [pl.BlockSpec((8, d_mod), lambda i, j: (0, 0)),
                         pl.BlockSpec((None, d_mod, LANE), lambda i, j: (ml, 0, chunk(i, j))),
                         pl.BlockSpec((None, 1, LANE), lambda i, j: (ml, 0, chunk(i, j)))]
            mod_out_spec = [pl.BlockSpec((8, LANE), lambda i, j: (0, chunk(i, j)))]
            mod_out_shape = [jax.ShapeDtypeStruct((8, n_mod_cols), F32)]
            cast_vmem += 2 * d_mod * LANE * 6
    n_mod = len(mod_out_spec)
    side = bool(riders) or n_mod
    outs = pl.pallas_call(
        functools.partial(_mm_swiglu_kernel, n_mod=n_mod, n_cast=len(riders),
                          metas=[s[3] for s in specs] + [n_chunks], n_j=n_j),
        grid=(n_i, n_j),
        in_specs=[
            pl.BlockSpec((tm, k), lambda i, j: (i, 0)),
            pl.BlockSpec((None, k, tn), lambda i, j: (layer, 0, j)),
            pl.BlockSpec((None, k, tn), lambda i, j: (layer, 0, j)),
        ] + mod_specs + [s[0] for s in specs],
        out_specs=[pl.BlockSpec((tm, tn), lambda i, j: (i, j))] + mod_out_spec + [s[1] for s in specs],
        out_shape=[jax.ShapeDtypeStruct((m, n), BF16)] + mod_out_shape + [s[2] for s in specs],
        compiler_params=_params(("arbitrary", "arbitrary") if side else ("parallel", "parallel"),
                                2 * (tm * k * 2 + 2 * k * tn * 2 + tm * tn * 2) + cast_vmem),
        name="matmul_swiglu",
    )(a, wg, wu, *mod_in, *[job.src for job, _ in riders])
    mod = outs[1] if n_mod else None
    if mod_job is not None and not n_mod:
        mod = _mod_vectors(mod_job[0], mod_job[1], mod_job[2], first_layer=mod_job[3], n_layers=1)[0]
    return outs[0], mod, finish(outs[1 + n_mod:])


def _mm_res_kernel(*refs, nk, n_a):
    a_refs, (w_ref, res_ref, gate_ref, o_ref), scratch = refs[:n_a], refs[n_a:n_a + 4], refs[n_a + 4:]
    k0 = 0
    p = None
    for a_ref in a_refs:
        ka = a_ref.shape[1]
        d = _dot(a_ref[...], w_ref[k0:k0 + ka, :])
        p = d if p is None else p + d
        k0 += ka
    if nk == 1:
        o_ref[...] = res_ref[...] + gate_ref[...] * p
        return
    acc_ref, = scratch
    kk = pl.program_id(2)

    @pl.when(kk == 0)
    def _():
        acc_ref[...] = p

    @pl.when(jnp.logical_and(kk > 0, kk < nk - 1))
    def _():
        acc_ref[...] += p

    @pl.when(kk == nk - 1)
    def _():
        o_ref[...] = res_ref[...] + gate_ref[...] * (acc_ref[...] + p)


def _matmul_residual(a_slabs, w, res, mod5, layer, k_gate, mod_row, nk=1, tn=COL_TILE):
    m = a_slabs[0].shape[0]
    k, n = w.shape[1], w.shape[2]
    tm = min(ROW_TILE, m)
    tk = k // nk
    assert nk == 1 or len(a_slabs) == 1
    a_specs = [pl.BlockSpec((tm, tk if nk > 1 else a.shape[1]), lambda i, j, q: (i, q)) for a in a_slabs]
    return pl.pallas_call(
        functools.partial(_mm_res_kernel, nk=nk, n_a=len(a_slabs)),
        grid=(m // tm, n // tn, nk),
        in_specs=a_specs + [
            pl.BlockSpec((None, tk, tn), lambda i, j, q: (layer, q, j)),
            pl.BlockSpec((tm, tn), lambda i, j, q: (i, j)),
            pl.BlockSpec((None, None, None, 1, tn), lambda i, j, q: (layer, mod_row(i, tm), k_gate, 0, j)),
        ],
        out_specs=pl.BlockSpec((tm, tn), lambda i, j, q: (i, j)),
        out_shape=jax.ShapeDtypeStruct((m, n), F32),
        scratch_shapes=[pltpu.VMEM((tm, tn), F32)] if nk > 1 else [],
        compiler_params=_params(
            ("parallel", "parallel", "arbitrary"), 2 * (tm * tk * 2 + tk * tn * 2 + 2 * tm * tn * 4) + tm * tn * 4
        ),
        name="matmul_residual",
    )(*a_slabs, w, res, mod5)


def _rope_tables(n_tok):
    quarter = HEAD_DIM // 4
    inv = 1.0 / (ROPE_THETA ** (jnp.arange(quarter, dtype=F32) / quarter))
    pos = jnp.arange(n_tok, dtype=jnp.int32)
    row = (pos // GRID_W).astype(F32)[:, None] * inv
    col = (pos % GRID_W).astype(F32)[:, None] * inv
    cos = jnp.concatenate([jnp.cos(row), jnp.cos(row), jnp.cos(col), jnp.cos(col)], axis=-1)
    sin = jnp.concatenate([-jnp.sin(row), jnp.sin(row), -jnp.sin(col), jnp.sin(col)], axis=-1)
    return cos, sin


def _rotary(cos_ref, sin_ref, scale):
    cos = cos_ref[...] * scale
    sin = sin_ref[...] * scale
    lane = lax.broadcasted_iota(jnp.int32, cos.shape, 1)
    first = (lane & (HEAD_DIM // 4)) == 0

    def rotate(x):
        x = x.astype(F32)
        partner = jnp.where(first, pltpu.roll(x, HEAD_DIM - HEAD_DIM // 4, 1), pltpu.roll(x, HEAD_DIM // 4, 1))
        return x * cos + partner * sin

    return rotate


def _rope_kernel(x_ref, cos_ref, sin_ref, o_ref, *, scale, n_heads):
    rotate = _rotary(cos_ref, sin_ref, scale)
    for h in range(n_heads):
        hs = slice(h * HEAD_DIM, (h + 1) * HEAD_DIM)
        o_ref[:, hs] = rotate(x_ref[:, hs]).astype(o_ref.dtype)


def _rope(p, col0, width, cos, sin, scale, tr=256):
    m = p.shape[0]
    n_tok = cos.shape[0]
    tpb = n_tok // tr
    assert col0 % width == 0
    return pl.pallas_call(
        functools.partial(_rope_kernel, scale=scale, n_heads=width // HEAD_DIM),
        grid=(m // tr,),
        in_specs=[
            pl.BlockSpec((tr, width), lambda i: (i, col0 // width)),
            pl.BlockSpec((tr, HEAD_DIM), lambda i: (i % tpb, 0)),
            pl.BlockSpec((tr, HEAD_DIM), lambda i: (i % tpb, 0)),
        ],
        out_specs=pl.BlockSpec((tr, width), lambda i: (i, 0)),
        out_shape=jax.ShapeDtypeStruct((m, width), BF16),
        compiler_params=_params(("parallel",), 2 * tr * (width * 6 + HEAD_DIM * 8)),
        name="rope",
    )(p, cos, sin)


def _gla_kernel(*refs, nh, scale, fwd):
    ql, kl, vl, dl, qc, kc, vc, dc, up_ref, bias_ref = refs[:10]
    if fwd:
        ol, oc, st_ref = refs[10:]
    else:
        fl, fc, rl, rc, gn_ref, ol, oc, st_ref = refs[10:]
    s = pl.program_id(1)
    is_ctx = s == 0
    n_chunk = CTX_BLOCK // GLA_CHUNK

    @pl.when(s == 0)
    def _():
        st_ref[...] = jnp.zeros_like(st_ref)

    tb = CTX_BLOCK

    def in_scan_order(x):
        if fwd:
            return x
        return jnp.concatenate([x[(n_chunk - 1 - p) * GLA_CHUNK:(n_chunk - p) * GLA_CHUNK] for p in range(n_chunk)],
                               axis=0)

    def pick(rc_, rl_, cols):
        return jnp.where(is_ctx, rc_[:, cols], rl_[:, cols])

    def load(rc_, rl_, cols):
        return in_scan_order(pick(rc_, rl_, cols))

    row = lax.broadcasted_iota(jnp.int32, (tb, tb), 0)
    col = lax.broadcasted_iota(jnp.int32, (tb, tb), 1)
    incl = jnp.logical_and(row // GLA_CHUNK == col // GLA_CHUNK, (row >= col) if fwd else (col >= row))
    row_chunk = lax.broadcasted_iota(jnp.int32, (tb, GLA_DK), 0) // GLA_CHUNK

    def per_chunk(x):
        return jnp.concatenate([jnp.where(row_chunk == p, x, 0.0) for p in range(n_chunk)], axis=1)

    z = _dot(load(dc, dl, slice(None)).astype(BF16), up_ref[...]) + bias_ref[...]
    g = (jnp.minimum(z, 0.0) - jnp.log(1.0 + jnp.exp(-jnp.abs(z)))) * (1.0 / GLA_NORMALIZER)
    tmat = jnp.where(incl, 1.0, 0.0).astype(BF16)
    g_hi = g.astype(BF16)
    g_rest = g - g_hi.astype(F32)
    g_mid = g_rest.astype(BF16)
    g_lo = (g_rest - g_mid.astype(F32)).astype(BF16)
    cum = _dot(tmat, g_hi) + _dot(tmat, g_mid) + _dot(tmat, g_lo)
    tot = [jnp.sum(g[p * GLA_CHUNK:(p + 1) * GLA_CHUNK], axis=0, keepdims=True) for p in range(n_chunk)]
    tot_rows = jnp.concatenate([jnp.broadcast_to(t, (GLA_CHUNK, t.shape[1])) for t in tot], axis=0)

    staged = []
    for h in range(nh):
        ks = slice(h * GLA_DK, (h + 1) * GLA_DK)
        vs = slice(h * GLA_DV, (h + 1) * GLA_DV)
        bh = cum[:, ks]
        q = load(qc, ql, ks)
        k = load(kc, kl, ks)
        v = load(vc, vl, vs).astype(BF16)
        qd = q * (scale * jnp.exp(bh))
        ki = (k * jnp.exp(-bh)).astype(BF16)
        ku = k * jnp.exp(tot_rows[:, ks] - bh)
        a = jnp.where(incl, _dot_nt(qd.astype(BF16), ki), 0.0)
        inc = _dot_tn(v, per_chunk(ku).astype(BF16))
        staged.append((a.astype(BF16), inc, per_chunk(qd).astype(BF16), v))

    for h in range(nh):
        ks = slice(h * GLA_DK, (h + 1) * GLA_DK)
        vs = slice(h * GLA_DV, (h + 1) * GLA_DV)
        a, inc, qd_chunks, v = staged[h]
        st = st_ref[h]
        entering = []
        for p in range(n_chunk):
            entering.append(st)
            st = st * jnp.exp(tot[p][:, ks]) + inc[:, p * GLA_DK:(p + 1) * GLA_DK]
        st_ref[h] = st
        o = _dot(a, v) + _dot_nt(qd_chunks, jnp.concatenate(entering, axis=1).astype(BF16))
        o = in_scan_order(o)
        if not fwd:
            o = o + pick(fc, fl, vs)
            o = o * lax.rsqrt(jnp.mean(o * o, axis=-1, keepdims=True) + NORM_EPS) * gn_ref[...]
            o = o * _silu(pick(rc, rl, vs))
        o = o.astype(ol.dtype)

        @pl.when(is_ctx)
        def _():
            oc[:, vs] = o

        @pl.when(jnp.logical_not(is_ctx))
        def _():
            ol[:, vs] = o


def _gla(p_lat, p_ctx, dn_lat, dn_ctx, up2, bias2, norm_g, nb, n_tok, nh):
    qk = nh * GLA_DK
    dv = nh * GLA_DV
    tb = CTX_BLOCK
    nbl = n_tok // tb
    m_lat, m_ctx = nb * n_tok, nb * tb

    def scan(fwd, extra_in, extra_specs, out_dtype):
        def lat_blk(b, s):
            t = jnp.maximum(s - 1, 0)
            return b * nbl + (t if fwd else nbl - 1 - t)

        lat = lambda w, c: pl.BlockSpec((tb, w), lambda b, s: (lat_blk(b, s), c))
        ctx = lambda w, c: pl.BlockSpec((tb, w), lambda b, s: (b, c))
        d = 0 if fwd else 1
        vm = (2 * (2 * (2 * tb * qk + tb * dv + tb * LANE) * 4 + 2 * tb * dv * 4 + len(extra_in) * tb * dv * 4)
              + nh * GLA_DV * GLA_DK * 4)
        return pl.pallas_call(
            functools.partial(_gla_kernel, nh=nh, scale=GLA_DK ** -0.5, fwd=fwd),
            grid=(nb, nbl + 1),
            in_specs=[
                lat(qk, 0), lat(qk, 1), lat(dv, 1), lat(LANE, 0),
                ctx(qk, 0), ctx(qk, 1), ctx(dv, 1), ctx(LANE, 0),
                pl.BlockSpec((None, LANE, qk), lambda b, s: (d, 0, 0)),
                pl.BlockSpec((None, 1, qk), lambda b, s: (d, 0, 0)),
            ] + extra_specs(lat, ctx),
            out_specs=[lat(dv, 0), ctx(dv, 0)],
            out_shape=[jax.ShapeDtypeStruct((m_lat, dv), out_dtype), jax.ShapeDtypeStruct((m_ctx, dv), out_dtype)],
            scratch_shapes=[pltpu.VMEM((nh, GLA_DV, GLA_DK), F32)],
            compiler_params=_params(("parallel", "arbitrary"), vm),
            name="gla_scan_fwd" if fwd else "gla_scan_bwd",
        )(p_lat, p_lat, p_lat, dn_lat, p_ctx, p_ctx, p_ctx, dn_ctx, up2, bias2, *extra_in)

    of_lat, of_ctx = scan(True, (), lambda lat, ctx: [], F32)
    return scan(
        False,
        (of_lat, of_ctx, p_lat, p_ctx, norm_g.reshape(1, GLA_DV)),
        lambda lat, ctx: [lat(dv, 0), ctx(dv, 0), lat(dv, 2), ctx(dv, 2),
                          pl.BlockSpec((1, GLA_DV), lambda b, s: (0, 0))],
        BF16,
    )


def _swa_kernel(sink_ref, q_ref, cos_ref, sin_ref, kp, kc, kn, vp, vc, vn, kx, vx, o_ref, *, n_blk, kvs):
    i = pl.program_id(2)
    log2e = math.log2(math.e)
    rotate = _rotary(cos_ref, sin_ref, HEAD_DIM ** -0.5 * log2e)
    span = SWA_BLOCK + 2 * SWA_WINDOW
    gq = SWA_GROUP * SWA_BLOCK
    kj = lax.broadcasted_iota(jnp.int32, (span, gq), 0)
    lane = lax.broadcasted_iota(jnp.int32, (span, gq), 1)
    qi = lane & (SWA_BLOCK - 1)
    key_pos = (i - 1) * SWA_BLOCK + kj
    ok = jnp.logical_and(jnp.abs(qi + SWA_WINDOW - kj) <= SWA_WINDOW,
                         jnp.logical_and(key_pos >= 0, key_pos < n_blk * SWA_BLOCK))
    head = lax.broadcasted_iota(jnp.int32, (1, gq), 1) // SWA_BLOCK
    for kv in range(kvs):
        hkv = pl.program_id(1) * kvs + kv
        ks = slice(kv * HEAD_DIM, (kv + 1) * HEAD_DIM)
        kwin = jnp.concatenate([kp[:, ks], kc[:, ks], kn[:, ks]], axis=0)
        vwin = jnp.concatenate([vp[:, ks], vc[:, ks], vn[:, ks]], axis=0).astype(BF16)
        kctx = kx[:, ks].astype(BF16)
        vctx = vx[:, ks].astype(BF16)
        q0 = kv * SWA_GROUP * HEAD_DIM
        qs = jnp.concatenate([rotate(q_ref[:, q0 + g * HEAD_DIM:q0 + (g + 1) * HEAD_DIM]).astype(BF16)
                              for g in range(SWA_GROUP)], axis=0)
        s_win = jnp.where(ok, _dot_nt(kwin, qs), -jnp.inf)
        s_ctx = _dot_nt(kctx, qs)
        sink = jnp.zeros((1, gq), F32)
        for g in range(SWA_GROUP):
            sink = jnp.where(head == g, sink_ref[0, hkv * SWA_GROUP + g] * log2e, sink)
        m = jnp.maximum(jnp.maximum(jnp.max(s_win, axis=0, keepdims=True), jnp.max(s_ctx, axis=0, keepdims=True)),
                        sink)
        p_win = jnp.exp2(s_win - m)
        p_ctx = jnp.exp2(s_ctx - m)
        den = jnp.sum(p_win, axis=0, keepdims=True) + jnp.sum(p_ctx, axis=0, keepdims=True) + jnp.exp2(sink - m)
        ot = (_dot_tn(vctx, p_ctx.astype(BF16)) + _dot_tn(vwin, p_win.astype(BF16))) / den
        for g in range(SWA_GROUP):
            hs = slice(q0 + g * HEAD_DIM, q0 + (g + 1) * HEAD_DIM)
            o_ref[:, hs] = ot[:, g * SWA_BLOCK:(g + 1) * SWA_BLOCK].T.astype(o_ref.dtype)


def _swa(sink, cos, sin, k_rope, p_lat, p_ctx, qcol, kcol, vcol, nb, n_tok, n_kv):
    n_blk = n_tok // SWA_BLOCK
    tb = SWA_BLOCK
    kvs = next(c for c in (4, 2, 1) if n_kv % c == 0 and kcol % (c * HEAD_DIM) == 0 and vcol % (c * HEAD_DIM) == 0
               and qcol % (c * SWA_GROUP * HEAD_DIM) == 0)
    kw = kvs * HEAD_DIM
    gw = kvs * SWA_GROUP * HEAD_DIM
    assert kcol % kw == 0 and vcol % kw == 0 and qcol % gw == 0
    kc0 = kcol // kw
    vc0 = vcol // kw
    qc0 = qcol // gw
    prev = lambda b, h, i: b * n_blk + jnp.maximum(i - 1, 0)
    cur = lambda b, h, i: b * n_blk + i
    nxt = lambda b, h, i: b * n_blk + jnp.minimum(i + 1, n_blk - 1)
    kspec = lambda f: pl.BlockSpec((tb, kw), lambda b, h, i: (f(b, h, i), h))
    vspec = lambda f: pl.BlockSpec((tb, kw), lambda b, h, i: (f(b, h, i), vc0 + h))
    return pl.pallas_call(
        functools.partial(_swa_kernel, n_blk=n_blk, kvs=kvs),
        grid=(nb, n_kv // kvs, n_blk),
        in_specs=[
            pl.BlockSpec(memory_space=pltpu.SMEM),
            pl.BlockSpec((tb, gw), lambda b, h, i: (cur(b, h, i), qc0 + h)),
            pl.BlockSpec((tb, HEAD_DIM), lambda b, h, i: (i, 0)),
            pl.BlockSpec((tb, HEAD_DIM), lambda b, h, i: (i, 0)),
            kspec(prev), kspec(cur), kspec(nxt),
            vspec(prev), vspec(cur), vspec(nxt),
            pl.BlockSpec((CTX_BLOCK, kw), lambda b, h, i: (b, kc0 + h)),
            pl.BlockSpec((CTX_BLOCK, kw), lambda b, h, i: (b, vc0 + h)),
        ],
        out_specs=pl.BlockSpec((tb, gw), lambda b, h, i: (cur(b, h, i), h)),
        out_shape=jax.ShapeDtypeStruct((nb * n_tok, n_kv * SWA_GROUP * HEAD_DIM), BF16),
        compiler_params=_params(("parallel", "parallel", "parallel"), 16 << 20),
        name="swa",
    )(sink, p_lat, cos, sin, k_rope, k_rope, k_rope, p_lat, p_lat, p_lat, p_ctx, p_ctx)


def _ctx_sink_kernel(sink_ref, q_ref, kx, vx, o_ref, *, scale):
    hkv = pl.program_id(1)
    kctx = kx[...].astype(BF16)
    vctx = vx[...].astype(BF16)
    for g in range(SWA_GROUP):
        hs = slice(g * HEAD_DIM, (g + 1) * HEAD_DIM)
        q = (q_ref[:, hs].astype(F32) * scale).astype(BF16)
        s = _dot_nt(q, kctx)
        sink = sink_ref[0, hkv * SWA_GROUP + g]
        m = jnp.maximum(jnp.max(s, axis=-1, keepdims=True), sink)
        p = jnp.exp(s - m)
        den = jnp.sum(p, axis=-1, keepdims=True) + jnp.exp(sink - m)
        o_ref[:, hs] = (_dot(p.astype(BF16), vctx) / den).astype(o_ref.dtype)


def _ctx_sink(sink, p_ctx, qcol, kcol, vcol, nb, n_kv):
    gw = SWA_GROUP * HEAD_DIM
    return pl.pallas_call(
        functools.partial(_ctx_sink_kernel, scale=HEAD_DIM ** -0.5),
        grid=(nb, n_kv),
        in_specs=[
            pl.BlockSpec(memory_space=pltpu.SMEM),
            pl.BlockSpec((CTX_BLOCK, gw), lambda b, h: (b, qcol // gw + h)),
            pl.BlockSpec((CTX_BLOCK, HEAD_DIM), lambda b, h: (b, kcol // HEAD_DIM + h)),
            pl.BlockSpec((CTX_BLOCK, HEAD_DIM), lambda b, h: (b, vcol // HEAD_DIM + h)),
        ],
        out_specs=pl.BlockSpec((CTX_BLOCK, gw), lambda b, h: (b, h)),
        out_shape=jax.ShapeDtypeStruct((nb * CTX_BLOCK, n_kv * gw), BF16),
        compiler_params=_params(("parallel", "parallel"), 4 << 20),
        name="ctx_sink_attention",
    )(sink, p_ctx, p_ctx, p_ctx)


def _na_bias_table(rpb, rows):
    qcol = np.arange(GRID_W)[:, None]
    kcol = np.arange(GRID_W)[None, :]
    col_start = np.clip(qcol - NA_COLS // 2, 0, GRID_W - NA_COLS)
    col_ok = (kcol >= col_start) & (kcol < col_start + NA_COLS)
    dc = np.clip(kcol - qcol + NA_COLS - 1, 0, 2 * NA_COLS - 2)
    n_dc = 2 * NA_COLS - 1
    onehot = (dc[None] == np.arange(n_dc)[:, None, None]).astype(np.float32)
    cols = jnp.einsum('hrd,dqk->hrqk', rpb.astype(F32), onehot, precision=lax.Precision.HIGHEST)
    cols = jnp.where(col_ok[None, None], cols, -jnp.inf).transpose(0, 1, 3, 2)
    masked = jnp.full(cols.shape[:1] + cols.shape[2:], -jnp.inf, F32)

    span = NA_ROWS + NA_QROWS - 1
    patterns = []
    for r0 in range(0, rows, NA_QROWS):
        rs = [min(max(r0 + i - NA_ROWS // 2, 0), rows - NA_ROWS) for i in range(NA_QROWS)]
        u = min(rs[0], rows - span)
        patterns.append(tuple((r0 + i - rs[i], rs[i] - u) for i in range(NA_QROWS)))
    variants = sorted(set(patterns), key=patterns.index)
    blocks = []
    for pat in variants:
        per_query_row = []
        for delta, off in pat:
            krows = [cols[:, kr - off - delta + NA_ROWS - 1] if 0 <= kr - off < NA_ROWS else masked
                     for kr in range(span)]
            per_query_row.append(jnp.stack(krows, axis=1))
        blocks.append(jnp.concatenate(per_query_row, axis=-1).reshape(rpb.shape[0], span * GRID_W, NA_QROWS * GRID_W))
    ids = np.array([variants.index(p) for p in patterns], dtype=np.int32)
    return jnp.stack(blocks, axis=1), ids


def _na_kernel(ids_ref, q_ref, k_ref, v_ref, kx, vx, bias_ref, o_ref, *, nh, scale):
    for h in range(nh):
        hs = slice(h * HEAD_DIM, (h + 1) * HEAD_DIM)
        q = (q_ref[:, hs].astype(F32) * scale).astype(BF16)
        s_nb = _dot_nt(k_ref[:, hs].astype(BF16), q) + bias_ref[h]
        s_ctx = _dot_nt(kx[:, hs].astype(BF16), q)
        m = jnp.maximum(jnp.max(s_nb, axis=0, keepdims=True), jnp.max(s_ctx, axis=0, keepdims=True))
        p_nb = jnp.exp(s_nb - m)
        p_ctx = jnp.exp(s_ctx - m)
        den = jnp.sum(p_nb, axis=0, keepdims=True) + jnp.sum(p_ctx, axis=0, keepdims=True)
        ot = (_dot_tn(vx[:, hs].astype(BF16), p_ctx.astype(BF16))
              + _dot_tn(v_ref[:, hs].astype(BF16), p_nb.astype(BF16)))
        o_ref[:, hs] = (ot / den).T.astype(o_ref.dtype)


def _na(p_lat, p_ctx, rpb, nb, n_tok, nh):
    rows = n_tok // GRID_W
    w = nh * HEAD_DIM
    span = NA_ROWS + NA_QROWS - 1
    win = span * GRID_W
    tq = NA_QROWS * GRID_W
    n_step = rows // NA_QROWS
    bias_tab, ids = _na_bias_table(rpb, rows)
    u = lambda g: jnp.clip(g * NA_QROWS - NA_ROWS // 2, 0, rows - span)
    kv = lambda c: pl.BlockSpec((pl.Element(win), pl.Element(w)),
                                lambda b, g, ids: ((b * rows + u(g)) * GRID_W, c * w))
    vm = 2 * (tq * w * 4 + 2 * win * w * 4 + 2 * CTX_BLOCK * w * 4 + nh * tq * win * 4 + tq * w * 2)
    return pl.pallas_call(
        functools.partial(_na_kernel, nh=nh, scale=HEAD_DIM ** -0.5),
        grid_spec=pltpu.PrefetchScalarGridSpec(
            num_scalar_prefetch=1,
            grid=(nb, n_step),
            in_specs=[
                pl.BlockSpec((tq, w), lambda b, g, ids: (b * n_step + g, 0)),
                kv(1), kv(2),
                pl.BlockSpec((CTX_BLOCK, w), lambda b, g, ids: (b, 1)),
                pl.BlockSpec((CTX_BLOCK, w), lambda b, g, ids: (b, 2)),
                pl.BlockSpec((nh, None, win, tq), lambda b, g, ids: (0, ids[g], 0, 0)),
            ],
            out_specs=pl.BlockSpec((tq, w), lambda b, g, ids: (b * n_step + g, 0)),
        ),
        out_shape=jax.ShapeDtypeStruct((nb * n_tok, w), BF16),
        compiler_params=_params(("parallel", "arbitrary"), vm),
        name="neighborhood_attention",
    )(jnp.asarray(ids), p_lat, p_lat, p_lat, p_ctx, p_ctx, bias_tab)


def _diff_kernel(qraw_ref, cos_ref, sin_ref, k_ref, v_ref, kx, vx, lq1, lk1, lq2, lk2, g_ref, o_ref,
                 m_ref, l_ref, acc_ref, sa_ref, sb_ref, pa_ref, pb_ref, pc_ref, aa_ref, ab_ref, ac_ref,
                 vt_ref, vxt_ref, q_ref, *, n_blk, tk, lam_init):

    rotate = _rotary(cos_ref, sin_ref, HEAD_DIM ** -0.5 * math.log2(math.e))
    for t in range(2):
        hs = slice(t * HEAD_DIM, (t + 1) * HEAD_DIM)
        q_ref[:, hs] = rotate(qraw_ref[:, hs]).astype(q_ref.dtype)

    @pl.when(pl.program_id(2) == 0)
    def _():
        for j in range(n_blk):
            vt_ref[j] = v_ref[j * tk:(j + 1) * tk, :].T
        vxt_ref[...] = vx[...].astype(BF16).T

    vxt = vxt_ref
    m_ref[...] = jnp.full_like(m_ref, -jnp.inf)
    l_ref[...] = jnp.zeros_like(l_ref)
    acc_ref[...] = jnp.zeros_like(acc_ref)

    def scores(keys):
        return [_dot_nt(keys[:, t * HEAD_DIM:(t + 1) * HEAD_DIM], q_ref[:, t * HEAD_DIM:(t + 1) * HEAD_DIM])
                for t in range(2)]

    def lat_scores(j, dst):
        s = scores(k_ref[pl.ds(pl.multiple_of(j * tk, tk), tk), :])
        dst[0] = s[0]
        dst[1] = s[1]

    def softmax(s, p_dst, a_dst):
        for t in range(2):
            m_old = m_ref[t]
            m_new = jnp.maximum(m_old, jnp.max(s[t], axis=0, keepdims=True))
            alpha = jnp.exp2(m_old - m_new)
            p = jnp.exp2(s[t] - m_new)
            l_ref[t] = alpha * l_ref[t] + jnp.sum(p, axis=0, keepdims=True)
            m_ref[t] = m_new
            p_dst[t] = p.astype(BF16)
            a_dst[t] = alpha

    def add_values(p_src, a_src, vt):
        for t in range(2):
            acc_ref[t] = a_src[t] * acc_ref[t] + _dot(vt, p_src[t])

    def stage(s_cur, p_cur, a_cur, next_block, s_next, prev):
        if next_block is not None:
            lat_scores(next_block, s_next)
        add_values(*prev)
        softmax((s_cur[0], s_cur[1]), p_cur, a_cur)

    s_ctx = scores(kx[...].astype(BF16))
    lat_scores(0, sa_ref)
    softmax(s_ctx, pc_ref, ac_ref)
    stage(sa_ref, pa_ref, aa_ref, 1, sb_ref, (pc_ref, ac_ref, vxt[...]))

    def body(jj, carry):
        j = 2 * jj + 1
        stage(sb_ref, pb_ref, ab_ref, j + 1, sa_ref, (pa_ref, aa_ref, vt_ref[j - 1]))
        stage(sa_ref, pa_ref, aa_ref, j + 2, sb_ref, (pb_ref, ab_ref, vt_ref[j]))
        return carry

    lax.fori_loop(0, (n_blk - 2) // 2, body, 0)
    stage(sb_ref, pb_ref, ab_ref, None, None, (pa_ref, aa_ref, vt_ref[n_blk - 2]))
    add_values(pb_ref, ab_ref, vt_ref[n_blk - 1])

    lam = (jnp.exp(jnp.sum(lq1[...] * lk1[...], axis=-1, keepdims=True))
           - jnp.exp(jnp.sum(lq2[...] * lk2[...], axis=-1, keepdims=True)) + lam_init)
    o = acc_ref[0] / l_ref[0] - lam * (acc_ref[1] / l_ref[1])
    y = o * lax.rsqrt(jnp.mean(o * o, axis=0, keepdims=True) + NORM_EPS) * (g_ref[...] * (1.0 - lam_init))
    o_ref[...] = y.T.astype(o_ref.dtype)


def _diff(cos, sin, k_rope, p_lat, p_ctx, qcol, kcol, vcol, lq1, lk1, lq2, lk2, norm_g, lam_init, nb, n_tok, nh,
          tq=512, tk=DIFF_KEY_BLOCK):
    nq = n_tok // tq
    n_blk = n_tok // tk
    assert n_blk >= 2 and n_blk % 2 == 0, "the key-block pipeline alternates two buffers"
    w = 2 * HEAD_DIM
    vec = lambda: pl.BlockSpec((1, HEAD_DIM), lambda b, h, i: (0, 0))
    vm = (2 * (tq * w * 2 + 2 * n_tok * w * 2 + 2 * CTX_BLOCK * w * 2 + tq * w * 2)
          + n_tok * w * 2 + 2 * tq * w * 4 + 2 * 2 * tk * tq * 6 + tq * w * 8)
    return pl.pallas_call(
        functools.partial(_diff_kernel, n_blk=n_blk, tk=tk, lam_init=lam_init),
        grid=(nb, nh, nq),
        in_specs=[
            pl.BlockSpec((tq, w), lambda b, h, i: (b * nq + i, qcol // w + h)),
            pl.BlockSpec((tq, HEAD_DIM), lambda b, h, i: (i, 0)),
            pl.BlockSpec((tq, HEAD_DIM), lambda b, h, i: (i, 0)),
            pl.BlockSpec((n_tok, w), lambda b, h, i: (b, h)),
            pl.BlockSpec((n_tok, w), lambda b, h, i: (b, vcol // w + h)),
            pl.BlockSpec((CTX_BLOCK, w), lambda b, h, i: (b, kcol // w + h)),
            pl.BlockSpec((CTX_BLOCK, w), lambda b, h, i: (b, vcol // w + h)),
            vec(), vec(), vec(), vec(),
            pl.BlockSpec((DIFF_DV, 1), lambda b, h, i: (0, 0)),
        ],
        out_specs=pl.BlockSpec((tq, w), lambda b, h, i: (b * nq + i, h)),
        out_shape=jax.ShapeDtypeStruct((nb * n_tok, nh * w), BF16),
        scratch_shapes=[
            pltpu.VMEM((2, 1, tq), F32),
            pltpu.VMEM((2, 1, tq), F32),
            pltpu.VMEM((2, DIFF_DV, tq), F32),
            pltpu.VMEM((2, tk, tq), F32),
            pltpu.VMEM((2, tk, tq), F32),
            pltpu.VMEM((2, tk, tq), BF16),
            pltpu.VMEM((2, tk, tq), BF16),
            pltpu.VMEM((2, CTX_BLOCK, tq), BF16),
            pltpu.VMEM((2, 1, tq), F32),
            pltpu.VMEM((2, 1, tq), F32),
            pltpu.VMEM((2, 1, tq), F32),
            pltpu.VMEM((n_blk, w, tk), BF16),
            pltpu.VMEM((w, CTX_BLOCK), BF16),
            pltpu.VMEM((tq, w), BF16),
        ],
        compiler_params=_params(("parallel", "parallel", "arbitrary"), vm),
        name="diff_attention",
    )(p_lat, cos, sin, k_rope, p_lat, p_ctx, p_ctx, lq1.reshape(1, -1), lk1.reshape(1, -1), lq2.reshape(1, -1),
      lk2.reshape(1, -1), norm_g.reshape(-1, 1))


def _split_even_kernel(w_ref, main_ref, dn_ref, *, g0):
    r = 2 * GLA_RANK
    main_ref[:, :g0] = w_ref[:, :g0].astype(main_ref.dtype)
    main_ref[:, g0:] = w_ref[:, g0 + r:].astype(main_ref.dtype)
    dn_ref[:, :r] = w_ref[:, g0:g0 + r].astype(dn_ref.dtype)
    dn_ref[:, r:] = jnp.zeros((dn_ref.shape[0], LANE - r), dn_ref.dtype)


def _split_even_weight(w_in, g0, tr=256):
    k, n = w_in.shape
    n_main = n - 2 * GLA_RANK
    return pl.pallas_call(
        functools.partial(_split_even_kernel, g0=g0),
        grid=(k // tr,),
        in_specs=[pl.BlockSpec((tr, n), lambda i: (i, 0))],
        out_specs=[pl.BlockSpec((tr, n_main), lambda i: (i, 0)), pl.BlockSpec((tr, LANE), lambda i: (i, 0))],
        out_shape=[jax.ShapeDtypeStruct((k, n_main), BF16), jax.ShapeDtypeStruct((k, LANE), BF16)],
        compiler_params=_params(("parallel",), 2 * tr * (n * 4 + n_main * 2 + LANE * 2)),
        name="split_even_weight",
    )(w_in)


def kernel(x, c, ctx, c_ctx, ada_w, ada_b, norm_mix_g, norm_ffn_g, w_out, ffn_w_gate, ffn_w_up, ffn_w_down,
           ev_w_in, gla_gate_up_f, gla_gate_bias_f, gla_gate_up_b, gla_gate_bias_b, gla_norm_g, swa_sink,
           od_w_in, na_rpb, diff_lq1, diff_lk1, diff_lq2, diff_lk2, diff_norm_g, final_norm_g):
    nb, n_tok, d = x.shape
    lc = ctx.shape[1]
    depth = ada_w.shape[0]
    f_hidden = ffn_w_gate.shape[2]
    half = d // 2
    assert lc == CTX_BLOCK and nb < 8 and depth == 2, "layout assumes a 256-token context, batch < 8, two layers"
    gla_heads = half // GLA_DV
    gla_qk = gla_heads * GLA_DK
    n_q = half // HEAD_DIM
    n_kv = n_q // SWA_GROUP
    diff_heads = half // (2 * HEAD_DIM)

    xs = x.reshape(nb * n_tok, d)
    cs = ctx.reshape(nb * lc, d)
    cos, sin = _rope_tables(n_tok)

    s8 = jnp.zeros((8, d), F32).at[:nb].set(c).at[nb].set(c_ctx)
    mod5 = _mod_vectors(s8, ada_w, ada_b, 0, 1).reshape(1, 8, 6, 1, d)
    lat_row = lambda i, tm: (i * tm) // n_tok
    ctx_row = lambda i, tm: nb

    f_pad = -(-f_hidden // FFN_PAD) * FFN_PAD

    def ffn_up(stream, layer, mod_row, wg, wu, cast_jobs=(), mod_job=None):
        h2 = _norm_mod(stream, norm_ffn_g[layer], mod5, layer, 3, 4, mod_row)
        return _matmul_swiglu(h2, wg, wu, layer, cast_jobs, mod_job)

    def ffn_down(a, stream, layer, mod_row, wd):
        return _matmul_residual([a], wd, stream, mod5, layer, 5, mod_row, nk=4, tn=2 * COL_TILE)

    layer = 0
    w_in = ev_w_in[0]
    g0 = 2 * gla_qk + 2 * half
    w_main, w_dn = _split_even_weight(w_in, g0)
    swa_q0 = g0
    swa_k0 = swa_q0 + half
    swa_v0 = swa_k0 + n_kv * HEAD_DIM
    up2 = jnp.zeros((2, LANE, gla_qk), F32)
    up2 = up2.at[0, :GLA_RANK].set(gla_gate_up_f[0]).at[1, GLA_RANK:2 * GLA_RANK].set(gla_gate_up_b[0]).astype(BF16)
    bias2 = jnp.stack([gla_gate_bias_f[0], gla_gate_bias_b[0]]).reshape(2, 1, gla_qk)

    h = _norm_mod(xs, norm_mix_g[layer], mod5, layer, 0, 1, lat_row)
    hc = _norm_mod(cs, norm_mix_g[layer], mod5, layer, 0, 1, ctx_row)
    p_lat, dn_lat, (wg, wu, wo) = _matmul(h, w_main, w_narrow=w_dn, cast_jobs=[
        _CastJob(ffn_w_gate, d, f_pad), _CastJob(ffn_w_up, d, f_pad), _CastJob(w_out, d, d)])
    p_ctx, dn_ctx, _ = _matmul(hc, w_main, w_narrow=w_dn)

    ya, yac = _gla(p_lat, p_ctx, dn_lat, dn_ctx, up2, bias2, gla_norm_g[0], nb, n_tok, gla_heads)

    sink = swa_sink[0].reshape(1, n_q)
    k_rope = _rope(p_lat, swa_k0, n_kv * HEAD_DIM, cos, sin, 1.0)
    yb = _swa(sink, cos, sin, k_rope, p_lat, p_ctx, swa_q0, swa_k0, swa_v0, nb, n_tok, n_kv)
    ybc = _ctx_sink(sink, p_ctx, swa_q0, swa_k0, swa_v0, nb, n_kv)

    xs = _matmul_residual([ya, yb], wo, xs, mod5, layer, 2, lat_row, tn=2 * COL_TILE)
    cs = _matmul_residual([yac, ybc], wo, cs, mod5, layer, 2, ctx_row, tn=2 * COL_TILE)
    a, mod_next, (wd, w_in1) = ffn_up(xs, layer, lat_row, wg, wu, cast_jobs=[
        _CastJob(ffn_w_down, f_pad, d), _CastJob(od_w_in, d, od_w_in.shape[2])], mod_job=(s8, ada_w, ada_b, 1))
    mod5 = jnp.concatenate([mod5, mod_next.reshape(1, 8, 6, 1, d)], axis=0)
    xs = ffn_down(a, xs, layer, lat_row, wd)
    cs = ffn_down(ffn_up(cs, layer, ctx_row, wg, wu)[0], cs, layer, ctx_row, wd)

    layer = 1
    w_in = w_in1[0]
    h = _norm_mod(xs, norm_mix_g[layer], mod5, layer, 0, 1, lat_row)
    hc = _norm_mod(cs, norm_mix_g[layer], mod5, layer, 0, 1, ctx_row)
    p_lat = _matmul(h, w_in, BF16, tm=2 * ROW_TILE)[0]
    p_ctx = _matmul(hc, w_in, BF16)[0]

    yn =_na(p_lat, p_ctx, na_rpb[0], nb, n_tok, n_q)

    dq0 = 3 * half
    dk0 = dq0 + half
    dv0 = dk0 + half
    lam_init = 0.8 - 0.6 * math.exp(-0.3 * layer)
    kd = _rope(p_lat, dk0, half, cos, sin, 1.0)
    yd = _diff(cos, sin, kd, p_lat, p_ctx, dq0, dk0, dv0, diff_lq1[0], diff_lk1[0], diff_lq2[0], diff_lk2[0],
               diff_norm_g[0], lam_init, nb, n_tok, diff_heads)

    xs = _matmul_residual([yn, yd], wo, xs, mod5, layer, 2, lat_row, tn=2 * COL_TILE)
    xs = ffn_down(ffn_up(xs, layer, lat_row, wg, wu)[0], xs, layer, lat_row, wd)
    return _final_norm(xs, final_norm_g).reshape(nb, n_tok, d)
```

```python
import functools
import math
from typing import NamedTuple

import numpy as np
import jax
import jax.numpy as jnp
from jax import lax
from jax.experimental import pallas as pl
from jax.experimental.pallas import tpu as pltpu

HEAD_DIM = 128
GRID_W = 64
ROPE_THETA = 10000.0
NORM_EPS = 1e-6
GLA_DK = 128
GLA_DV = 256
GLA_RANK = 16
GLA_NORMALIZER = 16.0
GLA_CHUNK = 64
SWA_GROUP = 4
SWA_WINDOW = 128
SWA_BLOCK = 128
NA_ROWS = 8
NA_COLS = 16
NA_QROWS = 4
DIFF_DV = 2 * HEAD_DIM
DIFF_KEY_BLOCK = 2048
CTX_BLOCK = 256

V7X_VMEM_BUDGET = 60 * 1024 * 1024
LANE = 128
ROW_TILE = 1024
COL_TILE = 512
FFN_PAD = 1024

F32 = jnp.float32
BF16 = jnp.bfloat16


def _params(sem, vmem_bytes):
    limit = int(min(V7X_VMEM_BUDGET, vmem_bytes * 5 // 4 + (4 << 20)))
    return pltpu.CompilerParams(dimension_semantics=sem, vmem_limit_bytes=limit)


def _dot(a, b):
    return jnp.dot(a, b, preferred_element_type=F32)


def _dot_nt(a, b):
    return lax.dot_general(a, b, (((1,), (1,)), ((), ())), preferred_element_type=F32)


def _dot_tn(a, b):
    return lax.dot_general(a, b, (((0,), (0,)), ((), ())), preferred_element_type=F32)


def _silu(t):
    return t * jax.nn.sigmoid(t)


def _mod_kernel(s_ref, w_ref, b_ref, o_ref):
    a = _silu(s_ref[...]).astype(BF16)
    o_ref[...] = _dot(a, w_ref[...].astype(BF16)) + b_ref[...]


def _mod_vectors(s8, ada_w, ada_b, first_layer, n_layers):
    depth, d, n = ada_w.shape
    tn = COL_TILE
    return pl.pallas_call(
        _mod_kernel,
        grid=(n_layers, n // tn),
        in_specs=[
            pl.BlockSpec((8, d), lambda l, j: (0, 0)),
            pl.BlockSpec((None, d, tn), lambda l, j: (first_layer + l, 0, j)),
            pl.BlockSpec((None, 1, tn), lambda l, j: (first_layer + l, 0, j)),
        ],
        out_specs=pl.BlockSpec((None, 8, tn), lambda l, j: (l, 0, j)),
        out_shape=jax.ShapeDtypeStruct((n_layers, 8, n), F32),
        compiler_params=_params(("parallel", "parallel"), 2 * d * tn * 4 + d * tn * 2),
        name="adaln_mod",
    )(s8, ada_w, ada_b.reshape(depth, 1, n))


def _norm_mod_kernel(x_ref, g_ref, sh_ref, sc_ref, o_ref):
    x = x_ref[...]
    r = lax.rsqrt(jnp.mean(x * x, axis=-1, keepdims=True) + NORM_EPS)
    y = x * r * g_ref[...]
    o_ref[...] = (y * (1.0 + sc_ref[...]) + sh_ref[...]).astype(o_ref.dtype)


def _norm_mod(x2, g, mod5, layer, k_shift, k_scale, mod_row, tm=512):
    m, d = x2.shape
    return pl.pallas_call(
        _norm_mod_kernel,
        grid=(m // tm,),
        in_specs=[
            pl.BlockSpec((tm, d), lambda i: (i, 0)),
            pl.BlockSpec((1, d), lambda i: (0, 0)),
            pl.BlockSpec((None, None, None, 1, d), lambda i: (layer, mod_row(i, tm), k_shift, 0, 0)),
            pl.BlockSpec((None, None, None, 1, d), lambda i: (layer, mod_row(i, tm), k_scale, 0, 0)),
        ],
        out_specs=pl.BlockSpec((tm, d), lambda i: (i, 0)),
        out_shape=jax.ShapeDtypeStruct((m, d), BF16),
        compiler_params=_params(("parallel",), 2 * tm * d * 6),
        name="norm_mod",
    )(x2, g.reshape(1, d), mod5, mod5)


def _final_norm_kernel(x_ref, g_ref, o_ref):
    x = x_ref[...]
    r = lax.rsqrt(jnp.mean(x * x, axis=-1, keepdims=True) + NORM_EPS)
    o_ref[...] = x * r * g_ref[...]


def _final_norm(x2, g, tm=512):
    m, d = x2.shape
    return pl.pallas_call(
        _final_norm_kernel,
        grid=(m // tm,),
        in_specs=[pl.BlockSpec((tm, d), lambda i: (i, 0)), pl.BlockSpec((1, d), lambda i: (0, 0))],
        out_specs=pl.BlockSpec((tm, d), lambda i: (i, 0)),
        out_shape=jax.ShapeDtypeStruct((m, d), F32),
        compiler_params=_params(("parallel",), 2 * tm * d * 8),
        name="final_norm",
    )(x2, g.reshape(1, d))


def _col_tile(n):
    return next(t for t in range(COL_TILE, 0, -LANE) if n % t == 0)


class _CastJob(NamedTuple):
    src: jax.Array
    r_pad: int
    c_pad: int


def _cast_plan(job, n_steps):
    depth, r, _ = job.src.shape
    for rc in range(16, r + 1, 16):
        if r % rc == 0 and job.r_pad % rc == 0 and depth * (job.r_pad // rc) <= n_steps:
            return rc
    return None


def _cast_specs(job, rc, n_j):
    depth, r, c = job.src.shape
    cpl, valid = job.r_pad // rc, r // rc
    n_total = depth * cpl

    def pos(i, j):
        t = jnp.minimum(i * n_j + j, n_total - 1)
        return t // cpl, t % cpl

    in_spec = pl.BlockSpec((None, rc, c), lambda i, j: (pos(i, j)[0], jnp.minimum(pos(i, j)[1], valid - 1), 0))
    out_spec = pl.BlockSpec((None, rc, job.c_pad), lambda i, j: (pos(i, j)[0], pos(i, j)[1], 0))
    out_shape = jax.ShapeDtypeStruct((depth, job.r_pad, job.c_pad), BF16)
    return in_spec, out_spec, out_shape, (cpl, valid, n_total)


def _run_casts(metas, src_refs, dst_refs, n_j):
    t = pl.program_id(0) * n_j + pl.program_id(1)
    for (cpl, valid, n_total), s, d in zip(metas, src_refs, dst_refs):
        chunk = jnp.minimum(t, n_total - 1) % cpl
        n = s.shape[-1]

        @pl.when(jnp.logical_and(t < n_total, chunk < valid))
        def _():
            d[:, :n] = s[...].astype(d.dtype)
            if d.shape[-1] > n:
                d[:, n:] = jnp.zeros((d.shape[0], d.shape[-1] - n), d.dtype)

        if cpl > valid:
            @pl.when(jnp.logical_and(t < n_total, chunk >= valid))
            def _():
                d[...] = jnp.zeros_like(d)


def _cast_only_kernel(*refs, metas, n_j):
    _run_casts(metas, refs[:len(metas)], refs[len(metas):], n_j)


def _attach_casts(jobs, n_i, n_j):
    plans = [_cast_plan(job, n_i * n_j) for job in jobs]
    riders = [(job, rc) for job, rc in zip(jobs, plans) if rc is not None]
    specs = [_cast_specs(job, rc, n_j) for job, rc in riders]

    def finish(rider_outs):
        rider_outs = list(rider_outs)
        outs = []
        for job, rc in zip(jobs, plans):
            if rc is not None:
                outs.append(rider_outs.pop(0))
                continue
            rc = next(c for c in (256, 128, 64, 32, 16) if job.src.shape[1] % c == 0 and job.r_pad % c == 0)
            n_steps = job.src.shape[0] * (job.r_pad // rc)
            i_s, o_s, o_shape, meta = _cast_specs(job, rc, n_steps)
            outs.append(pl.pallas_call(
                functools.partial(_cast_only_kernel, metas=[meta], n_j=n_steps),
                grid=(1, n_steps), in_specs=[i_s], out_specs=o_s, out_shape=o_shape,
                compiler_params=_params(("arbitrary", "arbitrary"), 2 * rc * (job.src.shape[2] * 4 + job.c_pad * 2)),
                name="cast_weights",
            )(job.src))
        return outs

    vmem = sum(2 * rc * (job.src.shape[2] * 4 + job.c_pad * 2) for job, rc in riders)
    return riders, specs, finish, vmem


def _mm_kernel(*refs, n_narrow, n_cast, metas, n_j):
    n_in = 2 + n_narrow + n_cast
    a_ref, w_ref = refs[:2]
    o_ref = refs[n_in]
    a = a_ref[...]
    o_ref[...] = _dot(a, w_ref[...]).astype(o_ref.dtype)
    if n_narrow:
        @pl.when(pl.program_id(1) == 0)
        def _():
            refs[n_in + 1][...] = _dot(a, refs[2][...])

    _run_casts(metas, refs[2 + n_narrow:n_in], refs[n_in + 1 + n_narrow:], n_j)


def _matmul(a, w, out_dtype=F32, w_narrow=None, cast_jobs=(), tm=ROW_TILE):
    m, k = a.shape
    n = w.shape[1]
    tm = min(tm, m)
    tn = _col_tile(n)
    ob = jnp.dtype(out_dtype).itemsize
    n_i, n_j = m // tm, n // tn
    riders, specs, finish, cast_vmem = _attach_casts(cast_jobs, n_i, n_j)
    narrow = [] if w_narrow is None else [w_narrow]
    outs = pl.pallas_call(
        functools.partial(_mm_kernel, n_narrow=len(narrow), n_cast=len(riders), metas=[s[3] for s in specs], n_j=n_j),
        grid=(n_i, n_j),
        in_specs=[pl.BlockSpec((tm, k), lambda i, j: (i, 0)), pl.BlockSpec((k, tn), lambda i, j: (0, j))]
        + [pl.BlockSpec((k, LANE), lambda i, j: (0, 0)) for _ in narrow] + [s[0] for s in specs],
        out_specs=[pl.BlockSpec((tm, tn), lambda i, j: (i, j))]
        + [pl.BlockSpec((tm, LANE), lambda i, j: (i, 0)) for _ in narrow] + [s[1] for s in specs],
        out_shape=[jax.ShapeDtypeStruct((m, n), out_dtype)]
        + [jax.ShapeDtypeStruct((m, LANE), F32) for _ in narrow] + [s[2] for s in specs],
        compiler_params=_params(("arbitrary", "arbitrary") if riders or narrow else ("parallel", "parallel"),
                                2 * (tm * k * 2 + k * tn * 2 + tm * tn * ob + k * LANE * 2 + tm * LANE * 4)
                                + cast_vmem),
        name="matmul",
    )(a, w, *narrow, *[job.src for job, _ in riders])
    n_main = 1 + len(narrow)
    return outs[0], (outs[1] if narrow else None), finish(outs[n_main:])


def _mm_swiglu_kernel(*refs, n_mod, n_cast, metas, n_j):
    n_in = 3 + 3 * n_mod + n_cast
    a_ref, wg_ref, wu_ref = refs[:3]
    o_ref = refs[n_in]
    a = a_ref[...]
    g = _dot(a, wg_ref[...])
    u = _dot(a, wu_ref[...])
    o_ref[...] = (_silu(g) * u).astype(o_ref.dtype)
    if n_mod:
        @pl.when(pl.program_id(0) * n_j + pl.program_id(1) < metas[-1])
        def _():
            _mod_kernel(refs[3], refs[4], refs[5], refs[n_in + 1])

    _run_casts(metas[:n_cast], refs[3 + 3 * n_mod:n_in], refs[n_in + 1 + n_mod:], n_j)


def _matmul_swiglu(a, wg, wu, layer, cast_jobs=(), mod_job=None):
    m, k = a.shape
    n = wg.shape[2]
    tm = min(ROW_TILE, m)
    tn = COL_TILE
    n_i, n_j = m // tm, n // tn
    riders, specs, finish, cast_vmem = _attach_casts(cast_jobs, n_i, n_j)
    mod_in, mod_specs, mod_out_spec, mod_out_shape, n_chunks = [], [], [], [], 0
    if mod_job is not None:
        s8, ada_w, ada_b, ml = mod_job
        d_mod, n_mod_cols = ada_w.shape[1], ada_w.shape[2]
        n_chunks = n_mod_cols // LANE
        if n_chunks <= n_i * n_j:
            chunk = lambda i, j: jnp.minimum(i * n_j + j, n_chunks - 1)
            mod_in = [s8, ada_w, ada_b.reshape(ada_w.shape[0], 1, n_mod_cols)]
            mod_specs = [pl.BlockSpec((8, d_mod), lambda i, j: (0, 0)),
                         pl.BlockSpec((None, d_mod, LANE), lambda i, j: (ml, 0, chunk(i, j))),
                         pl.BlockSpec((None, 1, LANE), lambda i, j: (ml, 0, chunk(i, j)))]
            mod_out_spec = [pl.BlockSpec((8, LANE), lambda i, j: (0, chunk(i, j)))]
            mod_out_shape = [jax.ShapeDtypeStruct((8, n_mod_cols), F32)]
            cast_vmem += 2 * d_mod * LANE * 6
    n_mod = len(mod_out_spec)
    side = bool(riders) or n_mod
    outs = pl.pallas_call(
        functools.partial(_mm_swiglu_kernel, n_mod=n_mod, n_cast=len(riders),
                          metas=[s[3] for s in specs] + [n_chunks], n_j=n_j),
        grid=(n_i, n_j),
        in_specs=[
            pl.BlockSpec((tm, k), lambda i, j: (i, 0)),
            pl.BlockSpec((None, k, tn), lambda i, j: (layer, 0, j)),
            pl.BlockSpec((None, k, tn), lambda i, j: (layer, 0, j)),
        ] + mod_specs + [s[0] for s in specs],
        out_specs=[pl.BlockSpec((tm, tn), lambda i, j: (i, j))] + mod_out_spec + [s[1] for s in specs],
        out_shape=[jax.ShapeDtypeStruct((m, n), BF16)] + mod_out_shape + [s[2] for s in specs],
        compiler_params=_params(("arbitrary", "arbitrary") if side else ("parallel", "parallel"),
                                2 * (tm * k * 2 + 2 * k * tn * 2 + tm * tn * 2) + cast_vmem),
        name="matmul_swiglu",
    )(a, wg, wu, *mod_in, *[job.src for job, _ in riders])
    mod = outs[1] if n_mod else None
    if mod_job is not None and not n_mod:
        mod = _mod_vectors(mod_job[0], mod_job[1], mod_job[2], first_layer=mod_job[3], n_layers=1)[0]
    return outs[0], mod, finish(outs[1 + n_mod:])


def _mm_res_kernel(*refs, nk, n_a):
    a_refs, (w_ref, res_ref, gate_ref, o_ref), scratch = refs[:n_a], refs[n_a:n_a + 4], refs[n_a + 4:]
    k0 = 0
    p = None
    for a_ref in a_refs:
        ka = a_ref.shape[1]
        d = _dot(a_ref[...], w_ref[k0:k0 + ka, :])
        p = d if p is None else p + d
        k0 += ka
    if nk == 1:
        o_ref[...] = res_ref[...] + gate_ref[...] * p
        return
    acc_ref, = scratch
    kk = pl.program_id(2)

    @pl.when(kk == 0)
    def _():
        acc_ref[...] = p

    @pl.when(jnp.logical_and(kk > 0, kk < nk - 1))
    def _():
        acc_ref[...] += p

    @pl.when(kk == nk - 1)
    def _():
        o_ref[...] = res_ref[...] + gate_ref[...] * (acc_ref[...] + p)


def _matmul_residual(a_slabs, w, res, mod5, layer, k_gate, mod_row, nk=1, tn=COL_TILE):
    m = a_slabs[0].shape[0]
    k, n = w.shape[1], w.shape[2]
    tm = min(ROW_TILE, m)
    tk = k // nk
    assert nk == 1 or len(a_slabs) == 1
    a_specs = [pl.BlockSpec((tm, tk if nk > 1 else a.shape[1]), lambda i, j, q: (i, q)) for a in a_slabs]
    return pl.pallas_call(
        functools.partial(_mm_res_kernel, nk=nk, n_a=len(a_slabs)),
        grid=(m // tm, n // tn, nk),
        in_specs=a_specs + [
            pl.BlockSpec((None, tk, tn), lambda i, j, q: (layer, q, j)),
            pl.BlockSpec((tm, tn), lambda i, j, q: (i, j)),
            pl.BlockSpec((None, None, None, 1, tn), lambda i, j, q: (layer, mod_row(i, tm), k_gate, 0, j)),
        ],
        out_specs=pl.BlockSpec((tm, tn), lambda i, j, q: (i, j)),
        out_shape=jax.ShapeDtypeStruct((m, n), F32),
        scratch_shapes=[pltpu.VMEM((tm, tn), F32)] if nk > 1 else [],
        compiler_params=_params(
            ("parallel", "parallel", "arbitrary"), 2 * (tm * tk * 2 + tk * tn * 2 + 2 * tm * tn * 4) + tm * tn * 4
        ),
        name="matmul_residual",
    )(*a_slabs, w, res, mod5)


def _rope_tables(n_tok):
    quarter = HEAD_DIM // 4
    inv = 1.0 / (ROPE_THETA ** (jnp.arange(quarter, dtype=F32) / quarter))
    pos = jnp.arange(n_tok, dtype=jnp.int32)
    row = (pos // GRID_W).astype(F32)[:, None] * inv
    col = (pos % GRID_W).astype(F32)[:, None] * inv
    cos = jnp.concatenate([jnp.cos(row), jnp.cos(row), jnp.cos(col), jnp.cos(col)], axis=-1)
    sin = jnp.concatenate([-jnp.sin(row), jnp.sin(row), -jnp.sin(col), jnp.sin(col)], axis=-1)
    return cos, sin


def _rotary(cos_ref, sin_ref, scale):
    cos = cos_ref[...] * scale
    sin = sin_ref[...] * scale
    lane = lax.broadcasted_iota(jnp.int32, cos.shape, 1)
    first = (lane & (HEAD_DIM // 4)) == 0

    def rotate(x):
        x = x.astype(F32)
        partner = jnp.where(first, pltpu.roll(x, HEAD_DIM - HEAD_DIM // 4, 1), pltpu.roll(x, HEAD_DIM // 4, 1))
        return x * cos + partner * sin

    return rotate


def _rope_kernel(x_ref, cos_ref, sin_ref, o_ref, *, scale, n_heads):
    rotate = _rotary(cos_ref, sin_ref, scale)
    for h in range(n_heads):
        hs = slice(h * HEAD_DIM, (h + 1) * HEAD_DIM)
        o_ref[:, hs] = rotate(x_ref[:, hs]).astype(o_ref.dtype)


def _rope(p, col0, width, cos, sin, scale, tr=256):
    m = p.shape[0]
    n_tok = cos.shape[0]
    tpb = n_tok // tr
    assert col0 % width == 0
    return pl.pallas_call(
        functools.partial(_rope_kernel, scale=scale, n_heads=width // HEAD_DIM),
        grid=(m // tr,),
        in_specs=[
            pl.BlockSpec((tr, width), lambda i: (i, col0 // width)),
            pl.BlockSpec((tr, HEAD_DIM), lambda i: (i % tpb, 0)),
            pl.BlockSpec((tr, HEAD_DIM), lambda i: (i % tpb, 0)),
        ],
        out_specs=pl.BlockSpec((tr, width), lambda i: (i, 0)),
        out_shape=jax.ShapeDtypeStruct((m, width), BF16),
        compiler_params=_params(("parallel",), 2 * tr * (width * 6 + HEAD_DIM * 8)),
        name="rope",
    )(p, cos, sin)


def _gla_kernel(*refs, nh, scale, fwd):
    ql, kl, vl, dl, qc, kc, vc, dc, up_ref, bias_ref = refs[:10]
    if fwd:
        ol, oc, st_ref = refs[10:]
    else:
        fl, fc, rl, rc, gn_ref, ol, oc, st_ref = refs[10:]
    s = pl.program_id(1)
    is_ctx = s == 0
    n_chunk = CTX_BLOCK // GLA_CHUNK

    @pl.when(s == 0)
    def _():
        st_ref[...] = jnp.zeros_like(st_ref)

    tb = CTX_BLOCK

    def in_scan_order(x):
        if fwd:
            return x
        return jnp.concatenate([x[(n_chunk - 1 - p) * GLA_CHUNK:(n_chunk - p) * GLA_CHUNK] for p in range(n_chunk)],
                               axis=0)

    def pick(rc_, rl_, cols):
        return jnp.where(is_ctx, rc_[:, cols], rl_[:, cols])

    def load(rc_, rl_, cols):
        return in_scan_order(pick(rc_, rl_, cols))

    row = lax.broadcasted_iota(jnp.int32, (tb, tb), 0)
    col = lax.broadcasted_iota(jnp.int32, (tb, tb), 1)
    incl = jnp.logical_and(row // GLA_CHUNK == col // GLA_CHUNK, (row >= col) if fwd else (col >= row))
    row_chunk = lax.broadcasted_iota(jnp.int32, (tb, GLA_DK), 0) // GLA_CHUNK

    def per_chunk(x):
        return jnp.concatenate([jnp.where(row_chunk == p, x, 0.0) for p in range(n_chunk)], axis=1)

    z = _dot(load(dc, dl, slice(None)).astype(BF16), up_ref[...]) + bias_ref[...]
    g = (jnp.minimum(z, 0.0) - jnp.log(1.0 + jnp.exp(-jnp.abs(z)))) * (1.0 / GLA_NORMALIZER)
    tmat = jnp.where(incl, 1.0, 0.0).astype(BF16)
    g_hi = g.astype(BF16)
    g_rest = g - g_hi.astype(F32)
    g_mid = g_rest.astype(BF16)
    g_lo = (g_rest - g_mid.astype(F32)).astype(BF16)
    cum = _dot(tmat, g_hi) + _dot(tmat, g_mid) + _dot(tmat, g_lo)
    tot = [jnp.sum(g[p * GLA_CHUNK:(p + 1) * GLA_CHUNK], axis=0, keepdims=True) for p in range(n_chunk)]
    tot_rows = jnp.concatenate([jnp.broadcast_to(t, (GLA_CHUNK, t.shape[1])) for t in tot], axis=0)

    staged = []
    for h in range(nh):
        ks = slice(h * GLA_DK, (h + 1) * GLA_DK)
        vs = slice(h * GLA_DV, (h + 1) * GLA_DV)
        bh = cum[:, ks]
        q = load(qc, ql, ks)
        k = load(kc, kl, ks)
        v = load(vc, vl, vs).astype(BF16)
        qd = q * (scale * jnp.exp(bh))
        ki = (k * jnp.exp(-bh)).astype(BF16)
        ku = k * jnp.exp(tot_rows[:, ks] - bh)
        a = jnp.where(incl, _dot_nt(qd.astype(BF16), ki), 0.0)
        inc = _dot_tn(v, per_chunk(ku).astype(BF16))
        staged.append((a.astype(BF16), inc, per_chunk(qd).astype(BF16), v))

    for h in range(nh):
        ks = slice(h * GLA_DK, (h + 1) * GLA_DK)
        vs = slice(h * GLA_DV, (h + 1) * GLA_DV)
        a, inc, qd_chunks, v = staged[h]
        st = st_ref[h]
        entering = []
        for p in range(n_chunk):
            entering.append(st)
            st = st * jnp.exp(tot[p][:, ks]) + inc[:, p * GLA_DK:(p + 1) * GLA_DK]
        st_ref[h] = st
        o = _dot(a, v) + _dot_nt(qd_chunks, jnp.concatenate(entering, axis=1).astype(BF16))
        o = in_scan_order(o)
        if not fwd:
            o = o + pick(fc, fl, vs)
            o = o * lax.rsqrt(jnp.mean(o * o, axis=-1, keepdims=True) + NORM_EPS) * gn_ref[...]
            o = o * _silu(pick(rc, rl, vs))
        o = o.astype(ol.dtype)

        @pl.when(is_ctx)
        def _():
            oc[:, vs] = o

        @pl.when(jnp.logical_not(is_ctx))
        def _():
            ol[:, vs] = o


def _gla(p_lat, p_ctx, dn_lat, dn_ctx, up2, bias2, norm_g, nb, n_tok, nh):
    qk = nh * GLA_DK
    dv = nh * GLA_DV
    tb = CTX_BLOCK
    nbl = n_tok // tb
    m_lat, m_ctx = nb * n_tok, nb * tb

    def scan(fwd, extra_in, extra_specs, out_dtype):
        def lat_blk(b, s):
            t = jnp.maximum(s - 1, 0)
            return b * nbl + (t if fwd else nbl - 1 - t)

        lat = lambda w, c: pl.BlockSpec((tb, w), lambda b, s: (lat_blk(b, s), c))
        ctx = lambda w, c: pl.BlockSpec((tb, w), lambda b, s: (b, c))
        d = 0 if fwd else 1
        vm = (2 * (2 * (2 * tb * qk + tb * dv + tb * LANE) * 4 + 2 * tb * dv * 4 + len(extra_in) * tb * dv * 4)
              + nh * GLA_DV * GLA_DK * 4)
        return pl.pallas_call(
            functools.partial(_gla_kernel, nh=nh, scale=GLA_DK ** -0.5, fwd=fwd),
            grid=(nb, nbl + 1),
            in_specs=[
                lat(qk, 0), lat(qk, 1), lat(dv, 1), lat(LANE, 0),
                ctx(qk, 0), ctx(qk, 1), ctx(dv, 1), ctx(LANE, 0),
                pl.BlockSpec((None, LANE, qk), lambda b, s: (d, 0, 0)),
                pl.BlockSpec((None, 1, qk), lambda b, s: (d, 0, 0)),
            ] + extra_specs(lat, ctx),
            out_specs=[lat(dv, 0), ctx(dv, 0)],
            out_shape=[jax.ShapeDtypeStruct((m_lat, dv), out_dtype), jax.ShapeDtypeStruct((m_ctx, dv), out_dtype)],
            scratch_shapes=[pltpu.VMEM((nh, GLA_DV, GLA_DK), F32)],
            compiler_params=_params(("parallel", "arbitrary"), vm),
            name="gla_scan_fwd" if fwd else "gla_scan_bwd",
        )(p_lat, p_lat, p_lat, dn_lat, p_ctx, p_ctx, p_ctx, dn_ctx, up2, bias2, *extra_in)

    of_lat, of_ctx = scan(True, (), lambda lat, ctx: [], F32)
    return scan(
        False,
        (of_lat, of_ctx, p_lat, p_ctx, norm_g.reshape(1, GLA_DV)),
        lambda lat, ctx: [lat(dv, 0), ctx(dv, 0), lat(dv, 2), ctx(dv, 2),
                          pl.BlockSpec((1, GLA_DV), lambda b, s: (0, 0))],
        BF16,
    )


def _swa_kernel(sink_ref, q_ref, cos_ref, sin_ref, kp, kc, kn, vp, vc, vn, kx, vx, o_ref, *, n_blk, kvs):
    i = pl.program_id(2)
    log2e = math.log2(math.e)
    rotate = _rotary(cos_ref, sin_ref, HEAD_DIM ** -0.5 * log2e)
    span = SWA_BLOCK + 2 * SWA_WINDOW
    gq = SWA_GROUP * SWA_BLOCK
    kj = lax.broadcasted_iota(jnp.int32, (span, gq), 0)
    lane = lax.broadcasted_iota(jnp.int32, (span, gq), 1)
    qi = lane & (SWA_BLOCK - 1)
    key_pos = (i - 1) * SWA_BLOCK + kj
    ok = jnp.logical_and(jnp.abs(qi + SWA_WINDOW - kj) <= SWA_WINDOW,
                         jnp.logical_and(key_pos >= 0, key_pos < n_blk * SWA_BLOCK))
    head = lax.broadcasted_iota(jnp.int32, (1, gq), 1) // SWA_BLOCK
    for kv in range(kvs):
        hkv = pl.program_id(1) * kvs + kv
        ks = slice(kv * HEAD_DIM, (kv + 1) * HEAD_DIM)
        kwin = jnp.concatenate([kp[:, ks], kc[:, ks], kn[:, ks]], axis=0)
        vwin = jnp.concatenate([vp[:, ks], vc[:, ks], vn[:, ks]], axis=0).astype(BF16)
        kctx = kx[:, ks].astype(BF16)
        vctx = vx[:, ks].astype(BF16)
        q0 = kv * SWA_GROUP * HEAD_DIM
        qs = jnp.concatenate([rotate(q_ref[:, q0 + g * HEAD_DIM:q0 + (g + 1) * HEAD_DIM]).astype(BF16)
                              for g in range(SWA_GROUP)], axis=0)
        s_win = jnp.where(ok, _dot_nt(kwin, qs), -jnp.inf)
        s_ctx = _dot_nt(kctx, qs)
        sink = jnp.zeros((1, gq), F32)
        for g in range(SWA_GROUP):
            sink = jnp.where(head == g, sink_ref[0, hkv * SWA_GROUP + g] * log2e, sink)
        m = jnp.maximum(jnp.maximum(jnp.max(s_win, axis=0, keepdims=True), jnp.max(s_ctx, axis=0, keepdims=True)),
                        sink)
        p_win = jnp.exp2(s_win - m)
        p_ctx = jnp.exp2(s_ctx - m)
        den = jnp.sum(p_win, axis=0, keepdims=True) + jnp.sum(p_ctx, axis=0, keepdims=True) + jnp.exp2(sink - m)
        ot = (_dot_tn(vctx, p_ctx.astype(BF16)) + _dot_tn(vwin, p_win.astype(BF16))) / den
        for g in range(SWA_GROUP):
            hs = slice(q0 + g * HEAD_DIM, q0 + (g + 1) * HEAD_DIM)
            o_ref[:, hs] = ot[:, g * SWA_BLOCK:(g + 1) * SWA_BLOCK].T.astype(o_ref.dtype)


def _swa(sink, cos, sin, k_rope, p_lat, p_ctx, qcol, kcol, vcol, nb, n_tok, n_kv):
    n_blk = n_tok // SWA_BLOCK
    tb = SWA_BLOCK
    kvs = next(c for c in (4, 2, 1) if n_kv % c == 0 and kcol % (c * HEAD_DIM) == 0 and vcol % (c * HEAD_DIM) == 0
               and qcol % (c * SWA_GROUP * HEAD_DIM) == 0)
    kw = kvs * HEAD_DIM
    gw = kvs * SWA_GROUP * HEAD_DIM
    assert kcol % kw == 0 and vcol % kw == 0 and qcol % gw == 0
    kc0 = kcol // kw
    vc0 = vcol // kw
    qc0 = qcol // gw
    prev = lambda b, h, i: b * n_blk + jnp.maximum(i - 1, 0)
    cur = lambda b, h, i: b * n_blk + i
    nxt = lambda b, h, i: b * n_blk + jnp.minimum(i + 1, n_blk - 1)
    kspec = lambda f: pl.BlockSpec((tb, kw), lambda b, h, i: (f(b, h, i), h))
    vspec = lambda f: pl.BlockSpec((tb, kw), lambda b, h, i: (f(b, h, i), vc0 + h))
    return pl.pallas_call(
        functools.partial(_swa_kernel, n_blk=n_blk, kvs=kvs),
        grid=(nb, n_kv // kvs, n_blk),
        in_specs=[
            pl.BlockSpec(memory_space=pltpu.SMEM),
            pl.BlockSpec((tb, gw), lambda b, h, i: (cur(b, h, i), qc0 + h)),
            pl.BlockSpec((tb, HEAD_DIM), lambda b, h, i: (i, 0)),
            pl.BlockSpec((tb, HEAD_DIM), lambda b, h, i: (i, 0)),
            kspec(prev), kspec(cur), kspec(nxt),
            vspec(prev), vspec(cur), vspec(nxt),
            pl.BlockSpec((CTX_BLOCK, kw), lambda b, h, i: (b, kc0 + h)),
            pl.BlockSpec((CTX_BLOCK, kw), lambda b, h, i: (b, vc0 + h)),
        ],
        out_specs=pl.BlockSpec((tb, gw), lambda b, h, i: (cur(b, h, i), h)),
        out_shape=jax.ShapeDtypeStruct((nb * n_tok, n_kv * SWA_GROUP * HEAD_DIM), BF16),
        compiler_params=_params(("parallel", "parallel", "parallel"), 16 << 20),
        name="swa",
    )(sink, p_lat, cos, sin, k_rope, k_rope, k_rope, p_lat, p_lat, p_lat, p_ctx, p_ctx)


def _ctx_sink_kernel(sink_ref, q_ref, kx, vx, o_ref, *, scale):
    hkv = pl.program_id(1)
    kctx = kx[...].astype(BF16)
    vctx = vx[...].astype(BF16)
    for g in range(SWA_GROUP):
        hs = slice(g * HEAD_DIM, (g + 1) * HEAD_DIM)
        q = (q_ref[:, hs].astype(F32) * scale).astype(BF16)
        s = _dot_nt(q, kctx)
        sink = sink_ref[0, hkv * SWA_GROUP + g]
        m = jnp.maximum(jnp.max(s, axis=-1, keepdims=True), sink)
        p = jnp.exp(s - m)
        den = jnp.sum(p, axis=-1, keepdims=True) + jnp.exp(sink - m)
        o_ref[:, hs] = (_dot(p.astype(BF16), vctx) / den).astype(o_ref.dtype)


def _ctx_sink(sink, p_ctx, qcol, kcol, vcol, nb, n_kv):
    gw = SWA_GROUP * HEAD_DIM
    return pl.pallas_call(
        functools.partial(_ctx_sink_kernel, scale=HEAD_DIM ** -0.5),
        grid=(nb, n_kv),
        in_specs=[
            pl.BlockSpec(memory_space=pltpu.SMEM),
            pl.BlockSpec((CTX_BLOCK, gw), lambda b, h: (b, qcol // gw + h)),
            pl.BlockSpec((CTX_BLOCK, HEAD_DIM), lambda b, h: (b, kcol // HEAD_DIM + h)),
            pl.BlockSpec((CTX_BLOCK, HEAD_DIM), lambda b, h: (b, vcol // HEAD_DIM + h)),
        ],
        out_specs=pl.BlockSpec((CTX_BLOCK, gw), lambda b, h: (b, h)),
        out_shape=jax.ShapeDtypeStruct((nb * CTX_BLOCK, n_kv * gw), BF16),
        compiler_params=_params(("parallel", "parallel"), 4 << 20),
        name="ctx_sink_attention",
    )(sink, p_ctx, p_ctx, p_ctx)


def _na_bias_table(rpb, rows):
    qcol = np.arange(GRID_W)[:, None]
    kcol = np.arange(GRID_W)[None, :]
    col_start = np.clip(qcol - NA_COLS // 2, 0, GRID_W - NA_COLS)
    col_ok = (kcol >= col_start) & (kcol < col_start + NA_COLS)
    dc = np.clip(kcol - qcol + NA_COLS - 1, 0, 2 * NA_COLS - 2)
    n_dc = 2 * NA_COLS - 1
    onehot = (dc[None] == np.arange(n_dc)[:, None, None]).astype(np.float32)
    cols = jnp.einsum('hrd,dqk->hrqk', rpb.astype(F32), onehot, precision=lax.Precision.HIGHEST)
    cols = jnp.where(col_ok[None, None], cols, -jnp.inf).transpose(0, 1, 3, 2)
    masked = jnp.full(cols.shape[:1] + cols.shape[2:], -jnp.inf, F32)

    span = NA_ROWS + NA_QROWS - 1
    patterns = []
    for r0 in range(0, rows, NA_QROWS):
        rs = [min(max(r0 + i - NA_ROWS // 2, 0), rows - NA_ROWS) for i in range(NA_QROWS)]
        u = min(rs[0], rows - span)
        patterns.append(tuple((r0 + i - rs[i], rs[i] - u) for i in range(NA_QROWS)))
    variants = sorted(set(patterns), key=patterns.index)
    blocks = []
    for pat in variants:
        per_query_row = []
        for delta, off in pat:
            krows = [cols[:, kr - off - delta + NA_ROWS - 1] if 0 <= kr - off < NA_ROWS else masked
                     for kr in range(span)]
            per_query_row.append(jnp.stack(krows, axis=1))
        blocks.append(jnp.concatenate(per_query_row, axis=-1).reshape(rpb.shape[0], span * GRID_W, NA_QROWS * GRID_W))
    ids = np.array([variants.index(p) for p in patterns], dtype=np.int32)
    return jnp.stack(blocks, axis=1), ids


def _na_kernel(ids_ref, q_ref, k_ref, v_ref, kx, vx, bias_ref, o_ref, *, nh, scale):
    for h in range(nh):
        hs = slice(h * HEAD_DIM, (h + 1) * HEAD_DIM)
        q = (q_ref[:, hs].astype(F32) * scale).astype(BF16)
        s_nb = _dot_nt(k_ref[:, hs].astype(BF16), q) + bias_ref[h]
        s_ctx = _dot_nt(kx[:, hs].astype(BF16), q)
        m = jnp.maximum(jnp.max(s_nb, axis=0, keepdims=True), jnp.max(s_ctx, axis=0, keepdims=True))
        p_nb = jnp.exp(s_nb - m)
        p_ctx = jnp.exp(s_ctx - m)
        den = jnp.sum(p_nb, axis=0, keepdims=True) + jnp.sum(p_ctx, axis=0, keepdims=True)
        ot = (_dot_tn(vx[:, hs].astype(BF16), p_ctx.astype(BF16))
              + _dot_tn(v_ref[:, hs].astype(BF16), p_nb.astype(BF16)))
        o_ref[:, hs] = (ot / den).T.astype(o_ref.dtype)


def _na(p_lat, p_ctx, rpb, nb, n_tok, nh):
    rows = n_tok // GRID_W
    w = nh * HEAD_DIM
    span = NA_ROWS + NA_QROWS - 1
    win = span * GRID_W
    tq = NA_QROWS * GRID_W
    n_step = rows // NA_QROWS
    bias_tab, ids = _na_bias_table(rpb, rows)
    u = lambda g: jnp.clip(g * NA_QROWS - NA_ROWS // 2, 0, rows - span)
    kv = lambda c: pl.BlockSpec((pl.Element(win), pl.Element(w)),
                                lambda b, g, ids: ((b * rows + u(g)) * GRID_W, c * w))
    vm = 2 * (tq * w * 4 + 2 * win * w * 4 + 2 * CTX_BLOCK * w * 4 + nh * tq * win * 4 + tq * w * 2)
    return pl.pallas_call(
        functools.partial(_na_kernel, nh=nh, scale=HEAD_DIM ** -0.5),
        grid_spec=pltpu.PrefetchScalarGridSpec(
            num_scalar_prefetch=1,
            grid=(nb, n_step),
            in_specs=[
                pl.BlockSpec((tq, w), lambda b, g, ids: (b * n_step + g, 0)),
                kv(1), kv(2),
                pl.BlockSpec((CTX_BLOCK, w), lambda b, g, ids: (b, 1)),
                pl.BlockSpec((CTX_BLOCK, w), lambda b, g, ids: (b, 2)),
                pl.BlockSpec((nh, None, win, tq), lambda b, g, ids: (0, ids[g], 0, 0)),
            ],
            out_specs=pl.BlockSpec((tq, w), lambda b, g, ids: (b * n_step + g, 0)),
        ),
        out_shape=jax.ShapeDtypeStruct((nb * n_tok, w), BF16),
        compiler_params=_params(("parallel", "arbitrary"), vm),
        name="neighborhood_attention",
    )(jnp.asarray(ids), p_lat, p_lat, p_lat, p_ctx, p_ctx, bias_tab)


def _diff_kernel(qraw_ref, cos_ref, sin_ref, k_ref, v_ref, kx, vx, lq1, lk1, lq2, lk2, g_ref, o_ref,
                 m_ref, l_ref, acc_ref, sa_ref, sb_ref, pa_ref, pb_ref, pc_ref, aa_ref, ab_ref, ac_ref,
                 vt_ref, vxt_ref, q_ref, *, n_blk, tk, lam_init):

    rotate = _rotary(cos_ref, sin_ref, HEAD_DIM ** -0.5 * math.log2(math.e))
    for t in range(2):
        hs = slice(t * HEAD_DIM, (t + 1) * HEAD_DIM)
        q_ref[:, hs] = rotate(qraw_ref[:, hs]).astype(q_ref.dtype)

    @pl.when(pl.program_id(2) == 0)
    def _():
        for j in range(n_blk):
            vt_ref[j] = v_ref[j * tk:(j + 1) * tk, :].T
        vxt_ref[...] = vx[...].astype(BF16).T

    vxt = vxt_ref
    m_ref[...] = jnp.full_like(m_ref, -jnp.inf)
    l_ref[...] = jnp.zeros_like(l_ref)
    acc_ref[...] = jnp.zeros_like(acc_ref)

    def scores(keys):
        return [_dot_nt(keys[:, t * HEAD_DIM:(t + 1) * HEAD_DIM], q_ref[:, t * HEAD_DIM:(t + 1) * HEAD_DIM])
                for t in range(2)]

    def lat_scores(j, dst):
        s = scores(k_ref[pl.ds(pl.multiple_of(j * tk, tk), tk), :])
        dst[0] = s[0]
        dst[1] = s[1]

    def softmax(s, p_dst, a_dst):
        for t in range(2):
            m_old = m_ref[t]
            m_new = jnp.maximum(m_old, jnp.max(s[t], axis=0, keepdims=True))
            alpha = jnp.exp2(m_old - m_new)
            p = jnp.exp2(s[t] - m_new)
            l_ref[t] = alpha * l_ref[t] + jnp.sum(p, axis=0, keepdims=True)
            m_ref[t] = m_new
            p_dst[t] = p.astype(BF16)
            a_dst[t] = alpha

    def add_values(p_src, a_src, vt):
        for t in range(2):
            acc_ref[t] = a_src[t] * acc_ref[t] + _dot(vt, p_src[t])

    def stage(s_cur, p_cur, a_cur, next_block, s_next, prev):
        if next_block is not None:
            lat_scores(next_block, s_next)
        add_values(*prev)
        softmax((s_cur[0], s_cur[1]), p_cur, a_cur)

    s_ctx = scores(kx[...].astype(BF16))
    lat_scores(0, sa_ref)
    softmax(s_ctx, pc_ref, ac_ref)
    stage(sa_ref, pa_ref, aa_ref, 1, sb_ref, (pc_ref, ac_ref, vxt[...]))

    def body(jj, carry):
        j = 2 * jj + 1
        stage(sb_ref, pb_ref, ab_ref, j + 1, sa_ref, (pa_ref, aa_ref, vt_ref[j - 1]))
        stage(sa_ref, pa_ref, aa_ref, j + 2, sb_ref, (pb_ref, ab_ref, vt_ref[j]))
        return carry

    lax.fori_loop(0, (n_blk - 2) // 2, body, 0)
    stage(sb_ref, pb_ref, ab_ref, None, None, (pa_ref, aa_ref, vt_ref[n_blk - 2]))
    add_values(pb_ref, ab_ref, vt_ref[n_blk - 1])

    lam = (jnp.exp(jnp.sum(lq1[...] * lk1[...], axis=-1, keepdims=True))
           - jnp.exp(jnp.sum(lq2[...] * lk2[...], axis=-1, keepdims=True)) + lam_init)
    o = acc_ref[0] / l_ref[0] - lam * (acc_ref[1] / l_ref[1])
    y = o * lax.rsqrt(jnp.mean(o * o, axis=0, keepdims=True) + NORM_EPS) * (g_ref[...] * (1.0 - lam_init))
    o_ref[...] = y.T.astype(o_ref.dtype)


def _diff(cos, sin, k_rope, p_lat, p_ctx, qcol, kcol, vcol, lq1, lk1, lq2, lk2, norm_g, lam_init, nb, n_tok, nh,
          tq=512, tk=DIFF_KEY_BLOCK):
    nq = n_tok // tq
    n_blk = n_tok // tk
    assert n_blk >= 2 and n_blk % 2 == 0, "the key-block pipeline alternates two buffers"
    w = 2 * HEAD_DIM
    vec = lambda: pl.BlockSpec((1, HEAD_DIM), lambda b, h, i: (0, 0))
    vm = (2 * (tq * w * 2 + 2 * n_tok * w * 2 + 2 * CTX_BLOCK * w * 2 + tq * w * 2)
          + n_tok * w * 2 + 2 * tq * w * 4 + 2 * 2 * tk * tq * 6 + tq * w * 8)
    return pl.pallas_call(
        functools.partial(_diff_kernel, n_blk=n_blk, tk=tk, lam_init=lam_init),
        grid=(nb, nh, nq),
        in_specs=[
            pl.BlockSpec((tq, w), lambda b, h, i: (b * nq + i, qcol // w + h)),
            pl.BlockSpec((tq, HEAD_DIM), lambda b, h, i: (i, 0)),
            pl.BlockSpec((tq, HEAD_DIM), lambda b, h, i: (i, 0)),
            pl.BlockSpec((n_tok, w), lambda b, h, i: (b, h)),
            pl.BlockSpec((n_tok, w), lambda b, h, i: (b, vcol // w + h)),
            pl.BlockSpec((CTX_BLOCK, w), lambda b, h, i: (b, kcol // w + h)),
            pl.BlockSpec((CTX_BLOCK, w), lambda b, h, i: (b, vcol // w + h)),
            vec(), vec(), vec(), vec(),
            pl.BlockSpec((DIFF_DV, 1), lambda b, h, i: (0, 0)),
        ],
        out_specs=pl.BlockSpec((tq, w), lambda b, h, i: (b * nq + i, h)),
        out_shape=jax.ShapeDtypeStruct((nb * n_tok, nh * w), BF16),
        scratch_shapes=[
            pltpu.VMEM((2, 1, tq), F32),
            pltpu.VMEM((2, 1, tq), F32),
            pltpu.VMEM((2, DIFF_DV, tq), F32),
            pltpu.VMEM((2, tk, tq), F32),
            pltpu.VMEM((2, tk, tq), F32),
            pltpu.VMEM((2, tk, tq), BF16),
            pltpu.VMEM((2, tk, tq), BF16),
            pltpu.VMEM((2, CTX_BLOCK, tq), BF16),
            pltpu.VMEM((2, 1, tq), F32),
            pltpu.VMEM((2, 1, tq), F32),
            pltpu.VMEM((2, 1, tq), F32),
            pltpu.VMEM((n_blk, w, tk), BF16),
            pltpu.VMEM((w, CTX_BLOCK), BF16),
            pltpu.VMEM((tq, w), BF16),
        ],
        compiler_params=_params(("parallel", "parallel", "arbitrary"), vm),
        name="diff_attention",
    )(p_lat, cos, sin, k_rope, p_lat, p_ctx, p_ctx, lq1.reshape(1, -1), lk1.reshape(1, -1), lq2.reshape(1, -1),
      lk2.reshape(1, -1), norm_g.reshape(-1, 1))


def _split_even_kernel(w_ref, main_ref, dn_ref, *, g0):
    r = 2 * GLA_RANK
    main_ref[:, :g0] = w_ref[:, :g0].astype(main_ref.dtype)
    main_ref[:, g0:] = w_ref[:, g0 + r:].astype(main_ref.dtype)
    dn_ref[:, :r] = w_ref[:, g0:g0 + r].astype(dn_ref.dtype)
    dn_ref[:, r:] = jnp.zeros((dn_ref.shape[0], LANE - r), dn_ref.dtype)


def _split_even_weight(w_stack, layer, g0, tr=256):
    _, k, n = w_stack.shape
    n_main = n - 2 * GLA_RANK
    return pl.pallas_call(
        functools.partial(_split_even_kernel, g0=g0),
        grid=(k // tr,),
        in_specs=[pl.BlockSpec((None, tr, n), lambda i: (layer, i, 0))],
        out_specs=[pl.BlockSpec((tr, n_main), lambda i: (i, 0)), pl.BlockSpec((tr, LANE), lambda i: (i, 0))],
        out_shape=[jax.ShapeDtypeStruct((k, n_main), BF16), jax.ShapeDtypeStruct((k, LANE), BF16)],
        compiler_params=_params(("parallel",), 2 * tr * (n * 4 + n_main * 2 + LANE * 2)),
        name="split_even_weight",
    )(w_stack)


def kernel(x, c, ctx, c_ctx, ada_w, ada_b, norm_mix_g, norm_ffn_g, w_out, ffn_w_gate, ffn_w_up, ffn_w_down,
           ev_w_in, gla_gate_up_f, gla_gate_bias_f, gla_gate_up_b, gla_gate_bias_b, gla_norm_g, swa_sink,
           od_w_in, na_rpb, diff_lq1, diff_lk1, diff_lq2, diff_lk2, diff_norm_g, final_norm_g):
    nb, n_tok, d = x.shape
    lc = ctx.shape[1]
    depth = ada_w.shape[0]
    f_hidden = ffn_w_gate.shape[2]
    half = d // 2
    assert lc == CTX_BLOCK and nb < 8 and depth == 2, "layout assumes a 256-token context, batch < 8, two layers"
    gla_heads = half // GLA_DV
    gla_qk = gla_heads * GLA_DK
    n_q = half // HEAD_DIM
    n_kv = n_q // SWA_GROUP
    diff_heads = half // (2 * HEAD_DIM)

    xs = x.reshape(nb * n_tok, d)
    cs = ctx.reshape(nb * lc, d)
    cos, sin = _rope_tables(n_tok)

    s8 = jnp.zeros((8, d), F32).at[:nb].set(c).at[nb].set(c_ctx)
    mod5 = _mod_vectors(s8, ada_w, ada_b, 0, 1).reshape(1, 8, 6, 1, d)
    lat_row = lambda i, tm: (i * tm) // n_tok
    ctx_row = lambda i, tm: nb

    f_pad = -(-f_hidden // FFN_PAD) * FFN_PAD

    def ffn_up(stream, layer, mod_row, wg, wu, cast_jobs=(), mod_job=None):
        h2 = _norm_mod(stream, norm_ffn_g[layer], mod5, layer, 3, 4, mod_row)
        return _matmul_swiglu(h2, wg, wu, layer, cast_jobs, mod_job)

    def ffn_down(a, stream, layer, mod_row, wd):
        return _matmul_residual([a], wd, stream, mod5, layer, 5, mod_row, nk=4, tn=2 * COL_TILE)

    layer = 0
    g0 = 2 * gla_qk + 2 * half
    w_main, w_dn = _split_even_weight(ev_w_in, 0, g0)
    swa_q0 = g0
    swa_k0 = swa_q0 + half
    swa_v0 = swa_k0 + n_kv * HEAD_DIM
    up2 = jnp.zeros((2, LANE, gla_qk), F32)
    up2 = up2.at[0, :GLA_RANK].set(gla_gate_up_f[0]).at[1, GLA_RANK:2 * GLA_RANK].set(gla_gate_up_b[0]).astype(BF16)
    bias2 = jnp.stack([gla_gate_bias_f[0], gla_gate_bias_b[0]]).reshape(2, 1, gla_qk)

    h = _norm_mod(xs, norm_mix_g[layer], mod5, layer, 0, 1, lat_row)
    hc = _norm_mod(cs, norm_mix_g[layer], mod5, layer, 0, 1, ctx_row)
    p_lat, dn_lat, (wg, wu, wo) = _matmul(h, w_main, w_narrow=w_dn, cast_jobs=[
        _CastJob(ffn_w_gate, d, f_pad), _CastJob(ffn_w_up, d, f_pad), _CastJob(w_out, d, d)])
    p_ctx, dn_ctx, _ = _matmul(hc, w_main, w_narrow=w_dn)

    ya, yac = _gla(p_lat, p_ctx, dn_lat, dn_ctx, up2, bias2, gla_norm_g[0], nb, n_tok, gla_heads)

    sink = swa_sink[0].reshape(1, n_q)
    k_rope = _rope(p_lat, swa_k0, n_kv * HEAD_DIM, cos, sin, 1.0)
    yb = _swa(sink, cos, sin, k_rope, p_lat, p_ctx, swa_q0, swa_k0, swa_v0, nb, n_tok, n_kv)
    ybc = _ctx_sink(sink, p_ctx, swa_q0, swa_k0, swa_v0, nb, n_kv)

    xs = _matmul_residual([ya, yb], wo, xs, mod5, layer, 2, lat_row, tn=2 * COL_TILE)
    cs = _matmul_residual([yac, ybc], wo, cs, mod5, layer, 2, ctx_row, tn=2 * COL_TILE)
    a, mod_next, (wd, w_in1) = ffn_up(xs, layer, lat_row, wg, wu, cast_jobs=[
        _CastJob(ffn_w_down, f_pad, d), _CastJob(od_w_in, d, od_w_in.shape[2])], mod_job=(s8, ada_w, ada_b, 1))
    mod5 = jnp.concatenate([mod5, mod_next.reshape(1, 8, 6, 1, d)], axis=0)
    xs = ffn_down(a, xs, layer, lat_row, wd)
    cs = ffn_down(ffn_up(cs, layer, ctx_row, wg, wu)[0], cs, layer, ctx_row, wd)

    layer = 1
    w_in = w_in1[0]
    h = _norm_mod(xs, norm_mix_g[layer], mod5, layer, 0, 1, lat_row)
    hc = _norm_mod(cs, norm_mix_g[layer], mod5, layer, 0, 1, ctx_row)
    p_lat = _matmul(h, w_in, BF16, tm=2 * ROW_TILE)[0]
    p_ctx = _matmul(hc, w_in, BF16)[0]

    yn =_na(p_lat, p_ctx, na_rpb[0], nb, n_tok, n_q)

    dq0 = 3 * half
    dk0 = dq0 + half
    dv0 = dk0 + half
    lam_init = 0.8 - 0.6 * math.exp(-0.3 * layer)
    kd = _rope(p_lat, dk0, half, cos, sin, 1.0)
    yd = _diff(cos, sin, kd, p_lat, p_ctx, dq0, dk0, dv0, diff_lq1[0], diff_lk1[0], diff_lq2[0], diff_lk2[0],
               diff_norm_g[0], lam_init, nb, n_tok, diff_heads)

    xs = _matmul_residual([yn, yd], wo, xs, mod5, layer, 2, lat_row, tn=2 * COL_TILE)
    xs = ffn_down(ffn_up(xs, layer, lat_row, wg, wu)[0], xs, layer, lat_row, wd)
    return _final_norm(xs, final_norm_g).reshape(nb, n_tok, d)
```

```python
import functools
import math
from typing import NamedTuple

import numpy as np
import jax
import jax.numpy as jnp
from jax import lax
from jax.experimental import pallas as pl
from jax.experimental.pallas import tpu as pltpu

HEAD_DIM = 128
GRID_W = 64
ROPE_THETA = 10000.0
NORM_EPS = 1e-6
GLA_DK = 128
GLA_DV = 256
GLA_RANK = 16
GLA_NORMALIZER = 16.0
GLA_CHUNK = 64
SWA_GROUP = 4
SWA_WINDOW = 128
SWA_BLOCK = 128
NA_ROWS = 8
NA_COLS = 16
NA_QROWS = 4
DIFF_DV = 2 * HEAD_DIM
DIFF_KEY_BLOCK = 2048
CTX_BLOCK = 256

V7X_VMEM_BUDGET = 60 * 1024 * 1024
LANE = 128
ROW_TILE = 1024
COL_TILE = 512
FFN_PAD = 1024

F32 = jnp.float32
BF16 = jnp.bfloat16


def _params(sem, vmem_bytes):
    limit = int(min(V7X_VMEM_BUDGET, vmem_bytes * 5 // 4 + (4 << 20)))
    return pltpu.CompilerParams(dimension_semantics=sem, vmem_limit_bytes=limit)


def _dot(a, b):
    return jnp.dot(a, b, preferred_element_type=F32)


def _dot_nt(a, b):
    return lax.dot_general(a, b, (((1,), (1,)), ((), ())), preferred_element_type=F32)


def _dot_tn(a, b):
    return lax.dot_general(a, b, (((0,), (0,)), ((), ())), preferred_element_type=F32)


def _silu(t):
    return t * jax.nn.sigmoid(t)


def _mod_kernel(s_ref, w_ref, b_ref, o_ref):
    a = _silu(s_ref[...]).astype(BF16)
    o_ref[...] = _dot(a, w_ref[...].astype(BF16)) + b_ref[...]


def _mod_vectors(s8, ada_w, ada_b, first_layer, n_layers):
    depth, d, n = ada_w.shape
    tn = COL_TILE
    return pl.pallas_call(
        _mod_kernel,
        grid=(n_layers, n // tn),
        in_specs=[
            pl.BlockSpec((8, d), lambda l, j: (0, 0)),
            pl.BlockSpec((None, d, tn), lambda l, j: (first_layer + l, 0, j)),
            pl.BlockSpec((None, 1, tn), lambda l, j: (first_layer + l, 0, j)),
        ],
        out_specs=pl.BlockSpec((None, 8, tn), lambda l, j: (l, 0, j)),
        out_shape=jax.ShapeDtypeStruct((n_layers, 8, n), F32),
        compiler_params=_params(("parallel", "parallel"), 2 * d * tn * 4 + d * tn * 2),
        name="adaln_mod",
    )(s8, ada_w, ada_b.reshape(depth, 1, n))


def _norm_mod_kernel(x_ref, g_ref, sh_ref, sc_ref, o_ref):
    x = x_ref[...]
    r = lax.rsqrt(jnp.mean(x * x, axis=-1, keepdims=True) + NORM_EPS)
    y = x * r * g_ref[...]
    o_ref[...] = (y * (1.0 + sc_ref[...]) + sh_ref[...]).astype(o_ref.dtype)


def _norm_mod(x2, g, mod5, layer, k_shift, k_scale, mod_row, tm=512):
    m, d = x2.shape
    return pl.pallas_call(
        _norm_mod_kernel,
        grid=(m // tm,),
        in_specs=[
            pl.BlockSpec((tm, d), lambda i: (i, 0)),
            pl.BlockSpec((1, d), lambda i: (0, 0)),
            pl.BlockSpec((None, None, None, 1, d), lambda i: (layer, mod_row(i, tm), k_shift, 0, 0)),
            pl.BlockSpec((None, None, None, 1, d), lambda i: (layer, mod_row(i, tm), k_scale, 0, 0)),
        ],
        out_specs=pl.BlockSpec((tm, d), lambda i: (i, 0)),
        out_shape=jax.ShapeDtypeStruct((m, d), BF16),
        compiler_params=_params(("parallel",), 2 * tm * d * 6),
        name="norm_mod",
    )(x2, g.reshape(1, d), mod5, mod5)


def _final_norm_kernel(x_ref, g_ref, o_ref):
    x = x_ref[...]
    r = lax.rsqrt(jnp.mean(x * x, axis=-1, keepdims=True) + NORM_EPS)
    o_ref[...] = x * r * g_ref[...]


def _final_norm(x2, g, tm=512):
    m, d = x2.shape
    return pl.pallas_call(
        _final_norm_kernel,
        grid=(m // tm,),
        in_specs=[pl.BlockSpec((tm, d), lambda i: (i, 0)), pl.BlockSpec((1, d), lambda i: (0, 0))],
        out_specs=pl.BlockSpec((tm, d), lambda i: (i, 0)),
        out_shape=jax.ShapeDtypeStruct((m, d), F32),
        compiler_params=_params(("parallel",), 2 * tm * d * 8),
        name="final_norm",
    )(x2, g.reshape(1, d))


def _col_tile(n):
    return next(t for t in range(COL_TILE, 0, -LANE) if n % t == 0)


class _CastJob(NamedTuple):
    src: jax.Array
    r_pad: int
    c_pad: int


def _cast_plan(job, n_steps):
    depth, r, _ = job.src.shape
    for rc in range(16, r + 1, 16):
        if r % rc == 0 and job.r_pad % rc == 0 and depth * (job.r_pad // rc) <= n_steps:
            return rc
    return None


def _cast_specs(job, rc, n_j):
    depth, r, c = job.src.shape
    cpl, valid = job.r_pad // rc, r // rc
    n_total = depth * cpl

    def pos(i, j):
        t = jnp.minimum(i * n_j + j, n_total - 1)
        return t // cpl, t % cpl

    in_spec = pl.BlockSpec((None, rc, c), lambda i, j: (pos(i, j)[0], jnp.minimum(pos(i, j)[1], valid - 1), 0))
    out_spec = pl.BlockSpec((None, rc, job.c_pad), lambda i, j: (pos(i, j)[0], pos(i, j)[1], 0))
    out_shape = jax.ShapeDtypeStruct((depth, job.r_pad, job.c_pad), BF16)
    return in_spec, out_spec, out_shape, (cpl, valid, n_total)


def _run_casts(metas, src_refs, dst_refs, n_j):
    t = pl.program_id(0) * n_j + pl.program_id(1)
    for (cpl, valid, n_total), s, d in zip(metas, src_refs, dst_refs):
        chunk = jnp.minimum(t, n_total - 1) % cpl
        n = s.shape[-1]

        @pl.when(jnp.logical_and(t < n_total, chunk < valid))
        def _():
            d[:, :n] = s[...].astype(d.dtype)
            if d.shape[-1] > n:
                d[:, n:] = jnp.zeros((d.shape[0], d.shape[-1] - n), d.dtype)

        if cpl > valid:
            @pl.when(jnp.logical_and(t < n_total, chunk >= valid))
            def _():
                d[...] = jnp.zeros_like(d)


def _cast_only_kernel(*refs, metas, n_j):
    _run_casts(metas, refs[:len(metas)], refs[len(metas):], n_j)


def _attach_casts(jobs, n_i, n_j):
    plans = [_cast_plan(job, n_i * n_j) for job in jobs]
    riders = [(job, rc) for job, rc in zip(jobs, plans) if rc is not None]
    specs = [_cast_specs(job, rc, n_j) for job, rc in riders]

    def finish(rider_outs):
        rider_outs = list(rider_outs)
        outs = []
        for job, rc in zip(jobs, plans):
            if rc is not None:
                outs.append(rider_outs.pop(0))
                continue
            rc = next(c for c in (256, 128, 64, 32, 16) if job.src.shape[1] % c == 0 and job.r_pad % c == 0)
            n_steps = job.src.shape[0] * (job.r_pad // rc)
            i_s, o_s, o_shape, meta = _cast_specs(job, rc, n_steps)
            outs.append(pl.pallas_call(
                functools.partial(_cast_only_kernel, metas=[meta], n_j=n_steps),
                grid=(1, n_steps), in_specs=[i_s], out_specs=o_s, out_shape=o_shape,
                compiler_params=_params(("arbitrary", "arbitrary"), 2 * rc * (job.src.shape[2] * 4 + job.c_pad * 2)),
                name="cast_weights",
            )(job.src))
        return outs

    vmem = sum(2 * rc * (job.src.shape[2] * 4 + job.c_pad * 2) for job, rc in riders)
    return riders, specs, finish, vmem


def _mm_kernel(*refs, n_narrow, n_cast, metas, n_j):
    n_in = 2 + n_narrow + n_cast
    a_ref, w_ref = refs[:2]
    o_ref = refs[n_in]
    a = a_ref[...]
    o_ref[...] = _dot(a, w_ref[...]).astype(o_ref.dtype)
    if n_narrow:
        @pl.when(pl.program_id(1) == 0)
        def _():
            refs[n_in + 1][...] = _dot(a, refs[2][...])

    _run_casts(metas, refs[2 + n_narrow:n_in], refs[n_in + 1 + n_narrow:], n_j)


def _matmul(a, w, out_dtype=F32, w_narrow=None, cast_jobs=(), tm=ROW_TILE):
    m, k = a.shape
    n = w.shape[1]
    tm = min(tm, m)
    tn = _col_tile(n)
    ob = jnp.dtype(out_dtype).itemsize
    n_i, n_j = m // tm, n // tn
    riders, specs, finish, cast_vmem = _attach_casts(cast_jobs, n_i, n_j)
    narrow = [] if w_narrow is None else [w_narrow]
    outs = pl.pallas_call(
        functools.partial(_mm_kernel, n_narrow=len(narrow), n_cast=len(riders), metas=[s[3] for s in specs], n_j=n_j),
        grid=(n_i, n_j),
        in_specs=[pl.BlockSpec((tm, k), lambda i, j: (i, 0)), pl.BlockSpec((k, tn), lambda i, j: (0, j))]
        + [pl.BlockSpec((k, LANE), lambda i, j: (0, 0)) for _ in narrow] + [s[0] for s in specs],
        out_specs=[pl.BlockSpec((tm, tn), lambda i, j: (i, j))]
        + [pl.BlockSpec((tm, LANE), lambda i, j: (i, 0)) for _ in narrow] + [s[1] for s in specs],
        out_shape=[jax.ShapeDtypeStruct((m, n), out_dtype)]
        + [jax.ShapeDtypeStruct((m, LANE), F32) for _ in narrow] + [s[2] for s in specs],
        compiler_params=_params(("arbitrary", "arbitrary") if riders or narrow else ("parallel", "parallel"),
                                2 * (tm * k * 2 + k * tn * 2 + tm * tn * ob + k * LANE * 2 + tm * LANE * 4)
                                + cast_vmem),
        name="matmul",
    )(a, w, *narrow, *[job.src for job, _ in riders])
    n_main = 1 + len(narrow)
    return outs[0], (outs[1] if narrow else None), finish(outs[n_main:])


def _mm_swiglu_kernel(*refs, n_mod, n_cast, metas, n_j):
    n_in = 3 + 3 * n_mod + n_cast
    a_ref, wg_ref, wu_ref = refs[:3]
    o_ref = refs[n_in]
    a = a_ref[...]
    g = _dot(a, wg_ref[...])
    u = _dot(a, wu_ref[...])
    o_ref[...] = (_silu(g) * u).astype(o_ref.dtype)
    if n_mod:
        @pl.when(pl.program_id(0) * n_j + pl.program_id(1) < metas[-1])
        def _():
            _mod_kernel(refs[3], refs[4], refs[5], refs[n_in + 1])

    _run_casts(metas[:n_cast], refs[3 + 3 * n_mod:n_in], refs[n_in + 1 + n_mod:], n_j)


def _matmul_swiglu(a, wg, wu, layer, cast_jobs=(), mod_job=None):
    m, k = a.shape
    n = wg.shape[2]
    tm = min(ROW_TILE, m)
    tn = COL_TILE
    n_i, n_j = m // tm, n // tn
    riders, specs, finish, cast_vmem = _attach_casts(cast_jobs, n_i, n_j)
    mod_in, mod_specs, mod_out_spec, mod_out_shape, n_chunks = [], [], [], [], 0
    if mod_job is not None:
        s8, ada_w, ada_b, ml = mod_job
        d_mod, n_mod_cols = ada_w.shape[1], ada_w.shape[2]
        n_chunks = n_mod_cols // LANE
        if n_chunks <= n_i * n_j:
            chunk = lambda i, j: jnp.minimum(i * n_j + j, n_chunks - 1)
            mod_in = [s8, ada_w, ada_b.reshape(ada_w.shape[0], 1, n_mod_cols)]
            mod_specs = [pl.BlockSpec((8, d_mod), lambda i, j: (0, 0)),
                         pl.BlockSpec((None, d_mod, LANE), lambda i, j: (ml, 0, chunk(i, j))),
                         pl.BlockSpec((None, 1, LANE), lambda i, j: (ml, 0, chunk(i, j)))]
            mod_out_spec = [pl.BlockSpec((8, LANE), lambda i, j: (0, chunk(i, j)))]
            mod_out_shape = [jax.ShapeDtypeStruct((8, n_mod_cols), F32)]
            cast_vmem += 2 * d_mod * LANE * 6
    n_mod = len(mod_out_spec)
    side = bool(riders) or n_mod
    outs = pl.pallas_call(
        functools.partial(_mm_swiglu_kernel, n_mod=n_mod, n_cast=len(riders),
                          metas=[s[3] for s in specs] + [n_chunks], n_j=n_j),
        grid=(n_i, n_j),
        in_specs=[
            pl.BlockSpec((tm, k), lambda i, j: (i, 0)),
            pl.BlockSpec((None, k, tn), lambda i, j: (layer, 0, j)),
            pl.BlockSpec((None, k, tn), lambda i, j: (layer, 0, j)),
        ] + mod_specs + [s[0] for s in specs],
        out_specs=[pl.BlockSpec((tm, tn), lambda i, j: (i, j))] + mod_out_spec + [s[1] for s in specs],
        out_shape=[jax.ShapeDtypeStruct((m, n), BF16)] + mod_out_shape + [s[2] for s in specs],
        compiler_params=_params(("arbitrary", "arbitrary") if side else ("parallel", "parallel"),
                                2 * (tm * k * 2 + 2 * k * tn * 2 + tm * tn * 2) + cast_vmem),
        name="matmul_swiglu",
    )(a, wg, wu, *mod_in, *[job.src for job, _ in riders])
    mod = outs[1] if n_mod else None
    if mod_job is not None and not n_mod:
        mod = _mod_vectors(mod_job[0], mod_job[1], mod_job[2], first_layer=mod_job[3], n_layers=1)[0]
    return outs[0], mod, finish(outs[1 + n_mod:])


def _mm_res_kernel(*refs, nk, n_a):
    a_refs, (w_ref, res_ref, gate_ref, o_ref), scratch = refs[:n_a], refs[n_a:n_a + 4], refs[n_a + 4:]
    k0 = 0
    p = None
    for a_ref in a_refs:
        ka = a_ref.shape[1]
        d = _dot(a_ref[...], w_ref[k0:k0 + ka, :])
        p = d if p is None else p + d
        k0 += ka
    if nk == 1:
        o_ref[...] = res_ref[...] + gate_ref[...] * p
        return
    acc_ref, = scratch
    kk = pl.program_id(2)

    @pl.when(kk == 0)
    def _():
        acc_ref[...] = p

    @pl.when(jnp.logical_and(kk > 0, kk < nk - 1))
    def _():
        acc_ref[...] += p

    @pl.when(kk == nk - 1)
    def _():
        o_ref[...] = res_ref[...] + gate_ref[...] * (acc_ref[...] + p)


def _matmul_residual(a_slabs, w, res, mod5, layer, k_gate, mod_row, nk=1, tn=COL_TILE):
    m = a_slabs[0].shape[0]
    k, n = w.shape[1], w.shape[2]
    tm = min(ROW_TILE, m)
    tk = k // nk
    assert nk == 1 or len(a_slabs) == 1
    a_specs = [pl.BlockSpec((tm, tk if nk > 1 else a.shape[1]), lambda i, j, q: (i, q)) for a in a_slabs]
    return pl.pallas_call(
        functools.partial(_mm_res_kernel, nk=nk, n_a=len(a_slabs)),
        grid=(m // tm, n // tn, nk),
        in_specs=a_specs + [
            pl.BlockSpec((None, tk, tn), lambda i, j, q: (layer, q, j)),
            pl.BlockSpec((tm, tn), lambda i, j, q: (i, j)),
            pl.BlockSpec((None, None, None, 1, tn), lambda i, j, q: (layer, mod_row(i, tm), k_gate, 0, j)),
        ],
        out_specs=pl.BlockSpec((tm, tn), lambda i, j, q: (i, j)),
        out_shape=jax.ShapeDtypeStruct((m, n), F32),
        scratch_shapes=[pltpu.VMEM((tm, tn), F32)] if nk > 1 else [],
        compiler_params=_params(
            ("parallel", "parallel", "arbitrary"), 2 * (tm * tk * 2 + tk * tn * 2 + 2 * tm * tn * 4) + tm * tn * 4
        ),
        name="matmul_residual",
    )(*a_slabs, w, res, mod5)


def _rope_tables(n_tok):
    quarter = HEAD_DIM // 4
    inv = 1.0 / (ROPE_THETA ** (jnp.arange(quarter, dtype=F32) / quarter))
    pos = jnp.arange(n_tok, dtype=jnp.int32)
    row = (pos // GRID_W).astype(F32)[:, None] * inv
    col = (pos % GRID_W).astype(F32)[:, None] * inv
    cos = jnp.concatenate([jnp.cos(row), jnp.cos(row), jnp.cos(col), jnp.cos(col)], axis=-1)
    sin = jnp.concatenate([-jnp.sin(row), jnp.sin(row), -jnp.sin(col), jnp.sin(col)], axis=-1)
    return cos, sin


def _rotary(cos_ref, sin_ref, scale):
    cos = cos_ref[...] * scale
    sin = sin_ref[...] * scale
    lane = lax.broadcasted_iota(jnp.int32, cos.shape, 1)
    first = (lane & (HEAD_DIM // 4)) == 0

    def rotate(x):
        x = x.astype(F32)
        partner = jnp.where(first, pltpu.roll(x, HEAD_DIM - HEAD_DIM // 4, 1), pltpu.roll(x, HEAD_DIM // 4, 1))
        return x * cos + partner * sin

    return rotate


def _rope_kernel(x_ref, cos_ref, sin_ref, o_ref, *, scale, n_heads):
    rotate = _rotary(cos_ref, sin_ref, scale)
    for h in range(n_heads):
        hs = slice(h * HEAD_DIM, (h + 1) * HEAD_DIM)
        o_ref[:, hs] = rotate(x_ref[:, hs]).astype(o_ref.dtype)


def _rope(p, col0, width, cos, sin, scale, tr=256):
    m = p.shape[0]
    n_tok = cos.shape[0]
    tpb = n_tok // tr
    assert col0 % width == 0
    return pl.pallas_call(
        functools.partial(_rope_kernel, scale=scale, n_heads=width // HEAD_DIM),
        grid=(m // tr,),
        in_specs=[
            pl.BlockSpec((tr, width), lambda i: (i, col0 // width)),
            pl.BlockSpec((tr, HEAD_DIM), lambda i: (i % tpb, 0)),
            pl.BlockSpec((tr, HEAD_DIM), lambda i: (i % tpb, 0)),
        ],
        out_specs=pl.BlockSpec((tr, width), lambda i: (i, 0)),
        out_shape=jax.ShapeDtypeStruct((m, width), BF16),
        compiler_params=_params(("parallel",), 2 * tr * (width * 6 + HEAD_DIM * 8)),
        name="rope",
    )(p, cos, sin)


def _gla_kernel(*refs, nh, scale, fwd):
    ql, kl, vl, dl, qc, kc, vc, dc, up_ref, bias_ref = refs[:10]
    if fwd:
        ol, oc, st_ref = refs[10:]
    else:
        fl, fc, rl, rc, gn_ref, ol, oc, st_ref = refs[10:]
    s = pl.program_id(1)
    is_ctx = s == 0
    n_chunk = CTX_BLOCK // GLA_CHUNK

    @pl.when(s == 0)
    def _():
        st_ref[...] = jnp.zeros_like(st_ref)

    tb = CTX_BLOCK

    def in_scan_order(x):
        if fwd:
            return x
        return jnp.concatenate([x[(n_chunk - 1 - p) * GLA_CHUNK:(n_chunk - p) * GLA_CHUNK] for p in range(n_chunk)],
                               axis=0)

    def pick(rc_, rl_, cols):
        return jnp.where(is_ctx, rc_[:, cols], rl_[:, cols])

    def load(rc_, rl_, cols):
        return in_scan_order(pick(rc_, rl_, cols))

    row = lax.broadcasted_iota(jnp.int32, (tb, tb), 0)
    col = lax.broadcasted_iota(jnp.int32, (tb, tb), 1)
    incl = jnp.logical_and(row // GLA_CHUNK == col // GLA_CHUNK, (row >= col) if fwd else (col >= row))
    row_chunk = lax.broadcasted_iota(jnp.int32, (tb, GLA_DK), 0) // GLA_CHUNK

    def per_chunk(x):
        return jnp.concatenate([jnp.where(row_chunk == p, x, 0.0) for p in range(n_chunk)], axis=1)

    z = _dot(load(dc, dl, slice(None)).astype(BF16), up_ref[...]) + bias_ref[...]
    g = (jnp.minimum(z, 0.0) - jnp.log(1.0 + jnp.exp(-jnp.abs(z)))) * (1.0 / GLA_NORMALIZER)
    tmat = jnp.where(incl, 1.0, 0.0).astype(BF16)
    g_hi = g.astype(BF16)
    g_rest = g - g_hi.astype(F32)
    g_mid = g_rest.astype(BF16)
    g_lo = (g_rest - g_mid.astype(F32)).astype(BF16)
    cum = _dot(tmat, g_hi) + _dot(tmat, g_mid) + _dot(tmat, g_lo)
    tot = [jnp.sum(g[p * GLA_CHUNK:(p + 1) * GLA_CHUNK], axis=0, keepdims=True) for p in range(n_chunk)]
    tot_rows = jnp.concatenate([jnp.broadcast_to(t, (GLA_CHUNK, t.shape[1])) for t in tot], axis=0)

    staged = []
    for h in range(nh):
        ks = slice(h * GLA_DK, (h + 1) * GLA_DK)
        vs = slice(h * GLA_DV, (h + 1) * GLA_DV)
        bh = cum[:, ks]
        q = load(qc, ql, ks)
        k = load(kc, kl, ks)
        v = load(vc, vl, vs).astype(BF16)
        qd = q * (scale * jnp.exp(bh))
        ki = (k * jnp.exp(-bh)).astype(BF16)
        ku = k * jnp.exp(tot_rows[:, ks] - bh)
        a = jnp.where(incl, _dot_nt(qd.astype(BF16), ki), 0.0)
        inc = _dot_tn(v, per_chunk(ku).astype(BF16))
        staged.append((a.astype(BF16), inc, per_chunk(qd).astype(BF16), v))

    for h in range(nh):
        ks = slice(h * GLA_DK, (h + 1) * GLA_DK)
        vs = slice(h * GLA_DV, (h + 1) * GLA_DV)
        a, inc, qd_chunks, v = staged[h]
        st = st_ref[h]
        entering = []
        for p in range(n_chunk):
            entering.append(st)
            st = st * jnp.exp(tot[p][:, ks]) + inc[:, p * GLA_DK:(p + 1) * GLA_DK]
        st_ref[h] = st
        o = _dot(a, v) + _dot_nt(qd_chunks, jnp.concatenate(entering, axis=1).astype(BF16))
        o = in_scan_order(o)
        if not fwd:
            o = o + pick(fc, fl, vs)
            o = o * lax.rsqrt(jnp.mean(o * o, axis=-1, keepdims=True) + NORM_EPS) * gn_ref[...]
            o = o * _silu(pick(rc, rl, vs))
        o = o.astype(ol.dtype)

        @pl.when(is_ctx)
        def _():
            oc[:, vs] = o

        @pl.when(jnp.logical_not(is_ctx))
        def _():
            ol[:, vs] = o


def _gla(p_lat, p_ctx, dn_lat, dn_ctx, up2, bias2, norm_g, nb, n_tok, nh):
    qk = nh * GLA_DK
    dv = nh * GLA_DV
    tb = CTX_BLOCK
    nbl = n_tok // tb
    m_lat, m_ctx = nb * n_tok, nb * tb

    def scan(fwd, extra_in, extra_specs, out_dtype):
        def lat_blk(b, s):
            t = jnp.maximum(s - 1, 0)
            return b * nbl + (t if fwd else nbl - 1 - t)

        lat = lambda w, c: pl.BlockSpec((tb, w), lambda b, s: (lat_blk(b, s), c))
        ctx = lambda w, c: pl.BlockSpec((tb, w), lambda b, s: (b, c))
        d = 0 if fwd else 1
        vm = (2 * (2 * (2 * tb * qk + tb * dv + tb * LANE) * 4 + 2 * tb * dv * 4 + len(extra_in) * tb * dv * 4)
              + nh * GLA_DV * GLA_DK * 4)
        return pl.pallas_call(
            functools.partial(_gla_kernel, nh=nh, scale=GLA_DK ** -0.5, fwd=fwd),
            grid=(nb, nbl + 1),
            in_specs=[
                lat(qk, 0), lat(qk, 1), lat(dv, 1), lat(LANE, 0),
                ctx(qk, 0), ctx(qk, 1), ctx(dv, 1), ctx(LANE, 0),
                pl.BlockSpec((None, LANE, qk), lambda b, s: (d, 0, 0)),
                pl.BlockSpec((None, 1, qk), lambda b, s: (d, 0, 0)),
            ] + extra_specs(lat, ctx),
            out_specs=[lat(dv, 0), ctx(dv, 0)],
            out_shape=[jax.ShapeDtypeStruct((m_lat, dv), out_dtype), jax.ShapeDtypeStruct((m_ctx, dv), out_dtype)],
            scratch_shapes=[pltpu.VMEM((nh, GLA_DV, GLA_DK), F32)],
            compiler_params=_params(("parallel", "arbitrary"), vm),
            name="gla_scan_fwd" if fwd else "gla_scan_bwd",
        )(p_lat, p_lat, p_lat, dn_lat, p_ctx, p_ctx, p_ctx, dn_ctx, up2, bias2, *extra_in)

    of_lat, of_ctx = scan(True, (), lambda lat, ctx: [], F32)
    return scan(
        False,
        (of_lat, of_ctx, p_lat, p_ctx, norm_g.reshape(1, GLA_DV)),
        lambda lat, ctx: [lat(dv, 0), ctx(dv, 0), lat(dv, 2), ctx(dv, 2),
                          pl.BlockSpec((1, GLA_DV), lambda b, s: (0, 0))],
        BF16,
    )


def _swa_kernel(sink_ref, q_ref, cos_ref, sin_ref, kp, kc, kn, vp, vc, vn, kx, vx, o_ref, *, n_blk, kvs):
    i = pl.program_id(2)
    log2e = math.log2(math.e)
    rotate = _rotary(cos_ref, sin_ref, HEAD_DIM ** -0.5 * log2e)
    span = SWA_BLOCK + 2 * SWA_WINDOW
    gq = SWA_GROUP * SWA_BLOCK
    kj = lax.broadcasted_iota(jnp.int32, (span, gq), 0)
    lane = lax.broadcasted_iota(jnp.int32, (span, gq), 1)
    qi = lane & (SWA_BLOCK - 1)
    key_pos = (i - 1) * SWA_BLOCK + kj
    ok = jnp.logical_and(jnp.abs(qi + SWA_WINDOW - kj) <= SWA_WINDOW,
                         jnp.logical_and(key_pos >= 0, key_pos < n_blk * SWA_BLOCK))
    head = lax.broadcasted_iota(jnp.int32, (1, gq), 1) // SWA_BLOCK
    for kv in range(kvs):
        hkv = pl.program_id(1) * kvs + kv
        ks = slice(kv * HEAD_DIM, (kv + 1) * HEAD_DIM)
        kwin = jnp.concatenate([kp[:, ks], kc[:, ks], kn[:, ks]], axis=0)
        vwin = jnp.concatenate([vp[:, ks], vc[:, ks], vn[:, ks]], axis=0).astype(BF16)
        kctx = kx[:, ks].astype(BF16)
        vctx = vx[:, ks].astype(BF16)
        q0 = kv * SWA_GROUP * HEAD_DIM
        qs = jnp.concatenate([rotate(q_ref[:, q0 + g * HEAD_DIM:q0 + (g + 1) * HEAD_DIM]).astype(BF16)
                              for g in range(SWA_GROUP)], axis=0)
        s_win = jnp.where(ok, _dot_nt(kwin, qs), -jnp.inf)
        s_ctx = _dot_nt(kctx, qs)
        sink = jnp.zeros((1, gq), F32)
        for g in range(SWA_GROUP):
            sink = jnp.where(head == g, sink_ref[0, hkv * SWA_GROUP + g] * log2e, sink)
        m = jnp.maximum(jnp.maximum(jnp.max(s_win, axis=0, keepdims=True), jnp.max(s_ctx, axis=0, keepdims=True)),
                        sink)
        p_win = jnp.exp2(s_win - m)
        p_ctx = jnp.exp2(s_ctx - m)
        den = jnp.sum(p_win, axis=0, keepdims=True) + jnp.sum(p_ctx, axis=0, keepdims=True) + jnp.exp2(sink - m)
        ot = (_dot_tn(vctx, p_ctx.astype(BF16)) + _dot_tn(vwin, p_win.astype(BF16))) / den
        for g in range(SWA_GROUP):
            hs = slice(q0 + g * HEAD_DIM, q0 + (g + 1) * HEAD_DIM)
            o_ref[:, hs] = ot[:, g * SWA_BLOCK:(g + 1) * SWA_BLOCK].T.astype(o_ref.dtype)


def _swa(sink, cos, sin, k_rope, p_lat, p_ctx, qcol, kcol, vcol, nb, n_tok, n_kv):
    n_blk = n_tok // SWA_BLOCK
    tb = SWA_BLOCK
    kvs = next(c for c in (4, 2, 1) if n_kv % c == 0 and kcol % (c * HEAD_DIM) == 0 and vcol % (c * HEAD_DIM) == 0
               and qcol % (c * SWA_GROUP * HEAD_DIM) == 0)
    kw = kvs * HEAD_DIM
    gw = kvs * SWA_GROUP * HEAD_DIM
    assert kcol % kw == 0 and vcol % kw == 0 and qcol % gw == 0
    kc0 = kcol // kw
    vc0 = vcol // kw
    qc0 = qcol // gw
    prev = lambda b, h, i: b * n_blk + jnp.maximum(i - 1, 0)
    cur = lambda b, h, i: b * n_blk + i
    nxt = lambda b, h, i: b * n_blk + jnp.minimum(i + 1, n_blk - 1)
    kspec = lambda f: pl.BlockSpec((tb, kw), lambda b, h, i: (f(b, h, i), h))
    vspec = lambda f: pl.BlockSpec((tb, kw), lambda b, h, i: (f(b, h, i), vc0 + h))
    return pl.pallas_call(
        functools.partial(_swa_kernel, n_blk=n_blk, kvs=kvs),
        grid=(nb, n_kv // kvs, n_blk),
        in_specs=[
            pl.BlockSpec(memory_space=pltpu.SMEM),
            pl.BlockSpec((tb, gw), lambda b, h, i: (cur(b, h, i), qc0 + h)),
            pl.BlockSpec((tb, HEAD_DIM), lambda b, h, i: (i, 0)),
            pl.BlockSpec((tb, HEAD_DIM), lambda b, h, i: (i, 0)),
            kspec(prev), kspec(cur), kspec(nxt),
            vspec(prev), vspec(cur), vspec(nxt),
            pl.BlockSpec((CTX_BLOCK, kw), lambda b, h, i: (b, kc0 + h)),
            pl.BlockSpec((CTX_BLOCK, kw), lambda b, h, i: (b, vc0 + h)),
        ],
        out_specs=pl.BlockSpec((tb, gw), lambda b, h, i: (cur(b, h, i), h)),
        out_shape=jax.ShapeDtypeStruct((nb * n_tok, n_kv * SWA_GROUP * HEAD_DIM), BF16),
        compiler_params=_params(("parallel", "parallel", "parallel"), 16 << 20),
        name="swa",
    )(sink, p_lat, cos, sin, k_rope, k_rope, k_rope, p_lat, p_lat, p_lat, p_ctx, p_ctx)


def _ctx_sink_kernel(sink_ref, q_ref, kx, vx, o_ref, *, scale):
    hkv = pl.program_id(1)
    kctx = kx[...].astype(BF16)
    vctx = vx[...].astype(BF16)
    for g in range(SWA_GROUP):
        hs = slice(g * HEAD_DIM, (g + 1) * HEAD_DIM)
        q = (q_ref[:, hs].astype(F32) * scale).astype(BF16)
        s = _dot_nt(q, kctx)
        sink = sink_ref[0, hkv * SWA_GROUP + g]
        m = jnp.maximum(jnp.max(s, axis=-1, keepdims=True), sink)
        p = jnp.exp(s - m)
        den = jnp.sum(p, axis=-1, keepdims=True) + jnp.exp(sink - m)
        o_ref[:, hs] = (_dot(p.astype(BF16), vctx) / den).astype(o_ref.dtype)


def _ctx_sink(sink, p_ctx, qcol, kcol, vcol, nb, n_kv):
    gw = SWA_GROUP * HEAD_DIM
    return pl.pallas_call(
        functools.partial(_ctx_sink_kernel, scale=HEAD_DIM ** -0.5),
        grid=(nb, n_kv),
        in_specs=[
            pl.BlockSpec(memory_space=pltpu.SMEM),
            pl.BlockSpec((CTX_BLOCK, gw), lambda b, h: (b, qcol // gw + h)),
            pl.BlockSpec((CTX_BLOCK, HEAD_DIM), lambda b, h: (b, kcol // HEAD_DIM + h)),
            pl.BlockSpec((CTX_BLOCK, HEAD_DIM), lambda b, h: (b, vcol // HEAD_DIM + h)),
        ],
        out_specs=pl.BlockSpec((CTX_BLOCK, gw), lambda b, h: (b, h)),
        out_shape=jax.ShapeDtypeStruct((nb * CTX_BLOCK, n_kv * gw), BF16),
        compiler_params=_params(("parallel", "parallel"), 4 << 20),
        name="ctx_sink_attention",
    )(sink, p_ctx, p_ctx, p_ctx)


def _na_bias_table(rpb, rows):
    qcol = np.arange(GRID_W)[:, None]
    kcol = np.arange(GRID_W)[None, :]
    col_start = np.clip(qcol - NA_COLS // 2, 0, GRID_W - NA_COLS)
    col_ok = (kcol >= col_start) & (kcol < col_start + NA_COLS)
    dc = np.clip(kcol - qcol + NA_COLS - 1, 0, 2 * NA_COLS - 2)
    n_dc = 2 * NA_COLS - 1
    onehot = (dc[None] == np.arange(n_dc)[:, None, None]).astype(np.float32)
    cols = jnp.einsum('hrd,dqk->hrqk', rpb.astype(F32), onehot, precision=lax.Precision.HIGHEST)
    cols = jnp.where(col_ok[None, None], cols, -jnp.inf).transpose(0, 1, 3, 2)
    masked = jnp.full(cols.shape[:1] + cols.shape[2:], -jnp.inf, F32)

    span = NA_ROWS + NA_QROWS - 1
    patterns = []
    for r0 in range(0, rows, NA_QROWS):
        rs = [min(max(r0 + i - NA_ROWS // 2, 0), rows - NA_ROWS) for i in range(NA_QROWS)]
        u = min(rs[0], rows - span)
        patterns.append(tuple((r0 + i - rs[i], rs[i] - u) for i in range(NA_QROWS)))
    variants = sorted(set(patterns), key=patterns.index)
    blocks = []
    for pat in variants:
        per_query_row = []
        for delta, off in pat:
            krows = [cols[:, kr - off - delta + NA_ROWS - 1] if 0 <= kr - off < NA_ROWS else masked
                     for kr in range(span)]
            per_query_row.append(jnp.stack(krows, axis=1))
        blocks.append(jnp.concatenate(per_query_row, axis=-1).reshape(rpb.shape[0], span * GRID_W, NA_QROWS * GRID_W))
    ids = np.array([variants.index(p) for p in patterns], dtype=np.int32)
    return jnp.stack(blocks, axis=1), ids


def _na_kernel(ids_ref, q_ref, k_ref, v_ref, kx, vx, bias_ref, o_ref, *, nh, scale):
    for h in range(nh):
        hs = slice(h * HEAD_DIM, (h + 1) * HEAD_DIM)
        q = (q_ref[:, hs].astype(F32) * scale).astype(BF16)
        s_nb = _dot_nt(k_ref[:, hs].astype(BF16), q) + bias_ref[h]
        s_ctx = _dot_nt(kx[:, hs].astype(BF16), q)
        m = jnp.maximum(jnp.max(s_nb, axis=0, keepdims=True), jnp.max(s_ctx, axis=0, keepdims=True))
        p_nb = jnp.exp(s_nb - m)
        p_ctx = jnp.exp(s_ctx - m)
        den = jnp.sum(p_nb, axis=0, keepdims=True) + jnp.sum(p_ctx, axis=0, keepdims=True)
        ot = (_dot_tn(vx[:, hs].astype(BF16), p_ctx.astype(BF16))
              + _dot_tn(v_ref[:, hs].astype(BF16), p_nb.astype(BF16)))
        o_ref[:, hs] = (ot / den).T.astype(o_ref.dtype)


def _na(p_lat, p_ctx, rpb, nb, n_tok, nh):
    rows = n_tok // GRID_W
    w = nh * HEAD_DIM
    span = NA_ROWS + NA_QROWS - 1
    win = span * GRID_W
    tq = NA_QROWS * GRID_W
    n_step = rows // NA_QROWS
    bias_tab, ids = _na_bias_table(rpb, rows)
    u = lambda g: jnp.clip(g * NA_QROWS - NA_ROWS // 2, 0, rows - span)
    kv = lambda c: pl.BlockSpec((pl.Element(win), pl.Element(w)),
                                lambda b, g, ids: ((b * rows + u(g)) * GRID_W, c * w))
    vm = 2 * (tq * w * 4 + 2 * win * w * 4 + 2 * CTX_BLOCK * w * 4 + nh * tq * win * 4 + tq * w * 2)
    return pl.pallas_call(
        functools.partial(_na_kernel, nh=nh, scale=HEAD_DIM ** -0.5),
        grid_spec=pltpu.PrefetchScalarGridSpec(
            num_scalar_prefetch=1,
            grid=(nb, n_step),
            in_specs=[
                pl.BlockSpec((tq, w), lambda b, g, ids: (b * n_step + g, 0)),
                kv(1), kv(2),
                pl.BlockSpec((CTX_BLOCK, w), lambda b, g, ids: (b, 1)),
                pl.BlockSpec((CTX_BLOCK, w), lambda b, g, ids: (b, 2)),
                pl.BlockSpec((nh, None, win, tq), lambda b, g, ids: (0, ids[g], 0, 0)),
            ],
            out_specs=pl.BlockSpec((tq, w), lambda b, g, ids: (b * n_step + g, 0)),
        ),
        out_shape=jax.ShapeDtypeStruct((nb * n_tok, w), BF16),
        compiler_params=_params(("parallel", "arbitrary"), vm),
        name="neighborhood_attention",
    )(jnp.asarray(ids), p_lat, p_lat, p_lat, p_ctx, p_ctx, bias_tab)


def _diff_kernel(qraw_ref, cos_ref, sin_ref, k_ref, v_ref, kx, vx, lq1, lk1, lq2, lk2, g_ref, o_ref,
                 m_ref, l_ref, acc_ref, sa_ref, sb_ref, pa_ref, pb_ref, pc_ref, aa_ref, ab_ref, ac_ref,
                 vt_ref, vxt_ref, q_ref, *, n_blk, tk, lam_init):

    rotate = _rotary(cos_ref, sin_ref, HEAD_DIM ** -0.5 * math.log2(math.e))
    for t in range(2):
        hs = slice(t * HEAD_DIM, (t + 1) * HEAD_DIM)
        q_ref[:, hs] = rotate(qraw_ref[:, hs]).astype(q_ref.dtype)

    @pl.when(pl.program_id(2) == 0)
    def _():
        for j in range(n_blk):
            vt_ref[j] = v_ref[j * tk:(j + 1) * tk, :].T
        vxt_ref[...] = vx[...].astype(BF16).T

    vxt = vxt_ref
    m_ref[...] = jnp.full_like(m_ref, -jnp.inf)
    l_ref[...] = jnp.zeros_like(l_ref)
    acc_ref[...] = jnp.zeros_like(acc_ref)

    def scores(keys):
        return [_dot_nt(keys[:, t * HEAD_DIM:(t + 1) * HEAD_DIM], q_ref[:, t * HEAD_DIM:(t + 1) * HEAD_DIM])
                for t in range(2)]

    def lat_scores(j, dst):
        s = scores(k_ref[pl.ds(pl.multiple_of(j * tk, tk), tk), :])
        dst[0] = s[0]
        dst[1] = s[1]

    def softmax(s, p_dst, a_dst):
        for t in range(2):
            m_old = m_ref[t]
            m_new = jnp.maximum(m_old, jnp.max(s[t], axis=0, keepdims=True))
            alpha = jnp.exp2(m_old - m_new)
            p = jnp.exp2(s[t] - m_new)
            l_ref[t] = alpha * l_ref[t] + jnp.sum(p, axis=0, keepdims=True)
            m_ref[t] = m_new
            p_dst[t] = p.astype(BF16)
            a_dst[t] = alpha

    def add_values(p_src, a_src, vt):
        for t in range(2):
            acc_ref[t] = a_src[t] * acc_ref[t] + _dot(vt, p_src[t])

    def stage(s_cur, p_cur, a_cur, next_block, s_next, prev):
        if next_block is not None:
            lat_scores(next_block, s_next)
        add_values(*prev)
        softmax((s_cur[0], s_cur[1]), p_cur, a_cur)

    s_ctx = scores(kx[...].astype(BF16))
    lat_scores(0, sa_ref)
    softmax(s_ctx, pc_ref, ac_ref)
    stage(sa_ref, pa_ref, aa_ref, 1, sb_ref, (pc_ref, ac_ref, vxt[...]))

    def body(jj, carry):
        j = 2 * jj + 1
        stage(sb_ref, pb_ref, ab_ref, j + 1, sa_ref, (pa_ref, aa_ref, vt_ref[j - 1]))
        stage(sa_ref, pa_ref, aa_ref, j + 2, sb_ref, (pb_ref, ab_ref, vt_ref[j]))
        return carry

    lax.fori_loop(0, (n_blk - 2) // 2, body, 0)
    stage(sb_ref, pb_ref, ab_ref, None, None, (pa_ref, aa_ref, vt_ref[n_blk - 2]))
    add_values(pb_ref, ab_ref, vt_ref[n_blk - 1])

    lam = (jnp.exp(jnp.sum(lq1[...] * lk1[...], axis=-1, keepdims=True))
           - jnp.exp(jnp.sum(lq2[...] * lk2[...], axis=-1, keepdims=True)) + lam_init)
    o = acc_ref[0] / l_ref[0] - lam * (acc_ref[1] / l_ref[1])
    y = o * lax.rsqrt(jnp.mean(o * o, axis=0, keepdims=True) + NORM_EPS) * (g_ref[...] * (1.0 - lam_init))
    o_ref[...] = y.T.astype(o_ref.dtype)


def _diff(cos, sin, k_rope, p_lat, p_ctx, qcol, kcol, vcol, lq1, lk1, lq2, lk2, norm_g, lam_init, nb, n_tok, nh,
          tq=512, tk=DIFF_KEY_BLOCK):
    nq = n_tok // tq
    n_blk = n_tok // tk
    assert n_blk >= 2 and n_blk % 2 == 0, "the key-block pipeline alternates two buffers"
    w = 2 * HEAD_DIM
    vec = lambda: pl.BlockSpec((1, HEAD_DIM), lambda b, h, i: (0, 0))
    vm = (2 * (tq * w * 2 + 2 * n_tok * w * 2 + 2 * CTX_BLOCK * w * 2 + tq * w * 2)
          + n_tok * w * 2 + 2 * tq * w * 4 + 2 * 2 * tk * tq * 6 + tq * w * 8)
    return pl.pallas_call(
        functools.partial(_diff_kernel, n_blk=n_blk, tk=tk, lam_init=lam_init),
        grid=(nb, nh, nq),
        in_specs=[
            pl.BlockSpec((tq, w), lambda b, h, i: (b * nq + i, qcol // w + h)),
            pl.BlockSpec((tq, HEAD_DIM), lambda b, h, i: (i, 0)),
            pl.BlockSpec((tq, HEAD_DIM), lambda b, h, i: (i, 0)),
            pl.BlockSpec((n_tok, w), lambda b, h, i: (b, h)),
            pl.BlockSpec((n_tok, w), lambda b, h, i: (b, vcol // w + h)),
            pl.BlockSpec((CTX_BLOCK, w), lambda b, h, i: (b, kcol // w + h)),
            pl.BlockSpec((CTX_BLOCK, w), lambda b, h, i: (b, vcol // w + h)),
            vec(), vec(), vec(), vec(),
            pl.BlockSpec((DIFF_DV, 1), lambda b, h, i: (0, 0)),
        ],
        out_specs=pl.BlockSpec((tq, w), lambda b, h, i: (b * nq + i, h)),
        out_shape=jax.ShapeDtypeStruct((nb * n_tok, nh * w), BF16),
        scratch_shapes=[
            pltpu.VMEM((2, 1, tq), F32),
            pltpu.VMEM((2, 1, tq), F32),
            pltpu.VMEM((2, DIFF_DV, tq), F32),
            pltpu.VMEM((2, tk, tq), F32),
            pltpu.VMEM((2, tk, tq), F32),
            pltpu.VMEM((2, tk, tq), BF16),
            pltpu.VMEM((2, tk, tq), BF16),
            pltpu.VMEM((2, CTX_BLOCK, tq), BF16),
            pltpu.VMEM((2, 1, tq), F32),
            pltpu.VMEM((2, 1, tq), F32),
            pltpu.VMEM((2, 1, tq), F32),
            pltpu.VMEM((n_blk, w, tk), BF16),
            pltpu.VMEM((w, CTX_BLOCK), BF16),
            pltpu.VMEM((tq, w), BF16),
        ],
        compiler_params=_params(("parallel", "parallel", "arbitrary"), vm),
        name="diff_attention",
    )(p_lat, cos, sin, k_rope, p_lat, p_ctx, p_ctx, lq1.reshape(1, -1), lk1.reshape(1, -1), lq2.reshape(1, -1),
      lk2.reshape(1, -1), norm_g.reshape(-1, 1))


def _pad_cols(w, n):
    return jnp.pad(w, ((0, 0), (0, n - w.shape[1])))


def kernel(x, c, ctx, c_ctx, ada_w, ada_b, norm_mix_g, norm_ffn_g, w_out, ffn_w_gate, ffn_w_up, ffn_w_down,
           ev_w_in, gla_gate_up_f, gla_gate_bias_f, gla_gate_up_b, gla_gate_bias_b, gla_norm_g, swa_sink,
           od_w_in, na_rpb, diff_lq1, diff_lk1, diff_lq2, diff_lk2, diff_norm_g, final_norm_g):
    nb, n_tok, d = x.shape
    lc = ctx.shape[1]
    depth = ada_w.shape[0]
    f_hidden = ffn_w_gate.shape[2]
    half = d // 2
    assert lc == CTX_BLOCK and nb < 8 and depth == 2, "layout assumes a 256-token context, batch < 8, two layers"
    gla_heads = half // GLA_DV
    gla_qk = gla_heads * GLA_DK
    n_q = half // HEAD_DIM
    n_kv = n_q // SWA_GROUP
    diff_heads = half // (2 * HEAD_DIM)

    xs = x.reshape(nb * n_tok, d)
    cs = ctx.reshape(nb * lc, d)
    cos, sin = _rope_tables(n_tok)

    s8 = jnp.zeros((8, d), F32).at[:nb].set(c).at[nb].set(c_ctx)
    mod5 = _mod_vectors(s8, ada_w, ada_b, 0, 1).reshape(1, 8, 6, 1, d)
    lat_row = lambda i, tm: (i * tm) // n_tok
    ctx_row = lambda i, tm: nb

    f_pad = -(-f_hidden // FFN_PAD) * FFN_PAD

    def ffn_up(stream, layer, mod_row, wg, wu, cast_jobs=(), mod_job=None):
        h2 = _norm_mod(stream, norm_ffn_g[layer], mod5, layer, 3, 4, mod_row)
        return _matmul_swiglu(h2, wg, wu, layer, cast_jobs, mod_job)

    def ffn_down(a, stream, layer, mod_row, wd):
        return _matmul_residual([a], wd, stream, mod5, layer, 5, mod_row, nk=4, tn=2 * COL_TILE)

    layer = 0
    w_in = ev_w_in[0]
    g0 = 2 * gla_qk + 2 * half
    w_main = jnp.concatenate([w_in[:, :g0], w_in[:, g0 + 2 * GLA_RANK:]], axis=1).astype(BF16)
    w_dn = _pad_cols(w_in[:, g0:g0 + 2 * GLA_RANK], LANE).astype(BF16)
    swa_q0 = g0
    swa_k0 = swa_q0 + half
    swa_v0 = swa_k0 + n_kv * HEAD_DIM
    up2 = jnp.zeros((2, LANE, gla_qk), F32)
    up2 = up2.at[0, :GLA_RANK].set(gla_gate_up_f[0]).at[1, GLA_RANK:2 * GLA_RANK].set(gla_gate_up_b[0]).astype(BF16)
    bias2 = jnp.stack([gla_gate_bias_f[0], gla_gate_bias_b[0]]).reshape(2, 1, gla_qk)

    h = _norm_mod(xs, norm_mix_g[layer], mod5, layer, 0, 1, lat_row)
    hc = _norm_mod(cs, norm_mix_g[layer], mod5, layer, 0, 1, ctx_row)
    p_lat, dn_lat, (wg, wu, wo) = _matmul(h, w_main, w_narrow=w_dn, cast_jobs=[
        _CastJob(ffn_w_gate, d, f_pad), _CastJob(ffn_w_up, d, f_pad), _CastJob(w_out, d, d)])
    p_ctx, dn_ctx, _ = _matmul(hc, w_main, w_narrow=w_dn)

    ya, yac = _gla(p_lat, p_ctx, dn_lat, dn_ctx, up2, bias2, gla_norm_g[0], nb, n_tok, gla_heads)

    sink = swa_sink[0].reshape(1, n_q)
    k_rope = _rope(p_lat, swa_k0, n_kv * HEAD_DIM, cos, sin, 1.0)
    yb = _swa(sink, cos, sin, k_rope, p_lat, p_ctx, swa_q0, swa_k0, swa_v0, nb, n_tok, n_kv)
    ybc = _ctx_sink(sink, p_ctx, swa_q0, swa_k0, swa_v0, nb, n_kv)

    xs = _matmul_residual([ya, yb], wo, xs, mod5, layer, 2, lat_row, tn=2 * COL_TILE)
    cs = _matmul_residual([yac, ybc], wo, cs, mod5, layer, 2, ctx_row, tn=2 * COL_TILE)
    a, mod_next, (wd, w_in1) = ffn_up(xs, layer, lat_row, wg, wu, cast_jobs=[
        _CastJob(ffn_w_down, f_pad, d), _CastJob(od_w_in, d, od_w_in.shape[2])], mod_job=(s8, ada_w, ada_b, 1))
    mod5 = jnp.concatenate([mod5, mod_next.reshape(1, 8, 6, 1, d)], axis=0)
    xs = ffn_down(a, xs, layer, lat_row, wd)
    cs = ffn_down(ffn_up(cs, layer, ctx_row, wg, wu)[0], cs, layer, ctx_row, wd)

    layer = 1
    w_in = w_in1[0]
    h = _norm_mod(xs, norm_mix_g[layer], mod5, layer, 0, 1, lat_row)
    hc = _norm_mod(cs, norm_mix_g[layer], mod5, layer, 0, 1, ctx_row)
    p_lat = _matmul(h, w_in, BF16, tm=2 * ROW_TILE)[0]
    p_ctx = _matmul(hc, w_in, BF16)[0]

    yn =_na(p_lat, p_ctx, na_rpb[0], nb, n_tok, n_q)

    dq0 = 3 * half
    dk0 = dq0 + half
    dv0 = dk0 + half
    lam_init = 0.8 - 0.6 * math.exp(-0.3 * layer)
    kd = _rope(p_lat, dk0, half, cos, sin, 1.0)
    yd = _diff(cos, sin, kd, p_lat, p_ctx, dq0, dk0, dv0, diff_lq1[0], diff_lk1[0], diff_lq2[0], diff_lk2[0],
               diff_norm_g[0], lam_init, nb, n_tok, diff_heads)

    xs = _matmul_residual([yn, yd], wo, xs, mod5, layer, 2, lat_row, tn=2 * COL_TILE)
    xs = ffn_down(ffn_up(xs, layer, lat_row, wg, wu)[0], xs, layer, lat_row, wd)
    return _final_norm(xs, final_norm_g).reshape(nb, n_tok, d)
```
